```python
import math
import jax, jax.numpy as jnp
from jax import lax
import numpy as np

D_MODEL = 1024
BATCH = 8
SEQ = 2048
DEPTH = 2

CHUNK = 64
Q_BLOCK = 128
EPS = 1e-6

ML_HEADS = 4
ML_QK_DIM = 64
ML_V_DIM = 128
ML_CONV = 4
DSA_HEADS = 4
DSA_HEAD_DIM = 128
DSA_LATENT = 128
IDX_HEADS = 4
IDX_DIM = 64
DSA_TOPK = 256
FOX_HEADS = 4
FOX_HEAD_DIM = 128
N_BRANCH = 3
BRANCH_WIDTH = 512
N_EXPERTS = 32
TOP_K = 4
D_EXPERT = D_MODEL
SWIGLU_LIMIT = 7.0
SWIGLU_ALPHA = 1.702
MOE_BLOCK = 128

SPLITS = (
    ML_HEADS * ML_QK_DIM,
    ML_HEADS * ML_QK_DIM,
    ML_HEADS * ML_V_DIM,
    ML_HEADS * ML_V_DIM,
    ML_HEADS,
    ML_HEADS,
    DSA_HEADS * DSA_HEAD_DIM,
    DSA_LATENT,
    IDX_HEADS * IDX_DIM,
    IDX_DIM,
    IDX_HEADS,
    3 * FOX_HEADS * FOX_HEAD_DIM,
    FOX_HEADS,
    N_BRANCH * D_MODEL,
)
D_IN = sum(SPLITS)

kernel_name = 'streaming_hybrid_mlstm_dsa_fox_moe'


def rms(x):
    x32 = x.astype(jnp.float32)
    return (x32 * lax.rsqrt(jnp.mean(x32 * x32, axis=-1, keepdims=True) + EPS)).astype(x.dtype)


def causal_dwconv(x, w, b):
    k = w.shape[0]
    y = lax.conv_general_dilated(x, w[:, None, :], window_strides=(1,), padding=[(k - 1, 0)],
                                 dimension_numbers=('NWC', 'WIO', 'NWC'),
                                 feature_group_count=x.shape[-1])
    return y + b


def mlstm_chunkwise(q, k, v, i_pre, f_pre):
    B, S, H, dk = q.shape
    dv = v.shape[-1]
    L = CHUNK
    NC = S // L

    def chunks(a):
        a = a.astype(jnp.float32).reshape((B, NC, L, H) + a.shape[3:])
        return jnp.moveaxis(a, 3, 1)

    qc = chunks(q) * dk ** -0.5
    kc = chunks(k)
    vc = chunks(v)
    ig = chunks(i_pre)
    b = jnp.cumsum(jax.nn.log_sigmoid(chunks(f_pre)), axis=-1)
    b_last = b[..., -1]
    causal = jnp.tril(jnp.ones((L, L), dtype=bool))
    d_log = jnp.where(causal, b[..., :, None] - b[..., None, :] + ig[..., None, :], -jnp.inf)
    w_state = b_last[..., None] - b + ig
    m_loc = jnp.max(w_state, axis=-1)
    e_state = jnp.exp(w_state - m_loc[..., None])
    c_loc = jnp.einsum('bhcl,bhclv,bhclk->bhcvk', e_state, vc, kc)
    n_loc = jnp.einsum('bhcl,bhclk->bhck', e_state, kc)

    def step(carry, inp):
        c_st, n_st, m_st = carry
        cl, nl, ml, bl = inp
        m_new = jnp.maximum(bl + m_st, ml)
        decay = jnp.exp(bl + m_st - m_new)
        scale = jnp.exp(ml - m_new)
        c_new = decay[..., None, None] * c_st + scale[..., None, None] * cl
        n_new = decay[..., None] * n_st + scale[..., None] * nl
        return (c_new, n_new, m_new), (c_st, n_st, m_st)

    init = (jnp.zeros((B, H, dv, dk), jnp.float32),
            jnp.zeros((B, H, dk), jnp.float32),
            jnp.full((B, H), -jnp.inf, jnp.float32))
    xs = (jnp.moveaxis(c_loc, 2, 0), jnp.moveaxis(n_loc, 2, 0),
          jnp.moveaxis(m_loc, 2, 0), jnp.moveaxis(b_last, 2, 0))
    _, (c_prev, n_prev, m_prev) = lax.scan(step, init, xs)
    c_prev = jnp.moveaxis(c_prev, 0, 2)
    n_prev = jnp.moveaxis(n_prev, 0, 2)
    m_prev = jnp.moveaxis(m_prev, 0, 2)

    inter_log = b + m_prev[..., None]
    m_out = jnp.maximum(inter_log, jnp.max(d_log, axis=-1))
    qk = jnp.einsum('bhctk,bhcsk->bhcts', qc, kc) * jnp.exp(d_log - m_out[..., None])
    a_inter = jnp.exp(inter_log - m_out)
    num = (jnp.einsum('bhcts,bhcsv->bhctv', qk, vc)
           + a_inter[..., None] * jnp.einsum('bhcvk,bhctk->bhctv', c_prev, qc))
    den = jnp.sum(qk, axis=-1) + a_inter * jnp.einsum('bhck,bhctk->bhct', n_prev, qc)
    hid = num / jnp.maximum(jnp.abs(den), jnp.exp(-m_out))[..., None]
    return jnp.moveaxis(hid, 1, 3).reshape(B, S, H, dv).astype(v.dtype)


def blockify(a, nqb):
    return jnp.moveaxis(a.reshape((a.shape[0], nqb, Q_BLOCK) + a.shape[2:]), 1, 0)


def dsa_attention(q, ckv, q_idx, k_idx, w_idx, kv_norm_g, w_uk, w_uv):
    B, S, H, dh = q.shape
    n_sel = min(DSA_TOPK, S // 4)
    nqb = S // Q_BLOCK
    ckv = rms(ckv) * kv_norm_g
    q_abs = jnp.einsum('bshd,hcd->bshc', q, w_uk) * dh ** -0.5
    key_chunk = jnp.arange(S) // CHUNK

    def one_block(args):
        qb, qib, wib, qpos = args
        q_chunk = qpos // CHUNK
        idx_logit = jnp.einsum('bthd,bsd->bths', qib, k_idx) * IDX_DIM ** -0.5
        score = jnp.einsum('bth,bths->bts', wib, jax.nn.relu(idx_logit)).astype(jnp.float32)
        score = jnp.where(key_chunk[None, None, :] <= q_chunk[None, :, None], score, -jnp.inf)
        _, sel = lax.top_k(score, n_sel)
        valid = (sel // CHUNK) <= q_chunk[None, :, None]
        kv_sel = jax.vmap(lambda cb, ib: cb[ib])(ckv, sel)
        logits = jnp.einsum('bthc,btkc->bthk', qb, kv_sel).astype(jnp.float32)
        logits = jnp.where(valid[:, :, None, :], logits, -jnp.inf)
        p = jax.nn.softmax(logits, axis=-1).astype(kv_sel.dtype)
        lat = jnp.einsum('bthk,btkc->bthc', p, kv_sel)
        return jnp.einsum('bthc,hcd->bthd', lat, w_uv)

    out = lax.map(one_block, (blockify(q_abs, nqb), blockify(q_idx, nqb), blockify(w_idx, nqb),
                              jnp.arange(S).reshape(nqb, Q_BLOCK)))
    return jnp.moveaxis(out, 0, 1).reshape(B, S, H * dh)


def forgetting_attention(q, k, v, f_pre):
    B, S, H, d = q.shape
    nqb = S // Q_BLOCK
    cum_f = jnp.cumsum(jax.nn.log_sigmoid(f_pre.astype(jnp.float32)), axis=1)
    cum_f_keys = jnp.transpose(cum_f, (0, 2, 1))
    pos = jnp.arange(S)

    def one_block(args):
        qb, fq, qpos = args
        logits = jnp.einsum('bthd,bshd->bhts', qb, k).astype(jnp.float32) * d ** -0.5
        logits = logits + jnp.transpose(fq, (0, 2, 1))[..., :, None] - cum_f_keys[..., None, :]
        logits = jnp.where(pos[None, :] <= qpos[:, None], logits, -jnp.inf)
        p = jax.nn.softmax(logits, axis=-1).astype(v.dtype)
        return jnp.einsum('bhts,bshd->bthd', p, v)

    out = lax.map(one_block, (blockify(q, nqb), blockify(cum_f, nqb), pos.reshape(nqb, Q_BLOCK)))
    return jnp.moveaxis(out, 0, 1).reshape(B, S, H * d)


def hybrid_mixer(h, w_in, b_in, conv_w, conv_b, ml_norm_g, kv_norm_g, w_uk, w_uv,
                 w_br_ml, w_br_dsa, w_br_fox, w_out):
    B, S, _ = h.shape
    proj = h @ w_in + b_in
    (ml_q, ml_k, ml_v, ml_o, ml_i, ml_f, d_q, d_ckv, d_iq, d_ik, d_iw,
     fx_qkv, fx_f, gates) = jnp.split(proj, np.cumsum(SPLITS)[:-1].tolist(), axis=-1)

    qk_conv = jax.nn.silu(causal_dwconv(jnp.concatenate([ml_q, ml_k], axis=-1), conv_w, conv_b))
    mq, mk = jnp.split(qk_conv, 2, axis=-1)
    hid = mlstm_chunkwise(mq.reshape(B, S, ML_HEADS, ML_QK_DIM), mk.reshape(B, S, ML_HEADS, ML_QK_DIM),
                          ml_v.reshape(B, S, ML_HEADS, ML_V_DIM), ml_i, ml_f)
    y_ml = (rms(hid) * ml_norm_g.reshape(ML_HEADS, ML_V_DIM)).reshape(B, S, -1) * jax.nn.sigmoid(ml_o)

    y_dsa = dsa_attention(d_q.reshape(B, S, DSA_HEADS, DSA_HEAD_DIM), d_ckv,
                          d_iq.reshape(B, S, IDX_HEADS, IDX_DIM), d_ik, d_iw * IDX_HEADS ** -0.5,
                          kv_norm_g, w_uk, w_uv)

    fq, fk, fv = jnp.split(fx_qkv.reshape(B, S, 3, FOX_HEADS, FOX_HEAD_DIM), 3, axis=2)
    y_fox = forgetting_attention(fq[:, :, 0], fk[:, :, 0], fv[:, :, 0], fx_f)

    g = jax.nn.sigmoid(gates).reshape(B, S, N_BRANCH, D_MODEL)
    merged = (g[:, :, 0] * (y_ml @ w_br_ml) + g[:, :, 1] * (y_dsa @ w_br_dsa)
              + g[:, :, 2] * (y_fox @ w_br_fox))
    return merged @ w_out


def moe_ffn(x, w_router, b_router, w1, b1, w2, b2):
    B, S, D = x.shape
    N = B * S
    NK = N * TOP_K
    xt = x.reshape(N, D)
    logits = (xt @ w_router + b_router).astype(jnp.float32)
    top_val, top_idx = lax.top_k(logits, TOP_K)
    gate = jax.nn.softmax(top_val, axis=-1).astype(x.dtype)

    e_flat = top_idx.reshape(-1)
    tok_flat = jnp.arange(NK, dtype=jnp.int32) // TOP_K
    order = jnp.argsort(e_flat)
    e_sorted = e_flat[order]
    counts = jnp.zeros((N_EXPERTS,), jnp.int32).at[e_flat].add(1)
    starts = jnp.cumsum(counts) - counts
    padded = (counts + MOE_BLOCK - 1) // MOE_BLOCK * MOE_BLOCK
    pad_ends = jnp.cumsum(padded)
    pad_starts = pad_ends - padded
    dest = pad_starts[e_sorted] + (jnp.arange(NK, dtype=jnp.int32) - starts[e_sorted])
    n_blocks = -(-NK // MOE_BLOCK) + N_EXPERTS
    n_rows = n_blocks * MOE_BLOCK
    row_tok = jnp.full((n_rows,), N, jnp.int32).at[dest].set(tok_flat[order])
    row_gate = jnp.zeros((n_rows,), x.dtype).at[dest].set(gate.reshape(-1)[order])
    blk_expert = jnp.minimum(jnp.searchsorted(pad_ends, jnp.arange(n_blocks) * MOE_BLOCK, side='right'),
                             N_EXPERTS - 1)
    x_pad = jnp.concatenate([xt, jnp.zeros((1, D), xt.dtype)], axis=0)
    xb = x_pad[row_tok].reshape(n_blocks, MOE_BLOCK, D)

    def expert_block(args):
        xe, e = args
        hdn = xe @ w1[e] + b1[e]
        glu = jnp.minimum(hdn[:, :D_EXPERT], SWIGLU_LIMIT)
        lin = jnp.clip(hdn[:, D_EXPERT:], -SWIGLU_LIMIT, SWIGLU_LIMIT)
        act = glu * jax.nn.sigmoid(SWIGLU_ALPHA * glu) * (lin + 1)
        return act @ w2[e] + b2[e]

    yb = lax.map(expert_block, (xb, blk_expert)).reshape(n_rows, D)
    y = jnp.zeros((N + 1, D), x.dtype).at[row_tok].add(yb * row_gate[:, None])
    return y[:N].reshape(B, S, D)


def setup_inputs(seed: int = 0) -> dict:
    key = jax.random.key(seed)
    ks = jax.random.split(key, 24)

    def nrm(k, shape, scale):
        return jax.random.normal(k, shape, jnp.float32) * scale

    offs = np.cumsum((0,) + SPLITS)
    ml_f_off = int(offs[5])
    fx_f_off = int(offs[12])
    b_in = nrm(ks[5], (DEPTH, D_IN), 0.01)
    b_in = b_in.at[:, ml_f_off:ml_f_off + ML_HEADS].add(jnp.linspace(3.0, 6.0, ML_HEADS))
    b_in = b_in.at[:, fx_f_off:fx_f_off + FOX_HEADS].add(jnp.linspace(1.0, 5.0, FOX_HEADS))
    return {
        'x': nrm(ks[0], (BATCH, SEQ, D_MODEL), 1.0),
        'c': nrm(ks[1], (BATCH, D_MODEL), 1.0),
        'w_ada': nrm(ks[2], (DEPTH, D_MODEL, 6 * D_MODEL), 0.5 * D_MODEL ** -0.5),
        'b_ada': nrm(ks[3], (DEPTH, 6 * D_MODEL), 0.01),
        'w_in': nrm(ks[4], (DEPTH, D_MODEL, D_IN), D_MODEL ** -0.5),
        'b_in': b_in,
        'conv_w': nrm(ks[6], (DEPTH, ML_CONV, 2 * ML_HEADS * ML_QK_DIM), ML_CONV ** -0.5),
        'conv_b': nrm(ks[7], (DEPTH, 2 * ML_HEADS * ML_QK_DIM), 0.01),
        'ml_norm_g': 1.0 + nrm(ks[8], (DEPTH, ML_HEADS * ML_V_DIM), 0.02),
        'kv_norm_g': 1.0 + nrm(ks[9], (DEPTH, DSA_LATENT), 0.02),
        'w_uk': nrm(ks[10], (DEPTH, DSA_HEADS, DSA_LATENT, DSA_HEAD_DIM), DSA_HEAD_DIM ** -0.5),
        'w_uv': nrm(ks[11], (DEPTH, DSA_HEADS, DSA_LATENT, DSA_HEAD_DIM), DSA_LATENT ** -0.5),
        'w_br_ml': nrm(ks[12], (DEPTH, ML_HEADS * ML_V_DIM, D_MODEL), BRANCH_WIDTH ** -0.5),
        'w_br_dsa': nrm(ks[13], (DEPTH, DSA_HEADS * DSA_HEAD_DIM, D_MODEL), BRANCH_WIDTH ** -0.5),
        'w_br_fox': nrm(ks[14], (DEPTH, FOX_HEADS * FOX_HEAD_DIM, D_MODEL), BRANCH_WIDTH ** -0.5),
        'w_out': nrm(ks[15], (DEPTH, D_MODEL, D_MODEL), D_MODEL ** -0.5),
        'w_router': nrm(ks[16], (DEPTH, D_MODEL, N_EXPERTS), D_MODEL ** -0.5),
        'b_router': nrm(ks[17], (DEPTH, N_EXPERTS), 0.01),
        'w1': nrm(ks[18], (DEPTH, N_EXPERTS, D_MODEL, 2 * D_EXPERT), D_MODEL ** -0.5),
        'b1': nrm(ks[19], (DEPTH, N_EXPERTS, 2 * D_EXPERT), 0.01),
        'w2': nrm(ks[20], (DEPTH, N_EXPERTS, D_EXPERT, D_MODEL), D_EXPERT ** -0.5),
        'b2': nrm(ks[21], (DEPTH, N_EXPERTS, D_MODEL), 0.01),
        'final_g': 1.0 + nrm(ks[22], (D_MODEL,), 0.02),
    }


def reference(x, c, w_ada, b_ada, w_in, b_in, conv_w, conv_b, ml_norm_g, kv_norm_g, w_uk, w_uv,
              w_br_ml, w_br_dsa, w_br_fox, w_out, w_router, b_router, w1, b1, w2, b2, final_g):
    cond = jax.nn.silu(c)
    for l in range(DEPTH):
        mod = cond @ w_ada[l] + b_ada[l]
        sh1, sc1, g1, sh2, sc2, g2 = jnp.split(mod[:, None, :], 6, axis=-1)
        h = rms(x) * (1 + sc1) + sh1
        x = x + g1 * hybrid_mixer(h, w_in[l], b_in[l], conv_w[l], conv_b[l], ml_norm_g[l], kv_norm_g[l],
                                  w_uk[l], w_uv[l], w_br_ml[l], w_br_dsa[l], w_br_fox[l], w_out[l])
        h = rms(x) * (1 + sc2) + sh2
        x = x + g2 * moe_ffn(h, w_router[l], b_router[l], w1[l], b1[l], w2[l], b2[l])
    return rms(x) * final_g
```

```python
import functools

import jax
import jax.numpy as jnp
from jax import lax
from jax.experimental import pallas as pl
from jax.experimental.pallas import tpu as pltpu

F32 = jnp.float32
BF16 = jnp.bfloat16
HIGHEST = lax.Precision.HIGHEST

EPS = 1e-6
CHUNK = 64

ML_HEADS, ML_QK, ML_V, ML_CONV = 4, 64, 128, 4
DSA_HEADS, DSA_DIM, DSA_LATENT = 4, 128, 128
IDX_HEADS, IDX_DIM, DSA_TOPK = 4, 64, 256
FOX_HEADS, FOX_DIM = 4, 128
N_BRANCH = 3
N_EXPERTS, TOP_K = 32, 4
SWIGLU_LIMIT, SWIGLU_ALPHA = 7.0, 1.702

LANE = 128
INT_MIN = -2 ** 31

C_MLQK = 0
C_MLV = 512
C_MLO = 1024
C_DQ = 1536
C_DIQ = 2048
C_CKV = 2304
C_MISC = 2432
C_FOX = 2560
C_GATE = 4096
NP = 7168
M_IK, M_MLI, M_MLF, M_IW, M_FXF = 0, 64, 68, 72, 76

VMEM_LIMIT = 56 * 1024 * 1024


def _dot(a, b, prec=None):
    return jnp.dot(a, b, preferred_element_type=F32, precision=prec)


def _dot_nt(a, b, prec=None):
    return lax.dot_general(a, b, (((1,), (1,)), ((), ())), preferred_element_type=F32, precision=prec)


def _dot_tn(a, b):
    return lax.dot_general(a, b, (((0,), (0,)), ((), ())), preferred_element_type=F32)


def _sigmoid(x):
    return 1.0 / (1.0 + jnp.exp(-x))


def _log_sigmoid(x):
    return jnp.minimum(x, 0.0) - jnp.log1p(jnp.exp(-jnp.abs(x)))


def _rms(x):
    return x * lax.rsqrt(jnp.mean(x * x, axis=-1, keepdims=True) + EPS)


def _params(sem, vmem=VMEM_LIMIT):
    return pltpu.CompilerParams(dimension_semantics=sem, vmem_limit_bytes=vmem)


def _ada_kernel(c_ref, w_ref, b_ref, o_ref):
    c = c_ref[...]
    o_ref[0] = _dot(c * _sigmoid(c), w_ref[0], HIGHEST) + b_ref[0]


def _ada_mod(c, w_ada, b_ada):
    depth, d, n = w_ada.shape
    bsz = c.shape[0]
    tn = 1536
    return pl.pallas_call(
        _ada_kernel,
        out_shape=jax.ShapeDtypeStruct((depth, bsz, n), F32),
        grid=(depth, n // tn),
        in_specs=[pl.BlockSpec((bsz, d), lambda l, j: (0, 0)),
                  pl.BlockSpec((1, d, tn), lambda l, j: (l, 0, j)),
                  pl.BlockSpec((1, 1, tn), lambda l, j: (l, 0, j))],
        out_specs=pl.BlockSpec((1, bsz, tn), lambda l, j: (l, 0, j)),
        compiler_params=_params(("parallel", "parallel")),
        name="ada_mod",
    )(c, w_ada, b_ada.reshape(depth, 1, n))


def _inproj_kernel(x_ref, mod_ref, w_ref, b_ref, o_ref, h_scr):
    @pl.when(pl.program_id(2) == 0)
    def _():
        h = _rms(x_ref[0]) * (1.0 + mod_ref[0, 1:2, :]) + mod_ref[0, 0:1, :]
        h_scr[...] = h.astype(BF16)

    o_ref[0] = _dot(h_scr[...], w_ref[...]) + b_ref[...]


def _in_proj(x, mod, w, b):
    bsz, s, d = x.shape
    n = w.shape[1]
    tm = min(1024, s)
    tn = 1024
    return pl.pallas_call(
        _inproj_kernel,
        out_shape=jax.ShapeDtypeStruct((bsz, s, n), F32),
        grid=(bsz, s // tm, n // tn),
        in_specs=[pl.BlockSpec((1, tm, d), lambda bi, i, j: (bi, i, 0)),
                  pl.BlockSpec((1, 6, d), lambda bi, i, j: (bi, 0, 0)),
                  pl.BlockSpec((d, tn), lambda bi, i, j: (0, j)),
                  pl.BlockSpec((1, tn), lambda bi, i, j: (0, j))],
        out_specs=pl.BlockSpec((1, tm, tn), lambda bi, i, j: (bi, i, j)),
        scratch_shapes=[pltpu.VMEM((tm, d), BF16)],
        compiler_params=_params(("parallel", "parallel", "arbitrary")),
        name="in_proj",
    )(x, mod, w, b)


def _mlstm_kernel(qk_ref, v_ref, o_ref, misc_ref, cw_ref, cb_ref, g_ref,
                  y_ref, fcol_ref, frow_ref,
                  xext, ct_scr, n_scr, m_scr, carry_scr, *, L):
    c = pl.program_id(1)
    nqk = ML_HEADS * ML_QK

    @pl.when(c == 0)
    def _():
        xext[0:8, :] = jnp.zeros((8, 2 * nqk), F32)
        ct_scr[...] = jnp.zeros(ct_scr.shape, F32)
        n_scr[...] = jnp.zeros(n_scr.shape, F32)
        m_scr[...] = jnp.full(m_scr.shape, -jnp.inf, F32)
        carry_scr[...] = jnp.zeros(carry_scr.shape, F32)

    @pl.when(c > 0)
    def _():
        xext[0:8, :] = xext[L:L + 8, :]

    xext[8:8 + L, :] = qk_ref[0]
    cw = cw_ref[...]
    conv = (cb_ref[...] + cw[3:4, :] * xext[8:8 + L, :] + cw[2:3, :] * xext[7:7 + L, :]
            + cw[1:2, :] * xext[6:6 + L, :] + cw[0:1, :] * xext[5:5 + L, :])
    qk = conv * _sigmoid(conv)

    misc = misc_ref[0]
    ls = _log_sigmoid(misc)
    row = lax.broadcasted_iota(jnp.int32, (L, L), 0)
    col = lax.broadcasted_iota(jnp.int32, (L, L), 1)
    causal = row >= col
    tri = jnp.where(causal, 1.0, 0.0).astype(F32)
    cs = _dot(tri, ls, HIGHEST)
    cs_t = cs.T
    misc_t = misc.T
    carry = carry_scr[0:1, :]
    fcol_ref[0] = cs + carry
    for h in range(FOX_HEADS):
        frow_ref[0, h:h + 1, :] = cs_t[M_FXF + h:M_FXF + h + 1, :] + carry[:, M_FXF + h:M_FXF + h + 1]
    frow_ref[0, 4:8, :] = jnp.zeros((4, L), F32)
    carry_scr[0:1, :] = carry + cs[L - 1:L, :]

    for h in range(ML_HEADS):
        qh = qk[:, h * ML_QK:(h + 1) * ML_QK] * (ML_QK ** -0.5)
        kh = qk[:, nqk + h * ML_QK:nqk + (h + 1) * ML_QK]
        vb = v_ref[0, :, h * ML_V:(h + 1) * ML_V].astype(BF16)
        i_col = misc[:, M_MLI + h:M_MLI + h + 1]
        i_row = misc_t[M_MLI + h:M_MLI + h + 1, :]
        b_col = cs[:, M_MLF + h:M_MLF + h + 1]
        b_row = cs_t[M_MLF + h:M_MLF + h + 1, :]
        b_last = b_col[L - 1:L, :]
        m_prev = m_scr[h:h + 1, 0:1]

        d_log = jnp.where(causal, b_col - b_row + i_row, -jnp.inf)
        inter_log = b_col + m_prev
        m_out = jnp.maximum(inter_log, jnp.max(d_log, axis=-1, keepdims=True))
        qb = qh.astype(BF16)
        kb = kh.astype(BF16)
        s = _dot_nt(qb, kb) * jnp.exp(d_log - m_out)
        a_inter = jnp.exp(inter_log - m_out)
        ct = ct_scr[h]
        n_row = n_scr[h]
        num = _dot(s.astype(BF16), vb) + a_inter * _dot(qb, ct.astype(BF16))
        den = jnp.sum(s, axis=-1, keepdims=True) + a_inter * jnp.sum(qh * n_row, axis=-1, keepdims=True)
        hid = num / jnp.maximum(jnp.abs(den), jnp.exp(-m_out))

        w_state = b_last - b_col + i_col
        m_loc = jnp.max(w_state, axis=0, keepdims=True)
        ke = kh * jnp.exp(w_state - m_loc)
        c_loc = _dot_tn(ke.astype(BF16), vb)
        n_loc = jnp.sum(ke, axis=0, keepdims=True)
        m_new = jnp.maximum(b_last + m_prev, m_loc)
        decay = jnp.exp(b_last + m_prev - m_new)
        scale = jnp.exp(m_loc - m_new)
        ct_scr[h] = decay * ct + scale * c_loc
        n_scr[h] = decay * n_row + scale * n_loc
        m_scr[h:h + 1, :] = jnp.broadcast_to(m_new, (1, LANE))

        y = (_rms(hid) * g_ref[:, h * ML_V:(h + 1) * ML_V]
             * _sigmoid(o_ref[0, :, h * ML_V:(h + 1) * ML_V]))
        y_ref[0, :, h * ML_V:(h + 1) * ML_V] = y.astype(BF16)


def _mlstm(proj, conv_w, conv_b, norm_g):
    bsz, s, _ = proj.shape
    L = min(256, s)
    w = 2 * ML_HEADS * ML_QK
    wv = ML_HEADS * ML_V
    return pl.pallas_call(
        functools.partial(_mlstm_kernel, L=L),
        out_shape=(jax.ShapeDtypeStruct((bsz, s, wv), BF16),
                   jax.ShapeDtypeStruct((bsz, s, LANE), F32),
                   jax.ShapeDtypeStruct((bsz, 8, s), F32)),
        grid=(bsz, s // L),
        in_specs=[pl.BlockSpec((1, L, w), lambda b, c: (b, c, C_MLQK // w)),
                  pl.BlockSpec((1, L, wv), lambda b, c: (b, c, C_MLV // wv)),
                  pl.BlockSpec((1, L, wv), lambda b, c: (b, c, C_MLO // wv)),
                  pl.BlockSpec((1, L, LANE), lambda b, c: (b, c, C_MISC // LANE)),
                  pl.BlockSpec((ML_CONV, w), lambda b, c: (0, 0)),
                  pl.BlockSpec((1, w), lambda b, c: (0, 0)),
                  pl.BlockSpec((1, wv), lambda b, c: (0, 0))],
        out_specs=(pl.BlockSpec((1, L, wv), lambda b, c: (b, c, 0)),
                   pl.BlockSpec((1, L, LANE), lambda b, c: (b, c, 0)),
                   pl.BlockSpec((1, 8, L), lambda b, c: (b, 0, c))),
        scratch_shapes=[pltpu.VMEM((L + 8, w), F32),
                        pltpu.VMEM((ML_HEADS, ML_QK, ML_V), F32),
                        pltpu.VMEM((ML_HEADS, 1, ML_QK), F32),
                        pltpu.VMEM((8, LANE), F32),
                        pltpu.VMEM((8, LANE), F32)],
        compiler_params=_params(("parallel", "arbitrary")),
        name="mlstm",
    )(proj, proj, proj, proj, conv_w, conv_b.reshape(1, w), norm_g.reshape(1, wv))


def _fox_kernel(q_ref, k_ref, v_ref, fcol_ref, frow_ref, y_ref, *, T):
    qi = pl.program_id(1)
    d = FOX_DIM
    qpos = qi * T + lax.broadcasted_iota(jnp.int32, (T, 1), 0)
    for h in range(FOX_HEADS):
        qh = (q_ref[0, :, h * d:(h + 1) * d] * (d ** -0.5)).astype(BF16)
        fq = fcol_ref[0, :, M_FXF + h:M_FXF + h + 1]

        def body(j, carry, h=h, qh=qh, fq=fq):
            m, l, acc = carry
            off = pl.multiple_of(j * T, T)
            kj = k_ref[0, pl.ds(off, T), h * d:(h + 1) * d].astype(BF16)
            vj = v_ref[0, pl.ds(off, T), h * d:(h + 1) * d].astype(BF16)
            fk = frow_ref[0, h:h + 1, pl.ds(off, T)]
            s = _dot_nt(qh, kj) + (fq - fk)
            kpos = off + lax.broadcasted_iota(jnp.int32, (1, T), 1)
            s = jnp.where(kpos <= qpos, s, -jnp.inf)
            m_new = jnp.maximum(m, jnp.max(s, axis=-1, keepdims=True))
            alpha = jnp.exp(m - m_new)
            p = jnp.exp(s - m_new)
            l = alpha * l + jnp.sum(p, axis=-1, keepdims=True)
            acc = alpha * acc + _dot(p.astype(BF16), vj)
            return m_new, l, acc

        init = (jnp.full((T, 1), -jnp.inf, F32), jnp.zeros((T, 1), F32), jnp.zeros((T, d), F32))
        _, l, acc = lax.fori_loop(0, qi + 1, body, init)
        y_ref[0, :, h * d:(h + 1) * d] = (acc / l).astype(BF16)


def _fox(proj, fcol, frow):
    bsz, s, _ = proj.shape
    T = min(256, s)
    w = FOX_HEADS * FOX_DIM
    return pl.pallas_call(
        functools.partial(_fox_kernel, T=T),
        out_shape=jax.ShapeDtypeStruct((bsz, s, w), BF16),
        grid=(bsz, s // T),
        in_specs=[pl.BlockSpec((1, T, w), lambda b, i: (b, i, C_FOX // w)),
                  pl.BlockSpec((1, s, w), lambda b, i: (b, 0, C_FOX // w + 1)),
                  pl.BlockSpec((1, s, w), lambda b, i: (b, 0, C_FOX // w + 2)),
                  pl.BlockSpec((1, T, LANE), lambda b, i: (b, i, 0)),
                  pl.BlockSpec((1, 8, s), lambda b, i: (b, 0, 0))],
        out_specs=pl.BlockSpec((1, T, w), lambda b, i: (b, i, 0)),
        compiler_params=_params(("parallel", "arbitrary")),
        name="fox_attention",
    )(proj, proj, proj, fcol, frow)


def _dsa_body(ext, qi, q_ref, qidx_ref, misc_all_ref, misc_q_ref, wuk_ref, wuv_ref, y_ref, ckvn_scr,
              *, T, n_sel, rank_tile):
    dh = DSA_DIM
    kidx = misc_all_ref[0, 0:ext, M_IK:M_IK + IDX_DIM]
    wq = misc_q_ref[0, :, M_IW:M_IW + IDX_HEADS] * (IDX_HEADS ** -0.5)
    score = jnp.zeros((T, ext), F32)
    for h in range(IDX_HEADS):
        lg = _dot_nt(qidx_ref[0, :, h * IDX_DIM:(h + 1) * IDX_DIM], kidx, HIGHEST) * (IDX_DIM ** -0.5)
        score = score + wq[:, h:h + 1] * jnp.maximum(lg, 0.0)
    score = jnp.where(score == 0.0, 0.0, score)
    kpos = lax.broadcasted_iota(jnp.int32, (1, ext), 1)
    qpos = qi * T + lax.broadcasted_iota(jnp.int32, (T, 1), 0)
    vis = (kpos // CHUNK) <= (qpos // CHUNK)
    bits = pltpu.bitcast(score, jnp.int32)
    key = jnp.where(bits < 0, bits ^ jnp.int32(0x7FFFFFFF), bits)
    key = jnp.where(vis, key, jnp.int32(INT_MIN))

    def count_ge(t):
        return jnp.sum(jnp.where(key >= t, 1.0, 0.0), axis=-1, keepdims=True)

    t0 = jnp.where(count_ge(jnp.zeros((T, 1), jnp.int32)) >= n_sel, jnp.int32(0), jnp.int32(INT_MIN))

    def bis(i, t):
        cand = t + jnp.left_shift(jnp.int32(1), jnp.int32(30) - i)
        return jnp.where(count_ge(cand) >= n_sel, cand, t)

    t = lax.fori_loop(0, 31, bis, t0)
    t = jnp.maximum(t, jnp.int32(INT_MIN + 1))
    n_gt = jnp.sum(jnp.where(key > t, 1.0, 0.0), axis=-1, keepdims=True)
    need = n_sel - n_gt
    eq = jnp.where(key == t, 1.0, 0.0).astype(BF16)
    r_i = lax.broadcasted_iota(jnp.int32, (rank_tile, rank_tile), 0)
    c_i = lax.broadcasted_iota(jnp.int32, (rank_tile, rank_tile), 1)
    upper = jnp.where(r_i < c_i, 1.0, 0.0).astype(BF16)
    carry = jnp.zeros((T, 1), F32)
    thr_parts = []
    for j in range(ext // rank_tile):
        tile = eq[:, j * rank_tile:(j + 1) * rank_tile]
        rank = _dot(tile, upper) + carry
        carry = carry + jnp.sum(tile.astype(F32), axis=-1, keepdims=True)
        thr_parts.append(jnp.where(rank < need, t, t + 1))
    thr = jnp.concatenate(thr_parts, axis=1) if len(thr_parts) > 1 else thr_parts[0]
    sel = key >= thr

    ckvn = ckvn_scr[0:ext, :]
    for h in range(DSA_HEADS):
        qa = _dot(q_ref[0, :, h * dh:(h + 1) * dh].astype(BF16), wuk_ref[h]) * (dh ** -0.5)
        lg = _dot_nt(qa.astype(BF16), ckvn)
        lg = jnp.where(sel, lg, -jnp.inf)
        p = jnp.exp(lg - jnp.max(lg, axis=-1, keepdims=True))
        l = jnp.sum(p, axis=-1, keepdims=True)
        lat = _dot(p.astype(BF16), ckvn) / l
        y_ref[0, :, h * dh:(h + 1) * dh] = _dot(lat.astype(BF16), wuv_ref[h]).astype(BF16)


def _dsa_kernel(q_ref, ckv_ref, qidx_ref, misc_all_ref, misc_q_ref, g_ref, wuk_ref, wuv_ref, y_ref,
                ckvn_scr, *, T, S, n_sel, n_cls, rank_tile):
    qi = pl.program_id(1)

    @pl.when(qi == 0)
    def _():
        ckvn_scr[...] = (_rms(ckv_ref[0]) * g_ref[...]).astype(BF16)

    per = (S // T) // n_cls
    for c in range(n_cls):
        ext = (c + 1) * per * T

        @pl.when(qi // per == c)
        def _(ext=ext):
            _dsa_body(ext, qi, q_ref, qidx_ref, misc_all_ref, misc_q_ref, wuk_ref, wuv_ref, y_ref,
                      ckvn_scr, T=T, n_sel=n_sel, rank_tile=rank_tile)


def _dsa(proj, kv_g, wuk_t, wuv):
    bsz, s, _ = proj.shape
    T = 128
    n_sel = min(DSA_TOPK, s // 4)
    n_cls = max(1, min(4, s // 512))
    rank_tile = 256
    w = DSA_HEADS * DSA_DIM
    wi = IDX_HEADS * IDX_DIM
    return pl.pallas_call(
        functools.partial(_dsa_kernel, T=T, S=s, n_sel=float(n_sel), n_cls=n_cls, rank_tile=rank_tile),
        out_shape=jax.ShapeDtypeStruct((bsz, s, w), BF16),
        grid=(bsz, s // T),
        in_specs=[pl.BlockSpec((1, T, w), lambda b, i: (b, i, C_DQ // w)),
                  pl.BlockSpec((1, s, DSA_LATENT), lambda b, i: (b, 0, C_CKV // DSA_LATENT)),
                  pl.BlockSpec((1, T, wi), lambda b, i: (b, i, C_DIQ // wi)),
                  pl.BlockSpec((1, s, LANE), lambda b, i: (b, 0, C_MISC // LANE)),
                  pl.BlockSpec((1, T, LANE), lambda b, i: (b, i, C_MISC // LANE)),
                  pl.BlockSpec((1, DSA_LATENT), lambda b, i: (0, 0)),
                  pl.BlockSpec((DSA_HEADS, DSA_DIM, DSA_LATENT), lambda b, i: (0, 0, 0)),
                  pl.BlockSpec((DSA_HEADS, DSA_LATENT, DSA_DIM), lambda b, i: (0, 0, 0))],
        out_specs=pl.BlockSpec((1, T, w), lambda b, i: (b, i, 0)),
        scratch_shapes=[pltpu.VMEM((s, DSA_LATENT), BF16)],
        compiler_params=_params(("parallel", "arbitrary")),
        name="dsa_attention",
    )(proj, proj, proj, proj, proj, kv_g.reshape(1, DSA_LATENT), wuk_t, wuv)


def _merge_kernel(yml_ref, ydsa_ref, yfox_ref, g0_ref, g1_ref, g2_ref, x_ref, mod_ref,
                  wml_ref, wdsa_ref, wfox_ref, wout_ref, o_ref):
    merged = (_sigmoid(g0_ref[0]) * _dot(yml_ref[0], wml_ref[...])
              + _sigmoid(g1_ref[0]) * _dot(ydsa_ref[0], wdsa_ref[...])
              + _sigmoid(g2_ref[0]) * _dot(yfox_ref[0], wfox_ref[...]))
    out = _dot(merged.astype(BF16), wout_ref[...])
    o_ref[0] = x_ref[0] + mod_ref[0, 2:3, :] * out


def _merge(x, proj, y_ml, y_dsa, y_fox, mod, w_ml, w_dsa, w_fox, w_out):
    bsz, s, d = x.shape
    tm = min(512, s)
    wb = y_ml.shape[-1]
    gb = C_GATE // d
    yspec = pl.BlockSpec((1, tm, wb), lambda b, i: (b, i, 0))
    wspec = pl.BlockSpec((wb, d), lambda b, i: (0, 0))
    return pl.pallas_call(
        _merge_kernel,
        out_shape=jax.ShapeDtypeStruct((bsz, s, d), F32),
        grid=(bsz, s // tm),
        in_specs=[yspec, yspec, yspec,
                  pl.BlockSpec((1, tm, d), lambda b, i: (b, i, gb)),
                  pl.BlockSpec((1, tm, d), lambda b, i: (b, i, gb + 1)),
                  pl.BlockSpec((1, tm, d), lambda b, i: (b, i, gb + 2)),
                  pl.BlockSpec((1, tm, d), lambda b, i: (b, i, 0)),
                  pl.BlockSpec((1, 6, d), lambda b, i: (b, 0, 0)),
                  wspec, wspec, wspec,
                  pl.BlockSpec((d, d), lambda b, i: (0, 0))],
        out_specs=pl.BlockSpec((1, tm, d), lambda b, i: (b, i, 0)),
        compiler_params=_params(("parallel", "parallel")),
        name="merge_out",
    )(y_ml, y_dsa, y_fox, proj, proj, proj, x, mod, w_ml, w_dsa, w_fox, w_out)


def _router_kernel(x_ref, mod_ref, wr_ref, br_ref, h_ref, route_ref, cnt_ref, carry_scr, *, T):
    first = (pl.program_id(0) == 0) & (pl.program_id(1) == 0)

    @pl.when(first)
    def _():
        carry_scr[...] = jnp.zeros(carry_scr.shape, F32)

    h = _rms(x_ref[0]) * (1.0 + mod_ref[0, 4:5, :]) + mod_ref[0, 3:4, :]
    h_ref[0] = h
    lg = _dot(h, wr_ref[...], HIGHEST) + br_ref[...]
    lane = lax.broadcasted_iota(jnp.int32, (T, LANE), 1).astype(F32)
    vals, hots = [], []
    for _ in range(TOP_K):
        mx = jnp.max(lg, axis=-1, keepdims=True)
        idx = jnp.min(jnp.where(lg == mx, lane, float(LANE)), axis=-1, keepdims=True)
        hot = lane == idx
        vals.append(mx)
        hots.append(hot)
        lg = jnp.where(hot, -jnp.inf, lg)
    exps = [jnp.exp(v - vals[0]) for v in vals]
    tot = exps[0] + exps[1] + exps[2] + exps[3]
    multi = jnp.zeros((T, LANE), F32)
    for hot in hots:
        multi = multi + jnp.where(hot, 1.0, 0.0)
    r_i = lax.broadcasted_iota(jnp.int32, (T, T), 0)
    c_i = lax.broadcasted_iota(jnp.int32, (T, T), 1)
    lower = jnp.where(c_i < r_i, 1.0, 0.0).astype(BF16)
    carry = carry_scr[0:1, :]
    before = _dot(lower, multi.astype(BF16)) + carry
    carry = carry + jnp.sum(multi, axis=0, keepdims=True)
    carry_scr[0:1, :] = carry
    cnt_ref[...] = jnp.broadcast_to(carry, (8, LANE))
    slab = jnp.zeros((T, LANE), F32)
    for k in range(TOP_K):
        e_k = jnp.sum(jnp.where(hots[k], lane, 0.0), axis=-1, keepdims=True)
        r_k = jnp.sum(jnp.where(hots[k], before, 0.0), axis=-1, keepdims=True)
        slab = slab + jnp.where(lane == float(k), e_k, 0.0)
        slab = slab + jnp.where(lane == float(TOP_K + k), r_k, 0.0)
        slab = slab + jnp.where(lane == float(2 * TOP_K + k), exps[k] / tot, 0.0)
    route_ref[0] = slab


def _router(x, mod, w_router, b_router):
    bsz, s, d = x.shape
    T = min(512, s)
    ne = w_router.shape[1]
    wr = jnp.zeros((d, LANE), F32).at[:, :ne].set(w_router)
    br = jnp.full((1, LANE), -1e30, F32).at[0, :ne].set(b_router)
    return pl.pallas_call(
        functools.partial(_router_kernel, T=T),
        out_shape=(jax.ShapeDtypeStruct((bsz, s, d), F32),
                   jax.ShapeDtypeStruct((bsz, s, LANE), F32),
                   jax.ShapeDtypeStruct((8, LANE), F32)),
        grid=(bsz, s // T),
        in_specs=[pl.BlockSpec((1, T, d), lambda b, i: (b, i, 0)),
                  pl.BlockSpec((1, 6, d), lambda b, i: (b, 0, 0)),
                  pl.BlockSpec((d, LANE), lambda b, i: (0, 0)),
                  pl.BlockSpec((1, LANE), lambda b, i: (0, 0))],
        out_specs=(pl.BlockSpec((1, T, d), lambda b, i: (b, i, 0)),
                   pl.BlockSpec((1, T, LANE), lambda b, i: (b, i, 0)),
                   pl.BlockSpec((8, LANE), lambda b, i: (0, 0))),
        scratch_shapes=[pltpu.VMEM((8, LANE), F32)],
        compiler_params=_params(("arbitrary", "arbitrary")),
        name="moe_router",
    )(x, mod, wr, br)


N_SEM = 64


def _dispatch_kernel(pad_end_ref, dest_ref, h_hbm, xs_hbm, zbuf, zsem, sems, *, TD, BM):
    i = pl.program_id(0)

    @pl.when(i == 0)
    def _():
        zbuf[...] = jnp.zeros(zbuf.shape, F32)
        n_rows = xs_hbm.shape[0]
        for e in range(N_EXPERTS):
            start = pl.multiple_of(jnp.maximum(pad_end_ref[e] - BM, 0), BM)
            cp = pltpu.make_async_copy(zbuf, xs_hbm.at[pl.ds(start, BM)], zsem)
            cp.start()
            cp.wait()
            start = pl.multiple_of(jnp.minimum(pad_end_ref[N_EXPERTS - 1] + e * BM, n_rows - BM), BM)
            cp = pltpu.make_async_copy(zbuf, xs_hbm.at[pl.ds(start, BM)], zsem)
            cp.start()
            cp.wait()

    n = TD * TOP_K

    def copy(j):
        tok = i * TD + j // TOP_K
        return pltpu.make_async_copy(h_hbm.at[pl.ds(tok, 1)], xs_hbm.at[pl.ds(dest_ref[j], 1)],
                                     sems.at[j % N_SEM])

    def issue(j, _):
        @pl.when(j >= N_SEM)
        def _():
            copy(j - N_SEM).wait()

        copy(j).start()
        return 0

    lax.fori_loop(0, n, issue, 0)

    def drain(j, _):
        copy(j).wait()
        return 0

    lax.fori_loop(n - N_SEM, n, drain, 0)


def _dispatch(pad_end, dest, h2, n_rows, bm):
    n, d = h2.shape
    td = min(512, n)
    return pl.pallas_call(
        functools.partial(_dispatch_kernel, TD=td, BM=bm),
        out_shape=jax.ShapeDtypeStruct((n_rows, d), F32),
        grid_spec=pltpu.PrefetchScalarGridSpec(
            num_scalar_prefetch=1,
            grid=(n // td,),
            in_specs=[pl.BlockSpec((td * TOP_K,), lambda i, pe: (i,), memory_space=pltpu.SMEM),
                      pl.BlockSpec(memory_space=pl.ANY)],
            out_specs=pl.BlockSpec(memory_space=pl.ANY),
            scratch_shapes=[pltpu.VMEM((bm, d), F32),
                            pltpu.SemaphoreType.DMA(()),
                            pltpu.SemaphoreType.DMA((N_SEM,))]),
        compiler_params=_params(("arbitrary",)),
        name="moe_dispatch",
    )(pad_end, dest, h2)


def _expert_kernel(be_ref, nu_ref, xs_ref, w1_ref, b1_ref, w2_ref, b2_ref, o_ref, w1b, w2b):
    i = pl.program_id(0)
    de = w2_ref.shape[1]
    prev = be_ref[jnp.maximum(i - 1, 0)]
    fresh = (i == 0) | (be_ref[i] != prev)

    @pl.when(fresh & (i < nu_ref[0]))
    def _():
        w1b[...] = w1_ref[0].astype(BF16)
        w2b[...] = w2_ref[0].astype(BF16)

    @pl.when(i < nu_ref[0])
    def _():
        hdn = _dot(xs_ref[...].astype(BF16), w1b[...]) + b1_ref[0]
        glu = jnp.minimum(hdn[:, :de], SWIGLU_LIMIT)
        lin = jnp.clip(hdn[:, de:], -SWIGLU_LIMIT, SWIGLU_LIMIT)
        act = glu * _sigmoid(SWIGLU_ALPHA * glu) * (lin + 1.0)
        o_ref[...] = _dot(act.astype(BF16), w2b[...]) + b2_ref[0]

    @pl.when(i >= nu_ref[0])
    def _():
        o_ref[...] = jnp.zeros(o_ref.shape, F32)


def _experts(blk_expert, n_used, xs, w1, b1, w2, b2, bm):
    n_rows, d = xs.shape
    ne, _, dh2 = w1.shape
    de = w2.shape[1]
    n_blocks = n_rows // bm

    def row_map(i, be, nu):
        return (jnp.minimum(i, nu[0] - 1), 0)

    return pl.pallas_call(
        _expert_kernel,
        out_shape=jax.ShapeDtypeStruct((n_rows, d), F32),
        grid_spec=pltpu.PrefetchScalarGridSpec(
            num_scalar_prefetch=2,
            grid=(n_blocks,),
            in_specs=[pl.BlockSpec((bm, d), row_map),
                      pl.BlockSpec((1, d, dh2), lambda i, be, nu: (be[i], 0, 0)),
                      pl.BlockSpec((1, 1, dh2), lambda i, be, nu: (be[i], 0, 0)),
                      pl.BlockSpec((1, de, d), lambda i, be, nu: (be[i], 0, 0)),
                      pl.BlockSpec((1, 1, d), lambda i, be, nu: (be[i], 0, 0))],
            out_specs=pl.BlockSpec((bm, d), lambda i, be, nu: (i, 0)),
            scratch_shapes=[pltpu.VMEM((d, dh2), BF16), pltpu.VMEM((de, d), BF16)]),
        compiler_params=_params(("arbitrary",)),
        name="moe_experts",
    )(blk_expert, n_used, xs, w1, b1.reshape(ne, 1, dh2), w2, b2.reshape(ne, 1, d))


def _combine_kernel(dest_ref, yb_hbm, x_ref, route_ref, mod_ref, fg_ref, o_ref, buf, sems, *, TC, final):
    n = TC * TOP_K

    def copy(j):
        return pltpu.make_async_copy(yb_hbm.at[pl.ds(dest_ref[j], 1)],
                                     buf.at[j % TOP_K, pl.ds(j // TOP_K, 1)], sems.at[j % N_SEM])

    def issue(j, _):
        @pl.when(j >= N_SEM)
        def _():
            copy(j - N_SEM).wait()

        copy(j).start()
        return 0

    lax.fori_loop(0, n, issue, 0)

    def drain(j, _):
        copy(j).wait()
        return 0

    lax.fori_loop(n - N_SEM, n, drain, 0)

    route = route_ref[0]
    y = jnp.zeros(x_ref.shape[1:], F32)
    for k in range(TOP_K):
        y = y + route[:, 2 * TOP_K + k:2 * TOP_K + k + 1] * buf[k]
    out = x_ref[0] + mod_ref[0, 5:6, :] * y
    if final:
        out = _rms(out) * fg_ref[...]
    o_ref[0] = out


def _combine(dest, yb, x, route, mod, final_g, final):
    bsz, s, d = x.shape
    tc = min(256, s)
    nt = s // tc
    return pl.pallas_call(
        functools.partial(_combine_kernel, TC=tc, final=final),
        out_shape=jax.ShapeDtypeStruct((bsz, s, d), F32),
        grid=(bsz, nt),
        in_specs=[pl.BlockSpec((tc * TOP_K,), lambda b, i: (b * nt + i,), memory_space=pltpu.SMEM),
                  pl.BlockSpec(memory_space=pl.ANY),
                  pl.BlockSpec((1, tc, d), lambda b, i: (b, i, 0)),
                  pl.BlockSpec((1, tc, LANE), lambda b, i: (b, i, 0)),
                  pl.BlockSpec((1, 6, d), lambda b, i: (b, 0, 0)),
                  pl.BlockSpec((1, d), lambda b, i: (0, 0))],
        out_specs=pl.BlockSpec((1, tc, d), lambda b, i: (b, i, 0)),
        scratch_shapes=[pltpu.VMEM((TOP_K, tc, d), F32), pltpu.SemaphoreType.DMA((N_SEM,))],
        compiler_params=_params(("arbitrary", "arbitrary")),
        name="moe_combine",
    )(dest, yb, x, route, mod, final_g.reshape(1, d))


def _rearranged_in_proj(w_in, b_in):
    sizes = (ML_HEADS * ML_QK, ML_HEADS * ML_QK, ML_HEADS * ML_V, ML_HEADS * ML_V, ML_HEADS, ML_HEADS,
             DSA_HEADS * DSA_DIM, DSA_LATENT, IDX_HEADS * IDX_DIM, IDX_DIM, IDX_HEADS,
             3 * FOX_HEADS * FOX_DIM, FOX_HEADS, N_BRANCH * w_in.shape[0])
    offs = [0]
    for sz in sizes:
        offs.append(offs[-1] + sz)
    (o_mq, o_mk, o_mv, o_mo, o_mi, o_mf, o_dq, o_ckv, o_iq, o_ik, o_iw, o_fx, o_ff, o_g, o_end) = offs
    pad = LANE - (IDX_DIM + 2 * ML_HEADS + IDX_HEADS + FOX_HEADS)

    def cols(a):
        parts = [a[..., o_mq:o_mi],
                 a[..., o_dq:o_ckv],
                 a[..., o_iq:o_ik],
                 a[..., o_ckv:o_iq],
                 a[..., o_ik:o_iw],
                 a[..., o_mi:o_dq],
                 a[..., o_iw:o_fx],
                 a[..., o_ff:o_g],
                 jnp.zeros(a.shape[:-1] + (pad,), a.dtype),
                 a[..., o_fx:o_ff],
                 a[..., o_g:o_end]]
        return jnp.concatenate(parts, axis=-1)

    return cols(w_in).astype(BF16), cols(b_in.reshape(1, -1))


def _moe_offsets(route, counts, bm, n_blocks):
    n = route.shape[0]
    e = route[:, 0:TOP_K].astype(jnp.int32)
    rank = route[:, TOP_K:2 * TOP_K].astype(jnp.int32)
    cnt = counts[0, :N_EXPERTS].astype(jnp.int32)
    padded = (cnt + bm - 1) // bm * bm
    pad_end = jnp.cumsum(padded)
    pad_start = pad_end - padded
    dest = (pad_start[e] + rank).reshape(n * TOP_K)
    blk_expert = jnp.minimum(
        jnp.searchsorted(pad_end, jnp.arange(n_blocks, dtype=jnp.int32) * bm, side='right'),
        N_EXPERTS - 1).astype(jnp.int32)
    n_used = (pad_end[-1:] // bm).astype(jnp.int32)
    return dest, pad_end.astype(jnp.int32), blk_expert, n_used


def kernel(x, c, w_ada, b_ada, w_in, b_in, conv_w, conv_b, ml_norm_g, kv_norm_g, w_uk, w_uv,
           w_br_ml, w_br_dsa, w_br_fox, w_out, w_router, b_router, w1, b1, w2, b2, final_g):
    bsz, s, d = x.shape
    depth = w_in.shape[0]
    n = bsz * s
    bm = 256
    n_blocks = n * TOP_K // bm + N_EXPERTS
    mods = _ada_mod(c, w_ada, b_ada).reshape(depth, bsz, 6, d)
    for l in range(depth):
        mod = mods[l]
        w_r, b_r = _rearranged_in_proj(w_in[l], b_in[l])
        proj = _in_proj(x, mod, w_r, b_r)
        y_ml, fcol, frow = _mlstm(proj, conv_w[l], conv_b[l], ml_norm_g[l])
        y_fox = _fox(proj, fcol, frow)
        y_dsa = _dsa(proj, kv_norm_g[l], jnp.swapaxes(w_uk[l], 1, 2).astype(BF16), w_uv[l].astype(BF16))
        x = _merge(x, proj, y_ml, y_dsa, y_fox, mod, w_br_ml[l].astype(BF16), w_br_dsa[l].astype(BF16),
                   w_br_fox[l].astype(BF16), w_out[l].astype(BF16))
        h2, route, counts = _router(x, mod, w_router[l], b_router[l])
        dest, pad_end, blk_expert, n_used = _moe_offsets(route.reshape(n, LANE), counts, bm, n_blocks)
        xs = _dispatch(pad_end, dest, h2.reshape(n, d), n_blocks * bm, bm)
        yb = _experts(blk_expert, n_used, xs, w1[l], b1[l], w2[l], b2[l], bm)
        x = _combine(dest, yb, x, route, mod, final_g, final=(l == depth - 1))
    return x
```

```python
import functools

import jax
import jax.numpy as jnp
from jax import lax
from jax.experimental import pallas as pl
from jax.experimental.pallas import tpu as pltpu

F32 = jnp.float32
BF16 = jnp.bfloat16
HIGHEST = lax.Precision.HIGHEST

EPS = 1e-6
CHUNK = 64

ML_HEADS, ML_QK, ML_V, ML_CONV = 4, 64, 128, 4
DSA_HEADS, DSA_DIM, DSA_LATENT = 4, 128, 128
IDX_HEADS, IDX_DIM, DSA_TOPK = 4, 64, 256
FOX_HEADS, FOX_DIM = 4, 128
N_BRANCH = 3
N_EXPERTS, TOP_K = 32, 4
SWIGLU_LIMIT, SWIGLU_ALPHA = 7.0, 1.702

LANE = 128
INT_MIN = -2 ** 31

C_MLQK = 0
C_MLV = 512
C_MLO = 1024
C_DQ = 1536
C_DIQ = 2048
C_CKV = 2304
C_MISC = 2432
C_FOX = 2560
C_GATE = 4096
NP = 7168
M_IK, M_MLI, M_MLF, M_IW, M_FXF = 0, 64, 68, 72, 76

VMEM_LIMIT = 56 * 1024 * 1024


def _dot(a, b, prec=None):
    return jnp.dot(a, b, preferred_element_type=F32, precision=prec)


def _dot_nt(a, b, prec=None):
    return lax.dot_general(a, b, (((1,), (1,)), ((), ())), preferred_element_type=F32, precision=prec)


def _dot_tn(a, b):
    return lax.dot_general(a, b, (((0,), (0,)), ((), ())), preferred_element_type=F32)


def _sigmoid(x):
    return 1.0 / (1.0 + jnp.exp(-x))


def _log_sigmoid(x):
    return jnp.minimum(x, 0.0) - jnp.log1p(jnp.exp(-jnp.abs(x)))


def _rms(x):
    return x * lax.rsqrt(jnp.mean(x * x, axis=-1, keepdims=True) + EPS)


def _params(sem, vmem=VMEM_LIMIT):
    return pltpu.CompilerParams(dimension_semantics=sem, vmem_limit_bytes=vmem)


def _ada_kernel(c_ref, w_ref, b_ref, o_ref):
    c = c_ref[...]
    o_ref[0] = _dot(c * _sigmoid(c), w_ref[0], HIGHEST) + b_ref[0]


def _ada_mod(c, w_ada, b_ada):
    depth, d, n = w_ada.shape
    bsz = c.shape[0]
    tn = 1536
    return pl.pallas_call(
        _ada_kernel,
        out_shape=jax.ShapeDtypeStruct((depth, bsz, n), F32),
        grid=(depth, n // tn),
        in_specs=[pl.BlockSpec((bsz, d), lambda l, j: (0, 0)),
                  pl.BlockSpec((1, d, tn), lambda l, j: (l, 0, j)),
                  pl.BlockSpec((1, 1, tn), lambda l, j: (l, 0, j))],
        out_specs=pl.BlockSpec((1, bsz, tn), lambda l, j: (l, 0, j)),
        compiler_params=_params(("parallel", "parallel")),
        name="ada_mod",
    )(c, w_ada, b_ada.reshape(depth, 1, n))


def _inproj_kernel(x_ref, mod_ref, w_ref, b_ref, o_ref, h_scr):
    @pl.when(pl.program_id(2) == 0)
    def _():
        h = _rms(x_ref[0]) * (1.0 + mod_ref[0, 1:2, :]) + mod_ref[0, 0:1, :]
        h_scr[...] = h.astype(BF16)

    o_ref[0] = _dot(h_scr[...], w_ref[...]) + b_ref[...]


def _in_proj(x, mod, w, b):
    bsz, s, d = x.shape
    n = w.shape[1]
    tm = min(1024, s)
    tn = 1024
    return pl.pallas_call(
        _inproj_kernel,
        out_shape=jax.ShapeDtypeStruct((bsz, s, n), F32),
        grid=(bsz, s // tm, n // tn),
        in_specs=[pl.BlockSpec((1, tm, d), lambda bi, i, j: (bi, i, 0)),
                  pl.BlockSpec((1, 6, d), lambda bi, i, j: (bi, 0, 0)),
                  pl.BlockSpec((d, tn), lambda bi, i, j: (0, j)),
                  pl.BlockSpec((1, tn), lambda bi, i, j: (0, j))],
        out_specs=pl.BlockSpec((1, tm, tn), lambda bi, i, j: (bi, i, j)),
        scratch_shapes=[pltpu.VMEM((tm, d), BF16)],
        compiler_params=_params(("parallel", "parallel", "arbitrary")),
        name="in_proj",
    )(x, mod, w, b)


def _mlstm_kernel(qk_ref, v_ref, o_ref, misc_ref, cw_ref, cb_ref, g_ref,
                  y_ref, fcol_ref, frow_ref,
                  xext, ct_scr, n_scr, m_scr, carry_scr, *, L):
    c = pl.program_id(1)
    nqk = ML_HEADS * ML_QK

    @pl.when(c == 0)
    def _():
        xext[0:8, :] = jnp.zeros((8, 2 * nqk), F32)
        ct_scr[...] = jnp.zeros(ct_scr.shape, F32)
        n_scr[...] = jnp.zeros(n_scr.shape, F32)
        m_scr[...] = jnp.full(m_scr.shape, -jnp.inf, F32)
        carry_scr[...] = jnp.zeros(carry_scr.shape, F32)

    @pl.when(c > 0)
    def _():
        xext[0:8, :] = xext[L:L + 8, :]

    xext[8:8 + L, :] = qk_ref[0]
    cw = cw_ref[...]
    conv = (cb_ref[...] + cw[3:4, :] * xext[8:8 + L, :] + cw[2:3, :] * xext[7:7 + L, :]
            + cw[1:2, :] * xext[6:6 + L, :] + cw[0:1, :] * xext[5:5 + L, :])
    qk = conv * _sigmoid(conv)

    misc = misc_ref[0]
    ls = _log_sigmoid(misc)
    row = lax.broadcasted_iota(jnp.int32, (L, L), 0)
    col = lax.broadcasted_iota(jnp.int32, (L, L), 1)
    causal = row >= col
    tri = jnp.where(causal, 1.0, 0.0).astype(F32)
    cs = _dot(tri, ls, HIGHEST)
    cs_t = cs.T
    misc_t = misc.T
    carry = carry_scr[0:1, :]
    fcol_ref[0] = cs + carry
    for h in range(FOX_HEADS):
        frow_ref[0, h:h + 1, :] = cs_t[M_FXF + h:M_FXF + h + 1, :] + carry[:, M_FXF + h:M_FXF + h + 1]
    frow_ref[0, 4:8, :] = jnp.zeros((4, L), F32)
    carry_scr[0:1, :] = carry + cs[L - 1:L, :]

    for h in range(ML_HEADS):
        qh = qk[:, h * ML_QK:(h + 1) * ML_QK] * (ML_QK ** -0.5)
        kh = qk[:, nqk + h * ML_QK:nqk + (h + 1) * ML_QK]
        vb = v_ref[0, :, h * ML_V:(h + 1) * ML_V].astype(BF16)
        i_col = misc[:, M_MLI + h:M_MLI + h + 1]
        i_row = misc_t[M_MLI + h:M_MLI + h + 1, :]
        b_col = cs[:, M_MLF + h:M_MLF + h + 1]
        b_row = cs_t[M_MLF + h:M_MLF + h + 1, :]
        b_last = b_col[L - 1:L, :]
        m_prev = m_scr[h:h + 1, 0:1]

        d_log = jnp.where(causal, b_col - b_row + i_row, -jnp.inf)
        inter_log = b_col + m_prev
        m_out = jnp.maximum(inter_log, jnp.max(d_log, axis=-1, keepdims=True))
        qb = qh.astype(BF16)
        kb = kh.astype(BF16)
        s = _dot_nt(qb, kb) * jnp.exp(d_log - m_out)
        a_inter = jnp.exp(inter_log - m_out)
        ct = ct_scr[h]
        n_row = n_scr[h]
        num = _dot(s.astype(BF16), vb) + a_inter * _dot(qb, ct.astype(BF16))
        den = jnp.sum(s, axis=-1, keepdims=True) + a_inter * jnp.sum(qh * n_row, axis=-1, keepdims=True)
        hid = num / jnp.maximum(jnp.abs(den), jnp.exp(-m_out))

        w_state = b_last - b_col + i_col
        m_loc = jnp.max(w_state, axis=0, keepdims=True)
        ke = kh * jnp.exp(w_state - m_loc)
        c_loc = _dot_tn(ke.astype(BF16), vb)
        n_loc = jnp.sum(ke, axis=0, keepdims=True)
        m_new = jnp.maximum(b_last + m_prev, m_loc)
        decay = jnp.exp(b_last + m_prev - m_new)
        scale = jnp.exp(m_loc - m_new)
        ct_scr[h] = decay * ct + scale * c_loc
        n_scr[h] = decay * n_row + scale * n_loc
        m_scr[h:h + 1, :] = jnp.broadcast_to(m_new, (1, LANE))

        y = (_rms(hid) * g_ref[:, h * ML_V:(h + 1) * ML_V]
             * _sigmoid(o_ref[0, :, h * ML_V:(h + 1) * ML_V]))
        y_ref[0, :, h * ML_V:(h + 1) * ML_V] = y.astype(BF16)


def _mlstm(proj, conv_w, conv_b, norm_g):
    bsz, s, _ = proj.shape
    L = min(256, s)
    w = 2 * ML_HEADS * ML_QK
    wv = ML_HEADS * ML_V
    return pl.pallas_call(
        functools.partial(_mlstm_kernel, L=L),
        out_shape=(jax.ShapeDtypeStruct((bsz, s, wv), BF16),
                   jax.ShapeDtypeStruct((bsz, s, LANE), F32),
                   jax.ShapeDtypeStruct((bsz, 8, s), F32)),
        grid=(bsz, s // L),
        in_specs=[pl.BlockSpec((1, L, w), lambda b, c: (b, c, C_MLQK // w)),
                  pl.BlockSpec((1, L, wv), lambda b, c: (b, c, C_MLV // wv)),
                  pl.BlockSpec((1, L, wv), lambda b, c: (b, c, C_MLO // wv)),
                  pl.BlockSpec((1, L, LANE), lambda b, c: (b, c, C_MISC // LANE)),
                  pl.BlockSpec((ML_CONV, w), lambda b, c: (0, 0)),
                  pl.BlockSpec((1, w), lambda b, c: (0, 0)),
                  pl.BlockSpec((1, wv), lambda b, c: (0, 0))],
        out_specs=(pl.BlockSpec((1, L, wv), lambda b, c: (b, c, 0)),
                   pl.BlockSpec((1, L, LANE), lambda b, c: (b, c, 0)),
                   pl.BlockSpec((1, 8, L), lambda b, c: (b, 0, c))),
        scratch_shapes=[pltpu.VMEM((L + 8, w), F32),
                        pltpu.VMEM((ML_HEADS, ML_QK, ML_V), F32),
                        pltpu.VMEM((ML_HEADS, 1, ML_QK), F32),
                        pltpu.VMEM((8, LANE), F32),
                        pltpu.VMEM((8, LANE), F32)],
        compiler_params=_params(("parallel", "arbitrary")),
        name="mlstm",
    )(proj, proj, proj, proj, conv_w, conv_b.reshape(1, w), norm_g.reshape(1, wv))


def _fox_kernel(q_ref, k_ref, v_ref, fcol_ref, frow_ref, y_ref, k_scr, vt_scr, qt_scr, *acc, T):
    qi = pl.program_id(1)
    d = FOX_DIM

    @pl.when(qi == 0)
    def _():
        for h in range(FOX_HEADS):
            k_scr[h] = k_ref[0, :, h * d:(h + 1) * d].astype(BF16)
            vt_scr[h] = v_ref[0, :, h * d:(h + 1) * d].T.astype(BF16)

    q_t = (q_ref[0] * (d ** -0.5)).T
    for h in range(FOX_HEADS):
        qt_scr[h] = q_t[h * d:(h + 1) * d, :].astype(BF16)
        acc[h][...] = jnp.zeros((d, T), F32)
    qpos = qi * T + lax.broadcasted_iota(jnp.int32, (1, T), 1)

    def tile(j, carry, masked):
        off = pl.multiple_of(j * T, T)
        new = []
        for h in range(FOX_HEADS):
            m_old, l_old = carry[2 * h], carry[2 * h + 1]
            s = _dot(k_scr[h, pl.ds(off, T), :], qt_scr[h])
            fk = fcol_ref[0, pl.ds(off, T), M_FXF + h:M_FXF + h + 1]
            s = s + (frow_ref[0, h:h + 1, :] - fk)
            if masked:
                kpos = off + lax.broadcasted_iota(jnp.int32, (T, 1), 0)
                s = jnp.where(kpos <= qpos, s, -jnp.inf)
            m_new = jnp.maximum(m_old, jnp.max(s, axis=0, keepdims=True))
            alpha = jnp.exp(m_old - m_new)
            p = jnp.exp(s - m_new)
            new.append(m_new)
            new.append(alpha * l_old + jnp.sum(p, axis=0, keepdims=True))
            acc[h][...] = alpha * acc[h][...] + _dot(vt_scr[h, :, pl.ds(off, T)], p.astype(BF16))
        return tuple(new)

    init = (jnp.full((1, T), -jnp.inf, F32), jnp.zeros((1, T), F32)) * FOX_HEADS
    carry = lax.fori_loop(0, qi, lambda j, c: tile(j, c, False), init)
    carry = tile(qi, carry, True)
    for h in range(FOX_HEADS):
        y_ref[0, :, h * d:(h + 1) * d] = (acc[h][...] / carry[2 * h + 1]).T.astype(BF16)


def _fox(proj, fcol, frow):
    bsz, s, _ = proj.shape
    T = min(256, s)
    w = FOX_HEADS * FOX_DIM
    return pl.pallas_call(
        functools.partial(_fox_kernel, T=T),
        out_shape=jax.ShapeDtypeStruct((bsz, s, w), BF16),
        grid=(bsz, s // T),
        in_specs=[pl.BlockSpec((1, T, w), lambda b, i: (b, i, C_FOX // w)),
                  pl.BlockSpec((1, s, w), lambda b, i: (b, 0, C_FOX // w + 1)),
                  pl.BlockSpec((1, s, w), lambda b, i: (b, 0, C_FOX // w + 2)),
                  pl.BlockSpec((1, s, LANE), lambda b, i: (b, 0, 0)),
                  pl.BlockSpec((1, 8, T), lambda b, i: (b, 0, i))],
        out_specs=pl.BlockSpec((1, T, w), lambda b, i: (b, i, 0)),
        scratch_shapes=[pltpu.VMEM((FOX_HEADS, s, FOX_DIM), BF16),
                        pltpu.VMEM((FOX_HEADS, FOX_DIM, s), BF16),
                        pltpu.VMEM((FOX_HEADS, FOX_DIM, T), BF16)]
        + [pltpu.VMEM((FOX_DIM, T), F32) for _ in range(FOX_HEADS)],
        compiler_params=_params(("parallel", "arbitrary")),
        name="fox_attention",
    )(proj, proj, proj, fcol, frow)


def _dsa_body(ext, qi, q_ref, qidx_ref, misc_q_ref, wuk_ref, wuv_ref, y_ref,
              ckvn_scr, ckvnt_scr, kidx_scr, sel_scr, *, T, n_sel, rank_tile):
    dh = DSA_DIM
    q_t = q_ref[0].T
    qidx_t = qidx_ref[0].T.astype(BF16)
    w_t = misc_q_ref[0].T[M_IW:M_IW + IDX_HEADS, :] * (IDX_HEADS ** -0.5)
    kidx = kidx_scr[0:ext, :]
    score = jnp.zeros((ext, T), F32)
    for h in range(IDX_HEADS):
        lg = _dot(kidx, qidx_t[h * IDX_DIM:(h + 1) * IDX_DIM, :]) * (IDX_DIM ** -0.5)
        score = score + w_t[h:h + 1, :] * jnp.maximum(lg, 0.0)
    kpos = lax.broadcasted_iota(jnp.int32, (ext, 1), 0)
    qchunk = (qi * T + lax.broadcasted_iota(jnp.int32, (1, T), 1)) // CHUNK
    score = jnp.where((kpos // CHUNK) <= qchunk, score, -jnp.inf)

    def as_float(c):
        return pltpu.bitcast(jnp.where(c < 0, c ^ jnp.int32(0x7FFFFFFF), c), F32)

    def count_ge(c):
        return jnp.sum(jnp.where(score >= as_float(c), 1.0, 0.0), axis=0, keepdims=True)

    t0 = jnp.where(count_ge(jnp.zeros((1, T), jnp.int32)) >= n_sel, jnp.int32(0), jnp.int32(INT_MIN))

    def bis(i, t):
        cand = t + jnp.left_shift(jnp.int32(1), jnp.int32(30) - i)
        return jnp.where(count_ge(cand) >= n_sel, cand, t)

    t = lax.fori_loop(0, 31, bis, t0)
    n_vis = ((qchunk + 1) * CHUNK).astype(F32)
    thr = jnp.where(n_vis <= n_sel, -3.0e38, as_float(t))
    need = n_sel - jnp.sum(jnp.where(score > thr, 1.0, 0.0), axis=0, keepdims=True)
    r_i = lax.broadcasted_iota(jnp.int32, (rank_tile, rank_tile), 0)
    c_i = lax.broadcasted_iota(jnp.int32, (rank_tile, rank_tile), 1)
    lower = jnp.where(c_i < r_i, 1.0, 0.0).astype(BF16)
    carry = jnp.zeros((1, T), F32)
    for j in range(ext // rank_tile):
        sc = score[j * rank_tile:(j + 1) * rank_tile, :]
        eq = jnp.where(sc == thr, 1.0, 0.0)
        rank = _dot(lower, eq.astype(BF16)) + carry
        carry = carry + jnp.sum(eq, axis=0, keepdims=True)
        sel_scr[j * rank_tile:(j + 1) * rank_tile, :] = jnp.where(
            sc > thr, 1.0, jnp.where(rank < need, eq, 0.0))

    sel = sel_scr[0:ext, :] > 0.5
    ckvn = ckvn_scr[0:ext, :]
    ckvn_t = ckvnt_scr[:, 0:ext]
    for h in range(DSA_HEADS):
        qa_t = _dot(wuk_ref[h], q_t[h * dh:(h + 1) * dh, :].astype(BF16)) * (dh ** -0.5)
        lg = _dot(ckvn, qa_t.astype(BF16))
        lg = jnp.where(sel, lg, -jnp.inf)
        p = jnp.exp(lg - jnp.max(lg, axis=0, keepdims=True))
        l = jnp.sum(p, axis=0, keepdims=True)
        lat_t = _dot(ckvn_t, p.astype(BF16)) / l
        out_t = _dot(wuv_ref[h], lat_t.astype(BF16))
        y_ref[0, :, h * dh:(h + 1) * dh] = out_t.T.astype(BF16)


def _dsa_kernel(q_ref, ckv_ref, qidx_ref, misc_all_ref, misc_q_ref, g_ref, wuk_ref, wuv_ref, y_ref,
                ckvn_scr, ckvnt_scr, kidx_scr, sel_scr, *, T, S, n_sel, n_cls, rank_tile):
    qi = pl.program_id(1)

    @pl.when(qi == 0)
    def _():
        ckvn = _rms(ckv_ref[0]) * g_ref[...]
        ckvn_scr[...] = ckvn.astype(BF16)
        ckvnt_scr[...] = ckvn.T.astype(BF16)
        kidx_scr[...] = misc_all_ref[0, :, M_IK:M_IK + IDX_DIM].astype(BF16)

    per = (S // T) // n_cls
    for c in range(n_cls):
        ext = (c + 1) * per * T

        @pl.when(qi // per == c)
        def _(ext=ext):
            _dsa_body(ext, qi, q_ref, qidx_ref, misc_q_ref, wuk_ref, wuv_ref, y_ref,
                      ckvn_scr, ckvnt_scr, kidx_scr, sel_scr, T=T, n_sel=n_sel, rank_tile=rank_tile)


def _dsa(proj, kv_g, wuk, wuv_t):
    bsz, s, _ = proj.shape
    T = min(256, s)
    n_sel = min(DSA_TOPK, s // 4)
    n_cls = max(1, min(4, s // 512))
    rank_tile = 256
    w = DSA_HEADS * DSA_DIM
    wi = IDX_HEADS * IDX_DIM
    return pl.pallas_call(
        functools.partial(_dsa_kernel, T=T, S=s, n_sel=float(n_sel), n_cls=n_cls, rank_tile=rank_tile),
        out_shape=jax.ShapeDtypeStruct((bsz, s, w), BF16),
        grid=(bsz, s // T),
        in_specs=[pl.BlockSpec((1, T, w), lambda b, i: (b, i, C_DQ // w)),
                  pl.BlockSpec((1, s, DSA_LATENT), lambda b, i: (b, 0, C_CKV // DSA_LATENT)),
                  pl.BlockSpec((1, T, wi), lambda b, i: (b, i, C_DIQ // wi)),
                  pl.BlockSpec((1, s, LANE), lambda b, i: (b, 0, C_MISC // LANE)),
                  pl.BlockSpec((1, T, LANE), lambda b, i: (b, i, C_MISC // LANE)),
                  pl.BlockSpec((1, DSA_LATENT), lambda b, i: (0, 0)),
                  pl.BlockSpec((DSA_HEADS, DSA_LATENT, DSA_DIM), lambda b, i: (0, 0, 0)),
                  pl.BlockSpec((DSA_HEADS, DSA_DIM, DSA_LATENT), lambda b, i: (0, 0, 0))],
        out_specs=pl.BlockSpec((1, T, w), lambda b, i: (b, i, 0)),
        scratch_shapes=[pltpu.VMEM((s, DSA_LATENT), BF16),
                        pltpu.VMEM((DSA_LATENT, s), BF16),
                        pltpu.VMEM((s, IDX_DIM), BF16),
                        pltpu.VMEM((s, T), F32)],
        compiler_params=_params(("parallel", "arbitrary")),
        name="dsa_attention",
    )(proj, proj, proj, proj, proj, kv_g.reshape(1, DSA_LATENT), wuk, wuv_t)


def _merge_kernel(yml_ref, ydsa_ref, yfox_ref, g0_ref, g1_ref, g2_ref, x_ref, mod_ref,
                  wml_ref, wdsa_ref, wfox_ref, wout_ref, o_ref):
    merged = (_sigmoid(g0_ref[0]) * _dot(yml_ref[0], wml_ref[...])
              + _sigmoid(g1_ref[0]) * _dot(ydsa_ref[0], wdsa_ref[...])
              + _sigmoid(g2_ref[0]) * _dot(yfox_ref[0], wfox_ref[...]))
    out = _dot(merged.astype(BF16), wout_ref[...])
    o_ref[0] = x_ref[0] + mod_ref[0, 2:3, :] * out


def _merge(x, proj, y_ml, y_dsa, y_fox, mod, w_ml, w_dsa, w_fox, w_out):
    bsz, s, d = x.shape
    tm = min(512, s)
    wb = y_ml.shape[-1]
    gb = C_GATE // d
    yspec = pl.BlockSpec((1, tm, wb), lambda b, i: (b, i, 0))
    wspec = pl.BlockSpec((wb, d), lambda b, i: (0, 0))
    return pl.pallas_call(
        _merge_kernel,
        out_shape=jax.ShapeDtypeStruct((bsz, s, d), F32),
        grid=(bsz, s // tm),
        in_specs=[yspec, yspec, yspec,
                  pl.BlockSpec((1, tm, d), lambda b, i: (b, i, gb)),
                  pl.BlockSpec((1, tm, d), lambda b, i: (b, i, gb + 1)),
                  pl.BlockSpec((1, tm, d), lambda b, i: (b, i, gb + 2)),
                  pl.BlockSpec((1, tm, d), lambda b, i: (b, i, 0)),
                  pl.BlockSpec((1, 6, d), lambda b, i: (b, 0, 0)),
                  wspec, wspec, wspec,
                  pl.BlockSpec((d, d), lambda b, i: (0, 0))],
        out_specs=pl.BlockSpec((1, tm, d), lambda b, i: (b, i, 0)),
        compiler_params=_params(("parallel", "parallel")),
        name="merge_out",
    )(y_ml, y_dsa, y_fox, proj, proj, proj, x, mod, w_ml, w_dsa, w_fox, w_out)


def _router_kernel(x_ref, mod_ref, wr_ref, br_ref, h_ref, route_ref, cnt_ref, carry_scr, *, T):
    first = (pl.program_id(0) == 0) & (pl.program_id(1) == 0)

    @pl.when(first)
    def _():
        carry_scr[...] = jnp.zeros(carry_scr.shape, F32)

    h = _rms(x_ref[0]) * (1.0 + mod_ref[0, 4:5, :]) + mod_ref[0, 3:4, :]
    h_ref[0] = h
    lg = _dot(h, wr_ref[...], HIGHEST) + br_ref[...]
    lane = lax.broadcasted_iota(jnp.int32, (T, LANE), 1).astype(F32)
    vals, hots = [], []
    for _ in range(TOP_K):
        mx = jnp.max(lg, axis=-1, keepdims=True)
        idx = jnp.min(jnp.where(lg == mx, lane, float(LANE)), axis=-1, keepdims=True)
        hot = lane == idx
        vals.append(mx)
        hots.append(hot)
        lg = jnp.where(hot, -jnp.inf, lg)
    exps = [jnp.exp(v - vals[0]) for v in vals]
    tot = exps[0] + exps[1] + exps[2] + exps[3]
    multi = jnp.zeros((T, LANE), F32)
    for hot in hots:
        multi = multi + jnp.where(hot, 1.0, 0.0)
    r_i = lax.broadcasted_iota(jnp.int32, (T, T), 0)
    c_i = lax.broadcasted_iota(jnp.int32, (T, T), 1)
    lower = jnp.where(c_i < r_i, 1.0, 0.0).astype(BF16)
    carry = carry_scr[0:1, :]
    before = _dot(lower, multi.astype(BF16)) + carry
    carry = carry + jnp.sum(multi, axis=0, keepdims=True)
    carry_scr[0:1, :] = carry
    cnt_ref[...] = jnp.broadcast_to(carry, (8, LANE))
    slab = jnp.zeros((T, LANE), F32)
    for k in range(TOP_K):
        e_k = jnp.sum(jnp.where(hots[k], lane, 0.0), axis=-1, keepdims=True)
        r_k = jnp.sum(jnp.where(hots[k], before, 0.0), axis=-1, keepdims=True)
        slab = slab + jnp.where(lane == float(k), e_k, 0.0)
        slab = slab + jnp.where(lane == float(TOP_K + k), r_k, 0.0)
        slab = slab + jnp.where(lane == float(2 * TOP_K + k), exps[k] / tot, 0.0)
    route_ref[0] = slab


def _router(x, mod, w_router, b_router):
    bsz, s, d = x.shape
    T = min(512, s)
    ne = w_router.shape[1]
    wr = jnp.zeros((d, LANE), F32).at[:, :ne].set(w_router)
    br = jnp.full((1, LANE), -1e30, F32).at[0, :ne].set(b_router)
    return pl.pallas_call(
        functools.partial(_router_kernel, T=T),
        out_shape=(jax.ShapeDtypeStruct((bsz, s, d), F32),
                   jax.ShapeDtypeStruct((bsz, s, LANE), F32),
                   jax.ShapeDtypeStruct((8, LANE), F32)),
        grid=(bsz, s // T),
        in_specs=[pl.BlockSpec((1, T, d), lambda b, i: (b, i, 0)),
                  pl.BlockSpec((1, 6, d), lambda b, i: (b, 0, 0)),
                  pl.BlockSpec((d, LANE), lambda b, i: (0, 0)),
                  pl.BlockSpec((1, LANE), lambda b, i: (0, 0))],
        out_specs=(pl.BlockSpec((1, T, d), lambda b, i: (b, i, 0)),
                   pl.BlockSpec((1, T, LANE), lambda b, i: (b, i, 0)),
                   pl.BlockSpec((8, LANE), lambda b, i: (0, 0))),
        scratch_shapes=[pltpu.VMEM((8, LANE), F32)],
        compiler_params=_params(("arbitrary", "arbitrary")),
        name="moe_router",
    )(x, mod, wr, br)


ZERO_ROWS = 128


def _dispatch_kernel(pad_end_ref, cnt_end_ref, dest_ref, h_ref, xs_hbm, zbuf, zsem, sems, *, TD, BM):
    i = pl.program_id(0)

    @pl.when(i == 0)
    def _():
        zbuf[...] = jnp.zeros(zbuf.shape, F32)
        n_rows = xs_hbm.shape[0]

        def pad_copy(e, g):
            start = pl.multiple_of(pad_end_ref[e] - g * ZERO_ROWS, ZERO_ROWS)
            return (start + ZERO_ROWS > cnt_end_ref[e],
                    pltpu.make_async_copy(zbuf.at[pl.ds(0, ZERO_ROWS)], xs_hbm.at[pl.ds(start, ZERO_ROWS)], zsem))

        def tail_copy(e):
            start = pl.multiple_of(pad_end_ref[N_EXPERTS - 1] + e * BM, BM)
            safe = pl.multiple_of(jnp.minimum(start, n_rows - BM), BM)
            return start < n_rows, pltpu.make_async_copy(zbuf, xs_hbm.at[pl.ds(safe, BM)], zsem)

        copies = [pad_copy(e, g) for e in range(N_EXPERTS) for g in range(1, BM // ZERO_ROWS + 1)]
        copies += [tail_copy(e) for e in range(N_EXPERTS)]
        for cond, cp in copies:
            pl.when(cond)(cp.start)
        for cond, cp in copies:
            pl.when(cond)(cp.wait)

    def issue(r, _):
        for k in range(TOP_K):
            pltpu.make_async_copy(h_ref.at[pl.ds(r, 1)], xs_hbm.at[pl.ds(dest_ref[r * TOP_K + k], 1)],
                                  sems.at[k]).start()
        return 0

    lax.fori_loop(0, TD, issue, 0, unroll=4)
    for k in range(TOP_K):
        pltpu.make_async_copy(h_ref, xs_hbm.at[pl.ds(0, TD)], sems.at[k]).wait()


def _dispatch(pad_end, cnt_end, dest, h2, n_rows, bm):
    n, d = h2.shape
    td = min(256, n)
    return pl.pallas_call(
        functools.partial(_dispatch_kernel, TD=td, BM=bm),
        out_shape=jax.ShapeDtypeStruct((n_rows, d), F32),
        grid_spec=pltpu.PrefetchScalarGridSpec(
            num_scalar_prefetch=2,
            grid=(n // td,),
            in_specs=[pl.BlockSpec((td * TOP_K,), lambda i, pe, ce: (i,), memory_space=pltpu.SMEM),
                      pl.BlockSpec((td, d), lambda i, pe, ce: (i, 0))],
            out_specs=pl.BlockSpec(memory_space=pl.ANY),
            scratch_shapes=[pltpu.VMEM((bm, d), F32),
                            pltpu.SemaphoreType.DMA(()),
                            pltpu.SemaphoreType.DMA((TOP_K,))]),
        compiler_params=pltpu.CompilerParams(dimension_semantics=("arbitrary",), vmem_limit_bytes=VMEM_LIMIT,
                                             disable_bounds_checks=True),
        name="moe_dispatch",
    )(pad_end, cnt_end, dest, h2)


def _expert_kernel(be_ref, nu_ref, xs_ref, w1_ref, b1_ref, w2_ref, b2_ref, o_ref, w1b, w2b):
    i = pl.program_id(0)
    de = w2_ref.shape[1]
    prev = be_ref[jnp.maximum(i - 1, 0)]
    fresh = (i == 0) | (be_ref[i] != prev)

    @pl.when(fresh & (i < nu_ref[0]))
    def _():
        w1b[...] = w1_ref[...].astype(BF16)
        w2b[...] = w2_ref[...].astype(BF16)

    @pl.when(i < nu_ref[0])
    def _():
        hdn = _dot(xs_ref[...].astype(BF16), w1b[...]) + b1_ref[...]
        glu = jnp.minimum(hdn[:, :de], SWIGLU_LIMIT)
        lin = jnp.clip(hdn[:, de:], -SWIGLU_LIMIT, SWIGLU_LIMIT)
        act = glu * _sigmoid(SWIGLU_ALPHA * glu) * (lin + 1.0)
        o_ref[...] = _dot(act.astype(BF16), w2b[...]) + b2_ref[...]

    @pl.when(i >= nu_ref[0])
    def _():
        o_ref[...] = jnp.zeros(o_ref.shape, F32)


def _experts(blk_expert, n_used, xs, w1, b1, w2, b2, layer, bm):
    n_rows, d = xs.shape
    depth, ne, _, dh2 = w1.shape
    de = w2.shape[2]
    n_blocks = n_rows // bm

    def row_map(i, be, nu):
        return (jnp.minimum(i, nu[0] - 1), 0)

    def w_map(i, be, nu):
        return (layer, be[i], 0, 0)

    return pl.pallas_call(
        _expert_kernel,
        out_shape=jax.ShapeDtypeStruct((n_rows, d), F32),
        grid_spec=pltpu.PrefetchScalarGridSpec(
            num_scalar_prefetch=2,
            grid=(n_blocks,),
            in_specs=[pl.BlockSpec((bm, d), row_map),
                      pl.BlockSpec((None, None, d, dh2), w_map),
                      pl.BlockSpec((None, None, 1, dh2), w_map),
                      pl.BlockSpec((None, None, de, d), w_map),
                      pl.BlockSpec((None, None, 1, d), w_map)],
            out_specs=pl.BlockSpec((bm, d), lambda i, be, nu: (i, 0)),
            scratch_shapes=[pltpu.VMEM((d, dh2), BF16), pltpu.VMEM((de, d), BF16)]),
        compiler_params=_params(("arbitrary",)),
        name="moe_experts",
    )(blk_expert, n_used, xs, w1, b1.reshape(depth, ne, 1, dh2), w2, b2.reshape(depth, ne, 1, d))


def _combine_kernel(dest_ref, yb_hbm, x_ref, route_ref, mod_ref, fg_ref, o_ref, buf, sems, *, TC, final):
    def issue(r, _):
        for k in range(TOP_K):
            pltpu.make_async_copy(yb_hbm.at[pl.ds(dest_ref[r * TOP_K + k], 1)], buf.at[k, pl.ds(r, 1)],
                                  sems.at[k]).start()
        return 0

    lax.fori_loop(0, TC, issue, 0, unroll=4)
    for k in range(TOP_K):
        pltpu.make_async_copy(yb_hbm.at[pl.ds(0, TC)], buf.at[k], sems.at[k]).wait()

    route = route_ref[0]
    y = jnp.zeros(x_ref.shape[1:], F32)
    for k in range(TOP_K):
        y = y + route[:, 2 * TOP_K + k:2 * TOP_K + k + 1] * buf[k]
    out = x_ref[0] + mod_ref[0, 5:6, :] * y
    if final:
        out = _rms(out) * fg_ref[...]
    o_ref[0] = out


def _combine(dest, yb, x, route, mod, final_g, final):
    bsz, s, d = x.shape
    tc = min(256, s)
    nt = s // tc
    return pl.pallas_call(
        functools.partial(_combine_kernel, TC=tc, final=final),
        out_shape=jax.ShapeDtypeStruct((bsz, s, d), F32),
        grid=(bsz, nt),
        in_specs=[pl.BlockSpec((tc * TOP_K,), lambda b, i: (b * nt + i,), memory_space=pltpu.SMEM),
                  pl.BlockSpec(memory_space=pl.ANY),
                  pl.BlockSpec((1, tc, d), lambda b, i: (b, i, 0)),
                  pl.BlockSpec((1, tc, LANE), lambda b, i: (b, i, 0)),
                  pl.BlockSpec((1, 6, d), lambda b, i: (b, 0, 0)),
                  pl.BlockSpec((1, d), lambda b, i: (0, 0))],
        out_specs=pl.BlockSpec((1, tc, d), lambda b, i: (b, i, 0)),
        scratch_shapes=[pltpu.VMEM((TOP_K, tc, d), F32), pltpu.SemaphoreType.DMA((TOP_K,))],
        compiler_params=pltpu.CompilerParams(dimension_semantics=("arbitrary", "arbitrary"),
                                             vmem_limit_bytes=VMEM_LIMIT, disable_bounds_checks=True),
        name="moe_combine",
    )(dest, yb, x, route, mod, final_g.reshape(1, d))


def _rearranged_in_proj(w_in, b_in):
    sizes = (ML_HEADS * ML_QK, ML_HEADS * ML_QK, ML_HEADS * ML_V, ML_HEADS * ML_V, ML_HEADS, ML_HEADS,
             DSA_HEADS * DSA_DIM, DSA_LATENT, IDX_HEADS * IDX_DIM, IDX_DIM, IDX_HEADS,
             3 * FOX_HEADS * FOX_DIM, FOX_HEADS, N_BRANCH * w_in.shape[0])
    offs = [0]
    for sz in sizes:
        offs.append(offs[-1] + sz)
    (o_mq, o_mk, o_mv, o_mo, o_mi, o_mf, o_dq, o_ckv, o_iq, o_ik, o_iw, o_fx, o_ff, o_g, o_end) = offs
    pad = LANE - (IDX_DIM + 2 * ML_HEADS + IDX_HEADS + FOX_HEADS)

    def cols(a):
        parts = [a[..., o_mq:o_mi],
                 a[..., o_dq:o_ckv],
                 a[..., o_iq:o_ik],
                 a[..., o_ckv:o_iq],
                 a[..., o_ik:o_iw],
                 a[..., o_mi:o_dq],
                 a[..., o_iw:o_fx],
                 a[..., o_ff:o_g],
                 jnp.zeros(a.shape[:-1] + (pad,), a.dtype),
                 a[..., o_fx:o_ff],
                 a[..., o_g:o_end]]
        return jnp.concatenate(parts, axis=-1)

    return cols(w_in).astype(BF16), cols(b_in.reshape(1, -1))


def _moe_offsets(route, counts, bm, n_blocks):
    n = route.shape[0]
    e = route[:, 0:TOP_K].astype(jnp.int32)
    rank = route[:, TOP_K:2 * TOP_K].astype(jnp.int32)
    cnt = counts[0, :N_EXPERTS].astype(jnp.int32)
    padded = (cnt + bm - 1) // bm * bm
    pad_end = jnp.cumsum(padded)
    pad_start = pad_end - padded
    dest = (pad_start[e] + rank).reshape(n * TOP_K)
    blk_row = jnp.arange(n_blocks, dtype=jnp.int32) * bm
    blk_expert = jnp.minimum(jnp.sum((pad_end[None, :] <= blk_row[:, None]).astype(jnp.int32), axis=1),
                             N_EXPERTS - 1)
    n_used = (pad_end[-1:] // bm).astype(jnp.int32)
    return dest, pad_end.astype(jnp.int32), (pad_start + cnt).astype(jnp.int32), blk_expert, n_used


def kernel(x, c, w_ada, b_ada, w_in, b_in, conv_w, conv_b, ml_norm_g, kv_norm_g, w_uk, w_uv,
           w_br_ml, w_br_dsa, w_br_fox, w_out, w_router, b_router, w1, b1, w2, b2, final_g):
    bsz, s, d = x.shape
    depth = w_in.shape[0]
    n = bsz * s
    bm = 512
    n_blocks = n * TOP_K // bm + N_EXPERTS
    mods = _ada_mod(c, w_ada, b_ada).reshape(depth, bsz, 6, d)
    for l in range(depth):
        mod = mods[l]
        w_r, b_r = _rearranged_in_proj(w_in[l], b_in[l])
        proj = _in_proj(x, mod, w_r, b_r)
        y_ml, fcol, frow = _mlstm(proj, conv_w[l], conv_b[l], ml_norm_g[l])
        y_fox = _fox(proj, fcol, frow)
        y_dsa = _dsa(proj, kv_norm_g[l], w_uk[l].astype(BF16), jnp.swapaxes(w_uv[l], 1, 2).astype(BF16))
        x = _merge(x, proj, y_ml, y_dsa, y_fox, mod, w_br_ml[l].astype(BF16), w_br_dsa[l].astype(BF16),
                   w_br_fox[l].astype(BF16), w_out[l].astype(BF16))
        h2, route, counts = _router(x, mod, w_router[l], b_router[l])
        dest, pad_end, cnt_end, blk_expert, n_used = _moe_offsets(route.reshape(n, LANE), counts, bm, n_blocks)
        xs = _dispatch(pad_end, cnt_end, dest, h2.reshape(n, d), n_blocks * bm, bm)
        yb = _experts(blk_expert, n_used, xs, w1, b1, w2, b2, l, bm)
        x = _combine(dest, yb, x, route, mod, final_g, final=(l == depth - 1))
    return x
```

```python
import functools

import jax
import jax.numpy as jnp
from jax import lax
from jax.experimental import pallas as pl
from jax.experimental.pallas import tpu as pltpu

F32 = jnp.float32
BF16 = jnp.bfloat16
HIGHEST = lax.Precision.HIGHEST

EPS = 1e-6
CHUNK = 64

ML_HEADS, ML_QK, ML_V, ML_CONV = 4, 64, 128, 4
DSA_HEADS, DSA_DIM, DSA_LATENT = 4, 128, 128
IDX_HEADS, IDX_DIM, DSA_TOPK = 4, 64, 256
FOX_HEADS, FOX_DIM = 4, 128
N_BRANCH = 3
N_EXPERTS, TOP_K = 32, 4
SWIGLU_LIMIT, SWIGLU_ALPHA = 7.0, 1.702

LANE = 128
INT_MIN = -2 ** 31

C_MLQK = 0
C_MLV = 512
C_MLO = 1024
C_DQ = 1536
C_DIQ = 2048
C_CKV = 2304
C_MISC = 2432
C_FOX = 2560
C_GATE = 4096
NP = 7168
M_IK, M_MLI, M_MLF, M_IW, M_FXF = 0, 64, 68, 72, 76

VMEM_LIMIT = 56 * 1024 * 1024


def _dot(a, b, prec=None):
    return jnp.dot(a, b, preferred_element_type=F32, precision=prec)


def _dot_nt(a, b, prec=None):
    return lax.dot_general(a, b, (((1,), (1,)), ((), ())), preferred_element_type=F32, precision=prec)


def _dot_tn(a, b):
    return lax.dot_general(a, b, (((0,), (0,)), ((), ())), preferred_element_type=F32)


def _sigmoid(x):
    return 1.0 / (1.0 + jnp.exp(-x))


def _log_sigmoid(x):
    return jnp.minimum(x, 0.0) - jnp.log1p(jnp.exp(-jnp.abs(x)))


def _rms(x):
    return x * lax.rsqrt(jnp.mean(x * x, axis=-1, keepdims=True) + EPS)


def _params(sem, vmem=VMEM_LIMIT):
    return pltpu.CompilerParams(dimension_semantics=sem, vmem_limit_bytes=vmem)


def _ada_kernel(c_ref, w_ref, b_ref, o_ref):
    c = c_ref[...]
    o_ref[0] = _dot(c * _sigmoid(c), w_ref[0], HIGHEST) + b_ref[0]


def _ada_mod(c, w_ada, b_ada):
    depth, d, n = w_ada.shape
    bsz = c.shape[0]
    tn = 1536
    return pl.pallas_call(
        _ada_kernel,
        out_shape=jax.ShapeDtypeStruct((depth, bsz, n), F32),
        grid=(depth, n // tn),
        in_specs=[pl.BlockSpec((bsz, d), lambda l, j: (0, 0)),
                  pl.BlockSpec((1, d, tn), lambda l, j: (l, 0, j)),
                  pl.BlockSpec((1, 1, tn), lambda l, j: (l, 0, j))],
        out_specs=pl.BlockSpec((1, bsz, tn), lambda l, j: (l, 0, j)),
        compiler_params=_params(("parallel", "parallel")),
        name="ada_mod",
    )(c, w_ada, b_ada.reshape(depth, 1, n))


def _inproj_kernel(x_ref, mod_ref, w_ref, b_ref, o_ref, misc_ref, h_scr, *, tn):
    j = pl.program_id(2)

    @pl.when(j == 0)
    def _():
        h = _rms(x_ref[0]) * (1.0 + mod_ref[0, 1:2, :]) + mod_ref[0, 0:1, :]
        h_scr[...] = h.astype(BF16)

    acc = _dot(h_scr[...], w_ref[...]) + b_ref[...]
    o_ref[0] = acc.astype(BF16)

    @pl.when(j == C_MISC // tn)
    def _():
        misc_ref[0] = acc[:, C_MISC % tn:C_MISC % tn + LANE]


def _in_proj(x, mod, w, b):
    bsz, s, d = x.shape
    n = w.shape[1]
    tm = min(1024, s)
    tn = 1024
    return pl.pallas_call(
        functools.partial(_inproj_kernel, tn=tn),
        out_shape=(jax.ShapeDtypeStruct((bsz, s, n), BF16), jax.ShapeDtypeStruct((bsz, s, LANE), F32)),
        grid=(bsz, s // tm, n // tn),
        in_specs=[pl.BlockSpec((1, tm, d), lambda bi, i, j: (bi, i, 0)),
                  pl.BlockSpec((1, 6, d), lambda bi, i, j: (bi, 0, 0)),
                  pl.BlockSpec((d, tn), lambda bi, i, j: (0, j)),
                  pl.BlockSpec((1, tn), lambda bi, i, j: (0, j))],
        out_specs=(pl.BlockSpec((1, tm, tn), lambda bi, i, j: (bi, i, j)),
                   pl.BlockSpec((1, tm, LANE), lambda bi, i, j: (bi, i, 0))),
        scratch_shapes=[pltpu.VMEM((tm, d), BF16)],
        compiler_params=_params(("parallel", "parallel", "arbitrary")),
        name="in_proj",
    )(x, mod, w, b)


def _mlstm_kernel(qk_ref, v_ref, o_ref, misc_ref, cw_ref, cb_ref, g_ref,
                  y_ref, fcol_ref, frow_ref,
                  xext, ct_scr, n_scr, m_scr, carry_scr, *, L):
    c = pl.program_id(1)
    nqk = ML_HEADS * ML_QK

    @pl.when(c == 0)
    def _():
        xext[0:8, :] = jnp.zeros((8, 2 * nqk), F32)
        ct_scr[...] = jnp.zeros(ct_scr.shape, F32)
        n_scr[...] = jnp.zeros(n_scr.shape, F32)
        m_scr[...] = jnp.full(m_scr.shape, -jnp.inf, F32)
        carry_scr[...] = jnp.zeros(carry_scr.shape, F32)

    @pl.when(c > 0)
    def _():
        xext[0:8, :] = xext[L:L + 8, :]

    xext[8:8 + L, :] = qk_ref[0].astype(F32)
    cw = cw_ref[...]
    conv = (cb_ref[...] + cw[3:4, :] * xext[8:8 + L, :] + cw[2:3, :] * xext[7:7 + L, :]
            + cw[1:2, :] * xext[6:6 + L, :] + cw[0:1, :] * xext[5:5 + L, :])
    qk = conv * _sigmoid(conv)

    misc = misc_ref[0]
    ls = _log_sigmoid(misc)
    row = lax.broadcasted_iota(jnp.int32, (L, L), 0)
    col = lax.broadcasted_iota(jnp.int32, (L, L), 1)
    causal = row >= col
    tri = jnp.where(causal, 1.0, 0.0).astype(F32)
    cs = _dot(tri, ls, HIGHEST)
    cs_t = cs.T
    misc_t = misc.T
    carry = carry_scr[0:1, :]
    fcol_ref[0] = cs + carry
    for h in range(FOX_HEADS):
        frow_ref[0, h:h + 1, :] = cs_t[M_FXF + h:M_FXF + h + 1, :] + carry[:, M_FXF + h:M_FXF + h + 1]
    frow_ref[0, 4:8, :] = jnp.zeros((4, L), F32)
    carry_scr[0:1, :] = carry + cs[L - 1:L, :]

    for h in range(ML_HEADS):
        qh = qk[:, h * ML_QK:(h + 1) * ML_QK] * (ML_QK ** -0.5)
        kh = qk[:, nqk + h * ML_QK:nqk + (h + 1) * ML_QK]
        vb = v_ref[0, :, h * ML_V:(h + 1) * ML_V].astype(BF16)
        i_col = misc[:, M_MLI + h:M_MLI + h + 1]
        i_row = misc_t[M_MLI + h:M_MLI + h + 1, :]
        b_col = cs[:, M_MLF + h:M_MLF + h + 1]
        b_row = cs_t[M_MLF + h:M_MLF + h + 1, :]
        b_last = b_col[L - 1:L, :]
        m_prev = m_scr[h:h + 1, 0:1]

        d_log = jnp.where(causal, b_col - b_row + i_row, -jnp.inf)
        inter_log = b_col + m_prev
        m_out = jnp.maximum(inter_log, jnp.max(d_log, axis=-1, keepdims=True))
        qb = qh.astype(BF16)
        kb = kh.astype(BF16)
        s = _dot_nt(qb, kb) * jnp.exp(d_log - m_out)
        a_inter = jnp.exp(inter_log - m_out)
        ct = ct_scr[h]
        n_row = n_scr[h]
        num = _dot(s.astype(BF16), vb) + a_inter * _dot(qb, ct.astype(BF16))
        den = jnp.sum(s, axis=-1, keepdims=True) + a_inter * jnp.sum(qh * n_row, axis=-1, keepdims=True)
        hid = num / jnp.maximum(jnp.abs(den), jnp.exp(-m_out))

        w_state = b_last - b_col + i_col
        m_loc = jnp.max(w_state, axis=0, keepdims=True)
        ke = kh * jnp.exp(w_state - m_loc)
        c_loc = _dot_tn(ke.astype(BF16), vb)
        n_loc = jnp.sum(ke, axis=0, keepdims=True)
        m_new = jnp.maximum(b_last + m_prev, m_loc)
        decay = jnp.exp(b_last + m_prev - m_new)
        scale = jnp.exp(m_loc - m_new)
        ct_scr[h] = decay * ct + scale * c_loc
        n_scr[h] = decay * n_row + scale * n_loc
        m_scr[h:h + 1, :] = jnp.broadcast_to(m_new, (1, LANE))

        y = (_rms(hid) * g_ref[:, h * ML_V:(h + 1) * ML_V]
             * _sigmoid(o_ref[0, :, h * ML_V:(h + 1) * ML_V].astype(F32)))
        y_ref[0, :, h * ML_V:(h + 1) * ML_V] = y.astype(BF16)


def _mlstm(proj, misc, conv_w, conv_b, norm_g):
    bsz, s, _ = proj.shape
    L = min(256, s)
    w = 2 * ML_HEADS * ML_QK
    wv = ML_HEADS * ML_V
    return pl.pallas_call(
        functools.partial(_mlstm_kernel, L=L),
        out_shape=(jax.ShapeDtypeStruct((bsz, s, wv), BF16),
                   jax.ShapeDtypeStruct((bsz, s, LANE), F32),
                   jax.ShapeDtypeStruct((bsz, 8, s), F32)),
        grid=(bsz, s // L),
        in_specs=[pl.BlockSpec((1, L, w), lambda b, c: (b, c, C_MLQK // w)),
                  pl.BlockSpec((1, L, wv), lambda b, c: (b, c, C_MLV // wv)),
                  pl.BlockSpec((1, L, wv), lambda b, c: (b, c, C_MLO // wv)),
                  pl.BlockSpec((1, L, LANE), lambda b, c: (b, c, 0)),
                  pl.BlockSpec((ML_CONV, w), lambda b, c: (0, 0)),
                  pl.BlockSpec((1, w), lambda b, c: (0, 0)),
                  pl.BlockSpec((1, wv), lambda b, c: (0, 0))],
        out_specs=(pl.BlockSpec((1, L, wv), lambda b, c: (b, c, 0)),
                   pl.BlockSpec((1, L, LANE), lambda b, c: (b, c, 0)),
                   pl.BlockSpec((1, 8, L), lambda b, c: (b, 0, c))),
        scratch_shapes=[pltpu.VMEM((L + 8, w), F32),
                        pltpu.VMEM((ML_HEADS, ML_QK, ML_V), F32),
                        pltpu.VMEM((ML_HEADS, 1, ML_QK), F32),
                        pltpu.VMEM((8, LANE), F32),
                        pltpu.VMEM((8, LANE), F32)],
        compiler_params=_params(("parallel", "arbitrary")),
        name="mlstm",
    )(proj, proj, proj, misc, conv_w, conv_b.reshape(1, w), norm_g.reshape(1, wv))


def _fox_kernel(q_ref, k_ref, v_ref, fcol_ref, frow_ref, y_ref, k_scr, vt_scr, qt_scr, *acc, T):
    qi = pl.program_id(1)
    d = FOX_DIM

    @pl.when(qi == 0)
    def _():
        for h in range(FOX_HEADS):
            k_scr[h] = k_ref[0, :, h * d:(h + 1) * d].astype(BF16)
            vt_scr[h] = v_ref[0, :, h * d:(h + 1) * d].astype(F32).T.astype(BF16)

    q_t = (q_ref[0].astype(F32) * (d ** -0.5)).T
    for h in range(FOX_HEADS):
        qt_scr[h] = q_t[h * d:(h + 1) * d, :].astype(BF16)
        acc[h][...] = jnp.zeros((d, T), F32)
    qpos = qi * T + lax.broadcasted_iota(jnp.int32, (1, T), 1)

    def tile(j, carry, masked):
        off = pl.multiple_of(j * T, T)
        new = []
        for h in range(FOX_HEADS):
            m_old, l_old = carry[2 * h], carry[2 * h + 1]
            s = _dot(k_scr[h, pl.ds(off, T), :], qt_scr[h])
            fk = fcol_ref[0, pl.ds(off, T), M_FXF + h:M_FXF + h + 1]
            s = s + (frow_ref[0, h:h + 1, :] - fk)
            if masked:
                kpos = off + lax.broadcasted_iota(jnp.int32, (T, 1), 0)
                s = jnp.where(kpos <= qpos, s, -jnp.inf)
            m_new = jnp.maximum(m_old, jnp.max(s, axis=0, keepdims=True))
            alpha = jnp.exp(m_old - m_new)
            p = jnp.exp(s - m_new)
            new.append(m_new)
            new.append(alpha * l_old + jnp.sum(p, axis=0, keepdims=True))
            acc[h][...] = alpha * acc[h][...] + _dot(vt_scr[h, :, pl.ds(off, T)], p.astype(BF16))
        return tuple(new)

    init = (jnp.full((1, T), -jnp.inf, F32), jnp.zeros((1, T), F32)) * FOX_HEADS
    carry = lax.fori_loop(0, qi, lambda j, c: tile(j, c, False), init)
    carry = tile(qi, carry, True)
    for h in range(FOX_HEADS):
        y_ref[0, :, h * d:(h + 1) * d] = (acc[h][...] / carry[2 * h + 1]).T.astype(BF16)


def _fox(proj, fcol, frow):
    bsz, s, _ = proj.shape
    T = min(256, s)
    w = FOX_HEADS * FOX_DIM
    return pl.pallas_call(
        functools.partial(_fox_kernel, T=T),
        out_shape=jax.ShapeDtypeStruct((bsz, s, w), BF16),
        grid=(bsz, s // T),
        in_specs=[pl.BlockSpec((1, T, w), lambda b, i: (b, i, C_FOX // w)),
                  pl.BlockSpec((1, s, w), lambda b, i: (b, 0, C_FOX // w + 1)),
                  pl.BlockSpec((1, s, w), lambda b, i: (b, 0, C_FOX // w + 2)),
                  pl.BlockSpec((1, s, LANE), lambda b, i: (b, 0, 0)),
                  pl.BlockSpec((1, 8, T), lambda b, i: (b, 0, i))],
        out_specs=pl.BlockSpec((1, T, w), lambda b, i: (b, i, 0)),
        scratch_shapes=[pltpu.VMEM((FOX_HEADS, s, FOX_DIM), BF16),
                        pltpu.VMEM((FOX_HEADS, FOX_DIM, s), BF16),
                        pltpu.VMEM((FOX_HEADS, FOX_DIM, T), BF16)]
        + [pltpu.VMEM((FOX_DIM, T), F32) for _ in range(FOX_HEADS)],
        compiler_params=_params(("parallel", "arbitrary")),
        name="fox_attention",
    )(proj, proj, proj, fcol, frow)


def _dsa_body(ext, qi, q_ref, qidx_ref, misc_q_ref, wuk_ref, wuv_ref, y_ref,
              ckvn_scr, ckvnt_scr, kidx_scr, sel_scr, *, T, n_sel, rank_tile):
    dh = DSA_DIM
    q_t = q_ref[0].astype(F32).T
    qidx_t = qidx_ref[0].astype(F32).T.astype(BF16)
    w_t = misc_q_ref[0].T[M_IW:M_IW + IDX_HEADS, :] * (IDX_HEADS ** -0.5)
    kidx = kidx_scr[0:ext, :]
    score = jnp.zeros((ext, T), F32)
    for h in range(IDX_HEADS):
        lg = _dot(kidx, qidx_t[h * IDX_DIM:(h + 1) * IDX_DIM, :]) * (IDX_DIM ** -0.5)
        score = score + w_t[h:h + 1, :] * jnp.maximum(lg, 0.0)
    kpos = lax.broadcasted_iota(jnp.int32, (ext, 1), 0)
    qchunk = (qi * T + lax.broadcasted_iota(jnp.int32, (1, T), 1)) // CHUNK
    score = jnp.where((kpos // CHUNK) <= qchunk, score, -jnp.inf)

    def as_float(c):
        return pltpu.bitcast(jnp.where(c < 0, c ^ jnp.int32(0x7FFFFFFF), c), F32)

    def count_ge(c):
        return jnp.sum(jnp.where(score >= as_float(c), 1.0, 0.0), axis=0, keepdims=True)

    t0 = jnp.where(count_ge(jnp.zeros((1, T), jnp.int32)) >= n_sel, jnp.int32(0), jnp.int32(INT_MIN))

    def bis(i, t):
        cand = t + jnp.left_shift(jnp.int32(1), jnp.int32(30) - i)
        return jnp.where(count_ge(cand) >= n_sel, cand, t)

    t = lax.fori_loop(0, 31, bis, t0)
    n_vis = ((qchunk + 1) * CHUNK).astype(F32)
    thr = jnp.where(n_vis <= n_sel, -3.0e38, as_float(t))
    need = n_sel - jnp.sum(jnp.where(score > thr, 1.0, 0.0), axis=0, keepdims=True)
    r_i = lax.broadcasted_iota(jnp.int32, (rank_tile, rank_tile), 0)
    c_i = lax.broadcasted_iota(jnp.int32, (rank_tile, rank_tile), 1)
    lower = jnp.where(c_i < r_i, 1.0, 0.0).astype(BF16)
    carry = jnp.zeros((1, T), F32)
    for j in range(ext // rank_tile):
        sc = score[j * rank_tile:(j + 1) * rank_tile, :]
        eq = jnp.where(sc == thr, 1.0, 0.0)
        rank = _dot(lower, eq.astype(BF16)) + carry
        carry = carry + jnp.sum(eq, axis=0, keepdims=True)
        sel_scr[j * rank_tile:(j + 1) * rank_tile, :] = jnp.where(
            sc > thr, 1.0, jnp.where(rank < need, eq, 0.0))

    sel = sel_scr[0:ext, :] > 0.5
    ckvn = ckvn_scr[0:ext, :]
    ckvn_t = ckvnt_scr[:, 0:ext]
    for h in range(DSA_HEADS):
        qa_t = _dot(wuk_ref[h], q_t[h * dh:(h + 1) * dh, :].astype(BF16)) * (dh ** -0.5)
        lg = _dot(ckvn, qa_t.astype(BF16))
        lg = jnp.where(sel, lg, -jnp.inf)
        p = jnp.exp(lg - jnp.max(lg, axis=0, keepdims=True))
        l = jnp.sum(p, axis=0, keepdims=True)
        lat_t = _dot(ckvn_t, p.astype(BF16)) / l
        out_t = _dot(wuv_ref[h], lat_t.astype(BF16))
        y_ref[0, :, h * dh:(h + 1) * dh] = out_t.T.astype(BF16)


def _dsa_kernel(q_ref, ckv_ref, qidx_ref, misc_all_ref, misc_q_ref, g_ref, wuk_ref, wuv_ref, y_ref,
                ckvn_scr, ckvnt_scr, kidx_scr, sel_scr, *, T, S, n_sel, n_cls, rank_tile):
    qi = pl.program_id(1)

    @pl.when(qi == 0)
    def _():
        ckvn = _rms(ckv_ref[0].astype(F32)) * g_ref[...]
        ckvn_scr[...] = ckvn.astype(BF16)
        ckvnt_scr[...] = ckvn.T.astype(BF16)
        kidx_scr[...] = misc_all_ref[0, :, M_IK:M_IK + IDX_DIM].astype(BF16)

    per = (S // T) // n_cls
    for c in range(n_cls):
        ext = (c + 1) * per * T

        @pl.when(qi // per == c)
        def _(ext=ext):
            _dsa_body(ext, qi, q_ref, qidx_ref, misc_q_ref, wuk_ref, wuv_ref, y_ref,
                      ckvn_scr, ckvnt_scr, kidx_scr, sel_scr, T=T, n_sel=n_sel, rank_tile=rank_tile)


def _dsa(proj, misc, kv_g, wuk, wuv_t):
    bsz, s, _ = proj.shape
    T = min(256, s)
    n_sel = min(DSA_TOPK, s // 4)
    n_cls = max(1, min(4, s // 512))
    rank_tile = 256
    w = DSA_HEADS * DSA_DIM
    wi = IDX_HEADS * IDX_DIM
    return pl.pallas_call(
        functools.partial(_dsa_kernel, T=T, S=s, n_sel=float(n_sel), n_cls=n_cls, rank_tile=rank_tile),
        out_shape=jax.ShapeDtypeStruct((bsz, s, w), BF16),
        grid=(bsz, s // T),
        in_specs=[pl.BlockSpec((1, T, w), lambda b, i: (b, i, C_DQ // w)),
                  pl.BlockSpec((1, s, DSA_LATENT), lambda b, i: (b, 0, C_CKV // DSA_LATENT)),
                  pl.BlockSpec((1, T, wi), lambda b, i: (b, i, C_DIQ // wi)),
                  pl.BlockSpec((1, s, LANE), lambda b, i: (b, 0, 0)),
                  pl.BlockSpec((1, T, LANE), lambda b, i: (b, i, 0)),
                  pl.BlockSpec((1, DSA_LATENT), lambda b, i: (0, 0)),
                  pl.BlockSpec((DSA_HEADS, DSA_LATENT, DSA_DIM), lambda b, i: (0, 0, 0)),
                  pl.BlockSpec((DSA_HEADS, DSA_DIM, DSA_LATENT), lambda b, i: (0, 0, 0))],
        out_specs=pl.BlockSpec((1, T, w), lambda b, i: (b, i, 0)),
        scratch_shapes=[pltpu.VMEM((s, DSA_LATENT), BF16),
                        pltpu.VMEM((DSA_LATENT, s), BF16),
                        pltpu.VMEM((s, IDX_DIM), BF16),
                        pltpu.VMEM((s, T), F32)],
        compiler_params=_params(("parallel", "arbitrary")),
        name="dsa_attention",
    )(proj, proj, proj, misc, misc, kv_g.reshape(1, DSA_LATENT), wuk, wuv_t)


def _merge_kernel(yml_ref, ydsa_ref, yfox_ref, g0_ref, g1_ref, g2_ref, x_ref, mod_ref,
                  wml_ref, wdsa_ref, wfox_ref, wout_ref, o_ref):
    merged = (_sigmoid(g0_ref[0].astype(F32)) * _dot(yml_ref[0], wml_ref[...])
              + _sigmoid(g1_ref[0].astype(F32)) * _dot(ydsa_ref[0], wdsa_ref[...])
              + _sigmoid(g2_ref[0].astype(F32)) * _dot(yfox_ref[0], wfox_ref[...]))
    out = _dot(merged.astype(BF16), wout_ref[...])
    o_ref[0] = x_ref[0] + mod_ref[0, 2:3, :] * out


def _merge(x, proj, y_ml, y_dsa, y_fox, mod, w_ml, w_dsa, w_fox, w_out):
    bsz, s, d = x.shape
    tm = min(512, s)
    wb = y_ml.shape[-1]
    gb = C_GATE // d
    yspec = pl.BlockSpec((1, tm, wb), lambda b, i: (b, i, 0))
    wspec = pl.BlockSpec((wb, d), lambda b, i: (0, 0))
    return pl.pallas_call(
        _merge_kernel,
        out_shape=jax.ShapeDtypeStruct((bsz, s, d), F32),
        grid=(bsz, s // tm),
        in_specs=[yspec, yspec, yspec,
                  pl.BlockSpec((1, tm, d), lambda b, i: (b, i, gb)),
                  pl.BlockSpec((1, tm, d), lambda b, i: (b, i, gb + 1)),
                  pl.BlockSpec((1, tm, d), lambda b, i: (b, i, gb + 2)),
                  pl.BlockSpec((1, tm, d), lambda b, i: (b, i, 0)),
                  pl.BlockSpec((1, 6, d), lambda b, i: (b, 0, 0)),
                  wspec, wspec, wspec,
                  pl.BlockSpec((d, d), lambda b, i: (0, 0))],
        out_specs=pl.BlockSpec((1, tm, d), lambda b, i: (b, i, 0)),
        compiler_params=_params(("parallel", "parallel")),
        name="merge_out",
    )(y_ml, y_dsa, y_fox, proj, proj, proj, x, mod, w_ml, w_dsa, w_fox, w_out)


def _router_kernel(x_ref, mod_ref, wr_ref, br_ref, h_ref, route_ref, cnt_ref, carry_scr, *, T):
    first = (pl.program_id(0) == 0) & (pl.program_id(1) == 0)

    @pl.when(first)
    def _():
        carry_scr[...] = jnp.zeros(carry_scr.shape, F32)

    h = _rms(x_ref[0]) * (1.0 + mod_ref[0, 4:5, :]) + mod_ref[0, 3:4, :]
    h_ref[0] = h
    lg = _dot(h, wr_ref[...], HIGHEST) + br_ref[...]
    lane = lax.broadcasted_iota(jnp.int32, (T, LANE), 1).astype(F32)
    vals, hots = [], []
    for _ in range(TOP_K):
        mx = jnp.max(lg, axis=-1, keepdims=True)
        idx = jnp.min(jnp.where(lg == mx, lane, float(LANE)), axis=-1, keepdims=True)
        hot = lane == idx
        vals.append(mx)
        hots.append(hot)
        lg = jnp.where(hot, -jnp.inf, lg)
    exps = [jnp.exp(v - vals[0]) for v in vals]
    tot = exps[0] + exps[1] + exps[2] + exps[3]
    multi = jnp.zeros((T, LANE), F32)
    for hot in hots:
        multi = multi + jnp.where(hot, 1.0, 0.0)
    r_i = lax.broadcasted_iota(jnp.int32, (T, T), 0)
    c_i = lax.broadcasted_iota(jnp.int32, (T, T), 1)
    lower = jnp.where(c_i < r_i, 1.0, 0.0).astype(BF16)
    carry = carry_scr[0:1, :]
    before = _dot(lower, multi.astype(BF16)) + carry
    carry = carry + jnp.sum(multi, axis=0, keepdims=True)
    carry_scr[0:1, :] = carry
    cnt_ref[...] = jnp.broadcast_to(carry, (8, LANE))
    slab = jnp.zeros((T, LANE), F32)
    for k in range(TOP_K):
        e_k = jnp.sum(jnp.where(hots[k], lane, 0.0), axis=-1, keepdims=True)
        r_k = jnp.sum(jnp.where(hots[k], before, 0.0), axis=-1, keepdims=True)
        slab = slab + jnp.where(lane == float(k), e_k, 0.0)
        slab = slab + jnp.where(lane == float(TOP_K + k), r_k, 0.0)
        slab = slab + jnp.where(lane == float(2 * TOP_K + k), exps[k] / tot, 0.0)
    route_ref[0] = slab


def _router(x, mod, w_router, b_router):
    bsz, s, d = x.shape
    T = min(512, s)
    ne = w_router.shape[1]
    wr = jnp.zeros((d, LANE), F32).at[:, :ne].set(w_router)
    br = jnp.full((1, LANE), -1e30, F32).at[0, :ne].set(b_router)
    return pl.pallas_call(
        functools.partial(_router_kernel, T=T),
        out_shape=(jax.ShapeDtypeStruct((bsz, s, d), F32),
                   jax.ShapeDtypeStruct((bsz, s, LANE), F32),
                   jax.ShapeDtypeStruct((8, LANE), F32)),
        grid=(bsz, s // T),
        in_specs=[pl.BlockSpec((1, T, d), lambda b, i: (b, i, 0)),
                  pl.BlockSpec((1, 6, d), lambda b, i: (b, 0, 0)),
                  pl.BlockSpec((d, LANE), lambda b, i: (0, 0)),
                  pl.BlockSpec((1, LANE), lambda b, i: (0, 0))],
        out_specs=(pl.BlockSpec((1, T, d), lambda b, i: (b, i, 0)),
                   pl.BlockSpec((1, T, LANE), lambda b, i: (b, i, 0)),
                   pl.BlockSpec((8, LANE), lambda b, i: (0, 0))),
        scratch_shapes=[pltpu.VMEM((8, LANE), F32)],
        compiler_params=_params(("arbitrary", "arbitrary")),
        name="moe_router",
    )(x, mod, wr, br)


def _expert_kernel(be_ref, nu_ref, tok_cur_ref, tok_nxt_ref, dst_prev_ref, h_hbm,
                   w1_ref, b1_ref, w2_ref, b2_ref, yk_hbm,
                   w1b, w2b, xbuf0, xbuf1, obuf0, obuf1, gsem, ssem, zsem, *, BM):
    i = pl.program_id(0)
    nu = nu_ref[0]
    de = w2_ref.shape[0]
    xbuf = (xbuf0, xbuf1)
    obuf = (obuf0, obuf1)

    def gather_start(tok_ref, s):
        for r in range(BM):
            pltpu.make_async_copy(h_hbm.at[pl.ds(tok_ref[r], 1)], xbuf[s].at[pl.ds(r, 1)], gsem.at[s]).start()

    def gather_wait(s):
        pltpu.make_async_copy(h_hbm.at[pl.ds(0, BM)], xbuf[s], gsem.at[s]).wait()

    def scatter_start(dst_ref, s):
        for r in range(BM):
            pltpu.make_async_copy(obuf[s].at[pl.ds(r, 1)], yk_hbm.at[pl.ds(dst_ref[r], 1)], ssem.at[s]).start()

    def scatter_wait(s):
        pltpu.make_async_copy(obuf[s], yk_hbm.at[pl.ds(0, BM)], ssem.at[s]).wait()

    @pl.when(i == 0)
    def _():
        obuf1[...] = jnp.zeros(obuf1.shape, F32)
        n_real = yk_hbm.shape[0] - 2 * BM
        for part in range(2):
            cp = pltpu.make_async_copy(obuf1, yk_hbm.at[pl.ds(n_real + part * BM, BM)], zsem)
            cp.start()
            cp.wait()
        gather_start(tok_cur_ref, 0)

    prev = be_ref[jnp.maximum(i - 1, 0)]
    fresh = (i == 0) | (be_ref[i] != prev)

    @pl.when(fresh & (i < nu))
    def _():
        w1b[...] = w1_ref[...].astype(BF16)
        w2b[...] = w2_ref[...].astype(BF16)

    for s in range(2):
        o = 1 - s
        mine = (i % 2) == s

        @pl.when(mine & (i <= nu))
        def _(s=s):
            gather_wait(s)

        @pl.when(mine & (i >= 1) & (i <= nu))
        def _(s=s):
            scatter_wait(s)

        @pl.when(mine & (i < nu))
        def _(s=s, o=o):
            gather_start(tok_nxt_ref, o)
            scatter_start(dst_prev_ref, o)
            hdn = _dot(xbuf[s][...].astype(BF16), w1b[...]) + b1_ref[...]
            glu = jnp.minimum(hdn[:, :de], SWIGLU_LIMIT)
            lin = jnp.clip(hdn[:, de:], -SWIGLU_LIMIT, SWIGLU_LIMIT)
            act = glu * _sigmoid(SWIGLU_ALPHA * glu) * (lin + 1.0)
            obuf[s][...] = _dot(act.astype(BF16), w2b[...]) + b2_ref[...]

        @pl.when(mine & (i == nu))
        def _(o=o):
            scatter_start(dst_prev_ref, o)
            scatter_wait(o)


def _experts(blk_expert, n_used, row_tok, row_dst, h2, w1, b1, w2, b2, layer, bm, n_blocks):
    n, d = h2.shape
    depth, ne, _, dh2 = w1.shape
    de = w2.shape[2]

    def w_map(i, be, nu):
        return (layer, be[i], 0, 0)

    smem = functools.partial(pl.BlockSpec, (bm,), memory_space=pltpu.SMEM)
    return pl.pallas_call(
        functools.partial(_expert_kernel, BM=bm),
        out_shape=jax.ShapeDtypeStruct((TOP_K * n + 2 * bm, d), F32),
        grid_spec=pltpu.PrefetchScalarGridSpec(
            num_scalar_prefetch=2,
            grid=(n_blocks + 1,),
            in_specs=[smem(index_map=lambda i, be, nu: (i,)),
                      smem(index_map=lambda i, be, nu: (i + 1,)),
                      smem(index_map=lambda i, be, nu: (i,)),
                      pl.BlockSpec(memory_space=pl.ANY),
                      pl.BlockSpec((None, None, d, dh2), w_map),
                      pl.BlockSpec((None, None, 1, dh2), w_map),
                      pl.BlockSpec((None, None, de, d), w_map),
                      pl.BlockSpec((None, None, 1, d), w_map)],
            out_specs=pl.BlockSpec(memory_space=pl.ANY),
            scratch_shapes=[pltpu.VMEM((d, dh2), BF16), pltpu.VMEM((de, d), BF16)]
            + [pltpu.VMEM((bm, d), F32) for _ in range(4)]
            + [pltpu.SemaphoreType.DMA((2,)), pltpu.SemaphoreType.DMA((2,)), pltpu.SemaphoreType.DMA(())]),
        compiler_params=pltpu.CompilerParams(dimension_semantics=("arbitrary",), vmem_limit_bytes=VMEM_LIMIT,
                                             disable_bounds_checks=True),
        name="moe_experts",
    )(blk_expert, n_used, row_tok, row_tok, row_dst, h2,
      w1, b1.reshape(depth, ne, 1, dh2), w2, b2.reshape(depth, ne, 1, d))


def _combine_kernel(y0_ref, y1_ref, y2_ref, y3_ref, x_ref, route_ref, mod_ref, fg_ref, o_ref, *, final):
    route = route_ref[0]
    y = jnp.zeros(x_ref.shape[1:], F32)
    for k, y_ref in enumerate((y0_ref, y1_ref, y2_ref, y3_ref)):
        y = y + route[:, 2 * TOP_K + k:2 * TOP_K + k + 1] * y_ref[...]
    out = x_ref[0] + mod_ref[0, 5:6, :] * y
    if final:
        out = _rms(out) * fg_ref[...]
    o_ref[0] = out


def _combine(yk, x, route, mod, final_g, final):
    bsz, s, d = x.shape
    tc = min(512, s)
    nt = s // tc
    per_slot = bsz * nt
    yspecs = [pl.BlockSpec((tc, d), functools.partial(lambda b, i, k: (k * per_slot + b * nt + i, 0), k=k))
              for k in range(TOP_K)]
    return pl.pallas_call(
        functools.partial(_combine_kernel, final=final),
        out_shape=jax.ShapeDtypeStruct((bsz, s, d), F32),
        grid=(bsz, nt),
        in_specs=yspecs + [pl.BlockSpec((1, tc, d), lambda b, i: (b, i, 0)),
                           pl.BlockSpec((1, tc, LANE), lambda b, i: (b, i, 0)),
                           pl.BlockSpec((1, 6, d), lambda b, i: (b, 0, 0)),
                           pl.BlockSpec((1, d), lambda b, i: (0, 0))],
        out_specs=pl.BlockSpec((1, tc, d), lambda b, i: (b, i, 0)),
        compiler_params=_params(("parallel", "parallel")),
        name="moe_combine",
    )(yk, yk, yk, yk, x, route, mod, final_g.reshape(1, d))


def _rearranged_in_proj(w_in, b_in):
    sizes = (ML_HEADS * ML_QK, ML_HEADS * ML_QK, ML_HEADS * ML_V, ML_HEADS * ML_V, ML_HEADS, ML_HEADS,
             DSA_HEADS * DSA_DIM, DSA_LATENT, IDX_HEADS * IDX_DIM, IDX_DIM, IDX_HEADS,
             3 * FOX_HEADS * FOX_DIM, FOX_HEADS, N_BRANCH * w_in.shape[0])
    offs = [0]
    for sz in sizes:
        offs.append(offs[-1] + sz)
    (o_mq, o_mk, o_mv, o_mo, o_mi, o_mf, o_dq, o_ckv, o_iq, o_ik, o_iw, o_fx, o_ff, o_g, o_end) = offs
    pad = LANE - (IDX_DIM + 2 * ML_HEADS + IDX_HEADS + FOX_HEADS)

    def cols(a):
        parts = [a[..., o_mq:o_mi],
                 a[..., o_dq:o_ckv],
                 a[..., o_iq:o_ik],
                 a[..., o_ckv:o_iq],
                 a[..., o_ik:o_iw],
                 a[..., o_mi:o_dq],
                 a[..., o_iw:o_fx],
                 a[..., o_ff:o_g],
                 jnp.zeros(a.shape[:-1] + (pad,), a.dtype),
                 a[..., o_fx:o_ff],
                 a[..., o_g:o_end]]
        return jnp.concatenate(parts, axis=-1)

    return cols(w_in).astype(BF16), cols(b_in.reshape(1, -1))


def _moe_plan(route, counts, bm, n_blocks):
    n = route.shape[0]
    nk = n * TOP_K
    n_rows = n_blocks * bm
    e = route[:, 0:TOP_K].astype(jnp.int32)
    rank = route[:, TOP_K:2 * TOP_K].astype(jnp.int32)
    cnt = counts[0, :N_EXPERTS].astype(jnp.int32)
    padded = (cnt + bm - 1) // bm * bm
    pad_end = jnp.cumsum(padded)
    pad_start = pad_end - padded
    dest = (pad_start[e] + rank).reshape(nk)
    pair = jnp.full((n_rows,), -1, jnp.int32).at[dest].set(jnp.arange(nk, dtype=jnp.int32))
    valid = pair >= 0
    tok = jnp.where(valid, pair // TOP_K, 0)
    row = jnp.arange(n_rows, dtype=jnp.int32)
    spare = nk + ((row // bm) % 2) * bm + row % bm
    dst = jnp.where(valid, (pair % TOP_K) * n + tok, spare)
    row_tok = jnp.concatenate([tok, jnp.zeros((2 * bm,), jnp.int32)])
    row_dst = jnp.concatenate([nk + bm + jnp.arange(bm, dtype=jnp.int32), dst])
    blk_row = jnp.arange(n_blocks + 1, dtype=jnp.int32) * bm
    blk_expert = jnp.minimum(jnp.sum((pad_end[None, :] <= blk_row[:, None]).astype(jnp.int32), axis=1),
                             N_EXPERTS - 1)
    n_used = (pad_end[-1:] // bm).astype(jnp.int32)
    return blk_expert, n_used, row_tok, row_dst


def kernel(x, c, w_ada, b_ada, w_in, b_in, conv_w, conv_b, ml_norm_g, kv_norm_g, w_uk, w_uv,
           w_br_ml, w_br_dsa, w_br_fox, w_out, w_router, b_router, w1, b1, w2, b2, final_g):
    bsz, s, d = x.shape
    depth = w_in.shape[0]
    n = bsz * s
    bm = 512
    n_blocks = n * TOP_K // bm + N_EXPERTS
    mods = _ada_mod(c, w_ada, b_ada).reshape(depth, bsz, 6, d)
    for l in range(depth):
        mod = mods[l]
        w_r, b_r = _rearranged_in_proj(w_in[l], b_in[l])
        proj, misc = _in_proj(x, mod, w_r, b_r)
        y_ml, fcol, frow = _mlstm(proj, misc, conv_w[l], conv_b[l], ml_norm_g[l])
        y_fox = _fox(proj, fcol, frow)
        y_dsa = _dsa(proj, misc, kv_norm_g[l], w_uk[l].astype(BF16), jnp.swapaxes(w_uv[l], 1, 2).astype(BF16))
        x = _merge(x, proj, y_ml, y_dsa, y_fox, mod, w_br_ml[l].astype(BF16), w_br_dsa[l].astype(BF16),
                   w_br_fox[l].astype(BF16), w_out[l].astype(BF16))
        h2, route, counts = _router(x, mod, w_router[l], b_router[l])
        blk_expert, n_used, row_tok, row_dst = _moe_plan(route.reshape(n, LANE), counts, bm, n_blocks)
        yk = _experts(blk_expert, n_used, row_tok, row_dst, h2.reshape(n, d), w1, b1, w2, b2, l, bm, n_blocks)
        x = _combine(yk, x, route, mod, final_g, final=(l == depth - 1))
    return x
```

```python
import functools

import jax
import jax.numpy as jnp
from jax import lax
from jax.experimental import pallas as pl
from jax.experimental.pallas import tpu as pltpu

F32 = jnp.float32
BF16 = jnp.bfloat16
HIGHEST = lax.Precision.HIGHEST

EPS = 1e-6
CHUNK = 64

ML_HEADS, ML_QK, ML_V, ML_CONV = 4, 64, 128, 4
DSA_HEADS, DSA_DIM, DSA_LATENT = 4, 128, 128
IDX_HEADS, IDX_DIM, DSA_TOPK = 4, 64, 256
FOX_HEADS, FOX_DIM = 4, 128
N_BRANCH = 3
N_EXPERTS, TOP_K = 32, 4
SWIGLU_LIMIT, SWIGLU_ALPHA = 7.0, 1.702

LANE = 128
ROW_ALIGN = 8
INT_MIN = -2 ** 31

MOE_TILE = 256
N_RUNS = N_EXPERTS + 1
LOC_ROWS = TOP_K * MOE_TILE + MOE_TILE
ZERO_ROWS = 128

C_MLQK = 0
C_MLV = 512
C_MLO = 1024
C_DQ = 1536
C_DIQ = 2048
C_CKV = 2304
C_MISC = 2432
C_FOX = 2560
C_GATE = 4096
NP = 7168
M_IK, M_MLI, M_MLF, M_IW, M_FXF = 0, 64, 68, 72, 76

VMEM_LIMIT = 56 * 1024 * 1024


def _dot(a, b, prec=None):
    return jnp.dot(a, b, preferred_element_type=F32, precision=prec)


def _dot_nt(a, b, prec=None):
    return lax.dot_general(a, b, (((1,), (1,)), ((), ())), preferred_element_type=F32, precision=prec)


def _dot_tn(a, b):
    return lax.dot_general(a, b, (((0,), (0,)), ((), ())), preferred_element_type=F32)


def _sigmoid(x):
    return 1.0 / (1.0 + jnp.exp(-x))


def _log_sigmoid(x):
    return jnp.minimum(x, 0.0) - jnp.log1p(jnp.exp(-jnp.abs(x)))


def _rms(x):
    return x * lax.rsqrt(jnp.mean(x * x, axis=-1, keepdims=True) + EPS)


def _params(sem, vmem=VMEM_LIMIT):
    return pltpu.CompilerParams(dimension_semantics=sem, vmem_limit_bytes=vmem)


def _ada_kernel(c_ref, w_ref, b_ref, o_ref):
    c = c_ref[...]
    o_ref[0] = _dot(c * _sigmoid(c), w_ref[0], HIGHEST) + b_ref[0]


def _ada_mod(c, w_ada, b_ada):
    depth, d, n = w_ada.shape
    bsz = c.shape[0]
    tn = 1536
    return pl.pallas_call(
        _ada_kernel,
        out_shape=jax.ShapeDtypeStruct((depth, bsz, n), F32),
        grid=(depth, n // tn),
        in_specs=[pl.BlockSpec((bsz, d), lambda l, j: (0, 0)),
                  pl.BlockSpec((1, d, tn), lambda l, j: (l, 0, j)),
                  pl.BlockSpec((1, 1, tn), lambda l, j: (l, 0, j))],
        out_specs=pl.BlockSpec((1, bsz, tn), lambda l, j: (l, 0, j)),
        compiler_params=_params(("parallel", "parallel")),
        name="ada_mod",
    )(c, w_ada, b_ada.reshape(depth, 1, n))


def _inproj_kernel(x_ref, mod_ref, w_ref, b_ref, o_ref, misc_ref, h_scr, *, tn):
    j = pl.program_id(2)

    @pl.when(j == 0)
    def _():
        h = _rms(x_ref[0]) * (1.0 + mod_ref[0, 1:2, :]) + mod_ref[0, 0:1, :]
        h_scr[...] = h.astype(BF16)

    acc = _dot(h_scr[...], w_ref[...]) + b_ref[...]
    o_ref[0] = acc.astype(BF16)

    @pl.when(j == C_MISC // tn)
    def _():
        misc_ref[0] = acc[:, C_MISC % tn:C_MISC % tn + LANE]


def _in_proj(x, mod, w, b):
    bsz, s, d = x.shape
    n = w.shape[1]
    tm = min(1024, s)
    tn = 1024
    return pl.pallas_call(
        functools.partial(_inproj_kernel, tn=tn),
        out_shape=(jax.ShapeDtypeStruct((bsz, s, n), BF16), jax.ShapeDtypeStruct((bsz, s, LANE), F32)),
        grid=(bsz, s // tm, n // tn),
        in_specs=[pl.BlockSpec((1, tm, d), lambda bi, i, j: (bi, i, 0)),
                  pl.BlockSpec((1, 6, d), lambda bi, i, j: (bi, 0, 0)),
                  pl.BlockSpec((d, tn), lambda bi, i, j: (0, j)),
                  pl.BlockSpec((1, tn), lambda bi, i, j: (0, j))],
        out_specs=(pl.BlockSpec((1, tm, tn), lambda bi, i, j: (bi, i, j)),
                   pl.BlockSpec((1, tm, LANE), lambda bi, i, j: (bi, i, 0))),
        scratch_shapes=[pltpu.VMEM((tm, d), BF16)],
        compiler_params=_params(("parallel", "parallel", "arbitrary")),
        name="in_proj",
    )(x, mod, w, b)


def _mlstm_kernel(qk_ref, v_ref, o_ref, misc_ref, cw_ref, cb_ref, g_ref,
                  y_ref, fcol_ref, frow_ref,
                  xext, ct_scr, n_scr, m_scr, carry_scr, *, L):
    c = pl.program_id(1)
    nqk = ML_HEADS * ML_QK

    @pl.when(c == 0)
    def _():
        xext[0:8, :] = jnp.zeros((8, 2 * nqk), F32)
        ct_scr[...] = jnp.zeros(ct_scr.shape, F32)
        n_scr[...] = jnp.zeros(n_scr.shape, F32)
        m_scr[...] = jnp.full(m_scr.shape, -jnp.inf, F32)
        carry_scr[...] = jnp.zeros(carry_scr.shape, F32)

    @pl.when(c > 0)
    def _():
        xext[0:8, :] = xext[L:L + 8, :]

    xext[8:8 + L, :] = qk_ref[0].astype(F32)
    cw = cw_ref[...]
    conv = (cb_ref[...] + cw[3:4, :] * xext[8:8 + L, :] + cw[2:3, :] * xext[7:7 + L, :]
            + cw[1:2, :] * xext[6:6 + L, :] + cw[0:1, :] * xext[5:5 + L, :])
    qk = conv * _sigmoid(conv)

    misc = misc_ref[0]
    ls = _log_sigmoid(misc)
    row = lax.broadcasted_iota(jnp.int32, (L, L), 0)
    col = lax.broadcasted_iota(jnp.int32, (L, L), 1)
    causal = row >= col
    tri = jnp.where(causal, 1.0, 0.0).astype(F32)
    cs = _dot(tri, ls, HIGHEST)
    cs_t = cs.T
    misc_t = misc.T
    carry = carry_scr[0:1, :]
    fcol_ref[0] = cs + carry
    for h in range(FOX_HEADS):
        frow_ref[0, h:h + 1, :] = cs_t[M_FXF + h:M_FXF + h + 1, :] + carry[:, M_FXF + h:M_FXF + h + 1]
    frow_ref[0, 4:8, :] = jnp.zeros((4, L), F32)
    carry_scr[0:1, :] = carry + cs[L - 1:L, :]

    for h in range(ML_HEADS):
        qh = qk[:, h * ML_QK:(h + 1) * ML_QK] * (ML_QK ** -0.5)
        kh = qk[:, nqk + h * ML_QK:nqk + (h + 1) * ML_QK]
        vb = v_ref[0, :, h * ML_V:(h + 1) * ML_V].astype(BF16)
        i_col = misc[:, M_MLI + h:M_MLI + h + 1]
        i_row = misc_t[M_MLI + h:M_MLI + h + 1, :]
        b_col = cs[:, M_MLF + h:M_MLF + h + 1]
        b_row = cs_t[M_MLF + h:M_MLF + h + 1, :]
        b_last = b_col[L - 1:L, :]
        m_prev = m_scr[h:h + 1, 0:1]

        d_log = jnp.where(causal, b_col - b_row + i_row, -jnp.inf)
        inter_log = b_col + m_prev
        m_out = jnp.maximum(inter_log, jnp.max(d_log, axis=-1, keepdims=True))
        qb = qh.astype(BF16)
        kb = kh.astype(BF16)
        s = _dot_nt(qb, kb) * jnp.exp(d_log - m_out)
        a_inter = jnp.exp(inter_log - m_out)
        ct = ct_scr[h]
        n_row = n_scr[h]
        num = _dot(s.astype(BF16), vb) + a_inter * _dot(qb, ct.astype(BF16))
        den = jnp.sum(s, axis=-1, keepdims=True) + a_inter * jnp.sum(qh * n_row, axis=-1, keepdims=True)
        hid = num / jnp.maximum(jnp.abs(den), jnp.exp(-m_out))

        w_state = b_last - b_col + i_col
        m_loc = jnp.max(w_state, axis=0, keepdims=True)
        ke = kh * jnp.exp(w_state - m_loc)
        c_loc = _dot_tn(ke.astype(BF16), vb)
        n_loc = jnp.sum(ke, axis=0, keepdims=True)
        m_new = jnp.maximum(b_last + m_prev, m_loc)
        decay = jnp.exp(b_last + m_prev - m_new)
        scale = jnp.exp(m_loc - m_new)
        ct_scr[h] = decay * ct + scale * c_loc
        n_scr[h] = decay * n_row + scale * n_loc
        m_scr[h:h + 1, :] = jnp.broadcast_to(m_new, (1, LANE))

        y = (_rms(hid) * g_ref[:, h * ML_V:(h + 1) * ML_V]
             * _sigmoid(o_ref[0, :, h * ML_V:(h + 1) * ML_V].astype(F32)))
        y_ref[0, :, h * ML_V:(h + 1) * ML_V] = y.astype(BF16)


def _mlstm(proj, misc, conv_w, conv_b, norm_g):
    bsz, s, _ = proj.shape
    L = min(256, s)
    w = 2 * ML_HEADS * ML_QK
    wv = ML_HEADS * ML_V
    return pl.pallas_call(
        functools.partial(_mlstm_kernel, L=L),
        out_shape=(jax.ShapeDtypeStruct((bsz, s, wv), BF16),
                   jax.ShapeDtypeStruct((bsz, s, LANE), F32),
                   jax.ShapeDtypeStruct((bsz, 8, s), F32)),
        grid=(bsz, s // L),
        in_specs=[pl.BlockSpec((1, L, w), lambda b, c: (b, c, C_MLQK // w)),
                  pl.BlockSpec((1, L, wv), lambda b, c: (b, c, C_MLV // wv)),
                  pl.BlockSpec((1, L, wv), lambda b, c: (b, c, C_MLO // wv)),
                  pl.BlockSpec((1, L, LANE), lambda b, c: (b, c, 0)),
                  pl.BlockSpec((ML_CONV, w), lambda b, c: (0, 0)),
                  pl.BlockSpec((1, w), lambda b, c: (0, 0)),
                  pl.BlockSpec((1, wv), lambda b, c: (0, 0))],
        out_specs=(pl.BlockSpec((1, L, wv), lambda b, c: (b, c, 0)),
                   pl.BlockSpec((1, L, LANE), lambda b, c: (b, c, 0)),
                   pl.BlockSpec((1, 8, L), lambda b, c: (b, 0, c))),
        scratch_shapes=[pltpu.VMEM((L + 8, w), F32),
                        pltpu.VMEM((ML_HEADS, ML_QK, ML_V), F32),
                        pltpu.VMEM((ML_HEADS, 1, ML_QK), F32),
                        pltpu.VMEM((8, LANE), F32),
                        pltpu.VMEM((8, LANE), F32)],
        compiler_params=_params(("parallel", "arbitrary")),
        name="mlstm",
    )(proj, proj, proj, misc, conv_w, conv_b.reshape(1, w), norm_g.reshape(1, wv))


def _fox_kernel(q_ref, k_ref, v_ref, fcol_ref, frow_ref, y_ref, k_scr, vt_scr, qt_scr, *acc, T):
    qi = pl.program_id(1)
    d = FOX_DIM

    @pl.when(qi == 0)
    def _():
        for h in range(FOX_HEADS):
            k_scr[h] = k_ref[0, :, h * d:(h + 1) * d].astype(BF16)
            vt_scr[h] = v_ref[0, :, h * d:(h + 1) * d].astype(F32).T.astype(BF16)

    q_t = (q_ref[0].astype(F32) * (d ** -0.5)).T
    for h in range(FOX_HEADS):
        qt_scr[h] = q_t[h * d:(h + 1) * d, :].astype(BF16)
        acc[h][...] = jnp.zeros((d, T), F32)
    qpos = qi * T + lax.broadcasted_iota(jnp.int32, (1, T), 1)

    def tile(j, carry, masked):
        off = pl.multiple_of(j * T, T)
        new = []
        for h in range(FOX_HEADS):
            m_old, l_old = carry[2 * h], carry[2 * h + 1]
            s = _dot(k_scr[h, pl.ds(off, T), :], qt_scr[h])
            fk = fcol_ref[0, pl.ds(off, T), M_FXF + h:M_FXF + h + 1]
            s = s + (frow_ref[0, h:h + 1, :] - fk)
            if masked:
                kpos = off + lax.broadcasted_iota(jnp.int32, (T, 1), 0)
                s = jnp.where(kpos <= qpos, s, -jnp.inf)
            m_new = jnp.maximum(m_old, jnp.max(s, axis=0, keepdims=True))
            alpha = jnp.exp(m_old - m_new)
            p = jnp.exp(s - m_new)
            new.append(m_new)
            new.append(alpha * l_old + jnp.sum(p, axis=0, keepdims=True))
            acc[h][...] = alpha * acc[h][...] + _dot(vt_scr[h, :, pl.ds(off, T)], p.astype(BF16))
        return tuple(new)

    init = (jnp.full((1, T), -jnp.inf, F32), jnp.zeros((1, T), F32)) * FOX_HEADS
    carry = lax.fori_loop(0, qi, lambda j, c: tile(j, c, False), init)
    carry = tile(qi, carry, True)
    for h in range(FOX_HEADS):
        y_ref[0, :, h * d:(h + 1) * d] = (acc[h][...] / carry[2 * h + 1]).T.astype(BF16)


def _fox(proj, fcol, frow):
    bsz, s, _ = proj.shape
    T = min(256, s)
    w = FOX_HEADS * FOX_DIM
    return pl.pallas_call(
        functools.partial(_fox_kernel, T=T),
        out_shape=jax.ShapeDtypeStruct((bsz, s, w), BF16),
        grid=(bsz, s // T),
        in_specs=[pl.BlockSpec((1, T, w), lambda b, i: (b, i, C_FOX // w)),
                  pl.BlockSpec((1, s, w), lambda b, i: (b, 0, C_FOX // w + 1)),
                  pl.BlockSpec((1, s, w), lambda b, i: (b, 0, C_FOX // w + 2)),
                  pl.BlockSpec((1, s, LANE), lambda b, i: (b, 0, 0)),
                  pl.BlockSpec((1, 8, T), lambda b, i: (b, 0, i))],
        out_specs=pl.BlockSpec((1, T, w), lambda b, i: (b, i, 0)),
        scratch_shapes=[pltpu.VMEM((FOX_HEADS, s, FOX_DIM), BF16),
                        pltpu.VMEM((FOX_HEADS, FOX_DIM, s), BF16),
                        pltpu.VMEM((FOX_HEADS, FOX_DIM, T), BF16)]
        + [pltpu.VMEM((FOX_DIM, T), F32) for _ in range(FOX_HEADS)],
        compiler_params=_params(("parallel", "arbitrary")),
        name="fox_attention",
    )(proj, proj, proj, fcol, frow)


def _dsa_body(ext, qi, q_ref, qidx_ref, misc_q_ref, wuk_ref, wuv_ref, y_ref,
              ckvn_scr, ckvnt_scr, kidx_scr, sel_scr, *, T, n_sel, rank_tile):
    dh = DSA_DIM
    q_t = q_ref[0].astype(F32).T
    qidx_t = qidx_ref[0].astype(F32).T.astype(BF16)
    w_t = misc_q_ref[0].T[M_IW:M_IW + IDX_HEADS, :] * (IDX_HEADS ** -0.5)
    kidx = kidx_scr[0:ext, :]
    score = jnp.zeros((ext, T), F32)
    for h in range(IDX_HEADS):
        lg = _dot(kidx, qidx_t[h * IDX_DIM:(h + 1) * IDX_DIM, :]) * (IDX_DIM ** -0.5)
        score = score + w_t[h:h + 1, :] * jnp.maximum(lg, 0.0)
    kpos = lax.broadcasted_iota(jnp.int32, (ext, 1), 0)
    qchunk = (qi * T + lax.broadcasted_iota(jnp.int32, (1, T), 1)) // CHUNK
    score = jnp.where((kpos // CHUNK) <= qchunk, score, -jnp.inf)

    def as_float(c):
        return pltpu.bitcast(jnp.where(c < 0, c ^ jnp.int32(0x7FFFFFFF), c), F32)

    def count_ge(c):
        return jnp.sum(jnp.where(score >= as_float(c), 1.0, 0.0), axis=0, keepdims=True)

    t0 = jnp.where(count_ge(jnp.zeros((1, T), jnp.int32)) >= n_sel, jnp.int32(0), jnp.int32(INT_MIN))

    def bis(i, t):
        cand = t + jnp.left_shift(jnp.int32(1), jnp.int32(30) - i)
        return jnp.where(count_ge(cand) >= n_sel, cand, t)

    t = lax.fori_loop(0, 31, bis, t0)
    n_vis = ((qchunk + 1) * CHUNK).astype(F32)
    thr = jnp.where(n_vis <= n_sel, -3.0e38, as_float(t))
    need = n_sel - jnp.sum(jnp.where(score > thr, 1.0, 0.0), axis=0, keepdims=True)
    r_i = lax.broadcasted_iota(jnp.int32, (rank_tile, rank_tile), 0)
    c_i = lax.broadcasted_iota(jnp.int32, (rank_tile, rank_tile), 1)
    lower = jnp.where(c_i < r_i, 1.0, 0.0).astype(BF16)
    carry = jnp.zeros((1, T), F32)
    for j in range(ext // rank_tile):
        sc = score[j * rank_tile:(j + 1) * rank_tile, :]
        eq = jnp.where(sc == thr, 1.0, 0.0)
        rank = _dot(lower, eq.astype(BF16)) + carry
        carry = carry + jnp.sum(eq, axis=0, keepdims=True)
        sel_scr[j * rank_tile:(j + 1) * rank_tile, :] = jnp.where(
            sc > thr, 1.0, jnp.where(rank < need, eq, 0.0))

    sel = sel_scr[0:ext, :] > 0.5
    ckvn = ckvn_scr[0:ext, :]
    ckvn_t = ckvnt_scr[:, 0:ext]
    for h in range(DSA_HEADS):
        qa_t = _dot(wuk_ref[h], q_t[h * dh:(h + 1) * dh, :].astype(BF16)) * (dh ** -0.5)
        lg = _dot(ckvn, qa_t.astype(BF16))
        lg = jnp.where(sel, lg, -jnp.inf)
        p = jnp.exp(lg - jnp.max(lg, axis=0, keepdims=True))
        l = jnp.sum(p, axis=0, keepdims=True)
        lat_t = _dot(ckvn_t, p.astype(BF16)) / l
        out_t = _dot(wuv_ref[h], lat_t.astype(BF16))
        y_ref[0, :, h * dh:(h + 1) * dh] = out_t.T.astype(BF16)


def _dsa_kernel(q_ref, ckv_ref, qidx_ref, misc_all_ref, misc_q_ref, g_ref, wuk_ref, wuv_ref, y_ref,
                ckvn_scr, ckvnt_scr, kidx_scr, sel_scr, *, T, S, n_sel, n_cls, rank_tile):
    qi = pl.program_id(1)

    @pl.when(qi == 0)
    def _():
        ckvn = _rms(ckv_ref[0].astype(F32)) * g_ref[...]
        ckvn_scr[...] = ckvn.astype(BF16)
        ckvnt_scr[...] = ckvn.T.astype(BF16)
        kidx_scr[...] = misc_all_ref[0, :, M_IK:M_IK + IDX_DIM].astype(BF16)

    per = (S // T) // n_cls
    for c in range(n_cls):
        ext = (c + 1) * per * T

        @pl.when(qi // per == c)
        def _(ext=ext):
            _dsa_body(ext, qi, q_ref, qidx_ref, misc_q_ref, wuk_ref, wuv_ref, y_ref,
                      ckvn_scr, ckvnt_scr, kidx_scr, sel_scr, T=T, n_sel=n_sel, rank_tile=rank_tile)


def _dsa(proj, misc, kv_g, wuk, wuv_t):
    bsz, s, _ = proj.shape
    T = min(256, s)
    n_sel = min(DSA_TOPK, s // 4)
    n_cls = max(1, min(4, s // 512))
    rank_tile = 256
    w = DSA_HEADS * DSA_DIM
    wi = IDX_HEADS * IDX_DIM
    return pl.pallas_call(
        functools.partial(_dsa_kernel, T=T, S=s, n_sel=float(n_sel), n_cls=n_cls, rank_tile=rank_tile),
        out_shape=jax.ShapeDtypeStruct((bsz, s, w), BF16),
        grid=(bsz, s // T),
        in_specs=[pl.BlockSpec((1, T, w), lambda b, i: (b, i, C_DQ // w)),
                  pl.BlockSpec((1, s, DSA_LATENT), lambda b, i: (b, 0, C_CKV // DSA_LATENT)),
                  pl.BlockSpec((1, T, wi), lambda b, i: (b, i, C_DIQ // wi)),
                  pl.BlockSpec((1, s, LANE), lambda b, i: (b, 0, 0)),
                  pl.BlockSpec((1, T, LANE), lambda b, i: (b, i, 0)),
                  pl.BlockSpec((1, DSA_LATENT), lambda b, i: (0, 0)),
                  pl.BlockSpec((DSA_HEADS, DSA_LATENT, DSA_DIM), lambda b, i: (0, 0, 0)),
                  pl.BlockSpec((DSA_HEADS, DSA_DIM, DSA_LATENT), lambda b, i: (0, 0, 0))],
        out_specs=pl.BlockSpec((1, T, w), lambda b, i: (b, i, 0)),
        scratch_shapes=[pltpu.VMEM((s, DSA_LATENT), BF16),
                        pltpu.VMEM((DSA_LATENT, s), BF16),
                        pltpu.VMEM((s, IDX_DIM), BF16),
                        pltpu.VMEM((s, T), F32)],
        compiler_params=_params(("parallel", "arbitrary")),
        name="dsa_attention",
    )(proj, proj, proj, misc, misc, kv_g.reshape(1, DSA_LATENT), wuk, wuv_t)


def _merge_kernel(yml_ref, ydsa_ref, yfox_ref, g0_ref, g1_ref, g2_ref, x_ref, mod_ref,
                  wml_ref, wdsa_ref, wfox_ref, wout_ref, o_ref):
    merged = (_sigmoid(g0_ref[0].astype(F32)) * _dot(yml_ref[0], wml_ref[...])
              + _sigmoid(g1_ref[0].astype(F32)) * _dot(ydsa_ref[0], wdsa_ref[...])
              + _sigmoid(g2_ref[0].astype(F32)) * _dot(yfox_ref[0], wfox_ref[...]))
    out = _dot(merged.astype(BF16), wout_ref[...])
    o_ref[0] = x_ref[0] + mod_ref[0, 2:3, :] * out


def _merge(x, proj, y_ml, y_dsa, y_fox, mod, w_ml, w_dsa, w_fox, w_out):
    bsz, s, d = x.shape
    tm = min(512, s)
    wb = y_ml.shape[-1]
    gb = C_GATE // d
    yspec = pl.BlockSpec((1, tm, wb), lambda b, i: (b, i, 0))
    wspec = pl.BlockSpec((wb, d), lambda b, i: (0, 0))
    return pl.pallas_call(
        _merge_kernel,
        out_shape=jax.ShapeDtypeStruct((bsz, s, d), F32),
        grid=(bsz, s // tm),
        in_specs=[yspec, yspec, yspec,
                  pl.BlockSpec((1, tm, d), lambda b, i: (b, i, gb)),
                  pl.BlockSpec((1, tm, d), lambda b, i: (b, i, gb + 1)),
                  pl.BlockSpec((1, tm, d), lambda b, i: (b, i, gb + 2)),
                  pl.BlockSpec((1, tm, d), lambda b, i: (b, i, 0)),
                  pl.BlockSpec((1, 6, d), lambda b, i: (b, 0, 0)),
                  wspec, wspec, wspec,
                  pl.BlockSpec((d, d), lambda b, i: (0, 0))],
        out_specs=pl.BlockSpec((1, tm, d), lambda b, i: (b, i, 0)),
        compiler_params=_params(("parallel", "parallel")),
        name="merge_out",
    )(y_ml, y_dsa, y_fox, proj, proj, proj, x, mod, w_ml, w_dsa, w_fox, w_out)


def _router_kernel(x_ref, mod_ref, wr_ref, br_ref, h_ref, route_ref, stat_ref, *, T):
    h = _rms(x_ref[0]) * (1.0 + mod_ref[0, 4:5, :]) + mod_ref[0, 3:4, :]
    h_ref[0] = h
    lg = _dot(h, wr_ref[...], HIGHEST) + br_ref[...]
    lane = lax.broadcasted_iota(jnp.int32, (T, LANE), 1).astype(F32)
    vals, hots = [], []
    for _ in range(TOP_K):
        mx = jnp.max(lg, axis=-1, keepdims=True)
        idx = jnp.min(jnp.where(lg == mx, lane, float(LANE)), axis=-1, keepdims=True)
        hot = lane == idx
        vals.append(mx)
        hots.append(hot)
        lg = jnp.where(hot, -jnp.inf, lg)
    exps = [jnp.exp(v - vals[0]) for v in vals]
    tot = exps[0] + exps[1] + exps[2] + exps[3]
    multi = jnp.zeros((T, LANE), F32)
    for hot in hots:
        multi = multi + jnp.where(hot, 1.0, 0.0)
    r_i = lax.broadcasted_iota(jnp.int32, (T, T), 0)
    c_i = lax.broadcasted_iota(jnp.int32, (T, T), 1)
    lower = jnp.where(c_i < r_i, 1.0, 0.0).astype(BF16)
    before = _dot(lower, multi.astype(BF16))
    cnt = jnp.broadcast_to(jnp.sum(multi, axis=0, keepdims=True), (8, LANE))
    cnt = jnp.floor((cnt + (ROW_ALIGN - 1.0)) * (1.0 / ROW_ALIGN)) * ROW_ALIGN
    a_i = lax.broadcasted_iota(jnp.int32, (LANE, LANE), 0)
    b_i = lax.broadcasted_iota(jnp.int32, (LANE, LANE), 1)
    start = _dot(cnt, jnp.where(a_i < b_i, 1.0, 0.0).astype(F32), HIGHEST)
    sub = lax.broadcasted_iota(jnp.int32, (8, LANE), 0)
    stat_ref[0] = jnp.where(sub == 0, cnt, jnp.where(sub == 1, start, 0.0))
    where_to = before + start[0:1, :]
    slab = jnp.zeros((T, LANE), F32)
    for k in range(TOP_K):
        e_k = jnp.sum(jnp.where(hots[k], lane, 0.0), axis=-1, keepdims=True)
        p_k = jnp.sum(jnp.where(hots[k], where_to, 0.0), axis=-1, keepdims=True)
        slab = slab + jnp.where(lane == float(k), e_k, 0.0)
        slab = slab + jnp.where(lane == float(TOP_K + k), p_k, 0.0)
        slab = slab + jnp.where(lane == float(2 * TOP_K + k), exps[k] / tot, 0.0)
    route_ref[0] = slab


def _router(x, mod, w_router, b_router):
    bsz, s, d = x.shape
    T = MOE_TILE
    nt = s // T
    ne = w_router.shape[1]
    wr = jnp.zeros((d, LANE), F32).at[:, :ne].set(w_router)
    br = jnp.full((1, LANE), -1e30, F32).at[0, :ne].set(b_router)
    return pl.pallas_call(
        functools.partial(_router_kernel, T=T),
        out_shape=(jax.ShapeDtypeStruct((bsz, s, d), F32),
                   jax.ShapeDtypeStruct((bsz, s, LANE), F32),
                   jax.ShapeDtypeStruct((bsz * nt, 8, LANE), F32)),
        grid=(bsz, nt),
        in_specs=[pl.BlockSpec((1, T, d), lambda b, i: (b, i, 0)),
                  pl.BlockSpec((1, 6, d), lambda b, i: (b, 0, 0)),
                  pl.BlockSpec((d, LANE), lambda b, i: (0, 0)),
                  pl.BlockSpec((1, LANE), lambda b, i: (0, 0))],
        out_specs=(pl.BlockSpec((1, T, d), lambda b, i: (b, i, 0)),
                   pl.BlockSpec((1, T, LANE), lambda b, i: (b, i, 0)),
                   pl.BlockSpec((1, 8, LANE), lambda b, i: (b * nt + i, 0, 0))),
        compiler_params=_params(("parallel", "parallel")),
        name="moe_router",
    )(x, mod, wr, br)


def _run_copies(tile, cnt_ref, start_ref, row_ref, make_copy):
    for r in range(N_RUNS):
        c = cnt_ref[tile * N_RUNS + r]
        src = start_ref[tile * N_RUNS + r]
        dst = row_ref[tile * N_RUNS + r]
        size = MOE_TILE
        while size >= ROW_ALIGN:
            done = c & (-2 * size)
            cp = make_copy(pl.multiple_of(src + done, ROW_ALIGN), pl.multiple_of(dst + done, ROW_ALIGN), size)
            pl.when((c & size) != 0)(cp.start)
            size //= 2


def _dispatch_kernel(cnt_ref, start_ref, row_ref, pad_end_ref, cnt_end_ref, h_ref, route_ref, xs_hbm,
                     loc, zbuf, zsem, sem, *, BM, N_TAIL):
    t = pl.program_id(0)

    @pl.when(t == 0)
    def _():
        zbuf[...] = jnp.zeros(zbuf.shape, F32)
        n_rows = xs_hbm.shape[0]

        def pad_copy(e, g):
            start = pl.multiple_of(pad_end_ref[e] - g * ZERO_ROWS, ZERO_ROWS)
            return (start + ZERO_ROWS > cnt_end_ref[e],
                    pltpu.make_async_copy(zbuf.at[pl.ds(0, ZERO_ROWS)], xs_hbm.at[pl.ds(start, ZERO_ROWS)], zsem))

        def tail_copy(e):
            start = pl.multiple_of(pad_end_ref[N_EXPERTS - 1] + e * BM, BM)
            safe = pl.multiple_of(jnp.minimum(start, n_rows - BM), BM)
            return start < n_rows, pltpu.make_async_copy(zbuf, xs_hbm.at[pl.ds(safe, BM)], zsem)

        copies = [pad_copy(e, g) for e in range(N_EXPERTS) for g in range(1, BM // ZERO_ROWS + 1)]
        copies += [tail_copy(e) for e in range(N_TAIL)]
        for cond, cp in copies:
            pl.when(cond)(cp.start)
        for cond, cp in copies:
            pl.when(cond)(cp.wait)

    pos_t = route_ref[...].T[TOP_K:2 * TOP_K, :]
    slot = lax.broadcasted_iota(jnp.int32, (LOC_ROWS, 1), 0).astype(F32)
    pick = jnp.zeros((LOC_ROWS, MOE_TILE), F32)
    for k in range(TOP_K):
        pick = pick + jnp.where(slot == pos_t[k:k + 1, :], 1.0, 0.0)
    loc[...] = _dot(pick.astype(BF16), h_ref[...].astype(BF16))
    _run_copies(t, cnt_ref, start_ref, row_ref,
                lambda s, d, n: pltpu.make_async_copy(loc.at[pl.ds(s, n)], xs_hbm.at[pl.ds(d, n)], sem))
    pltpu.make_async_copy(loc, xs_hbm.at[pl.ds(0, LOC_ROWS)], sem).wait()


def _dispatch(plan, h2, route, n_rows, bm):
    n, d = h2.shape
    pairs = LOC_ROWS
    return pl.pallas_call(
        functools.partial(_dispatch_kernel, BM=bm, N_TAIL=n_rows // bm - (n * TOP_K) // bm),
        out_shape=jax.ShapeDtypeStruct((n_rows, d), F32),
        grid_spec=pltpu.PrefetchScalarGridSpec(
            num_scalar_prefetch=5,
            grid=(n // MOE_TILE,),
            in_specs=[pl.BlockSpec((MOE_TILE, d), lambda t, *_: (t, 0)),
                      pl.BlockSpec((MOE_TILE, LANE), lambda t, *_: (t, 0))],
            out_specs=pl.BlockSpec(memory_space=pl.ANY),
            scratch_shapes=[pltpu.VMEM((pairs, d), F32),
                            pltpu.VMEM((bm, d), F32),
                            pltpu.SemaphoreType.DMA(()),
                            pltpu.SemaphoreType.DMA(())]),
        compiler_params=pltpu.CompilerParams(dimension_semantics=("arbitrary",), vmem_limit_bytes=VMEM_LIMIT,
                                             disable_bounds_checks=True),
        name="moe_dispatch",
    )(plan["cnt"], plan["start"], plan["row"], plan["pad_end"], plan["cnt_end"], h2, route)


def _expert_kernel(be_ref, nu_ref, xs_ref, w1_ref, b1_ref, w2_ref, b2_ref, o_ref, w1b, w2b):
    i = pl.program_id(0)
    de = w2_ref.shape[0]
    prev = be_ref[jnp.maximum(i - 1, 0)]
    fresh = (i == 0) | (be_ref[i] != prev)

    @pl.when(fresh & (i < nu_ref[0]))
    def _():
        w1b[...] = w1_ref[...].astype(BF16)
        w2b[...] = w2_ref[...].astype(BF16)

    @pl.when(i < nu_ref[0])
    def _():
        hdn = _dot(xs_ref[...].astype(BF16), w1b[...]) + b1_ref[...]
        glu = jnp.minimum(hdn[:, :de], SWIGLU_LIMIT)
        lin = jnp.clip(hdn[:, de:], -SWIGLU_LIMIT, SWIGLU_LIMIT)
        act = glu * _sigmoid(SWIGLU_ALPHA * glu) * (lin + 1.0)
        o_ref[...] = _dot(act.astype(BF16), w2b[...]) + b2_ref[...]

    @pl.when(i >= nu_ref[0])
    def _():
        o_ref[...] = jnp.zeros(o_ref.shape, F32)


def _experts(blk_expert, n_used, xs, w1, b1, w2, b2, layer, bm):
    n_rows, d = xs.shape
    depth, ne, _, dh2 = w1.shape
    de = w2.shape[2]
    n_blocks = n_rows // bm

    def row_map(i, be, nu):
        return (jnp.minimum(i, nu[0] - 1), 0)

    def w_map(i, be, nu):
        return (layer, be[i], 0, 0)

    return pl.pallas_call(
        _expert_kernel,
        out_shape=jax.ShapeDtypeStruct((n_rows, d), F32),
        grid_spec=pltpu.PrefetchScalarGridSpec(
            num_scalar_prefetch=2,
            grid=(n_blocks,),
            in_specs=[pl.BlockSpec((bm, d), row_map),
                      pl.BlockSpec((None, None, d, dh2), w_map),
                      pl.BlockSpec((None, None, 1, dh2), w_map),
                      pl.BlockSpec((None, None, de, d), w_map),
                      pl.BlockSpec((None, None, 1, d), w_map)],
            out_specs=pl.BlockSpec((bm, d), lambda i, be, nu: (i, 0)),
            scratch_shapes=[pltpu.VMEM((d, dh2), BF16), pltpu.VMEM((de, d), BF16)]),
        compiler_params=_params(("arbitrary",)),
        name="moe_experts",
    )(blk_expert, n_used, xs, w1, b1.reshape(depth, ne, 1, dh2), w2, b2.reshape(depth, ne, 1, d))


def _combine_kernel(cnt_ref, start_ref, row_ref, yb_hbm, x_ref, route_ref, mod_ref, fg_ref, o_ref,
                    loc, sem, *, nt, final):
    t = pl.program_id(0) * nt + pl.program_id(1)
    pairs = LOC_ROWS
    _run_copies(t, cnt_ref, start_ref, row_ref,
                lambda s, d, n: pltpu.make_async_copy(yb_hbm.at[pl.ds(d, n)], loc.at[pl.ds(s, n)], sem))
    pltpu.make_async_copy(yb_hbm.at[pl.ds(0, pairs)], loc, sem).wait()
    route = route_ref[0]
    slot = lax.broadcasted_iota(jnp.int32, (1, pairs), 1).astype(F32)
    wgt = jnp.zeros((MOE_TILE, pairs), F32)
    for k in range(TOP_K):
        wgt = wgt + jnp.where(slot == route[:, TOP_K + k:TOP_K + k + 1],
                              route[:, 2 * TOP_K + k:2 * TOP_K + k + 1], 0.0)
    hi = wgt.astype(BF16)
    lo = (wgt - hi.astype(F32)).astype(BF16)
    rows = loc[...].astype(BF16)
    y = _dot(hi, rows) + _dot(lo, rows)
    out = x_ref[0] + mod_ref[0, 5:6, :] * y
    if final:
        out = _rms(out) * fg_ref[...]
    o_ref[0] = out


def _combine(plan, yb, x, route, mod, final_g, final):
    bsz, s, d = x.shape
    nt = s // MOE_TILE
    pairs = LOC_ROWS
    return pl.pallas_call(
        functools.partial(_combine_kernel, nt=nt, final=final),
        out_shape=jax.ShapeDtypeStruct((bsz, s, d), F32),
        grid_spec=pltpu.PrefetchScalarGridSpec(
            num_scalar_prefetch=3,
            grid=(bsz, nt),
            in_specs=[pl.BlockSpec(memory_space=pl.ANY),
                      pl.BlockSpec((1, MOE_TILE, d), lambda b, i, *_: (b, i, 0)),
                      pl.BlockSpec((1, MOE_TILE, LANE), lambda b, i, *_: (b, i, 0)),
                      pl.BlockSpec((1, 6, d), lambda b, i, *_: (b, 0, 0)),
                      pl.BlockSpec((1, d), lambda b, i, *_: (0, 0))],
            out_specs=pl.BlockSpec((1, MOE_TILE, d), lambda b, i, *_: (b, i, 0)),
            scratch_shapes=[pltpu.VMEM((pairs, d), F32), pltpu.SemaphoreType.DMA(())]),
        compiler_params=pltpu.CompilerParams(dimension_semantics=("arbitrary", "arbitrary"),
                                             vmem_limit_bytes=VMEM_LIMIT, disable_bounds_checks=True),
        name="moe_combine",
    )(plan["cnt"], plan["start"], plan["row"], yb, x, route, mod, final_g.reshape(1, d))


def _rearranged_in_proj(w_in, b_in):
    sizes = (ML_HEADS * ML_QK, ML_HEADS * ML_QK, ML_HEADS * ML_V, ML_HEADS * ML_V, ML_HEADS, ML_HEADS,
             DSA_HEADS * DSA_DIM, DSA_LATENT, IDX_HEADS * IDX_DIM, IDX_DIM, IDX_HEADS,
             3 * FOX_HEADS * FOX_DIM, FOX_HEADS, N_BRANCH * w_in.shape[0])
    offs = [0]
    for sz in sizes:
        offs.append(offs[-1] + sz)
    (o_mq, o_mk, o_mv, o_mo, o_mi, o_mf, o_dq, o_ckv, o_iq, o_ik, o_iw, o_fx, o_ff, o_g, o_end) = offs
    pad = LANE - (IDX_DIM + 2 * ML_HEADS + IDX_HEADS + FOX_HEADS)

    def cols(a):
        parts = [a[..., o_mq:o_mi],
                 a[..., o_dq:o_ckv],
                 a[..., o_iq:o_ik],
                 a[..., o_ckv:o_iq],
                 a[..., o_ik:o_iw],
                 a[..., o_mi:o_dq],
                 a[..., o_iw:o_fx],
                 a[..., o_ff:o_g],
                 jnp.zeros(a.shape[:-1] + (pad,), a.dtype),
                 a[..., o_fx:o_ff],
                 a[..., o_g:o_end]]
        return jnp.concatenate(parts, axis=-1)

    return cols(w_in).astype(BF16), cols(b_in.reshape(1, -1))


def _moe_plan(stats, bm, n_blocks):
    cnt = stats[:, 0, :N_EXPERTS].astype(jnp.int32)
    start = stats[:, 1, :N_EXPERTS].astype(jnp.int32)
    total = jnp.sum(cnt, axis=0)
    padded = (total + bm - 1) // bm * bm
    pad_end = jnp.cumsum(padded)
    pad_start = pad_end - padded
    row = pad_start[None, :] + jnp.cumsum(cnt, axis=0) - cnt
    used = jnp.sum(cnt, axis=1, keepdims=True)
    cnt = jnp.concatenate([cnt, LOC_ROWS - used], axis=1)
    start = jnp.concatenate([start, used], axis=1)
    row = jnp.concatenate([row, jnp.full_like(used, n_blocks * bm)], axis=1)
    blk_row = jnp.arange(n_blocks + 1, dtype=jnp.int32) * bm
    blk_expert = jnp.minimum(jnp.sum((pad_end[None, :] <= blk_row[:, None]).astype(jnp.int32), axis=1),
                             N_EXPERTS - 1)
    n_used = (pad_end[-1:] // bm).astype(jnp.int32)
    plan = dict(cnt=cnt.reshape(-1), start=start.reshape(-1), row=row.reshape(-1).astype(jnp.int32),
                pad_end=pad_end.astype(jnp.int32), cnt_end=(pad_start + total).astype(jnp.int32))
    return plan, blk_expert, n_used


def kernel(x, c, w_ada, b_ada, w_in, b_in, conv_w, conv_b, ml_norm_g, kv_norm_g, w_uk, w_uv,
           w_br_ml, w_br_dsa, w_br_fox, w_out, w_router, b_router, w1, b1, w2, b2, final_g):
    bsz, s, d = x.shape
    depth = w_in.shape[0]
    n = bsz * s
    bm = 512
    n_blocks = -(-(n * TOP_K + (ROW_ALIGN - 1) * N_EXPERTS * (n // MOE_TILE)) // bm) + N_EXPERTS
    mods = _ada_mod(c, w_ada, b_ada).reshape(depth, bsz, 6, d)
    for l in range(depth):
        mod = mods[l]
        w_r, b_r = _rearranged_in_proj(w_in[l], b_in[l])
        proj, misc = _in_proj(x, mod, w_r, b_r)
        y_ml, fcol, frow = _mlstm(proj, misc, conv_w[l], conv_b[l], ml_norm_g[l])
        y_fox = _fox(proj, fcol, frow)
        y_dsa = _dsa(proj, misc, kv_norm_g[l], w_uk[l].astype(BF16), jnp.swapaxes(w_uv[l], 1, 2).astype(BF16))
        x = _merge(x, proj, y_ml, y_dsa, y_fox, mod, w_br_ml[l].astype(BF16), w_br_dsa[l].astype(BF16),
                   w_br_fox[l].astype(BF16), w_out[l].astype(BF16))
        h2, route, stats = _router(x, mod, w_router[l], b_router[l])
        plan, blk_expert, n_used = _moe_plan(stats, bm, n_blocks)
        xs = _dispatch(plan, h2.reshape(n, d), route.reshape(n, LANE), (n_blocks + 1) * bm, bm)
        yb = _experts(blk_expert, n_used, xs, w1, b1, w2, b2, l, bm)
        x = _combine(plan, yb, x, route, mod, final_g, final=(l == depth - 1))
    return x
```

```python
import functools

import jax
import jax.numpy as jnp
from jax import lax
from jax.experimental import pallas as pl
from jax.experimental.pallas import tpu as pltpu

F32 = jnp.float32
BF16 = jnp.bfloat16
HIGHEST = lax.Precision.HIGHEST

EPS = 1e-6
CHUNK = 64

ML_HEADS, ML_QK, ML_V, ML_CONV = 4, 64, 128, 4
DSA_HEADS, DSA_DIM, DSA_LATENT = 4, 128, 128
IDX_HEADS, IDX_DIM, DSA_TOPK = 4, 64, 256
FOX_HEADS, FOX_DIM = 4, 128
N_BRANCH = 3
N_EXPERTS, TOP_K = 32, 4
SWIGLU_LIMIT, SWIGLU_ALPHA = 7.0, 1.702

LANE = 128
ROW_ALIGN = 8
INT_MIN = -2 ** 31

MOE_TILE = 256
N_RUNS = N_EXPERTS + 1
LOC_ROWS = TOP_K * MOE_TILE + MOE_TILE
ZERO_ROWS = 128

C_MLQK = 0
C_MLV = 512
C_MLO = 1024
C_DQ = 1536
C_DIQ = 2048
C_CKV = 2304
C_MISC = 2432
C_FOX = 2560
C_GATE = 4096
NP = 7168
M_IK, M_MLI, M_MLF, M_IW, M_FXF = 0, 64, 68, 72, 76

VMEM_LIMIT = 56 * 1024 * 1024


def _dot(a, b, prec=None):
    return jnp.dot(a, b, preferred_element_type=F32, precision=prec)


def _dot_nt(a, b, prec=None):
    return lax.dot_general(a, b, (((1,), (1,)), ((), ())), preferred_element_type=F32, precision=prec)


def _dot_tn(a, b):
    return lax.dot_general(a, b, (((0,), (0,)), ((), ())), preferred_element_type=F32)


def _sigmoid(x):
    return 1.0 / (1.0 + jnp.exp(-x))


def _log_sigmoid(x):
    return jnp.minimum(x, 0.0) - jnp.log1p(jnp.exp(-jnp.abs(x)))


def _rms(x):
    return x * lax.rsqrt(jnp.mean(x * x, axis=-1, keepdims=True) + EPS)


def _params(sem, vmem=VMEM_LIMIT):
    return pltpu.CompilerParams(dimension_semantics=sem, vmem_limit_bytes=vmem)


def _ada_kernel(c_ref, w_ref, b_ref, o_ref):
    c = c_ref[...]
    o_ref[0] = _dot(c * _sigmoid(c), w_ref[0], HIGHEST) + b_ref[0]


def _ada_mod(c, w_ada, b_ada):
    depth, d, n = w_ada.shape
    bsz = c.shape[0]
    tn = 1536
    return pl.pallas_call(
        _ada_kernel,
        out_shape=jax.ShapeDtypeStruct((depth, bsz, n), F32),
        grid=(depth, n // tn),
        in_specs=[pl.BlockSpec((bsz, d), lambda l, j: (0, 0)),
                  pl.BlockSpec((1, d, tn), lambda l, j: (l, 0, j)),
                  pl.BlockSpec((1, 1, tn), lambda l, j: (l, 0, j))],
        out_specs=pl.BlockSpec((1, bsz, tn), lambda l, j: (l, 0, j)),
        compiler_params=_params(("parallel", "parallel")),
        name="ada_mod",
    )(c, w_ada, b_ada.reshape(depth, 1, n))


def _inproj_kernel(x_ref, mod_ref, w_ref, b_ref, o_ref, misc_ref, h_scr, *, tn):
    j = pl.program_id(2)

    @pl.when(j == 0)
    def _():
        h = _rms(x_ref[0]) * (1.0 + mod_ref[0, 1:2, :]) + mod_ref[0, 0:1, :]
        h_scr[...] = h.astype(BF16)

    acc = _dot(h_scr[...], w_ref[...]) + b_ref[...]
    o_ref[0] = acc.astype(BF16)

    @pl.when(j == C_MISC // tn)
    def _():
        misc_ref[0] = acc[:, C_MISC % tn:C_MISC % tn + LANE]


def _in_proj(x, mod, w, b):
    bsz, s, d = x.shape
    n = w.shape[1]
    tm = min(1024, s)
    tn = 1024
    return pl.pallas_call(
        functools.partial(_inproj_kernel, tn=tn),
        out_shape=(jax.ShapeDtypeStruct((bsz, s, n), BF16), jax.ShapeDtypeStruct((bsz, s, LANE), F32)),
        grid=(bsz, s // tm, n // tn),
        in_specs=[pl.BlockSpec((1, tm, d), lambda bi, i, j: (bi, i, 0)),
                  pl.BlockSpec((1, 6, d), lambda bi, i, j: (bi, 0, 0)),
                  pl.BlockSpec((d, tn), lambda bi, i, j: (0, j)),
                  pl.BlockSpec((1, tn), lambda bi, i, j: (0, j))],
        out_specs=(pl.BlockSpec((1, tm, tn), lambda bi, i, j: (bi, i, j)),
                   pl.BlockSpec((1, tm, LANE), lambda bi, i, j: (bi, i, 0))),
        scratch_shapes=[pltpu.VMEM((tm, d), BF16)],
        compiler_params=_params(("parallel", "parallel", "arbitrary")),
        name="in_proj",
    )(x, mod, w, b)


def _mlstm_kernel(qk_ref, v_ref, o_ref, misc_ref, cw_ref, cb_ref, g_ref,
                  y_ref, fcol_ref, frow_ref,
                  xext, ct_scr, n_scr, m_scr, carry_scr, *, L):
    c = pl.program_id(1)
    nqk = ML_HEADS * ML_QK

    @pl.when(c == 0)
    def _():
        xext[0:8, :] = jnp.zeros((8, 2 * nqk), F32)
        ct_scr[...] = jnp.zeros(ct_scr.shape, F32)
        n_scr[...] = jnp.zeros(n_scr.shape, F32)
        m_scr[...] = jnp.full(m_scr.shape, -jnp.inf, F32)
        carry_scr[...] = jnp.zeros(carry_scr.shape, F32)

    @pl.when(c > 0)
    def _():
        xext[0:8, :] = xext[L:L + 8, :]

    xext[8:8 + L, :] = qk_ref[0].astype(F32)
    cw = cw_ref[...]
    conv = (cb_ref[...] + cw[3:4, :] * xext[8:8 + L, :] + cw[2:3, :] * xext[7:7 + L, :]
            + cw[1:2, :] * xext[6:6 + L, :] + cw[0:1, :] * xext[5:5 + L, :])
    qk = conv * _sigmoid(conv)

    misc = misc_ref[0]
    ls = _log_sigmoid(misc)
    row = lax.broadcasted_iota(jnp.int32, (L, L), 0)
    col = lax.broadcasted_iota(jnp.int32, (L, L), 1)
    causal = row >= col
    tri = jnp.where(causal, 1.0, 0.0).astype(F32)
    cs = _dot(tri, ls, HIGHEST)
    cs_t = cs.T
    misc_t = misc.T
    carry = carry_scr[0:1, :]
    fcol_ref[0] = cs + carry
    for h in range(FOX_HEADS):
        frow_ref[0, h:h + 1, :] = cs_t[M_FXF + h:M_FXF + h + 1, :] + carry[:, M_FXF + h:M_FXF + h + 1]
    frow_ref[0, 4:8, :] = jnp.zeros((4, L), F32)
    carry_scr[0:1, :] = carry + cs[L - 1:L, :]

    for h in range(ML_HEADS):
        qh = qk[:, h * ML_QK:(h + 1) * ML_QK] * (ML_QK ** -0.5)
        kh = qk[:, nqk + h * ML_QK:nqk + (h + 1) * ML_QK]
        vb = v_ref[0, :, h * ML_V:(h + 1) * ML_V].astype(BF16)
        i_col = misc[:, M_MLI + h:M_MLI + h + 1]
        i_row = misc_t[M_MLI + h:M_MLI + h + 1, :]
        b_col = cs[:, M_MLF + h:M_MLF + h + 1]
        b_row = cs_t[M_MLF + h:M_MLF + h + 1, :]
        b_last = b_col[L - 1:L, :]
        m_prev = m_scr[h:h + 1, 0:1]

        d_log = jnp.where(causal, b_col - b_row + i_row, -jnp.inf)
        inter_log = b_col + m_prev
        m_out = jnp.maximum(inter_log, jnp.max(d_log, axis=-1, keepdims=True))
        qb = qh.astype(BF16)
        kb = kh.astype(BF16)
        s = _dot_nt(qb, kb) * jnp.exp(d_log - m_out)
        a_inter = jnp.exp(inter_log - m_out)
        ct = ct_scr[h]
        n_row = n_scr[h]
        num = _dot(s.astype(BF16), vb) + a_inter * _dot(qb, ct.astype(BF16))
        den = jnp.sum(s, axis=-1, keepdims=True) + a_inter * jnp.sum(qh * n_row, axis=-1, keepdims=True)
        hid = num / jnp.maximum(jnp.abs(den), jnp.exp(-m_out))

        w_state = b_last - b_col + i_col
        m_loc = jnp.max(w_state, axis=0, keepdims=True)
        ke = kh * jnp.exp(w_state - m_loc)
        c_loc = _dot_tn(ke.astype(BF16), vb)
        n_loc = jnp.sum(ke, axis=0, keepdims=True)
        m_new = jnp.maximum(b_last + m_prev, m_loc)
        decay = jnp.exp(b_last + m_prev - m_new)
        scale = jnp.exp(m_loc - m_new)
        ct_scr[h] = decay * ct + scale * c_loc
        n_scr[h] = decay * n_row + scale * n_loc
        m_scr[h:h + 1, :] = jnp.broadcast_to(m_new, (1, LANE))

        y = (_rms(hid) * g_ref[:, h * ML_V:(h + 1) * ML_V]
             * _sigmoid(o_ref[0, :, h * ML_V:(h + 1) * ML_V].astype(F32)))
        y_ref[0, :, h * ML_V:(h + 1) * ML_V] = y.astype(BF16)


def _mlstm(proj, misc, conv_w, conv_b, norm_g):
    bsz, s, _ = proj.shape
    L = min(256, s)
    w = 2 * ML_HEADS * ML_QK
    wv = ML_HEADS * ML_V
    return pl.pallas_call(
        functools.partial(_mlstm_kernel, L=L),
        out_shape=(jax.ShapeDtypeStruct((bsz, s, wv), BF16),
                   jax.ShapeDtypeStruct((bsz, s, LANE), F32),
                   jax.ShapeDtypeStruct((bsz, 8, s), F32)),
        grid=(bsz, s // L),
        in_specs=[pl.BlockSpec((1, L, w), lambda b, c: (b, c, C_MLQK // w)),
                  pl.BlockSpec((1, L, wv), lambda b, c: (b, c, C_MLV // wv)),
                  pl.BlockSpec((1, L, wv), lambda b, c: (b, c, C_MLO // wv)),
                  pl.BlockSpec((1, L, LANE), lambda b, c: (b, c, 0)),
                  pl.BlockSpec((ML_CONV, w), lambda b, c: (0, 0)),
                  pl.BlockSpec((1, w), lambda b, c: (0, 0)),
                  pl.BlockSpec((1, wv), lambda b, c: (0, 0))],
        out_specs=(pl.BlockSpec((1, L, wv), lambda b, c: (b, c, 0)),
                   pl.BlockSpec((1, L, LANE), lambda b, c: (b, c, 0)),
                   pl.BlockSpec((1, 8, L), lambda b, c: (b, 0, c))),
        scratch_shapes=[pltpu.VMEM((L + 8, w), F32),
                        pltpu.VMEM((ML_HEADS, ML_QK, ML_V), F32),
                        pltpu.VMEM((ML_HEADS, 1, ML_QK), F32),
                        pltpu.VMEM((8, LANE), F32),
                        pltpu.VMEM((8, LANE), F32)],
        compiler_params=_params(("parallel", "arbitrary")),
        name="mlstm",
    )(proj, proj, proj, misc, conv_w, conv_b.reshape(1, w), norm_g.reshape(1, wv))


def _fox_kernel(q_ref, k_ref, v_ref, fcol_ref, frow_ref, y_ref, k_scr, vt_scr, *, T, S):
    qi = pl.program_id(1)
    d = FOX_DIM

    @pl.when(qi == 0)
    def _():
        for h in range(FOX_HEADS):
            k_scr[h] = k_ref[0, :, h * d:(h + 1) * d].astype(BF16)
            vt_scr[h] = v_ref[0, :, h * d:(h + 1) * d].astype(F32).T.astype(BF16)

    q_t = (q_ref[0].astype(F32) * (d ** -0.5)).T

    def body(ext):
        kpos = lax.broadcasted_iota(jnp.int32, (ext, 1), 0)
        qpos = (ext - T) + lax.broadcasted_iota(jnp.int32, (1, T), 1)
        causal = kpos <= qpos
        for h in range(FOX_HEADS):
            s = _dot(k_scr[h, 0:ext, :], q_t[h * d:(h + 1) * d, :].astype(BF16))
            s = s + (frow_ref[0, h:h + 1, :] - fcol_ref[0, 0:ext, M_FXF + h:M_FXF + h + 1])
            s = jnp.where(causal, s, -jnp.inf)
            p = jnp.exp(s - jnp.max(s, axis=0, keepdims=True))
            l = jnp.sum(p, axis=0, keepdims=True)
            out_t = _dot(vt_scr[h, :, 0:ext], p.astype(BF16)) / l
            y_ref[0, :, h * d:(h + 1) * d] = out_t.T.astype(BF16)

    for c in range(S // T):
        pl.when(qi == c)(functools.partial(body, (c + 1) * T))


def _fox(proj, fcol, frow):
    bsz, s, _ = proj.shape
    T = min(256, s)
    w = FOX_HEADS * FOX_DIM
    return pl.pallas_call(
        functools.partial(_fox_kernel, T=T, S=s),
        out_shape=jax.ShapeDtypeStruct((bsz, s, w), BF16),
        grid=(bsz, s // T),
        in_specs=[pl.BlockSpec((1, T, w), lambda b, i: (b, i, C_FOX // w)),
                  pl.BlockSpec((1, s, w), lambda b, i: (b, 0, C_FOX // w + 1)),
                  pl.BlockSpec((1, s, w), lambda b, i: (b, 0, C_FOX // w + 2)),
                  pl.BlockSpec((1, s, LANE), lambda b, i: (b, 0, 0)),
                  pl.BlockSpec((1, 8, T), lambda b, i: (b, 0, i))],
        out_specs=pl.BlockSpec((1, T, w), lambda b, i: (b, i, 0)),
        scratch_shapes=[pltpu.VMEM((FOX_HEADS, s, FOX_DIM), BF16),
                        pltpu.VMEM((FOX_HEADS, FOX_DIM, s), BF16)],
        compiler_params=_params(("parallel", "arbitrary")),
        name="fox_attention",
    )(proj, proj, proj, fcol, frow)


def _dsa_body(ext, qi, q_ref, qidx_ref, misc_q_ref, wuk_ref, wuv_ref, y_ref,
              ckvn_scr, ckvnt_scr, kidx_scr, sel_scr, *, T, n_sel, rank_tile):
    dh = DSA_DIM
    q_t = q_ref[0].astype(F32).T
    qidx_t = qidx_ref[0].astype(F32).T.astype(BF16)
    w_t = misc_q_ref[0].T[M_IW:M_IW + IDX_HEADS, :] * (IDX_HEADS ** -0.5)
    kidx = kidx_scr[0:ext, :]
    score = jnp.zeros((ext, T), F32)
    for h in range(IDX_HEADS):
        lg = _dot(kidx, qidx_t[h * IDX_DIM:(h + 1) * IDX_DIM, :]) * (IDX_DIM ** -0.5)
        score = score + w_t[h:h + 1, :] * jnp.maximum(lg, 0.0)
    kpos = lax.broadcasted_iota(jnp.int32, (ext, 1), 0)
    qchunk = (qi * T + lax.broadcasted_iota(jnp.int32, (1, T), 1)) // CHUNK
    score = jnp.where((kpos // CHUNK) <= qchunk, score, -jnp.inf)

    def as_float(c):
        return pltpu.bitcast(jnp.where(c < 0, c ^ jnp.int32(0x7FFFFFFF), c), F32)

    def count_ge(c):
        return jnp.sum(jnp.where(score >= as_float(c), 1.0, 0.0), axis=0, keepdims=True)

    t0 = jnp.where(count_ge(jnp.zeros((1, T), jnp.int32)) >= n_sel, jnp.int32(0), jnp.int32(INT_MIN))

    def bis(i, t):
        cand = t + jnp.left_shift(jnp.int32(1), jnp.int32(30) - i)
        return jnp.where(count_ge(cand) >= n_sel, cand, t)

    t = lax.fori_loop(0, 31, bis, t0)
    n_vis = ((qchunk + 1) * CHUNK).astype(F32)
    thr = jnp.where(n_vis <= n_sel, -3.0e38, as_float(t))
    need = n_sel - jnp.sum(jnp.where(score > thr, 1.0, 0.0), axis=0, keepdims=True)
    r_i = lax.broadcasted_iota(jnp.int32, (rank_tile, rank_tile), 0)
    c_i = lax.broadcasted_iota(jnp.int32, (rank_tile, rank_tile), 1)
    lower = jnp.where(c_i < r_i, 1.0, 0.0).astype(BF16)
    carry = jnp.zeros((1, T), F32)
    for j in range(ext // rank_tile):
        sc = score[j * rank_tile:(j + 1) * rank_tile, :]
        eq = jnp.where(sc == thr, 1.0, 0.0)
        rank = _dot(lower, eq.astype(BF16)) + carry
        carry = carry + jnp.sum(eq, axis=0, keepdims=True)
        sel_scr[j * rank_tile:(j + 1) * rank_tile, :] = jnp.where(
            sc > thr, 1.0, jnp.where(rank < need, eq, 0.0))

    sel = sel_scr[0:ext, :] > 0.5
    ckvn = ckvn_scr[0:ext, :]
    ckvn_t = ckvnt_scr[:, 0:ext]
    for h in range(DSA_HEADS):
        qa_t = _dot(wuk_ref[h], q_t[h * dh:(h + 1) * dh, :].astype(BF16)) * (dh ** -0.5)
        lg = _dot(ckvn, qa_t.astype(BF16))
        lg = jnp.where(sel, lg, -jnp.inf)
        p = jnp.exp(lg - jnp.max(lg, axis=0, keepdims=True))
        l = jnp.sum(p, axis=0, keepdims=True)
        lat_t = _dot(ckvn_t, p.astype(BF16)) / l
        out_t = _dot(wuv_ref[h], lat_t.astype(BF16))
        y_ref[0, :, h * dh:(h + 1) * dh] = out_t.T.astype(BF16)


def _dsa_kernel(q_ref, ckv_ref, qidx_ref, misc_all_ref, misc_q_ref, g_ref, wuk_ref, wuv_ref, y_ref,
                ckvn_scr, ckvnt_scr, kidx_scr, sel_scr, *, T, S, n_sel, n_cls, rank_tile):
    qi = pl.program_id(1)

    @pl.when(qi == 0)
    def _():
        ckvn = _rms(ckv_ref[0].astype(F32)) * g_ref[...]
        ckvn_scr[...] = ckvn.astype(BF16)
        ckvnt_scr[...] = ckvn.T.astype(BF16)
        kidx_scr[...] = misc_all_ref[0, :, M_IK:M_IK + IDX_DIM].astype(BF16)

    per = (S // T) // n_cls
    for c in range(n_cls):
        ext = (c + 1) * per * T

        @pl.when(qi // per == c)
        def _(ext=ext):
            _dsa_body(ext, qi, q_ref, qidx_ref, misc_q_ref, wuk_ref, wuv_ref, y_ref,
                      ckvn_scr, ckvnt_scr, kidx_scr, sel_scr, T=T, n_sel=n_sel, rank_tile=rank_tile)


def _dsa(proj, misc, kv_g, wuk, wuv_t):
    bsz, s, _ = proj.shape
    T = min(256, s)
    n_sel = min(DSA_TOPK, s // 4)
    n_cls = max(1, min(4, s // 512))
    rank_tile = 256
    w = DSA_HEADS * DSA_DIM
    wi = IDX_HEADS * IDX_DIM
    return pl.pallas_call(
        functools.partial(_dsa_kernel, T=T, S=s, n_sel=float(n_sel), n_cls=n_cls, rank_tile=rank_tile),
        out_shape=jax.ShapeDtypeStruct((bsz, s, w), BF16),
        grid=(bsz, s // T),
        in_specs=[pl.BlockSpec((1, T, w), lambda b, i: (b, i, C_DQ // w)),
                  pl.BlockSpec((1, s, DSA_LATENT), lambda b, i: (b, 0, C_CKV // DSA_LATENT)),
                  pl.BlockSpec((1, T, wi), lambda b, i: (b, i, C_DIQ // wi)),
                  pl.BlockSpec((1, s, LANE), lambda b, i: (b, 0, 0)),
                  pl.BlockSpec((1, T, LANE), lambda b, i: (b, i, 0)),
                  pl.BlockSpec((1, DSA_LATENT), lambda b, i: (0, 0)),
                  pl.BlockSpec((DSA_HEADS, DSA_LATENT, DSA_DIM), lambda b, i: (0, 0, 0)),
                  pl.BlockSpec((DSA_HEADS, DSA_DIM, DSA_LATENT), lambda b, i: (0, 0, 0))],
        out_specs=pl.BlockSpec((1, T, w), lambda b, i: (b, i, 0)),
        scratch_shapes=[pltpu.VMEM((s, DSA_LATENT), BF16),
                        pltpu.VMEM((DSA_LATENT, s), BF16),
                        pltpu.VMEM((s, IDX_DIM), BF16),
                        pltpu.VMEM((s, T), F32)],
        compiler_params=_params(("parallel", "arbitrary")),
        name="dsa_attention",
    )(proj, proj, proj, misc, misc, kv_g.reshape(1, DSA_LATENT), wuk, wuv_t)


def _merge_kernel(yml_ref, ydsa_ref, yfox_ref, g0_ref, g1_ref, g2_ref, x_ref, mod_ref,
                  wml_ref, wdsa_ref, wfox_ref, wout_ref, o_ref):
    merged = (_sigmoid(g0_ref[0].astype(F32)) * _dot(yml_ref[0], wml_ref[...])
              + _sigmoid(g1_ref[0].astype(F32)) * _dot(ydsa_ref[0], wdsa_ref[...])
              + _sigmoid(g2_ref[0].astype(F32)) * _dot(yfox_ref[0], wfox_ref[...]))
    out = _dot(merged.astype(BF16), wout_ref[...])
    o_ref[0] = x_ref[0] + mod_ref[0, 2:3, :] * out


def _merge(x, proj, y_ml, y_dsa, y_fox, mod, w_ml, w_dsa, w_fox, w_out):
    bsz, s, d = x.shape
    tm = min(512, s)
    wb = y_ml.shape[-1]
    gb = C_GATE // d
    yspec = pl.BlockSpec((1, tm, wb), lambda b, i: (b, i, 0))
    wspec = pl.BlockSpec((wb, d), lambda b, i: (0, 0))
    return pl.pallas_call(
        _merge_kernel,
        out_shape=jax.ShapeDtypeStruct((bsz, s, d), F32),
        grid=(bsz, s // tm),
        in_specs=[yspec, yspec, yspec,
                  pl.BlockSpec((1, tm, d), lambda b, i: (b, i, gb)),
                  pl.BlockSpec((1, tm, d), lambda b, i: (b, i, gb + 1)),
                  pl.BlockSpec((1, tm, d), lambda b, i: (b, i, gb + 2)),
                  pl.BlockSpec((1, tm, d), lambda b, i: (b, i, 0)),
                  pl.BlockSpec((1, 6, d), lambda b, i: (b, 0, 0)),
                  wspec, wspec, wspec,
                  pl.BlockSpec((d, d), lambda b, i: (0, 0))],
        out_specs=pl.BlockSpec((1, tm, d), lambda b, i: (b, i, 0)),
        compiler_params=_params(("parallel", "parallel")),
        name="merge_out",
    )(y_ml, y_dsa, y_fox, proj, proj, proj, x, mod, w_ml, w_dsa, w_fox, w_out)


def _router_kernel(x_ref, mod_ref, wr_ref, br_ref, h_ref, route_ref, stat_ref, *, T):
    h = _rms(x_ref[0]) * (1.0 + mod_ref[0, 4:5, :]) + mod_ref[0, 3:4, :]
    h_ref[0] = h
    lg = _dot(h, wr_ref[...], HIGHEST) + br_ref[...]
    lane = lax.broadcasted_iota(jnp.int32, (T, LANE), 1).astype(F32)
    vals, hots = [], []
    for _ in range(TOP_K):
        mx = jnp.max(lg, axis=-1, keepdims=True)
        idx = jnp.min(jnp.where(lg == mx, lane, float(LANE)), axis=-1, keepdims=True)
        hot = lane == idx
        vals.append(mx)
        hots.append(hot)
        lg = jnp.where(hot, -jnp.inf, lg)
    exps = [jnp.exp(v - vals[0]) for v in vals]
    tot = exps[0] + exps[1] + exps[2] + exps[3]
    multi = jnp.zeros((T, LANE), F32)
    for hot in hots:
        multi = multi + jnp.where(hot, 1.0, 0.0)
    r_i = lax.broadcasted_iota(jnp.int32, (T, T), 0)
    c_i = lax.broadcasted_iota(jnp.int32, (T, T), 1)
    lower = jnp.where(c_i < r_i, 1.0, 0.0).astype(BF16)
    before = _dot(lower, multi.astype(BF16))
    cnt = jnp.broadcast_to(jnp.sum(multi, axis=0, keepdims=True), (8, LANE))
    cnt = jnp.floor((cnt + (ROW_ALIGN - 1.0)) * (1.0 / ROW_ALIGN)) * ROW_ALIGN
    a_i = lax.broadcasted_iota(jnp.int32, (LANE, LANE), 0)
    b_i = lax.broadcasted_iota(jnp.int32, (LANE, LANE), 1)
    start = _dot(cnt, jnp.where(a_i < b_i, 1.0, 0.0).astype(F32), HIGHEST)
    sub = lax.broadcasted_iota(jnp.int32, (8, LANE), 0)
    stat_ref[0] = jnp.where(sub == 0, cnt, jnp.where(sub == 1, start, 0.0))
    where_to = before + start[0:1, :]
    slab = jnp.zeros((T, LANE), F32)
    for k in range(TOP_K):
        e_k = jnp.sum(jnp.where(hots[k], lane, 0.0), axis=-1, keepdims=True)
        p_k = jnp.sum(jnp.where(hots[k], where_to, 0.0), axis=-1, keepdims=True)
        slab = slab + jnp.where(lane == float(k), e_k, 0.0)
        slab = slab + jnp.where(lane == float(TOP_K + k), p_k, 0.0)
        slab = slab + jnp.where(lane == float(2 * TOP_K + k), exps[k] / tot, 0.0)
    route_ref[0] = slab


def _router(x, mod, w_router, b_router):
    bsz, s, d = x.shape
    T = MOE_TILE
    nt = s // T
    ne = w_router.shape[1]
    wr = jnp.zeros((d, LANE), F32).at[:, :ne].set(w_router)
    br = jnp.full((1, LANE), -1e30, F32).at[0, :ne].set(b_router)
    return pl.pallas_call(
        functools.partial(_router_kernel, T=T),
        out_shape=(jax.ShapeDtypeStruct((bsz, s, d), F32),
                   jax.ShapeDtypeStruct((bsz, s, LANE), F32),
                   jax.ShapeDtypeStruct((bsz * nt, 8, LANE), F32)),
        grid=(bsz, nt),
        in_specs=[pl.BlockSpec((1, T, d), lambda b, i: (b, i, 0)),
                  pl.BlockSpec((1, 6, d), lambda b, i: (b, 0, 0)),
                  pl.BlockSpec((d, LANE), lambda b, i: (0, 0)),
                  pl.BlockSpec((1, LANE), lambda b, i: (0, 0))],
        out_specs=(pl.BlockSpec((1, T, d), lambda b, i: (b, i, 0)),
                   pl.BlockSpec((1, T, LANE), lambda b, i: (b, i, 0)),
                   pl.BlockSpec((1, 8, LANE), lambda b, i: (b * nt + i, 0, 0))),
        compiler_params=_params(("parallel", "parallel")),
        name="moe_router",
    )(x, mod, wr, br)


def _run_copies(tile, cnt_ref, start_ref, row_ref, make_copy):
    for r in range(N_RUNS):
        c = cnt_ref[tile * N_RUNS + r]
        src = start_ref[tile * N_RUNS + r]
        dst = row_ref[tile * N_RUNS + r]
        size = MOE_TILE
        while size >= ROW_ALIGN:
            done = c & (-2 * size)
            cp = make_copy(pl.multiple_of(src + done, ROW_ALIGN), pl.multiple_of(dst + done, ROW_ALIGN), size)
            pl.when((c & size) != 0)(cp.start)
            size //= 2


def _dispatch_kernel(cnt_ref, start_ref, row_ref, pad_end_ref, cnt_end_ref, h_ref, route_ref, xs_hbm,
                     loc0, loc1, zbuf, zsem, sem, *, BM, N_TAIL, N_T):
    t = pl.program_id(0)
    loc = (loc0, loc1)

    @pl.when(t == 0)
    def _():
        zbuf[...] = jnp.zeros(zbuf.shape, F32)
        n_rows = xs_hbm.shape[0]

        def pad_copy(e, g):
            start = pl.multiple_of(pad_end_ref[e] - g * ZERO_ROWS, ZERO_ROWS)
            return (start + ZERO_ROWS > cnt_end_ref[e],
                    pltpu.make_async_copy(zbuf.at[pl.ds(0, ZERO_ROWS)], xs_hbm.at[pl.ds(start, ZERO_ROWS)], zsem))

        def tail_copy(e):
            start = pl.multiple_of(pad_end_ref[N_EXPERTS - 1] + e * BM, BM)
            safe = pl.multiple_of(jnp.minimum(start, n_rows - BM), BM)
            return start < n_rows, pltpu.make_async_copy(zbuf, xs_hbm.at[pl.ds(safe, BM)], zsem)

        copies = [pad_copy(e, g) for e in range(N_EXPERTS) for g in range(1, BM // ZERO_ROWS + 1)]
        copies += [tail_copy(e) for e in range(N_TAIL)]
        for cond, cp in copies:
            pl.when(cond)(cp.start)
        for cond, cp in copies:
            pl.when(cond)(cp.wait)

    pos_t = route_ref[...].T[TOP_K:2 * TOP_K, :]
    slot = lax.broadcasted_iota(jnp.int32, (LOC_ROWS, 1), 0).astype(F32)
    pick = jnp.zeros((LOC_ROWS, MOE_TILE), F32)
    for k in range(TOP_K):
        pick = pick + jnp.where(slot == pos_t[k:k + 1, :], 1.0, 0.0)
    pick = pick.astype(BF16)

    def drain(s):
        pltpu.make_async_copy(loc[s], xs_hbm.at[pl.ds(0, LOC_ROWS)], sem.at[s]).wait()

    for s in range(2):
        @pl.when(t % 2 == s)
        def _(s=s):
            pl.when(t >= 2)(functools.partial(drain, s))
            loc[s][...] = _dot(pick, h_ref[...].astype(BF16))
            _run_copies(t, cnt_ref, start_ref, row_ref,
                        lambda a, b, n: pltpu.make_async_copy(loc[s].at[pl.ds(a, n)], xs_hbm.at[pl.ds(b, n)],
                                                              sem.at[s]))

    @pl.when(t == N_T - 1)
    def _():
        if N_T >= 2:
            drain(N_T % 2)
        drain((N_T - 1) % 2)


def _dispatch(plan, h2, route, n_rows, bm):
    n, d = h2.shape
    pairs = LOC_ROWS
    return pl.pallas_call(
        functools.partial(_dispatch_kernel, BM=bm, N_TAIL=n_rows // bm - (n * TOP_K) // bm, N_T=n // MOE_TILE),
        out_shape=jax.ShapeDtypeStruct((n_rows, d), F32),
        grid_spec=pltpu.PrefetchScalarGridSpec(
            num_scalar_prefetch=5,
            grid=(n // MOE_TILE,),
            in_specs=[pl.BlockSpec((MOE_TILE, d), lambda t, *_: (t, 0)),
                      pl.BlockSpec((MOE_TILE, LANE), lambda t, *_: (t, 0))],
            out_specs=pl.BlockSpec(memory_space=pl.ANY),
            scratch_shapes=[pltpu.VMEM((pairs, d), F32),
                            pltpu.VMEM((pairs, d), F32),
                            pltpu.VMEM((bm, d), F32),
                            pltpu.SemaphoreType.DMA(()),
                            pltpu.SemaphoreType.DMA((2,))]),
        compiler_params=pltpu.CompilerParams(dimension_semantics=("arbitrary",), vmem_limit_bytes=VMEM_LIMIT,
                                             disable_bounds_checks=True),
        name="moe_dispatch",
    )(plan["cnt"], plan["start"], plan["row"], plan["pad_end"], plan["cnt_end"], h2, route)


def _expert_kernel(be_ref, nu_ref, xs_ref, w1_ref, b1_ref, w2_ref, b2_ref, o_ref, w1b, w2b):
    i = pl.program_id(0)
    de = w2_ref.shape[0]
    prev = be_ref[jnp.maximum(i - 1, 0)]
    fresh = (i == 0) | (be_ref[i] != prev)

    @pl.when(fresh & (i < nu_ref[0]))
    def _():
        w1b[...] = w1_ref[...].astype(BF16)
        w2b[...] = w2_ref[...].astype(BF16)

    @pl.when(i < nu_ref[0])
    def _():
        hdn = _dot(xs_ref[...].astype(BF16), w1b[...]) + b1_ref[...]
        glu = jnp.minimum(hdn[:, :de], SWIGLU_LIMIT)
        lin = jnp.clip(hdn[:, de:], -SWIGLU_LIMIT, SWIGLU_LIMIT)
        act = glu * _sigmoid(SWIGLU_ALPHA * glu) * (lin + 1.0)
        o_ref[...] = _dot(act.astype(BF16), w2b[...]) + b2_ref[...]

    @pl.when(i >= nu_ref[0])
    def _():
        o_ref[...] = jnp.zeros(o_ref.shape, F32)


def _experts(blk_expert, n_used, xs, w1, b1, w2, b2, layer, bm):
    n_rows, d = xs.shape
    depth, ne, _, dh2 = w1.shape
    de = w2.shape[2]
    n_blocks = n_rows // bm

    def row_map(i, be, nu):
        return (jnp.minimum(i, nu[0] - 1), 0)

    def w_map(i, be, nu):
        return (layer, be[i], 0, 0)

    return pl.pallas_call(
        _expert_kernel,
        out_shape=jax.ShapeDtypeStruct((n_rows, d), F32),
        grid_spec=pltpu.PrefetchScalarGridSpec(
            num_scalar_prefetch=2,
            grid=(n_blocks,),
            in_specs=[pl.BlockSpec((bm, d), row_map),
                      pl.BlockSpec((None, None, d, dh2), w_map),
                      pl.BlockSpec((None, None, 1, dh2), w_map),
                      pl.BlockSpec((None, None, de, d), w_map),
                      pl.BlockSpec((None, None, 1, d), w_map)],
            out_specs=pl.BlockSpec((bm, d), lambda i, be, nu: (i, 0)),
            scratch_shapes=[pltpu.VMEM((d, dh2), BF16), pltpu.VMEM((de, d), BF16)]),
        compiler_params=_params(("arbitrary",)),
        name="moe_experts",
    )(blk_expert, n_used, xs, w1, b1.reshape(depth, ne, 1, dh2), w2, b2.reshape(depth, ne, 1, d))


def _combine_kernel(cnt_ref, start_ref, row_ref, yb_hbm, x_ref, route_ref, mod_ref, fg_ref, o_ref,
                    loc0, loc1, sem, *, nt, n_tiles, final):
    t = pl.program_id(0) * nt + pl.program_id(1)
    pairs = LOC_ROWS
    loc = (loc0, loc1)

    def fetch(tile, s):
        _run_copies(tile, cnt_ref, start_ref, row_ref,
                    lambda a, b, n: pltpu.make_async_copy(yb_hbm.at[pl.ds(b, n)], loc[s].at[pl.ds(a, n)],
                                                          sem.at[s]))

    pl.when(t == 0)(functools.partial(fetch, 0, 0))
    route = route_ref[0]
    slot = lax.broadcasted_iota(jnp.int32, (1, pairs), 1).astype(F32)
    wgt = jnp.zeros((MOE_TILE, pairs), F32)
    for k in range(TOP_K):
        wgt = wgt + jnp.where(slot == route[:, TOP_K + k:TOP_K + k + 1],
                              route[:, 2 * TOP_K + k:2 * TOP_K + k + 1], 0.0)
    hi = wgt.astype(BF16)
    lo = (wgt - hi.astype(F32)).astype(BF16)

    for s in range(2):
        @pl.when(t % 2 == s)
        def _(s=s):
            pl.when(t + 1 < n_tiles)(functools.partial(fetch, t + 1, 1 - s))
            pltpu.make_async_copy(yb_hbm.at[pl.ds(0, pairs)], loc[s], sem.at[s]).wait()
            rows = loc[s][...].astype(BF16)
            y = _dot(hi, rows) + _dot(lo, rows)
            out = x_ref[0] + mod_ref[0, 5:6, :] * y
            if final:
                out = _rms(out) * fg_ref[...]
            o_ref[0] = out


def _combine(plan, yb, x, route, mod, final_g, final):
    bsz, s, d = x.shape
    nt = s // MOE_TILE
    pairs = LOC_ROWS
    return pl.pallas_call(
        functools.partial(_combine_kernel, nt=nt, n_tiles=bsz * nt, final=final),
        out_shape=jax.ShapeDtypeStruct((bsz, s, d), F32),
        grid_spec=pltpu.PrefetchScalarGridSpec(
            num_scalar_prefetch=3,
            grid=(bsz, nt),
            in_specs=[pl.BlockSpec(memory_space=pl.ANY),
                      pl.BlockSpec((1, MOE_TILE, d), lambda b, i, *_: (b, i, 0)),
                      pl.BlockSpec((1, MOE_TILE, LANE), lambda b, i, *_: (b, i, 0)),
                      pl.BlockSpec((1, 6, d), lambda b, i, *_: (b, 0, 0)),
                      pl.BlockSpec((1, d), lambda b, i, *_: (0, 0))],
            out_specs=pl.BlockSpec((1, MOE_TILE, d), lambda b, i, *_: (b, i, 0)),
            scratch_shapes=[pltpu.VMEM((pairs, d), F32), pltpu.VMEM((pairs, d), F32),
                            pltpu.SemaphoreType.DMA((2,))]),
        compiler_params=pltpu.CompilerParams(dimension_semantics=("arbitrary", "arbitrary"),
                                             vmem_limit_bytes=VMEM_LIMIT, disable_bounds_checks=True),
        name="moe_combine",
    )(plan["cnt"], plan["start"], plan["row"], yb, x, route, mod, final_g.reshape(1, d))


def _rearranged_in_proj(w_in, b_in):
    sizes = (ML_HEADS * ML_QK, ML_HEADS * ML_QK, ML_HEADS * ML_V, ML_HEADS * ML_V, ML_HEADS, ML_HEADS,
             DSA_HEADS * DSA_DIM, DSA_LATENT, IDX_HEADS * IDX_DIM, IDX_DIM, IDX_HEADS,
             3 * FOX_HEADS * FOX_DIM, FOX_HEADS, N_BRANCH * w_in.shape[0])
    offs = [0]
    for sz in sizes:
        offs.append(offs[-1] + sz)
    (o_mq, o_mk, o_mv, o_mo, o_mi, o_mf, o_dq, o_ckv, o_iq, o_ik, o_iw, o_fx, o_ff, o_g, o_end) = offs
    pad = LANE - (IDX_DIM + 2 * ML_HEADS + IDX_HEADS + FOX_HEADS)

    def cols(a):
        parts = [a[..., o_mq:o_mi],
                 a[..., o_dq:o_ckv],
                 a[..., o_iq:o_ik],
                 a[..., o_ckv:o_iq],
                 a[..., o_ik:o_iw],
                 a[..., o_mi:o_dq],
                 a[..., o_iw:o_fx],
                 a[..., o_ff:o_g],
                 jnp.zeros(a.shape[:-1] + (pad,), a.dtype),
                 a[..., o_fx:o_ff],
                 a[..., o_g:o_end]]
        return jnp.concatenate(parts, axis=-1)

    return cols(w_in).astype(BF16), cols(b_in.reshape(1, -1))


def _moe_plan(stats, bm, n_blocks):
    cnt = stats[:, 0, :N_EXPERTS].astype(jnp.int32)
    start = stats[:, 1, :N_EXPERTS].astype(jnp.int32)
    total = jnp.sum(cnt, axis=0)
    padded = (total + bm - 1) // bm * bm
    pad_end = jnp.cumsum(padded)
    pad_start = pad_end - padded
    row = pad_start[None, :] + jnp.cumsum(cnt, axis=0) - cnt
    used = jnp.sum(cnt, axis=1, keepdims=True)
    cnt = jnp.concatenate([cnt, LOC_ROWS - used], axis=1)
    start = jnp.concatenate([start, used], axis=1)
    parity = jnp.arange(cnt.shape[0], dtype=jnp.int32)[:, None] % 2
    row = jnp.concatenate([row, n_blocks * bm + parity * MOE_TILE], axis=1)
    blk_row = jnp.arange(n_blocks + 1, dtype=jnp.int32) * bm
    blk_expert = jnp.minimum(jnp.sum((pad_end[None, :] <= blk_row[:, None]).astype(jnp.int32), axis=1),
                             N_EXPERTS - 1)
    n_used = (pad_end[-1:] // bm).astype(jnp.int32)
    plan = dict(cnt=cnt.reshape(-1), start=start.reshape(-1), row=row.reshape(-1).astype(jnp.int32),
                pad_end=pad_end.astype(jnp.int32), cnt_end=(pad_start + total).astype(jnp.int32))
    return plan, blk_expert, n_used


def kernel(x, c, w_ada, b_ada, w_in, b_in, conv_w, conv_b, ml_norm_g, kv_norm_g, w_uk, w_uv,
           w_br_ml, w_br_dsa, w_br_fox, w_out, w_router, b_router, w1, b1, w2, b2, final_g):
    bsz, s, d = x.shape
    depth = w_in.shape[0]
    n = bsz * s
    bm = 512
    n_blocks = -(-(n * TOP_K + (ROW_ALIGN - 1) * N_EXPERTS * (n // MOE_TILE)) // bm) + N_EXPERTS
    mods = _ada_mod(c, w_ada, b_ada).reshape(depth, bsz, 6, d)
    for l in range(depth):
        mod = mods[l]
        w_r, b_r = _rearranged_in_proj(w_in[l], b_in[l])
        proj, misc = _in_proj(x, mod, w_r, b_r)
        y_ml, fcol, frow = _mlstm(proj, misc, conv_w[l], conv_b[l], ml_norm_g[l])
        y_fox = _fox(proj, fcol, frow)
        y_dsa = _dsa(proj, misc, kv_norm_g[l], w_uk[l].astype(BF16), jnp.swapaxes(w_uv[l], 1, 2).astype(BF16))
        x = _merge(x, proj, y_ml, y_dsa, y_fox, mod, w_br_ml[l].astype(BF16), w_br_dsa[l].astype(BF16),
                   w_br_fox[l].astype(BF16), w_out[l].astype(BF16))
        h2, route, stats = _router(x, mod, w_router[l], b_router[l])
        plan, blk_expert, n_used = _moe_plan(stats, bm, n_blocks)
        xs = _dispatch(plan, h2.reshape(n, d), route.reshape(n, LANE), (n_blocks + 1) * bm, bm)
        yb = _experts(blk_expert, n_used, xs, w1, b1, w2, b2, l, bm)
        x = _combine(plan, yb, x, route, mod, final_g, final=(l == depth - 1))
    return x
```

```python
import functools

import jax
import jax.numpy as jnp
from jax import lax
from jax.experimental import pallas as pl
from jax.experimental.pallas import tpu as pltpu

F32 = jnp.float32
BF16 = jnp.bfloat16
HIGHEST = lax.Precision.HIGHEST

EPS = 1e-6
LOG2E = 1.4426950408889634
CHUNK = 64

ML_HEADS, ML_QK, ML_V, ML_CONV = 4, 64, 128, 4
DSA_HEADS, DSA_DIM, DSA_LATENT = 4, 128, 128
IDX_HEADS, IDX_DIM, DSA_TOPK = 4, 64, 256
FOX_HEADS, FOX_DIM = 4, 128
N_BRANCH = 3
N_EXPERTS, TOP_K = 32, 4
SWIGLU_LIMIT, SWIGLU_ALPHA = 7.0, 1.702

LANE = 128
ROW_ALIGN = 8
INT_MIN = -2 ** 31

MOE_TILE = 256
N_RUNS = N_EXPERTS + 1
LOC_ROWS = TOP_K * MOE_TILE + MOE_TILE
ZERO_ROWS = 128

C_MLQK = 0
C_MLV = 512
C_MLO = 1024
C_DQ = 1536
C_DIQ = 2048
C_CKV = 2304
C_MISC = 2432
C_FOX = 2560
C_GATE = 4096
NP = 7168
M_IK, M_MLI, M_MLF, M_IW, M_FXF = 0, 64, 68, 72, 76

VMEM_LIMIT = 56 * 1024 * 1024


def _dot(a, b, prec=None):
    return jnp.dot(a, b, preferred_element_type=F32, precision=prec)


def _dot_nt(a, b, prec=None):
    return lax.dot_general(a, b, (((1,), (1,)), ((), ())), preferred_element_type=F32, precision=prec)


def _dot_tn(a, b):
    return lax.dot_general(a, b, (((0,), (0,)), ((), ())), preferred_element_type=F32)


def _sigmoid(x):
    return 1.0 / (1.0 + jnp.exp(-x))


def _log_sigmoid(x):
    return jnp.minimum(x, 0.0) - jnp.log1p(jnp.exp(-jnp.abs(x)))


def _rms(x):
    return x * lax.rsqrt(jnp.mean(x * x, axis=-1, keepdims=True) + EPS)


def _params(sem, vmem=VMEM_LIMIT):
    return pltpu.CompilerParams(dimension_semantics=sem, vmem_limit_bytes=vmem)


def _ada_kernel(c_ref, w_ref, b_ref, o_ref):
    c = c_ref[...]
    o_ref[0] = _dot(c * _sigmoid(c), w_ref[0], HIGHEST) + b_ref[0]


def _ada_mod(c, w_ada, b_ada):
    depth, d, n = w_ada.shape
    bsz = c.shape[0]
    tn = 1536
    return pl.pallas_call(
        _ada_kernel,
        out_shape=jax.ShapeDtypeStruct((depth, bsz, n), F32),
        grid=(depth, n // tn),
        in_specs=[pl.BlockSpec((bsz, d), lambda l, j: (0, 0)),
                  pl.BlockSpec((1, d, tn), lambda l, j: (l, 0, j)),
                  pl.BlockSpec((1, 1, tn), lambda l, j: (l, 0, j))],
        out_specs=pl.BlockSpec((1, bsz, tn), lambda l, j: (l, 0, j)),
        compiler_params=_params(("parallel", "parallel")),
        name="ada_mod",
    )(c, w_ada, b_ada.reshape(depth, 1, n))


def _inproj_kernel(x_ref, mod_ref, w_ref, b_ref, o_ref, misc_ref, h_scr, *, tn):
    j = pl.program_id(2)

    @pl.when(j == 0)
    def _():
        h = _rms(x_ref[0]) * (1.0 + mod_ref[0, 1:2, :]) + mod_ref[0, 0:1, :]
        h_scr[...] = h.astype(BF16)

    acc = _dot(h_scr[...], w_ref[...]) + b_ref[...]
    o_ref[0] = acc.astype(BF16)

    @pl.when(j == C_MISC // tn)
    def _():
        misc_ref[0] = acc[:, C_MISC % tn:C_MISC % tn + LANE]


def _in_proj(x, mod, w, b):
    bsz, s, d = x.shape
    n = w.shape[1]
    tm = min(1024, s)
    tn = 1024
    return pl.pallas_call(
        functools.partial(_inproj_kernel, tn=tn),
        out_shape=(jax.ShapeDtypeStruct((bsz, s, n), BF16), jax.ShapeDtypeStruct((bsz, s, LANE), F32)),
        grid=(bsz, s // tm, n // tn),
        in_specs=[pl.BlockSpec((1, tm, d), lambda bi, i, j: (bi, i, 0)),
                  pl.BlockSpec((1, 6, d), lambda bi, i, j: (bi, 0, 0)),
                  pl.BlockSpec((d, tn), lambda bi, i, j: (0, j)),
                  pl.BlockSpec((1, tn), lambda bi, i, j: (0, j))],
        out_specs=(pl.BlockSpec((1, tm, tn), lambda bi, i, j: (bi, i, j)),
                   pl.BlockSpec((1, tm, LANE), lambda bi, i, j: (bi, i, 0))),
        scratch_shapes=[pltpu.VMEM((tm, d), BF16)],
        compiler_params=_params(("parallel", "parallel", "arbitrary")),
        name="in_proj",
    )(x, mod, w, b)


def _mlstm_kernel(qk_ref, v_ref, o_ref, misc_ref, cw_ref, cb_ref, g_ref,
                  y_ref, fcol_ref, frow_ref,
                  xext, ct_scr, n_scr, m_scr, carry_scr, *, L):
    c = pl.program_id(1)
    nqk = ML_HEADS * ML_QK

    @pl.when(c == 0)
    def _():
        xext[0:8, :] = jnp.zeros((8, 2 * nqk), F32)
        ct_scr[...] = jnp.zeros(ct_scr.shape, F32)
        n_scr[...] = jnp.zeros(n_scr.shape, F32)
        m_scr[...] = jnp.full(m_scr.shape, -jnp.inf, F32)
        carry_scr[...] = jnp.zeros(carry_scr.shape, F32)

    @pl.when(c > 0)
    def _():
        xext[0:8, :] = xext[L:L + 8, :]

    xext[8:8 + L, :] = qk_ref[0].astype(F32)
    cw = cw_ref[...]
    conv = (cb_ref[...] + cw[3:4, :] * xext[8:8 + L, :] + cw[2:3, :] * xext[7:7 + L, :]
            + cw[1:2, :] * xext[6:6 + L, :] + cw[0:1, :] * xext[5:5 + L, :])
    qk = conv * _sigmoid(conv)

    misc = misc_ref[0]
    ls = _log_sigmoid(misc)
    row = lax.broadcasted_iota(jnp.int32, (L, L), 0)
    col = lax.broadcasted_iota(jnp.int32, (L, L), 1)
    causal = row >= col
    tri = jnp.where(causal, 1.0, 0.0).astype(F32)
    cs = _dot(tri, ls, HIGHEST)
    cs_t = cs.T
    misc_t = misc.T
    carry = carry_scr[0:1, :]
    fcol_ref[0] = cs + carry
    for h in range(FOX_HEADS):
        frow_ref[0, h:h + 1, :] = cs_t[M_FXF + h:M_FXF + h + 1, :] + carry[:, M_FXF + h:M_FXF + h + 1]
    frow_ref[0, 4:8, :] = jnp.zeros((4, L), F32)
    carry_scr[0:1, :] = carry + cs[L - 1:L, :]

    for h in range(ML_HEADS):
        qh = qk[:, h * ML_QK:(h + 1) * ML_QK] * (ML_QK ** -0.5)
        kh = qk[:, nqk + h * ML_QK:nqk + (h + 1) * ML_QK]
        vb = v_ref[0, :, h * ML_V:(h + 1) * ML_V].astype(BF16)
        i_col = misc[:, M_MLI + h:M_MLI + h + 1]
        i_row = misc_t[M_MLI + h:M_MLI + h + 1, :]
        b_col = cs[:, M_MLF + h:M_MLF + h + 1]
        b_row = cs_t[M_MLF + h:M_MLF + h + 1, :]
        b_last = b_col[L - 1:L, :]
        m_prev = m_scr[h:h + 1, 0:1]

        d_log = jnp.where(causal, b_col - b_row + i_row, -jnp.inf)
        inter_log = b_col + m_prev
        m_out = jnp.maximum(inter_log, jnp.max(d_log, axis=-1, keepdims=True))
        qb = qh.astype(BF16)
        kb = kh.astype(BF16)
        s = _dot_nt(qb, kb) * jnp.exp(d_log - m_out)
        a_inter = jnp.exp(inter_log - m_out)
        ct = ct_scr[h]
        n_row = n_scr[h]
        num = _dot(s.astype(BF16), vb) + a_inter * _dot(qb, ct.astype(BF16))
        den = jnp.sum(s, axis=-1, keepdims=True) + a_inter * jnp.sum(qh * n_row, axis=-1, keepdims=True)
        hid = num / jnp.maximum(jnp.abs(den), jnp.exp(-m_out))

        w_state = b_last - b_col + i_col
        m_loc = jnp.max(w_state, axis=0, keepdims=True)
        ke = kh * jnp.exp(w_state - m_loc)
        c_loc = _dot_tn(ke.astype(BF16), vb)
        n_loc = jnp.sum(ke, axis=0, keepdims=True)
        m_new = jnp.maximum(b_last + m_prev, m_loc)
        decay = jnp.exp(b_last + m_prev - m_new)
        scale = jnp.exp(m_loc - m_new)
        ct_scr[h] = decay * ct + scale * c_loc
        n_scr[h] = decay * n_row + scale * n_loc
        m_scr[h:h + 1, :] = jnp.broadcast_to(m_new, (1, LANE))

        y = (_rms(hid) * g_ref[:, h * ML_V:(h + 1) * ML_V]
             * _sigmoid(o_ref[0, :, h * ML_V:(h + 1) * ML_V].astype(F32)))
        y_ref[0, :, h * ML_V:(h + 1) * ML_V] = y.astype(BF16)


def _mlstm(proj, misc, conv_w, conv_b, norm_g):
    bsz, s, _ = proj.shape
    L = min(256, s)
    w = 2 * ML_HEADS * ML_QK
    wv = ML_HEADS * ML_V
    return pl.pallas_call(
        functools.partial(_mlstm_kernel, L=L),
        out_shape=(jax.ShapeDtypeStruct((bsz, s, wv), BF16),
                   jax.ShapeDtypeStruct((bsz, s, LANE), F32),
                   jax.ShapeDtypeStruct((bsz, 8, s), F32)),
        grid=(bsz, s // L),
        in_specs=[pl.BlockSpec((1, L, w), lambda b, c: (b, c, C_MLQK // w)),
                  pl.BlockSpec((1, L, wv), lambda b, c: (b, c, C_MLV // wv)),
                  pl.BlockSpec((1, L, wv), lambda b, c: (b, c, C_MLO // wv)),
                  pl.BlockSpec((1, L, LANE), lambda b, c: (b, c, 0)),
                  pl.BlockSpec((ML_CONV, w), lambda b, c: (0, 0)),
                  pl.BlockSpec((1, w), lambda b, c: (0, 0)),
                  pl.BlockSpec((1, wv), lambda b, c: (0, 0))],
        out_specs=(pl.BlockSpec((1, L, wv), lambda b, c: (b, c, 0)),
                   pl.BlockSpec((1, L, LANE), lambda b, c: (b, c, 0)),
                   pl.BlockSpec((1, 8, L), lambda b, c: (b, 0, c))),
        scratch_shapes=[pltpu.VMEM((L + 8, w), F32),
                        pltpu.VMEM((ML_HEADS, ML_QK, ML_V), F32),
                        pltpu.VMEM((ML_HEADS, 1, ML_QK), F32),
                        pltpu.VMEM((8, LANE), F32),
                        pltpu.VMEM((8, LANE), F32)],
        compiler_params=_params(("parallel", "arbitrary")),
        name="mlstm",
    )(proj, proj, proj, misc, conv_w, conv_b.reshape(1, w), norm_g.reshape(1, wv))


def _fox_kernel(q_ref, k_ref, v_ref, fcol_ref, frow_ref, y_ref, k_scr, vt_scr, *, T, S):
    qi = pl.program_id(1)
    d = FOX_DIM

    @pl.when(qi == 0)
    def _():
        for h in range(FOX_HEADS):
            k_scr[h] = k_ref[0, :, h * d:(h + 1) * d].astype(BF16)
            vt_scr[h] = v_ref[0, :, h * d:(h + 1) * d].astype(F32).T.astype(BF16)

    q_t = (q_ref[0].astype(F32) * (d ** -0.5 * LOG2E)).T

    def body(ext):
        kpos = lax.broadcasted_iota(jnp.int32, (ext, 1), 0)
        qpos = (ext - T) + lax.broadcasted_iota(jnp.int32, (1, T), 1)
        causal = kpos <= qpos
        for h in range(FOX_HEADS):
            s = _dot(k_scr[h, 0:ext, :], q_t[h * d:(h + 1) * d, :].astype(BF16))
            s = s + (frow_ref[0, h:h + 1, :] * LOG2E - fcol_ref[0, 0:ext, M_FXF + h:M_FXF + h + 1] * LOG2E)
            s = jnp.where(causal, s, -jnp.inf)
            p = jnp.exp2(s - jnp.max(s, axis=0, keepdims=True))
            l = jnp.sum(p, axis=0, keepdims=True)
            out_t = _dot(vt_scr[h, :, 0:ext], p.astype(BF16)) / l
            y_ref[0, :, h * d:(h + 1) * d] = out_t.T.astype(BF16)

    for c in range(S // T):
        pl.when(qi == c)(functools.partial(body, (c + 1) * T))


def _fox(proj, fcol, frow):
    bsz, s, _ = proj.shape
    T = min(256, s)
    w = FOX_HEADS * FOX_DIM
    return pl.pallas_call(
        functools.partial(_fox_kernel, T=T, S=s),
        out_shape=jax.ShapeDtypeStruct((bsz, s, w), BF16),
        grid=(bsz, s // T),
        in_specs=[pl.BlockSpec((1, T, w), lambda b, i: (b, i, C_FOX // w)),
                  pl.BlockSpec((1, s, w), lambda b, i: (b, 0, C_FOX // w + 1)),
                  pl.BlockSpec((1, s, w), lambda b, i: (b, 0, C_FOX // w + 2)),
                  pl.BlockSpec((1, s, LANE), lambda b, i: (b, 0, 0)),
                  pl.BlockSpec((1, 8, T), lambda b, i: (b, 0, i))],
        out_specs=pl.BlockSpec((1, T, w), lambda b, i: (b, i, 0)),
        scratch_shapes=[pltpu.VMEM((FOX_HEADS, s, FOX_DIM), BF16),
                        pltpu.VMEM((FOX_HEADS, FOX_DIM, s), BF16)],
        compiler_params=_params(("parallel", "arbitrary")),
        name="fox_attention",
    )(proj, proj, proj, fcol, frow)


def _dsa_body(ext, qi, q_ref, qidx_ref, misc_q_ref, wuk_ref, wuv_ref, y_ref,
              ckvn_scr, ckvnt_scr, kidx_scr, sel_scr, *, T, n_sel, rank_tile):
    dh = DSA_DIM
    q_t = q_ref[0].astype(F32).T
    qidx_t = qidx_ref[0].astype(F32).T.astype(BF16)
    w_t = misc_q_ref[0].T[M_IW:M_IW + IDX_HEADS, :] * (IDX_HEADS ** -0.5)
    kidx = kidx_scr[0:ext, :]
    score = jnp.zeros((ext, T), F32)
    for h in range(IDX_HEADS):
        lg = _dot(kidx, qidx_t[h * IDX_DIM:(h + 1) * IDX_DIM, :]) * (IDX_DIM ** -0.5)
        score = score + w_t[h:h + 1, :] * jnp.maximum(lg, 0.0)
    kpos = lax.broadcasted_iota(jnp.int32, (ext, 1), 0)
    qchunk = (qi * T + lax.broadcasted_iota(jnp.int32, (1, T), 1)) // CHUNK
    score = jnp.where((kpos // CHUNK) <= qchunk, score, -jnp.inf)

    def as_float(c):
        return pltpu.bitcast(jnp.where(c < 0, c ^ jnp.int32(0x7FFFFFFF), c), F32)

    def count_ge(c):
        return jnp.sum(jnp.where(score >= as_float(c), 1.0, 0.0), axis=0, keepdims=True)

    t0 = jnp.where(count_ge(jnp.zeros((1, T), jnp.int32)) >= n_sel, jnp.int32(0), jnp.int32(INT_MIN))

    def bis(i, t):
        cand = t + jnp.left_shift(jnp.int32(1), jnp.int32(30) - i)
        return jnp.where(count_ge(cand) >= n_sel, cand, t)

    t = lax.fori_loop(0, 31, bis, t0)
    n_vis = ((qchunk + 1) * CHUNK).astype(F32)
    thr = jnp.where(n_vis <= n_sel, -3.0e38, as_float(t))
    need = n_sel - jnp.sum(jnp.where(score > thr, 1.0, 0.0), axis=0, keepdims=True)
    r_i = lax.broadcasted_iota(jnp.int32, (rank_tile, rank_tile), 0)
    c_i = lax.broadcasted_iota(jnp.int32, (rank_tile, rank_tile), 1)
    lower = jnp.where(c_i < r_i, 1.0, 0.0).astype(BF16)
    carry = jnp.zeros((1, T), F32)
    for j in range(ext // rank_tile):
        sc = score[j * rank_tile:(j + 1) * rank_tile, :]
        eq = jnp.where(sc == thr, 1.0, 0.0)
        rank = _dot(lower, eq.astype(BF16)) + carry
        carry = carry + jnp.sum(eq, axis=0, keepdims=True)
        sel_scr[j * rank_tile:(j + 1) * rank_tile, :] = jnp.where(
            sc > thr, 1.0, jnp.where(rank < need, eq, 0.0))

    sel = sel_scr[0:ext, :] > 0.5
    ckvn = ckvn_scr[0:ext, :]
    ckvn_t = ckvnt_scr[:, 0:ext]
    for h in range(DSA_HEADS):
        qa_t = _dot(wuk_ref[h], q_t[h * dh:(h + 1) * dh, :].astype(BF16)) * (dh ** -0.5 * LOG2E)
        lg = _dot(ckvn, qa_t.astype(BF16))
        lg = jnp.where(sel, lg, -jnp.inf)
        p = jnp.exp2(lg - jnp.max(lg, axis=0, keepdims=True))
        l = jnp.sum(p, axis=0, keepdims=True)
        lat_t = _dot(ckvn_t, p.astype(BF16)) / l
        out_t = _dot(wuv_ref[h], lat_t.astype(BF16))
        y_ref[0, :, h * dh:(h + 1) * dh] = out_t.T.astype(BF16)


def _dsa_kernel(q_ref, ckv_ref, qidx_ref, misc_all_ref, misc_q_ref, g_ref, wuk_ref, wuv_ref, y_ref,
                ckvn_scr, ckvnt_scr, kidx_scr, sel_scr, *, T, S, n_sel, n_cls, rank_tile):
    qi = pl.program_id(1)

    @pl.when(qi == 0)
    def _():
        ckvn = _rms(ckv_ref[0].astype(F32)) * g_ref[...]
        ckvn_scr[...] = ckvn.astype(BF16)
        ckvnt_scr[...] = ckvn.T.astype(BF16)
        kidx_scr[...] = misc_all_ref[0, :, M_IK:M_IK + IDX_DIM].astype(BF16)

    per = (S // T) // n_cls
    for c in range(n_cls):
        ext = (c + 1) * per * T

        @pl.when(qi // per == c)
        def _(ext=ext):
            _dsa_body(ext, qi, q_ref, qidx_ref, misc_q_ref, wuk_ref, wuv_ref, y_ref,
                      ckvn_scr, ckvnt_scr, kidx_scr, sel_scr, T=T, n_sel=n_sel, rank_tile=rank_tile)


def _dsa(proj, misc, kv_g, wuk, wuv_t):
    bsz, s, _ = proj.shape
    T = min(256, s)
    n_sel = min(DSA_TOPK, s // 4)
    n_cls = s // T
    rank_tile = 256
    w = DSA_HEADS * DSA_DIM
    wi = IDX_HEADS * IDX_DIM
    return pl.pallas_call(
        functools.partial(_dsa_kernel, T=T, S=s, n_sel=float(n_sel), n_cls=n_cls, rank_tile=rank_tile),
        out_shape=jax.ShapeDtypeStruct((bsz, s, w), BF16),
        grid=(bsz, s // T),
        in_specs=[pl.BlockSpec((1, T, w), lambda b, i: (b, i, C_DQ // w)),
                  pl.BlockSpec((1, s, DSA_LATENT), lambda b, i: (b, 0, C_CKV // DSA_LATENT)),
                  pl.BlockSpec((1, T, wi), lambda b, i: (b, i, C_DIQ // wi)),
                  pl.BlockSpec((1, s, LANE), lambda b, i: (b, 0, 0)),
                  pl.BlockSpec((1, T, LANE), lambda b, i: (b, i, 0)),
                  pl.BlockSpec((1, DSA_LATENT), lambda b, i: (0, 0)),
                  pl.BlockSpec((DSA_HEADS, DSA_LATENT, DSA_DIM), lambda b, i: (0, 0, 0)),
                  pl.BlockSpec((DSA_HEADS, DSA_DIM, DSA_LATENT), lambda b, i: (0, 0, 0))],
        out_specs=pl.BlockSpec((1, T, w), lambda b, i: (b, i, 0)),
        scratch_shapes=[pltpu.VMEM((s, DSA_LATENT), BF16),
                        pltpu.VMEM((DSA_LATENT, s), BF16),
                        pltpu.VMEM((s, IDX_DIM), BF16),
                        pltpu.VMEM((s, T), F32)],
        compiler_params=_params(("parallel", "arbitrary")),
        name="dsa_attention",
    )(proj, proj, proj, misc, misc, kv_g.reshape(1, DSA_LATENT), wuk, wuv_t)


def _merge_kernel(yml_ref, ydsa_ref, yfox_ref, g0_ref, g1_ref, g2_ref, x_ref, mod_ref,
                  wml_ref, wdsa_ref, wfox_ref, wout_ref, o_ref):
    merged = (_sigmoid(g0_ref[0].astype(F32)) * _dot(yml_ref[0], wml_ref[...])
              + _sigmoid(g1_ref[0].astype(F32)) * _dot(ydsa_ref[0], wdsa_ref[...])
              + _sigmoid(g2_ref[0].astype(F32)) * _dot(yfox_ref[0], wfox_ref[...]))
    out = _dot(merged.astype(BF16), wout_ref[...])
    o_ref[0] = x_ref[0] + mod_ref[0, 2:3, :] * out


def _merge(x, proj, y_ml, y_dsa, y_fox, mod, w_ml, w_dsa, w_fox, w_out):
    bsz, s, d = x.shape
    tm = min(512, s)
    wb = y_ml.shape[-1]
    gb = C_GATE // d
    yspec = pl.BlockSpec((1, tm, wb), lambda b, i: (b, i, 0))
    wspec = pl.BlockSpec((wb, d), lambda b, i: (0, 0))
    return pl.pallas_call(
        _merge_kernel,
        out_shape=jax.ShapeDtypeStruct((bsz, s, d), F32),
        grid=(bsz, s // tm),
        in_specs=[yspec, yspec, yspec,
                  pl.BlockSpec((1, tm, d), lambda b, i: (b, i, gb)),
                  pl.BlockSpec((1, tm, d), lambda b, i: (b, i, gb + 1)),
                  pl.BlockSpec((1, tm, d), lambda b, i: (b, i, gb + 2)),
                  pl.BlockSpec((1, tm, d), lambda b, i: (b, i, 0)),
                  pl.BlockSpec((1, 6, d), lambda b, i: (b, 0, 0)),
                  wspec, wspec, wspec,
                  pl.BlockSpec((d, d), lambda b, i: (0, 0))],
        out_specs=pl.BlockSpec((1, tm, d), lambda b, i: (b, i, 0)),
        compiler_params=_params(("parallel", "parallel")),
        name="merge_out",
    )(y_ml, y_dsa, y_fox, proj, proj, proj, x, mod, w_ml, w_dsa, w_fox, w_out)


def _router_kernel(x_ref, mod_ref, wr_ref, br_ref, h_ref, route_ref, stat_ref, *, T):
    h = _rms(x_ref[0]) * (1.0 + mod_ref[0, 4:5, :]) + mod_ref[0, 3:4, :]
    h_ref[0] = h
    lg = _dot(h, wr_ref[...], HIGHEST) + br_ref[...]
    lane = lax.broadcasted_iota(jnp.int32, (T, LANE), 1).astype(F32)
    vals, hots = [], []
    for _ in range(TOP_K):
        mx = jnp.max(lg, axis=-1, keepdims=True)
        idx = jnp.min(jnp.where(lg == mx, lane, float(LANE)), axis=-1, keepdims=True)
        hot = lane == idx
        vals.append(mx)
        hots.append(hot)
        lg = jnp.where(hot, -jnp.inf, lg)
    exps = [jnp.exp(v - vals[0]) for v in vals]
    tot = exps[0] + exps[1] + exps[2] + exps[3]
    multi = jnp.zeros((T, LANE), F32)
    for hot in hots:
        multi = multi + jnp.where(hot, 1.0, 0.0)
    r_i = lax.broadcasted_iota(jnp.int32, (T, T), 0)
    c_i = lax.broadcasted_iota(jnp.int32, (T, T), 1)
    lower = jnp.where(c_i < r_i, 1.0, 0.0).astype(BF16)
    before = _dot(lower, multi.astype(BF16))
    cnt = jnp.broadcast_to(jnp.sum(multi, axis=0, keepdims=True), (8, LANE))
    cnt = jnp.floor((cnt + (ROW_ALIGN - 1.0)) * (1.0 / ROW_ALIGN)) * ROW_ALIGN
    a_i = lax.broadcasted_iota(jnp.int32, (LANE, LANE), 0)
    b_i = lax.broadcasted_iota(jnp.int32, (LANE, LANE), 1)
    start = _dot(cnt, jnp.where(a_i < b_i, 1.0, 0.0).astype(F32), HIGHEST)
    sub = lax.broadcasted_iota(jnp.int32, (8, LANE), 0)
    stat_ref[0] = jnp.where(sub == 0, cnt, jnp.where(sub == 1, start, 0.0))
    where_to = before + start[0:1, :]
    slab = jnp.zeros((T, LANE), F32)
    for k in range(TOP_K):
        e_k = jnp.sum(jnp.where(hots[k], lane, 0.0), axis=-1, keepdims=True)
        p_k = jnp.sum(jnp.where(hots[k], where_to, 0.0), axis=-1, keepdims=True)
        slab = slab + jnp.where(lane == float(k), e_k, 0.0)
        slab = slab + jnp.where(lane == float(TOP_K + k), p_k, 0.0)
        slab = slab + jnp.where(lane == float(2 * TOP_K + k), exps[k] / tot, 0.0)
    route_ref[0] = slab


def _router(x, mod, w_router, b_router):
    bsz, s, d = x.shape
    T = MOE_TILE
    nt = s // T
    ne = w_router.shape[1]
    wr = jnp.zeros((d, LANE), F32).at[:, :ne].set(w_router)
    br = jnp.full((1, LANE), -1e30, F32).at[0, :ne].set(b_router)
    return pl.pallas_call(
        functools.partial(_router_kernel, T=T),
        out_shape=(jax.ShapeDtypeStruct((bsz, s, d), F32),
                   jax.ShapeDtypeStruct((bsz, s, LANE), F32),
                   jax.ShapeDtypeStruct((bsz * nt, 8, LANE), F32)),
        grid=(bsz, nt),
        in_specs=[pl.BlockSpec((1, T, d), lambda b, i: (b, i, 0)),
                  pl.BlockSpec((1, 6, d), lambda b, i: (b, 0, 0)),
                  pl.BlockSpec((d, LANE), lambda b, i: (0, 0)),
                  pl.BlockSpec((1, LANE), lambda b, i: (0, 0))],
        out_specs=(pl.BlockSpec((1, T, d), lambda b, i: (b, i, 0)),
                   pl.BlockSpec((1, T, LANE), lambda b, i: (b, i, 0)),
                   pl.BlockSpec((1, 8, LANE), lambda b, i: (b * nt + i, 0, 0))),
        compiler_params=_params(("parallel", "parallel")),
        name="moe_router",
    )(x, mod, wr, br)


def _run_copies(tile, cnt_ref, start_ref, row_ref, make_copy):
    for r in range(N_RUNS):
        c = cnt_ref[tile * N_RUNS + r]
        src = start_ref[tile * N_RUNS + r]
        dst = row_ref[tile * N_RUNS + r]
        size = MOE_TILE
        while size >= ROW_ALIGN:
            done = c & (-2 * size)
            cp = make_copy(pl.multiple_of(src + done, ROW_ALIGN), pl.multiple_of(dst + done, ROW_ALIGN), size)
            pl.when((c & size) != 0)(cp.start)
            size //= 2


def _dispatch_kernel(cnt_ref, start_ref, row_ref, pad_end_ref, cnt_end_ref, h_ref, route_ref, xs_hbm,
                     loc0, loc1, zbuf, zsem, sem, *, BM, N_TAIL, N_T):
    t = pl.program_id(0)
    loc = (loc0, loc1)

    @pl.when(t == 0)
    def _():
        zbuf[...] = jnp.zeros(zbuf.shape, F32)
        n_rows = xs_hbm.shape[0]

        def pad_copy(e, g):
            start = pl.multiple_of(pad_end_ref[e] - g * ZERO_ROWS, ZERO_ROWS)
            return (start + ZERO_ROWS > cnt_end_ref[e],
                    pltpu.make_async_copy(zbuf.at[pl.ds(0, ZERO_ROWS)], xs_hbm.at[pl.ds(start, ZERO_ROWS)], zsem))

        def tail_copy(e):
            start = pl.multiple_of(pad_end_ref[N_EXPERTS - 1] + e * BM, BM)
            safe = pl.multiple_of(jnp.minimum(start, n_rows - BM), BM)
            return start < n_rows, pltpu.make_async_copy(zbuf, xs_hbm.at[pl.ds(safe, BM)], zsem)

        copies = [pad_copy(e, g) for e in range(N_EXPERTS) for g in range(1, BM // ZERO_ROWS + 1)]
        copies += [tail_copy(e) for e in range(N_TAIL)]
        for cond, cp in copies:
            pl.when(cond)(cp.start)
        for cond, cp in copies:
            pl.when(cond)(cp.wait)

    pos_t = route_ref[...].T[TOP_K:2 * TOP_K, :]
    slot = lax.broadcasted_iota(jnp.int32, (LOC_ROWS, 1), 0).astype(F32)
    pick = jnp.zeros((LOC_ROWS, MOE_TILE), F32)
    for k in range(TOP_K):
        pick = pick + jnp.where(slot == pos_t[k:k + 1, :], 1.0, 0.0)
    pick = pick.astype(BF16)

    def drain(s):
        pltpu.make_async_copy(loc[s], xs_hbm.at[pl.ds(0, LOC_ROWS)], sem.at[s]).wait()

    for s in range(2):
        @pl.when(t % 2 == s)
        def _(s=s):
            pl.when(t >= 2)(functools.partial(drain, s))
            loc[s][...] = _dot(pick, h_ref[...].astype(BF16))
            _run_copies(t, cnt_ref, start_ref, row_ref,
                        lambda a, b, n: pltpu.make_async_copy(loc[s].at[pl.ds(a, n)], xs_hbm.at[pl.ds(b, n)],
                                                              sem.at[s]))

    @pl.when(t == N_T - 1)
    def _():
        if N_T >= 2:
            drain(N_T % 2)
        drain((N_T - 1) % 2)


def _dispatch(plan, h2, route, n_rows, bm):
    n, d = h2.shape
    pairs = LOC_ROWS
    return pl.pallas_call(
        functools.partial(_dispatch_kernel, BM=bm, N_TAIL=n_rows // bm - (n * TOP_K) // bm, N_T=n // MOE_TILE),
        out_shape=jax.ShapeDtypeStruct((n_rows, d), F32),
        grid_spec=pltpu.PrefetchScalarGridSpec(
            num_scalar_prefetch=5,
            grid=(n // MOE_TILE,),
            in_specs=[pl.BlockSpec((MOE_TILE, d), lambda t, *_: (t, 0)),
                      pl.BlockSpec((MOE_TILE, LANE), lambda t, *_: (t, 0))],
            out_specs=pl.BlockSpec(memory_space=pl.ANY),
            scratch_shapes=[pltpu.VMEM((pairs, d), F32),
                            pltpu.VMEM((pairs, d), F32),
                            pltpu.VMEM((bm, d), F32),
                            pltpu.SemaphoreType.DMA(()),
                            pltpu.SemaphoreType.DMA((2,))]),
        compiler_params=pltpu.CompilerParams(dimension_semantics=("arbitrary",), vmem_limit_bytes=VMEM_LIMIT,
                                             disable_bounds_checks=True),
        name="moe_dispatch",
    )(plan["cnt"], plan["start"], plan["row"], plan["pad_end"], plan["cnt_end"], h2, route)


def _expert_kernel(be_ref, nu_ref, xs_ref, w1_ref, b1_ref, w2_ref, b2_ref, o_ref, w1b, w2b):
    i = pl.program_id(0)
    de = w2_ref.shape[0]
    prev = be_ref[jnp.maximum(i - 1, 0)]
    fresh = (i == 0) | (be_ref[i] != prev)

    @pl.when(fresh & (i < nu_ref[0]))
    def _():
        w1b[...] = w1_ref[...].astype(BF16)
        w2b[...] = w2_ref[...].astype(BF16)

    @pl.when(i < nu_ref[0])
    def _():
        hdn = _dot(xs_ref[...].astype(BF16), w1b[...]) + b1_ref[...]
        glu = jnp.minimum(hdn[:, :de], SWIGLU_LIMIT)
        lin = jnp.clip(hdn[:, de:], -SWIGLU_LIMIT, SWIGLU_LIMIT)
        act = glu * _sigmoid(SWIGLU_ALPHA * glu) * (lin + 1.0)
        o_ref[...] = _dot(act.astype(BF16), w2b[...]) + b2_ref[...]

    @pl.when(i >= nu_ref[0])
    def _():
        o_ref[...] = jnp.zeros(o_ref.shape, F32)


def _experts(blk_expert, n_used, xs, w1, b1, w2, b2, layer, bm):
    n_rows, d = xs.shape
    depth, ne, _, dh2 = w1.shape
    de = w2.shape[2]
    n_blocks = n_rows // bm

    def row_map(i, be, nu):
        return (jnp.minimum(i, nu[0] - 1), 0)

    def w_map(i, be, nu):
        return (layer, be[i], 0, 0)

    return pl.pallas_call(
        _expert_kernel,
        out_shape=jax.ShapeDtypeStruct((n_rows, d), F32),
        grid_spec=pltpu.PrefetchScalarGridSpec(
            num_scalar_prefetch=2,
            grid=(n_blocks,),
            in_specs=[pl.BlockSpec((bm, d), row_map),
                      pl.BlockSpec((None, None, d, dh2), w_map),
                      pl.BlockSpec((None, None, 1, dh2), w_map),
                      pl.BlockSpec((None, None, de, d), w_map),
                      pl.BlockSpec((None, None, 1, d), w_map)],
            out_specs=pl.BlockSpec((bm, d), lambda i, be, nu: (i, 0)),
            scratch_shapes=[pltpu.VMEM((d, dh2), BF16), pltpu.VMEM((de, d), BF16)]),
        compiler_params=_params(("arbitrary",)),
        name="moe_experts",
    )(blk_expert, n_used, xs, w1, b1.reshape(depth, ne, 1, dh2), w2, b2.reshape(depth, ne, 1, d))


def _combine_kernel(cnt_ref, start_ref, row_ref, yb_hbm, x_ref, route_ref, mod_ref, fg_ref, o_ref,
                    loc0, loc1, sem, *, nt, n_tiles, final):
    t = pl.program_id(0) * nt + pl.program_id(1)
    pairs = LOC_ROWS
    loc = (loc0, loc1)

    def fetch(tile, s):
        _run_copies(tile, cnt_ref, start_ref, row_ref,
                    lambda a, b, n: pltpu.make_async_copy(yb_hbm.at[pl.ds(b, n)], loc[s].at[pl.ds(a, n)],
                                                          sem.at[s]))

    pl.when(t == 0)(functools.partial(fetch, 0, 0))
    route = route_ref[0]
    slot = lax.broadcasted_iota(jnp.int32, (1, pairs), 1).astype(F32)
    wgt = jnp.zeros((MOE_TILE, pairs), F32)
    for k in range(TOP_K):
        wgt = wgt + jnp.where(slot == route[:, TOP_K + k:TOP_K + k + 1],
                              route[:, 2 * TOP_K + k:2 * TOP_K + k + 1], 0.0)
    hi = wgt.astype(BF16)
    lo = (wgt - hi.astype(F32)).astype(BF16)

    for s in range(2):
        @pl.when(t % 2 == s)
        def _(s=s):
            pl.when(t + 1 < n_tiles)(functools.partial(fetch, t + 1, 1 - s))
            pltpu.make_async_copy(yb_hbm.at[pl.ds(0, pairs)], loc[s], sem.at[s]).wait()
            rows = loc[s][...].astype(BF16)
            y = _dot(hi, rows) + _dot(lo, rows)
            out = x_ref[0] + mod_ref[0, 5:6, :] * y
            if final:
                out = _rms(out) * fg_ref[...]
            o_ref[0] = out


def _combine(plan, yb, x, route, mod, final_g, final):
    bsz, s, d = x.shape
    nt = s // MOE_TILE
    pairs = LOC_ROWS
    return pl.pallas_call(
        functools.partial(_combine_kernel, nt=nt, n_tiles=bsz * nt, final=final),
        out_shape=jax.ShapeDtypeStruct((bsz, s, d), F32),
        grid_spec=pltpu.PrefetchScalarGridSpec(
            num_scalar_prefetch=3,
            grid=(bsz, nt),
            in_specs=[pl.BlockSpec(memory_space=pl.ANY),
                      pl.BlockSpec((1, MOE_TILE, d), lambda b, i, *_: (b, i, 0)),
                      pl.BlockSpec((1, MOE_TILE, LANE), lambda b, i, *_: (b, i, 0)),
                      pl.BlockSpec((1, 6, d), lambda b, i, *_: (b, 0, 0)),
                      pl.BlockSpec((1, d), lambda b, i, *_: (0, 0))],
            out_specs=pl.BlockSpec((1, MOE_TILE, d), lambda b, i, *_: (b, i, 0)),
            scratch_shapes=[pltpu.VMEM((pairs, d), F32), pltpu.VMEM((pairs, d), F32),
                            pltpu.SemaphoreType.DMA((2,))]),
        compiler_params=pltpu.CompilerParams(dimension_semantics=("arbitrary", "arbitrary"),
                                             vmem_limit_bytes=VMEM_LIMIT, disable_bounds_checks=True),
        name="moe_combine",
    )(plan["cnt"], plan["start"], plan["row"], yb, x, route, mod, final_g.reshape(1, d))


def _rearranged_in_proj(w_in, b_in):
    sizes = (ML_HEADS * ML_QK, ML_HEADS * ML_QK, ML_HEADS * ML_V, ML_HEADS * ML_V, ML_HEADS, ML_HEADS,
             DSA_HEADS * DSA_DIM, DSA_LATENT, IDX_HEADS * IDX_DIM, IDX_DIM, IDX_HEADS,
             3 * FOX_HEADS * FOX_DIM, FOX_HEADS, N_BRANCH * w_in.shape[0])
    offs = [0]
    for sz in sizes:
        offs.append(offs[-1] + sz)
    (o_mq, o_mk, o_mv, o_mo, o_mi, o_mf, o_dq, o_ckv, o_iq, o_ik, o_iw, o_fx, o_ff, o_g, o_end) = offs
    pad = LANE - (IDX_DIM + 2 * ML_HEADS + IDX_HEADS + FOX_HEADS)

    def cols(a):
        parts = [a[..., o_mq:o_mi],
                 a[..., o_dq:o_ckv],
                 a[..., o_iq:o_ik],
                 a[..., o_ckv:o_iq],
                 a[..., o_ik:o_iw],
                 a[..., o_mi:o_dq],
                 a[..., o_iw:o_fx],
                 a[..., o_ff:o_g],
                 jnp.zeros(a.shape[:-1] + (pad,), a.dtype),
                 a[..., o_fx:o_ff],
                 a[..., o_g:o_end]]
        return jnp.concatenate(parts, axis=-1)

    return cols(w_in).astype(BF16), cols(b_in.reshape(1, -1))


def _moe_plan(stats, bm, n_blocks):
    cnt = stats[:, 0, :N_EXPERTS].astype(jnp.int32)
    start = stats[:, 1, :N_EXPERTS].astype(jnp.int32)
    total = jnp.sum(cnt, axis=0)
    padded = (total + bm - 1) // bm * bm
    pad_end = jnp.cumsum(padded)
    pad_start = pad_end - padded
    row = pad_start[None, :] + jnp.cumsum(cnt, axis=0) - cnt
    used = jnp.sum(cnt, axis=1, keepdims=True)
    cnt = jnp.concatenate([cnt, LOC_ROWS - used], axis=1)
    start = jnp.concatenate([start, used], axis=1)
    parity = jnp.arange(cnt.shape[0], dtype=jnp.int32)[:, None] % 2
    row = jnp.concatenate([row, n_blocks * bm + parity * MOE_TILE], axis=1)
    blk_row = jnp.arange(n_blocks + 1, dtype=jnp.int32) * bm
    blk_expert = jnp.minimum(jnp.sum((pad_end[None, :] <= blk_row[:, None]).astype(jnp.int32), axis=1),
                             N_EXPERTS - 1)
    n_used = (pad_end[-1:] // bm).astype(jnp.int32)
    plan = dict(cnt=cnt.reshape(-1), start=start.reshape(-1), row=row.reshape(-1).astype(jnp.int32),
                pad_end=pad_end.astype(jnp.int32), cnt_end=(pad_start + total).astype(jnp.int32))
    return plan, blk_expert, n_used


def kernel(x, c, w_ada, b_ada, w_in, b_in, conv_w, conv_b, ml_norm_g, kv_norm_g, w_uk, w_uv,
           w_br_ml, w_br_dsa, w_br_fox, w_out, w_router, b_router, w1, b1, w2, b2, final_g):
    bsz, s, d = x.shape
    depth = w_in.shape[0]
    n = bsz * s
    bm = 512
    n_blocks = -(-(n * TOP_K + (ROW_ALIGN - 1) * N_EXPERTS * (n // MOE_TILE)) // bm) + N_EXPERTS
    mods = _ada_mod(c, w_ada, b_ada).reshape(depth, bsz, 6, d)
    for l in range(depth):
        mod = mods[l]
        w_r, b_r = _rearranged_in_proj(w_in[l], b_in[l])
        proj, misc = _in_proj(x, mod, w_r, b_r)
        y_ml, fcol, frow = _mlstm(proj, misc, conv_w[l], conv_b[l], ml_norm_g[l])
        y_fox = _fox(proj, fcol, frow)
        y_dsa = _dsa(proj, misc, kv_norm_g[l], w_uk[l].astype(BF16), jnp.swapaxes(w_uv[l], 1, 2).astype(BF16))
        x = _merge(x, proj, y_ml, y_dsa, y_fox, mod, w_br_ml[l].astype(BF16), w_br_dsa[l].astype(BF16),
                   w_br_fox[l].astype(BF16), w_out[l].astype(BF16))
        h2, route, stats = _router(x, mod, w_router[l], b_router[l])
        plan, blk_expert, n_used = _moe_plan(stats, bm, n_blocks)
        xs = _dispatch(plan, h2.reshape(n, d), route.reshape(n, LANE), (n_blocks + 1) * bm, bm)
        yb = _experts(blk_expert, n_used, xs, w1, b1, w2, b2, l, bm)
        x = _combine(plan, yb, x, route, mod, final_g, final=(l == depth - 1))
    return x
```

```python
import functools

import jax
import jax.numpy as jnp
from jax import lax
from jax.experimental import pallas as pl
from jax.experimental.pallas import tpu as pltpu

F32 = jnp.float32
BF16 = jnp.bfloat16
HIGHEST = lax.Precision.HIGHEST

EPS = 1e-6
LOG2E = 1.4426950408889634
CHUNK = 64

ML_HEADS, ML_QK, ML_V, ML_CONV = 4, 64, 128, 4
DSA_HEADS, DSA_DIM, DSA_LATENT = 4, 128, 128
IDX_HEADS, IDX_DIM, DSA_TOPK = 4, 64, 256
FOX_HEADS, FOX_DIM = 4, 128
N_BRANCH = 3
N_EXPERTS, TOP_K = 32, 4
SWIGLU_LIMIT, SWIGLU_ALPHA = 7.0, 1.702

LANE = 128
ROW_ALIGN = 8
INT_MIN = -2 ** 31

MOE_TILE = 256
N_RUNS = N_EXPERTS + 1
LOC_ROWS = TOP_K * MOE_TILE + MOE_TILE
ZERO_ROWS = 128

C_MLQK = 0
C_MLV = 512
C_MLO = 1024
C_DQ = 1536
C_DIQ = 2048
C_CKV = 2304
C_MISC = 2432
C_FOX = 2560
C_GATE = 4096
NP = 7168
M_IK, M_MLI, M_MLF, M_IW, M_FXF = 0, 64, 68, 72, 76

VMEM_LIMIT = 56 * 1024 * 1024


def _dot(a, b, prec=None):
    return jnp.dot(a, b, preferred_element_type=F32, precision=prec)


def _dot_nt(a, b, prec=None):
    return lax.dot_general(a, b, (((1,), (1,)), ((), ())), preferred_element_type=F32, precision=prec)


def _dot_tn(a, b):
    return lax.dot_general(a, b, (((0,), (0,)), ((), ())), preferred_element_type=F32)


def _sigmoid(x):
    return 1.0 / (1.0 + jnp.exp(-x))


def _log_sigmoid(x):
    return jnp.minimum(x, 0.0) - jnp.log1p(jnp.exp(-jnp.abs(x)))


def _rms(x):
    return x * lax.rsqrt(jnp.mean(x * x, axis=-1, keepdims=True) + EPS)


def _params(sem, vmem=VMEM_LIMIT):
    return pltpu.CompilerParams(dimension_semantics=sem, vmem_limit_bytes=vmem)


def _ada_kernel(c_ref, w_ref, b_ref, o_ref):
    c = c_ref[...]
    o_ref[0] = _dot(c * _sigmoid(c), w_ref[0], HIGHEST) + b_ref[0]


def _ada_mod(c, w_ada, b_ada):
    depth, d, n = w_ada.shape
    bsz = c.shape[0]
    tn = 1536
    return pl.pallas_call(
        _ada_kernel,
        out_shape=jax.ShapeDtypeStruct((depth, bsz, n), F32),
        grid=(depth, n // tn),
        in_specs=[pl.BlockSpec((bsz, d), lambda l, j: (0, 0)),
                  pl.BlockSpec((1, d, tn), lambda l, j: (l, 0, j)),
                  pl.BlockSpec((1, 1, tn), lambda l, j: (l, 0, j))],
        out_specs=pl.BlockSpec((1, bsz, tn), lambda l, j: (l, 0, j)),
        compiler_params=_params(("parallel", "parallel")),
        name="ada_mod",
    )(c, w_ada, b_ada.reshape(depth, 1, n))


def _inproj_kernel(x_ref, mod_ref, w_ref, b_ref, o_ref, misc_ref, h_scr, *, tn):
    j = pl.program_id(2)

    @pl.when(j == 0)
    def _():
        h = _rms(x_ref[0]) * (1.0 + mod_ref[0, 1:2, :]) + mod_ref[0, 0:1, :]
        h_scr[...] = h.astype(BF16)

    acc = _dot(h_scr[...], w_ref[...]) + b_ref[...]
    o_ref[0] = acc.astype(BF16)

    @pl.when(j == C_MISC // tn)
    def _():
        misc_ref[0] = acc[:, C_MISC % tn:C_MISC % tn + LANE]


def _in_proj(x, mod, w, b):
    bsz, s, d = x.shape
    n = w.shape[1]
    tm = min(2048, s)
    tn = 1024
    return pl.pallas_call(
        functools.partial(_inproj_kernel, tn=tn),
        out_shape=(jax.ShapeDtypeStruct((bsz, s, n), BF16), jax.ShapeDtypeStruct((bsz, s, LANE), F32)),
        grid=(bsz, s // tm, n // tn),
        in_specs=[pl.BlockSpec((1, tm, d), lambda bi, i, j: (bi, i, 0)),
                  pl.BlockSpec((1, 6, d), lambda bi, i, j: (bi, 0, 0)),
                  pl.BlockSpec((d, tn), lambda bi, i, j: (0, j)),
                  pl.BlockSpec((1, tn), lambda bi, i, j: (0, j))],
        out_specs=(pl.BlockSpec((1, tm, tn), lambda bi, i, j: (bi, i, j)),
                   pl.BlockSpec((1, tm, LANE), lambda bi, i, j: (bi, i, 0))),
        scratch_shapes=[pltpu.VMEM((tm, d), BF16)],
        compiler_params=_params(("parallel", "parallel", "arbitrary")),
        name="in_proj",
    )(x, mod, w, b)


def _mlstm_kernel(qk_ref, v_ref, o_ref, misc_ref, cw_ref, cb_ref, g_ref,
                  y_ref, fcol_ref, frow_ref,
                  xext, ct_scr, n_scr, m_scr, carry_scr, *, L):
    c = pl.program_id(1)
    nqk = ML_HEADS * ML_QK

    @pl.when(c == 0)
    def _():
        xext[0:8, :] = jnp.zeros((8, 2 * nqk), F32)
        ct_scr[...] = jnp.zeros(ct_scr.shape, F32)
        n_scr[...] = jnp.zeros(n_scr.shape, F32)
        m_scr[...] = jnp.full(m_scr.shape, -jnp.inf, F32)
        carry_scr[...] = jnp.zeros(carry_scr.shape, F32)

    @pl.when(c > 0)
    def _():
        xext[0:8, :] = xext[L:L + 8, :]

    xext[8:8 + L, :] = qk_ref[0].astype(F32)
    cw = cw_ref[...]
    conv = (cb_ref[...] + cw[3:4, :] * xext[8:8 + L, :] + cw[2:3, :] * xext[7:7 + L, :]
            + cw[1:2, :] * xext[6:6 + L, :] + cw[0:1, :] * xext[5:5 + L, :])
    qk = conv * _sigmoid(conv)

    misc = misc_ref[0]
    ls = _log_sigmoid(misc)
    row = lax.broadcasted_iota(jnp.int32, (L, L), 0)
    col = lax.broadcasted_iota(jnp.int32, (L, L), 1)
    causal = row >= col
    tri = jnp.where(causal, 1.0, 0.0).astype(F32)
    cs = _dot(tri, ls, HIGHEST)
    cs_t = cs.T
    misc_t = misc.T
    carry = carry_scr[0:1, :]
    fcol_ref[0] = cs + carry
    for h in range(FOX_HEADS):
        frow_ref[0, h:h + 1, :] = cs_t[M_FXF + h:M_FXF + h + 1, :] + carry[:, M_FXF + h:M_FXF + h + 1]
    frow_ref[0, 4:8, :] = jnp.zeros((4, L), F32)
    carry_scr[0:1, :] = carry + cs[L - 1:L, :]

    for h in range(ML_HEADS):
        qh = qk[:, h * ML_QK:(h + 1) * ML_QK] * (ML_QK ** -0.5)
        kh = qk[:, nqk + h * ML_QK:nqk + (h + 1) * ML_QK]
        vb = v_ref[0, :, h * ML_V:(h + 1) * ML_V].astype(BF16)
        i_col = misc[:, M_MLI + h:M_MLI + h + 1]
        i_row = misc_t[M_MLI + h:M_MLI + h + 1, :]
        b_col = cs[:, M_MLF + h:M_MLF + h + 1]
        b_row = cs_t[M_MLF + h:M_MLF + h + 1, :]
        b_last = b_col[L - 1:L, :]
        m_prev = m_scr[h:h + 1, 0:1]

        d_log = jnp.where(causal, b_col - b_row + i_row, -jnp.inf)
        inter_log = b_col + m_prev
        m_out = jnp.maximum(inter_log, jnp.max(d_log, axis=-1, keepdims=True))
        qb = qh.astype(BF16)
        kb = kh.astype(BF16)
        s = _dot_nt(qb, kb) * jnp.exp(d_log - m_out)
        a_inter = jnp.exp(inter_log - m_out)
        ct = ct_scr[h]
        n_row = n_scr[h]
        num = _dot(s.astype(BF16), vb) + a_inter * _dot(qb, ct.astype(BF16))
        den = jnp.sum(s, axis=-1, keepdims=True) + a_inter * jnp.sum(qh * n_row, axis=-1, keepdims=True)
        hid = num / jnp.maximum(jnp.abs(den), jnp.exp(-m_out))

        w_state = b_last - b_col + i_col
        m_loc = jnp.max(w_state, axis=0, keepdims=True)
        ke = kh * jnp.exp(w_state - m_loc)
        c_loc = _dot_tn(ke.astype(BF16), vb)
        n_loc = jnp.sum(ke, axis=0, keepdims=True)
        m_new = jnp.maximum(b_last + m_prev, m_loc)
        decay = jnp.exp(b_last + m_prev - m_new)
        scale = jnp.exp(m_loc - m_new)
        ct_scr[h] = decay * ct + scale * c_loc
        n_scr[h] = decay * n_row + scale * n_loc
        m_scr[h:h + 1, :] = jnp.broadcast_to(m_new, (1, LANE))

        y = (_rms(hid) * g_ref[:, h * ML_V:(h + 1) * ML_V]
             * _sigmoid(o_ref[0, :, h * ML_V:(h + 1) * ML_V].astype(F32)))
        y_ref[0, :, h * ML_V:(h + 1) * ML_V] = y.astype(BF16)


def _mlstm(proj, misc, conv_w, conv_b, norm_g):
    bsz, s, _ = proj.shape
    L = min(256, s)
    w = 2 * ML_HEADS * ML_QK
    wv = ML_HEADS * ML_V
    return pl.pallas_call(
        functools.partial(_mlstm_kernel, L=L),
        out_shape=(jax.ShapeDtypeStruct((bsz, s, wv), BF16),
                   jax.ShapeDtypeStruct((bsz, s, LANE), F32),
                   jax.ShapeDtypeStruct((bsz, 8, s), F32)),
        grid=(bsz, s // L),
        in_specs=[pl.BlockSpec((1, L, w), lambda b, c: (b, c, C_MLQK // w)),
                  pl.BlockSpec((1, L, wv), lambda b, c: (b, c, C_MLV // wv)),
                  pl.BlockSpec((1, L, wv), lambda b, c: (b, c, C_MLO // wv)),
                  pl.BlockSpec((1, L, LANE), lambda b, c: (b, c, 0)),
                  pl.BlockSpec((ML_CONV, w), lambda b, c: (0, 0)),
                  pl.BlockSpec((1, w), lambda b, c: (0, 0)),
                  pl.BlockSpec((1, wv), lambda b, c: (0, 0))],
        out_specs=(pl.BlockSpec((1, L, wv), lambda b, c: (b, c, 0)),
                   pl.BlockSpec((1, L, LANE), lambda b, c: (b, c, 0)),
                   pl.BlockSpec((1, 8, L), lambda b, c: (b, 0, c))),
        scratch_shapes=[pltpu.VMEM((L + 8, w), F32),
                        pltpu.VMEM((ML_HEADS, ML_QK, ML_V), F32),
                        pltpu.VMEM((ML_HEADS, 1, ML_QK), F32),
                        pltpu.VMEM((8, LANE), F32),
                        pltpu.VMEM((8, LANE), F32)],
        compiler_params=_params(("parallel", "arbitrary")),
        name="mlstm",
    )(proj, proj, proj, misc, conv_w, conv_b.reshape(1, w), norm_g.reshape(1, wv))


def _fox_kernel(q_ref, k_ref, v_ref, fcol_ref, frow_ref, y_ref, k_scr, vt_scr, fk_scr, qt_scr, out_scr, *, T, S):
    qi = pl.program_id(1)
    d = FOX_DIM

    @pl.when(qi == 0)
    def _():
        for h in range(FOX_HEADS):
            k_scr[h] = k_ref[0, :, h * d:(h + 1) * d].astype(BF16)
            vt_scr[h] = v_ref[0, :, h * d:(h + 1) * d].astype(F32).T.astype(BF16)
            fk_scr[h] = fcol_ref[0, :, M_FXF + h:M_FXF + h + 1] * LOG2E

    q_t = (q_ref[0].astype(F32) * (d ** -0.5 * LOG2E)).T
    for h in range(FOX_HEADS):
        qt_scr[h] = q_t[h * d:(h + 1) * d, :].astype(BF16)

    def body(ext):
        kpos = lax.broadcasted_iota(jnp.int32, (ext, 1), 0)
        qpos = (ext - T) + lax.broadcasted_iota(jnp.int32, (1, T), 1)
        causal = kpos <= qpos

        def head(h, _):
            s = _dot(k_scr[h, 0:ext, :], qt_scr[h])
            s = s + (frow_ref[0, pl.ds(h, 1), :] * LOG2E - fk_scr[h, 0:ext, :])
            s = jnp.where(causal, s, -jnp.inf)
            p = jnp.exp2(s - jnp.max(s, axis=0, keepdims=True))
            l = jnp.sum(p, axis=0, keepdims=True)
            out_t = _dot(vt_scr[h, :, 0:ext], p.astype(BF16)) / l
            out_scr[h] = out_t.T.astype(BF16)
            return 0

        lax.fori_loop(0, FOX_HEADS, head, 0)

    for c in range(S // T):
        pl.when(qi == c)(functools.partial(body, (c + 1) * T))
    for h in range(FOX_HEADS):
        y_ref[0, :, h * d:(h + 1) * d] = out_scr[h]


def _fox(proj, fcol, frow):
    bsz, s, _ = proj.shape
    T = min(256, s)
    w = FOX_HEADS * FOX_DIM
    return pl.pallas_call(
        functools.partial(_fox_kernel, T=T, S=s),
        out_shape=jax.ShapeDtypeStruct((bsz, s, w), BF16),
        grid=(bsz, s // T),
        in_specs=[pl.BlockSpec((1, T, w), lambda b, i: (b, i, C_FOX // w)),
                  pl.BlockSpec((1, s, w), lambda b, i: (b, 0, C_FOX // w + 1)),
                  pl.BlockSpec((1, s, w), lambda b, i: (b, 0, C_FOX // w + 2)),
                  pl.BlockSpec((1, s, LANE), lambda b, i: (b, 0, 0)),
                  pl.BlockSpec((1, 8, T), lambda b, i: (b, 0, i))],
        out_specs=pl.BlockSpec((1, T, w), lambda b, i: (b, i, 0)),
        scratch_shapes=[pltpu.VMEM((FOX_HEADS, s, FOX_DIM), BF16),
                        pltpu.VMEM((FOX_HEADS, FOX_DIM, s), BF16),
                        pltpu.VMEM((FOX_HEADS, s, 1), F32),
                        pltpu.VMEM((FOX_HEADS, FOX_DIM, T), BF16),
                        pltpu.VMEM((FOX_HEADS, T, FOX_DIM), BF16)],
        compiler_params=_params(("parallel", "arbitrary")),
        name="fox_attention",
    )(proj, proj, proj, fcol, frow)


def _dsa_body(ext, qi, wuk_ref, wuv_ref, ckvn_scr, ckvnt_scr, kidx_scr, sel_scr, score_scr,
              qt_scr, qi_scr, w_scr, out_scr, *, T, n_sel, rank_tile):
    dh = DSA_DIM
    score_scr[0:ext, :] = jnp.zeros((ext, T), F32)

    def idx_head(h, _):
        lg = _dot(kidx_scr[0:ext, :], qi_scr[h]) * (IDX_DIM ** -0.5)
        score_scr[0:ext, :] += w_scr[pl.ds(h, 1), :] * jnp.maximum(lg, 0.0)
        return 0

    lax.fori_loop(0, IDX_HEADS, idx_head, 0)
    kpos = lax.broadcasted_iota(jnp.int32, (ext, 1), 0)
    qchunk = (qi * T + lax.broadcasted_iota(jnp.int32, (1, T), 1)) // CHUNK
    score = jnp.where((kpos // CHUNK) <= qchunk, score_scr[0:ext, :], -jnp.inf)

    def as_float(c):
        return pltpu.bitcast(jnp.where(c < 0, c ^ jnp.int32(0x7FFFFFFF), c), F32)

    def count_ge(c):
        return jnp.sum(jnp.where(score >= as_float(c), 1.0, 0.0), axis=0, keepdims=True)

    t0 = jnp.where(count_ge(jnp.zeros((1, T), jnp.int32)) >= n_sel, jnp.int32(0), jnp.int32(INT_MIN))

    def bis(i, t):
        cand = t + jnp.left_shift(jnp.int32(1), jnp.int32(30) - i)
        return jnp.where(count_ge(cand) >= n_sel, cand, t)

    t = lax.fori_loop(0, 31, bis, t0)
    n_vis = ((qchunk + 1) * CHUNK).astype(F32)
    thr = jnp.where(n_vis <= n_sel, -3.0e38, as_float(t))
    need = n_sel - jnp.sum(jnp.where(score > thr, 1.0, 0.0), axis=0, keepdims=True)
    r_i = lax.broadcasted_iota(jnp.int32, (rank_tile, rank_tile), 0)
    c_i = lax.broadcasted_iota(jnp.int32, (rank_tile, rank_tile), 1)
    lower = jnp.where(c_i < r_i, 1.0, 0.0).astype(BF16)
    carry = jnp.zeros((1, T), F32)
    for j in range(ext // rank_tile):
        sc = score[j * rank_tile:(j + 1) * rank_tile, :]
        eq = jnp.where(sc == thr, 1.0, 0.0)
        rank = _dot(lower, eq.astype(BF16)) + carry
        carry = carry + jnp.sum(eq, axis=0, keepdims=True)
        sel_scr[j * rank_tile:(j + 1) * rank_tile, :] = jnp.where(
            sc > thr, 1.0, jnp.where(rank < need, eq, 0.0))

    def att_head(h, _):
        qa_t = _dot(wuk_ref[h], qt_scr[h]) * (dh ** -0.5 * LOG2E)
        lg = _dot(ckvn_scr[0:ext, :], qa_t.astype(BF16))
        lg = jnp.where(sel_scr[0:ext, :] > 0.5, lg, -jnp.inf)
        p = jnp.exp2(lg - jnp.max(lg, axis=0, keepdims=True))
        l = jnp.sum(p, axis=0, keepdims=True)
        lat_t = _dot(ckvnt_scr[:, 0:ext], p.astype(BF16)) / l
        out_t = _dot(wuv_ref[h], lat_t.astype(BF16))
        out_scr[h] = out_t.T.astype(BF16)
        return 0

    lax.fori_loop(0, DSA_HEADS, att_head, 0)


def _dsa_kernel(q_ref, ckv_ref, qidx_ref, misc_all_ref, misc_q_ref, g_ref, wuk_ref, wuv_ref, y_ref,
                ckvn_scr, ckvnt_scr, kidx_scr, sel_scr, score_scr, qt_scr, qi_scr, w_scr, out_scr,
                *, T, S, n_sel, n_cls, rank_tile):
    qi = pl.program_id(1)
    dh = DSA_DIM

    @pl.when(qi == 0)
    def _():
        ckvn = _rms(ckv_ref[0].astype(F32)) * g_ref[...]
        ckvn_scr[...] = ckvn.astype(BF16)
        ckvnt_scr[...] = ckvn.T.astype(BF16)
        kidx_scr[...] = misc_all_ref[0, :, M_IK:M_IK + IDX_DIM].astype(BF16)

    q_t = q_ref[0].astype(F32).T
    for h in range(DSA_HEADS):
        qt_scr[h] = q_t[h * dh:(h + 1) * dh, :].astype(BF16)
    qidx_t = qidx_ref[0].astype(F32).T
    for h in range(IDX_HEADS):
        qi_scr[h] = qidx_t[h * IDX_DIM:(h + 1) * IDX_DIM, :].astype(BF16)
    w_scr[...] = misc_q_ref[0].T[M_IW:M_IW + 8, :] * (IDX_HEADS ** -0.5)

    per = (S // T) // n_cls
    for c in range(n_cls):
        ext = (c + 1) * per * T

        @pl.when(qi // per == c)
        def _(ext=ext):
            _dsa_body(ext, qi, wuk_ref, wuv_ref, ckvn_scr, ckvnt_scr, kidx_scr, sel_scr, score_scr,
                      qt_scr, qi_scr, w_scr, out_scr, T=T, n_sel=n_sel, rank_tile=rank_tile)

    for h in range(DSA_HEADS):
        y_ref[0, :, h * dh:(h + 1) * dh] = out_scr[h]


def _dsa(proj, misc, kv_g, wuk, wuv_t):
    bsz, s, _ = proj.shape
    T = min(256, s)
    n_sel = min(DSA_TOPK, s // 4)
    n_cls = s // T
    rank_tile = 256
    w = DSA_HEADS * DSA_DIM
    wi = IDX_HEADS * IDX_DIM
    return pl.pallas_call(
        functools.partial(_dsa_kernel, T=T, S=s, n_sel=float(n_sel), n_cls=n_cls, rank_tile=rank_tile),
        out_shape=jax.ShapeDtypeStruct((bsz, s, w), BF16),
        grid=(bsz, s // T),
        in_specs=[pl.BlockSpec((1, T, w), lambda b, i: (b, i, C_DQ // w)),
                  pl.BlockSpec((1, s, DSA_LATENT), lambda b, i: (b, 0, C_CKV // DSA_LATENT)),
                  pl.BlockSpec((1, T, wi), lambda b, i: (b, i, C_DIQ // wi)),
                  pl.BlockSpec((1, s, LANE), lambda b, i: (b, 0, 0)),
                  pl.BlockSpec((1, T, LANE), lambda b, i: (b, i, 0)),
                  pl.BlockSpec((1, DSA_LATENT), lambda b, i: (0, 0)),
                  pl.BlockSpec((DSA_HEADS, DSA_LATENT, DSA_DIM), lambda b, i: (0, 0, 0)),
                  pl.BlockSpec((DSA_HEADS, DSA_DIM, DSA_LATENT), lambda b, i: (0, 0, 0))],
        out_specs=pl.BlockSpec((1, T, w), lambda b, i: (b, i, 0)),
        scratch_shapes=[pltpu.VMEM((s, DSA_LATENT), BF16),
                        pltpu.VMEM((DSA_LATENT, s), BF16),
                        pltpu.VMEM((s, IDX_DIM), BF16),
                        pltpu.VMEM((s, T), F32),
                        pltpu.VMEM((s, T), F32),
                        pltpu.VMEM((DSA_HEADS, DSA_DIM, T), BF16),
                        pltpu.VMEM((IDX_HEADS, IDX_DIM, T), BF16),
                        pltpu.VMEM((8, T), F32),
                        pltpu.VMEM((DSA_HEADS, T, DSA_DIM), BF16)],
        compiler_params=_params(("parallel", "arbitrary")),
        name="dsa_attention",
    )(proj, proj, proj, misc, misc, kv_g.reshape(1, DSA_LATENT), wuk, wuv_t)


def _merge_kernel(yml_ref, ydsa_ref, yfox_ref, g0_ref, g1_ref, g2_ref, x_ref, mod_ref,
                  wml_ref, wdsa_ref, wfox_ref, wout_ref, o_ref):
    merged = (_sigmoid(g0_ref[0].astype(F32)) * _dot(yml_ref[0], wml_ref[...])
              + _sigmoid(g1_ref[0].astype(F32)) * _dot(ydsa_ref[0], wdsa_ref[...])
              + _sigmoid(g2_ref[0].astype(F32)) * _dot(yfox_ref[0], wfox_ref[...]))
    out = _dot(merged.astype(BF16), wout_ref[...])
    o_ref[0] = x_ref[0] + mod_ref[0, 2:3, :] * out


def _merge(x, proj, y_ml, y_dsa, y_fox, mod, w_ml, w_dsa, w_fox, w_out):
    bsz, s, d = x.shape
    tm = min(512, s)
    wb = y_ml.shape[-1]
    gb = C_GATE // d
    yspec = pl.BlockSpec((1, tm, wb), lambda b, i: (b, i, 0))
    wspec = pl.BlockSpec((wb, d), lambda b, i: (0, 0))
    return pl.pallas_call(
        _merge_kernel,
        out_shape=jax.ShapeDtypeStruct((bsz, s, d), F32),
        grid=(bsz, s // tm),
        in_specs=[yspec, yspec, yspec,
                  pl.BlockSpec((1, tm, d), lambda b, i: (b, i, gb)),
                  pl.BlockSpec((1, tm, d), lambda b, i: (b, i, gb + 1)),
                  pl.BlockSpec((1, tm, d), lambda b, i: (b, i, gb + 2)),
                  pl.BlockSpec((1, tm, d), lambda b, i: (b, i, 0)),
                  pl.BlockSpec((1, 6, d), lambda b, i: (b, 0, 0)),
                  wspec, wspec, wspec,
                  pl.BlockSpec((d, d), lambda b, i: (0, 0))],
        out_specs=pl.BlockSpec((1, tm, d), lambda b, i: (b, i, 0)),
        compiler_params=_params(("parallel", "parallel")),
        name="merge_out",
    )(y_ml, y_dsa, y_fox, proj, proj, proj, x, mod, w_ml, w_dsa, w_fox, w_out)


def _router_kernel(x_ref, mod_ref, wr_ref, br_ref, h_ref, route_ref, stat_ref, *, T):
    h = _rms(x_ref[0]) * (1.0 + mod_ref[0, 4:5, :]) + mod_ref[0, 3:4, :]
    h_ref[0] = h
    lg = _dot(h, wr_ref[...], HIGHEST) + br_ref[...]
    lane = lax.broadcasted_iota(jnp.int32, (T, LANE), 1).astype(F32)
    vals, hots = [], []
    for _ in range(TOP_K):
        mx = jnp.max(lg, axis=-1, keepdims=True)
        idx = jnp.min(jnp.where(lg == mx, lane, float(LANE)), axis=-1, keepdims=True)
        hot = lane == idx
        vals.append(mx)
        hots.append(hot)
        lg = jnp.where(hot, -jnp.inf, lg)
    exps = [jnp.exp(v - vals[0]) for v in vals]
    tot = exps[0] + exps[1] + exps[2] + exps[3]
    multi = jnp.zeros((T, LANE), F32)
    for hot in hots:
        multi = multi + jnp.where(hot, 1.0, 0.0)
    r_i = lax.broadcasted_iota(jnp.int32, (T, T), 0)
    c_i = lax.broadcasted_iota(jnp.int32, (T, T), 1)
    lower = jnp.where(c_i < r_i, 1.0, 0.0).astype(BF16)
    before = _dot(lower, multi.astype(BF16))
    cnt = jnp.broadcast_to(jnp.sum(multi, axis=0, keepdims=True), (8, LANE))
    cnt = jnp.floor((cnt + (ROW_ALIGN - 1.0)) * (1.0 / ROW_ALIGN)) * ROW_ALIGN
    a_i = lax.broadcasted_iota(jnp.int32, (LANE, LANE), 0)
    b_i = lax.broadcasted_iota(jnp.int32, (LANE, LANE), 1)
    start = _dot(cnt, jnp.where(a_i < b_i, 1.0, 0.0).astype(F32), HIGHEST)
    sub = lax.broadcasted_iota(jnp.int32, (8, LANE), 0)
    stat_ref[0] = jnp.where(sub == 0, cnt, jnp.where(sub == 1, start, 0.0))
    where_to = before + start[0:1, :]
    slab = jnp.zeros((T, LANE), F32)
    for k in range(TOP_K):
        e_k = jnp.sum(jnp.where(hots[k], lane, 0.0), axis=-1, keepdims=True)
        p_k = jnp.sum(jnp.where(hots[k], where_to, 0.0), axis=-1, keepdims=True)
        slab = slab + jnp.where(lane == float(k), e_k, 0.0)
        slab = slab + jnp.where(lane == float(TOP_K + k), p_k, 0.0)
        slab = slab + jnp.where(lane == float(2 * TOP_K + k), exps[k] / tot, 0.0)
    route_ref[0] = slab


def _router(x, mod, w_router, b_router):
    bsz, s, d = x.shape
    T = MOE_TILE
    nt = s // T
    ne = w_router.shape[1]
    wr = jnp.zeros((d, LANE), F32).at[:, :ne].set(w_router)
    br = jnp.full((1, LANE), -1e30, F32).at[0, :ne].set(b_router)
    return pl.pallas_call(
        functools.partial(_router_kernel, T=T),
        out_shape=(jax.ShapeDtypeStruct((bsz, s, d), F32),
                   jax.ShapeDtypeStruct((bsz, s, LANE), F32),
                   jax.ShapeDtypeStruct((bsz * nt, 8, LANE), F32)),
        grid=(bsz, nt),
        in_specs=[pl.BlockSpec((1, T, d), lambda b, i: (b, i, 0)),
                  pl.BlockSpec((1, 6, d), lambda b, i: (b, 0, 0)),
                  pl.BlockSpec((d, LANE), lambda b, i: (0, 0)),
                  pl.BlockSpec((1, LANE), lambda b, i: (0, 0))],
        out_specs=(pl.BlockSpec((1, T, d), lambda b, i: (b, i, 0)),
                   pl.BlockSpec((1, T, LANE), lambda b, i: (b, i, 0)),
                   pl.BlockSpec((1, 8, LANE), lambda b, i: (b * nt + i, 0, 0))),
        compiler_params=_params(("parallel", "parallel")),
        name="moe_router",
    )(x, mod, wr, br)


def _run_copies(tile, cnt_ref, start_ref, row_ref, make_copy):
    for r in range(N_RUNS):
        c = cnt_ref[tile * N_RUNS + r]
        src = start_ref[tile * N_RUNS + r]
        dst = row_ref[tile * N_RUNS + r]
        size = MOE_TILE
        while size >= ROW_ALIGN:
            done = c & (-2 * size)
            cp = make_copy(pl.multiple_of(src + done, ROW_ALIGN), pl.multiple_of(dst + done, ROW_ALIGN), size)
            pl.when((c & size) != 0)(cp.start)
            size //= 2


def _dispatch_kernel(cnt_ref, start_ref, row_ref, pad_end_ref, cnt_end_ref, h_ref, route_ref, xs_hbm,
                     loc0, loc1, zbuf, zsem, sem, *, BM, N_TAIL, N_T):
    t = pl.program_id(0)
    loc = (loc0, loc1)

    @pl.when(t == 0)
    def _():
        zbuf[...] = jnp.zeros(zbuf.shape, F32)
        n_rows = xs_hbm.shape[0]

        def pad_copy(e, g):
            start = pl.multiple_of(pad_end_ref[e] - g * ZERO_ROWS, ZERO_ROWS)
            return (start + ZERO_ROWS > cnt_end_ref[e],
                    pltpu.make_async_copy(zbuf.at[pl.ds(0, ZERO_ROWS)], xs_hbm.at[pl.ds(start, ZERO_ROWS)], zsem))

        def tail_copy(e):
            start = pl.multiple_of(pad_end_ref[N_EXPERTS - 1] + e * BM, BM)
            safe = pl.multiple_of(jnp.minimum(start, n_rows - BM), BM)
            return start < n_rows, pltpu.make_async_copy(zbuf, xs_hbm.at[pl.ds(safe, BM)], zsem)

        copies = [pad_copy(e, g) for e in range(N_EXPERTS) for g in range(1, BM // ZERO_ROWS + 1)]
        copies += [tail_copy(e) for e in range(N_TAIL)]
        for cond, cp in copies:
            pl.when(cond)(cp.start)
        for cond, cp in copies:
            pl.when(cond)(cp.wait)

    pos_t = route_ref[...].T[TOP_K:2 * TOP_K, :]
    slot = lax.broadcasted_iota(jnp.int32, (LOC_ROWS, 1), 0).astype(F32)
    pick = jnp.zeros((LOC_ROWS, MOE_TILE), F32)
    for k in range(TOP_K):
        pick = pick + jnp.where(slot == pos_t[k:k + 1, :], 1.0, 0.0)
    pick = pick.astype(BF16)

    def drain(s):
        pltpu.make_async_copy(loc[s], xs_hbm.at[pl.ds(0, LOC_ROWS)], sem.at[s]).wait()

    for s in range(2):
        @pl.when(t % 2 == s)
        def _(s=s):
            pl.when(t >= 2)(functools.partial(drain, s))
            loc[s][...] = _dot(pick, h_ref[...].astype(BF16))
            _run_copies(t, cnt_ref, start_ref, row_ref,
                        lambda a, b, n: pltpu.make_async_copy(loc[s].at[pl.ds(a, n)], xs_hbm.at[pl.ds(b, n)],
                                                              sem.at[s]))

    @pl.when(t == N_T - 1)
    def _():
        if N_T >= 2:
            drain(N_T % 2)
        drain((N_T - 1) % 2)


def _dispatch(plan, h2, route, n_rows, bm):
    n, d = h2.shape
    pairs = LOC_ROWS
    return pl.pallas_call(
        functools.partial(_dispatch_kernel, BM=bm, N_TAIL=n_rows // bm - (n * TOP_K) // bm, N_T=n // MOE_TILE),
        out_shape=jax.ShapeDtypeStruct((n_rows, d), F32),
        grid_spec=pltpu.PrefetchScalarGridSpec(
            num_scalar_prefetch=5,
            grid=(n // MOE_TILE,),
            in_specs=[pl.BlockSpec((MOE_TILE, d), lambda t, *_: (t, 0)),
                      pl.BlockSpec((MOE_TILE, LANE), lambda t, *_: (t, 0))],
            out_specs=pl.BlockSpec(memory_space=pl.ANY),
            scratch_shapes=[pltpu.VMEM((pairs, d), F32),
                            pltpu.VMEM((pairs, d), F32),
                            pltpu.VMEM((bm, d), F32),
                            pltpu.SemaphoreType.DMA(()),
                            pltpu.SemaphoreType.DMA((2,))]),
        compiler_params=pltpu.CompilerParams(dimension_semantics=("arbitrary",), vmem_limit_bytes=VMEM_LIMIT,
                                             disable_bounds_checks=True),
        name="moe_dispatch",
    )(plan["cnt"], plan["start"], plan["row"], plan["pad_end"], plan["cnt_end"], h2, route)


def _expert_kernel(be_ref, nu_ref, xs_ref, w1_ref, b1_ref, w2_ref, b2_ref, o_ref, w1b, w2b):
    i = pl.program_id(0)
    de = w2_ref.shape[0]
    prev = be_ref[jnp.maximum(i - 1, 0)]
    fresh = (i == 0) | (be_ref[i] != prev)

    @pl.when(fresh & (i < nu_ref[0]))
    def _():
        w1b[...] = w1_ref[...].astype(BF16)
        w2b[...] = w2_ref[...].astype(BF16)

    @pl.when(i < nu_ref[0])
    def _():
        hdn = _dot(xs_ref[...].astype(BF16), w1b[...]) + b1_ref[...]
        glu = jnp.minimum(hdn[:, :de], SWIGLU_LIMIT)
        lin = jnp.clip(hdn[:, de:], -SWIGLU_LIMIT, SWIGLU_LIMIT)
        act = glu * _sigmoid(SWIGLU_ALPHA * glu) * (lin + 1.0)
        o_ref[...] = _dot(act.astype(BF16), w2b[...]) + b2_ref[...]

    @pl.when(i >= nu_ref[0])
    def _():
        o_ref[...] = jnp.zeros(o_ref.shape, F32)


def _experts(blk_expert, n_used, xs, w1, b1, w2, b2, layer, bm):
    n_rows, d = xs.shape
    depth, ne, _, dh2 = w1.shape
    de = w2.shape[2]
    n_blocks = n_rows // bm

    def row_map(i, be, nu):
        return (jnp.minimum(i, nu[0] - 1), 0)

    def w_map(i, be, nu):
        return (layer, be[i], 0, 0)

    return pl.pallas_call(
        _expert_kernel,
        out_shape=jax.ShapeDtypeStruct((n_rows, d), F32),
        grid_spec=pltpu.PrefetchScalarGridSpec(
            num_scalar_prefetch=2,
            grid=(n_blocks,),
            in_specs=[pl.BlockSpec((bm, d), row_map),
                      pl.BlockSpec((None, None, d, dh2), w_map),
                      pl.BlockSpec((None, None, 1, dh2), w_map),
                      pl.BlockSpec((None, None, de, d), w_map),
                      pl.BlockSpec((None, None, 1, d), w_map)],
            out_specs=pl.BlockSpec((bm, d), lambda i, be, nu: (i, 0)),
            scratch_shapes=[pltpu.VMEM((d, dh2), BF16), pltpu.VMEM((de, d), BF16)]),
        compiler_params=_params(("arbitrary",)),
        name="moe_experts",
    )(blk_expert, n_used, xs, w1, b1.reshape(depth, ne, 1, dh2), w2, b2.reshape(depth, ne, 1, d))


def _combine_kernel(cnt_ref, start_ref, row_ref, yb_hbm, x_ref, route_ref, mod_ref, fg_ref, o_ref,
                    loc0, loc1, sem, *, nt, n_tiles, final):
    t = pl.program_id(0) * nt + pl.program_id(1)
    pairs = LOC_ROWS
    loc = (loc0, loc1)

    def fetch(tile, s):
        _run_copies(tile, cnt_ref, start_ref, row_ref,
                    lambda a, b, n: pltpu.make_async_copy(yb_hbm.at[pl.ds(b, n)], loc[s].at[pl.ds(a, n)],
                                                          sem.at[s]))

    pl.when(t == 0)(functools.partial(fetch, 0, 0))
    route = route_ref[0]
    slot = lax.broadcasted_iota(jnp.int32, (1, pairs), 1).astype(F32)
    wgt = jnp.zeros((MOE_TILE, pairs), F32)
    for k in range(TOP_K):
        wgt = wgt + jnp.where(slot == route[:, TOP_K + k:TOP_K + k + 1],
                              route[:, 2 * TOP_K + k:2 * TOP_K + k + 1], 0.0)
    hi = wgt.astype(BF16)
    lo = (wgt - hi.astype(F32)).astype(BF16)

    for s in range(2):
        @pl.when(t % 2 == s)
        def _(s=s):
            pl.when(t + 1 < n_tiles)(functools.partial(fetch, t + 1, 1 - s))
            pltpu.make_async_copy(yb_hbm.at[pl.ds(0, pairs)], loc[s], sem.at[s]).wait()
            rows = loc[s][...].astype(BF16)
            y = _dot(hi, rows) + _dot(lo, rows)
            out = x_ref[0] + mod_ref[0, 5:6, :] * y
            if final:
                out = _rms(out) * fg_ref[...]
            o_ref[0] = out


def _combine(plan, yb, x, route, mod, final_g, final):
    bsz, s, d = x.shape
    nt = s // MOE_TILE
    pairs = LOC_ROWS
    return pl.pallas_call(
        functools.partial(_combine_kernel, nt=nt, n_tiles=bsz * nt, final=final),
        out_shape=jax.ShapeDtypeStruct((bsz, s, d), F32),
        grid_spec=pltpu.PrefetchScalarGridSpec(
            num_scalar_prefetch=3,
            grid=(bsz, nt),
            in_specs=[pl.BlockSpec(memory_space=pl.ANY),
                      pl.BlockSpec((1, MOE_TILE, d), lambda b, i, *_: (b, i, 0)),
                      pl.BlockSpec((1, MOE_TILE, LANE), lambda b, i, *_: (b, i, 0)),
                      pl.BlockSpec((1, 6, d), lambda b, i, *_: (b, 0, 0)),
                      pl.BlockSpec((1, d), lambda b, i, *_: (0, 0))],
            out_specs=pl.BlockSpec((1, MOE_TILE, d), lambda b, i, *_: (b, i, 0)),
            scratch_shapes=[pltpu.VMEM((pairs, d), F32), pltpu.VMEM((pairs, d), F32),
                            pltpu.SemaphoreType.DMA((2,))]),
        compiler_params=pltpu.CompilerParams(dimension_semantics=("arbitrary", "arbitrary"),
                                             vmem_limit_bytes=VMEM_LIMIT, disable_bounds_checks=True),
        name="moe_combine",
    )(plan["cnt"], plan["start"], plan["row"], yb, x, route, mod, final_g.reshape(1, d))


def _rearranged_in_proj(w_in, b_in):
    sizes = (ML_HEADS * ML_QK, ML_HEADS * ML_QK, ML_HEADS * ML_V, ML_HEADS * ML_V, ML_HEADS, ML_HEADS,
             DSA_HEADS * DSA_DIM, DSA_LATENT, IDX_HEADS * IDX_DIM, IDX_DIM, IDX_HEADS,
             3 * FOX_HEADS * FOX_DIM, FOX_HEADS, N_BRANCH * w_in.shape[0])
    offs = [0]
    for sz in sizes:
        offs.append(offs[-1] + sz)
    (o_mq, o_mk, o_mv, o_mo, o_mi, o_mf, o_dq, o_ckv, o_iq, o_ik, o_iw, o_fx, o_ff, o_g, o_end) = offs
    pad = LANE - (IDX_DIM + 2 * ML_HEADS + IDX_HEADS + FOX_HEADS)

    def cols(a):
        parts = [a[..., o_mq:o_mi],
                 a[..., o_dq:o_ckv],
                 a[..., o_iq:o_ik],
                 a[..., o_ckv:o_iq],
                 a[..., o_ik:o_iw],
                 a[..., o_mi:o_dq],
                 a[..., o_iw:o_fx],
                 a[..., o_ff:o_g],
                 jnp.zeros(a.shape[:-1] + (pad,), a.dtype),
                 a[..., o_fx:o_ff],
                 a[..., o_g:o_end]]
        return jnp.concatenate(parts, axis=-1)

    return cols(w_in).astype(BF16), cols(b_in.reshape(1, -1))


def _moe_plan(stats, bm, n_blocks):
    cnt = stats[:, 0, :N_EXPERTS].astype(jnp.int32)
    start = stats[:, 1, :N_EXPERTS].astype(jnp.int32)
    total = jnp.sum(cnt, axis=0)
    padded = (total + bm - 1) // bm * bm
    pad_end = jnp.cumsum(padded)
    pad_start = pad_end - padded
    row = pad_start[None, :] + jnp.cumsum(cnt, axis=0) - cnt
    used = jnp.sum(cnt, axis=1, keepdims=True)
    cnt = jnp.concatenate([cnt, LOC_ROWS - used], axis=1)
    start = jnp.concatenate([start, used], axis=1)
    parity = jnp.arange(cnt.shape[0], dtype=jnp.int32)[:, None] % 2
    row = jnp.concatenate([row, n_blocks * bm + parity * MOE_TILE], axis=1)
    blk_row = jnp.arange(n_blocks + 1, dtype=jnp.int32) * bm
    blk_expert = jnp.minimum(jnp.sum((pad_end[None, :] <= blk_row[:, None]).astype(jnp.int32), axis=1),
                             N_EXPERTS - 1)
    n_used = (pad_end[-1:] // bm).astype(jnp.int32)
    plan = dict(cnt=cnt.reshape(-1), start=start.reshape(-1), row=row.reshape(-1).astype(jnp.int32),
                pad_end=pad_end.astype(jnp.int32), cnt_end=(pad_start + total).astype(jnp.int32))
    return plan, blk_expert, n_used


def kernel(x, c, w_ada, b_ada, w_in, b_in, conv_w, conv_b, ml_norm_g, kv_norm_g, w_uk, w_uv,
           w_br_ml, w_br_dsa, w_br_fox, w_out, w_router, b_router, w1, b1, w2, b2, final_g):
    bsz, s, d = x.shape
    depth = w_in.shape[0]
    n = bsz * s
    bm = 512
    n_blocks = -(-(n * TOP_K + (ROW_ALIGN - 1) * N_EXPERTS * (n // MOE_TILE)) // bm) + N_EXPERTS
    mods = _ada_mod(c, w_ada, b_ada).reshape(depth, bsz, 6, d)
    for l in range(depth):
        mod = mods[l]
        w_r, b_r = _rearranged_in_proj(w_in[l], b_in[l])
        proj, misc = _in_proj(x, mod, w_r, b_r)
        y_ml, fcol, frow = _mlstm(proj, misc, conv_w[l], conv_b[l], ml_norm_g[l])
        y_fox = _fox(proj, fcol, frow)
        y_dsa = _dsa(proj, misc, kv_norm_g[l], w_uk[l].astype(BF16), jnp.swapaxes(w_uv[l], 1, 2).astype(BF16))
        x = _merge(x, proj, y_ml, y_dsa, y_fox, mod, w_br_ml[l].astype(BF16), w_br_dsa[l].astype(BF16),
                   w_br_fox[l].astype(BF16), w_out[l].astype(BF16))
        h2, route, stats = _router(x, mod, w_router[l], b_router[l])
        plan, blk_expert, n_used = _moe_plan(stats, bm, n_blocks)
        xs = _dispatch(plan, h2.reshape(n, d), route.reshape(n, LANE), (n_blocks + 1) * bm, bm)
        yb = _experts(blk_expert, n_used, xs, w1, b1, w2, b2, l, bm)
        x = _combine(plan, yb, x, route, mod, final_g, final=(l == depth - 1))
    return x
```

```python
import functools

import jax
import jax.numpy as jnp
from jax import lax
from jax.experimental import pallas as pl
from jax.experimental.pallas import tpu as pltpu

F32 = jnp.float32
BF16 = jnp.bfloat16
HIGHEST = lax.Precision.HIGHEST

EPS = 1e-6
LOG2E = 1.4426950408889634
CHUNK = 64

ML_HEADS, ML_QK, ML_V, ML_CONV = 4, 64, 128, 4
DSA_HEADS, DSA_DIM, DSA_LATENT = 4, 128, 128
IDX_HEADS, IDX_DIM, DSA_TOPK = 4, 64, 256
FOX_HEADS, FOX_DIM = 4, 128
N_BRANCH = 3
N_EXPERTS, TOP_K = 32, 4
SWIGLU_LIMIT, SWIGLU_ALPHA = 7.0, 1.702

LANE = 128
ROW_ALIGN = 8
INT_MIN = -2 ** 31

MOE_TILE = 256
N_RUNS = N_EXPERTS + 1
LOC_ROWS = TOP_K * MOE_TILE + MOE_TILE
ZERO_ROWS = 128

C_MLQK = 0
C_MLV = 512
C_MLO = 1024
C_DQ = 1536
C_DIQ = 2048
C_CKV = 2304
C_MISC = 2432
C_FOX = 2560
C_GATE = 4096
M_IK, M_MLI, M_MLF, M_IW, M_FXF = 0, 64, 68, 72, 76

VMEM_LIMIT = 56 * 1024 * 1024


def _dot(a, b, prec=None):
    return jnp.dot(a, b, preferred_element_type=F32, precision=prec)


def _dot_nt(a, b, prec=None):
    return lax.dot_general(a, b, (((1,), (1,)), ((), ())), preferred_element_type=F32, precision=prec)


def _dot_tn(a, b):
    return lax.dot_general(a, b, (((0,), (0,)), ((), ())), preferred_element_type=F32)


def _sigmoid(x):
    return 1.0 / (1.0 + jnp.exp(-x))


def _log_sigmoid(x):
    return jnp.minimum(x, 0.0) - jnp.log1p(jnp.exp(-jnp.abs(x)))


def _rms(x):
    return x * lax.rsqrt(jnp.mean(x * x, axis=-1, keepdims=True) + EPS)


def _params(sem, vmem=VMEM_LIMIT):
    return pltpu.CompilerParams(dimension_semantics=sem, vmem_limit_bytes=vmem)


def _ada_kernel(c_ref, w_ref, b_ref, o_ref):
    c = c_ref[...]
    o_ref[0] = _dot(c * _sigmoid(c), w_ref[0], HIGHEST) + b_ref[0]


def _ada_mod(c, w_ada, b_ada):
    depth, d, n = w_ada.shape
    bsz = c.shape[0]
    tn = 1536
    return pl.pallas_call(
        _ada_kernel,
        out_shape=jax.ShapeDtypeStruct((depth, bsz, n), F32),
        grid=(depth, n // tn),
        in_specs=[pl.BlockSpec((bsz, d), lambda l, j: (0, 0)),
                  pl.BlockSpec((1, d, tn), lambda l, j: (l, 0, j)),
                  pl.BlockSpec((1, 1, tn), lambda l, j: (l, 0, j))],
        out_specs=pl.BlockSpec((1, bsz, tn), lambda l, j: (l, 0, j)),
        compiler_params=_params(("parallel", "parallel")),
        name="ada_mod",
    )(c, w_ada, b_ada.reshape(depth, 1, n))


def _inproj_kernel(x_ref, mod_ref, w_ref, b_ref, o_ref, misc_ref, h_scr, *, tn):
    j = pl.program_id(2)

    @pl.when(j == 0)
    def _():
        h = _rms(x_ref[0]) * (1.0 + mod_ref[0, 1:2, :]) + mod_ref[0, 0:1, :]
        h_scr[...] = h.astype(BF16)

    acc = _dot(h_scr[...], w_ref[...]) + b_ref[...]
    o_ref[0] = acc.astype(BF16)

    @pl.when(j == C_MISC // tn)
    def _():
        misc_ref[0] = acc[:, C_MISC % tn:C_MISC % tn + LANE]


def _in_proj(x, mod, w, b):
    bsz, s, d = x.shape
    n = w.shape[1]
    tm = min(2048, s)
    tn = 1024
    return pl.pallas_call(
        functools.partial(_inproj_kernel, tn=tn),
        out_shape=(jax.ShapeDtypeStruct((bsz, s, n), BF16), jax.ShapeDtypeStruct((bsz, s, LANE), F32)),
        grid=(bsz, s // tm, n // tn),
        in_specs=[pl.BlockSpec((1, tm, d), lambda bi, i, j: (bi, i, 0)),
                  pl.BlockSpec((1, 6, d), lambda bi, i, j: (bi, 0, 0)),
                  pl.BlockSpec((d, tn), lambda bi, i, j: (0, j)),
                  pl.BlockSpec((1, tn), lambda bi, i, j: (0, j))],
        out_specs=(pl.BlockSpec((1, tm, tn), lambda bi, i, j: (bi, i, j)),
                   pl.BlockSpec((1, tm, LANE), lambda bi, i, j: (bi, i, 0))),
        scratch_shapes=[pltpu.VMEM((tm, d), BF16)],
        compiler_params=_params(("parallel", "parallel", "arbitrary")),
        name="in_proj",
    )(x, mod, w, b)


def _mlstm_kernel(qk_ref, v_ref, o_ref, misc_ref, cw_ref, cb_ref, g_ref,
                  y_ref, fcol_ref, frow_ref,
                  xext, ct_scr, n_scr, m_scr, carry_scr, *, L):
    c = pl.program_id(1)
    nqk = ML_HEADS * ML_QK

    @pl.when(c == 0)
    def _():
        xext[0:8, :] = jnp.zeros((8, 2 * nqk), F32)
        ct_scr[...] = jnp.zeros(ct_scr.shape, F32)
        n_scr[...] = jnp.zeros(n_scr.shape, F32)
        m_scr[...] = jnp.full(m_scr.shape, -jnp.inf, F32)
        carry_scr[...] = jnp.zeros(carry_scr.shape, F32)

    @pl.when(c > 0)
    def _():
        xext[0:8, :] = xext[L:L + 8, :]

    xext[8:8 + L, :] = qk_ref[0].astype(F32)
    cw = cw_ref[...]
    conv = (cb_ref[...] + cw[3:4, :] * xext[8:8 + L, :] + cw[2:3, :] * xext[7:7 + L, :]
            + cw[1:2, :] * xext[6:6 + L, :] + cw[0:1, :] * xext[5:5 + L, :])
    qk = conv * _sigmoid(conv)

    misc = misc_ref[0]
    ls = _log_sigmoid(misc)
    row = lax.broadcasted_iota(jnp.int32, (L, L), 0)
    col = lax.broadcasted_iota(jnp.int32, (L, L), 1)
    causal = row >= col
    tri = jnp.where(causal, 1.0, 0.0).astype(F32)
    cs = _dot(tri, ls, HIGHEST)
    cs_t = cs.T
    misc_t = misc.T
    carry = carry_scr[0:1, :]
    fcol_ref[0] = cs + carry
    for h in range(FOX_HEADS):
        frow_ref[0, h:h + 1, :] = cs_t[M_FXF + h:M_FXF + h + 1, :] + carry[:, M_FXF + h:M_FXF + h + 1]
    frow_ref[0, 4:8, :] = jnp.zeros((4, L), F32)
    carry_scr[0:1, :] = carry + cs[L - 1:L, :]

    for h in range(ML_HEADS):
        qh = qk[:, h * ML_QK:(h + 1) * ML_QK] * (ML_QK ** -0.5)
        kh = qk[:, nqk + h * ML_QK:nqk + (h + 1) * ML_QK]
        vb = v_ref[0, :, h * ML_V:(h + 1) * ML_V].astype(BF16)
        i_col = misc[:, M_MLI + h:M_MLI + h + 1]
        i_row = misc_t[M_MLI + h:M_MLI + h + 1, :]
        b_col = cs[:, M_MLF + h:M_MLF + h + 1]
        b_row = cs_t[M_MLF + h:M_MLF + h + 1, :]
        b_last = b_col[L - 1:L, :]
        m_prev = m_scr[h:h + 1, 0:1]

        d_log = jnp.where(causal, b_col - b_row + i_row, -jnp.inf)
        inter_log = b_col + m_prev
        m_out = jnp.maximum(inter_log, jnp.max(d_log, axis=-1, keepdims=True))
        qb = qh.astype(BF16)
        kb = kh.astype(BF16)
        s = _dot_nt(qb, kb) * jnp.exp(d_log - m_out)
        a_inter = jnp.exp(inter_log - m_out)
        ct = ct_scr[h]
        n_row = n_scr[h]
        num = _dot(s.astype(BF16), vb) + a_inter * _dot(qb, ct.astype(BF16))
        den = jnp.sum(s, axis=-1, keepdims=True) + a_inter * jnp.sum(qh * n_row, axis=-1, keepdims=True)
        hid = num / jnp.maximum(jnp.abs(den), jnp.exp(-m_out))

        w_state = b_last - b_col + i_col
        m_loc = jnp.max(w_state, axis=0, keepdims=True)
        ke = kh * jnp.exp(w_state - m_loc)
        c_loc = _dot_tn(ke.astype(BF16), vb)
        n_loc = jnp.sum(ke, axis=0, keepdims=True)
        m_new = jnp.maximum(b_last + m_prev, m_loc)
        decay = jnp.exp(b_last + m_prev - m_new)
        scale = jnp.exp(m_loc - m_new)
        ct_scr[h] = decay * ct + scale * c_loc
        n_scr[h] = decay * n_row + scale * n_loc
        m_scr[h:h + 1, :] = jnp.broadcast_to(m_new, (1, LANE))

        y = (_rms(hid) * g_ref[:, h * ML_V:(h + 1) * ML_V]
             * _sigmoid(o_ref[0, :, h * ML_V:(h + 1) * ML_V].astype(F32)))
        y_ref[0, :, h * ML_V:(h + 1) * ML_V] = y.astype(BF16)


def _mlstm(proj, misc, conv_w, conv_b, norm_g):
    bsz, s, _ = proj.shape
    L = min(256, s)
    w = 2 * ML_HEADS * ML_QK
    wv = ML_HEADS * ML_V
    return pl.pallas_call(
        functools.partial(_mlstm_kernel, L=L),
        out_shape=(jax.ShapeDtypeStruct((bsz, s, wv), BF16),
                   jax.ShapeDtypeStruct((bsz, s, LANE), F32),
                   jax.ShapeDtypeStruct((bsz, 8, s), F32)),
        grid=(bsz, s // L),
        in_specs=[pl.BlockSpec((1, L, w), lambda b, c: (b, c, C_MLQK // w)),
                  pl.BlockSpec((1, L, wv), lambda b, c: (b, c, C_MLV // wv)),
                  pl.BlockSpec((1, L, wv), lambda b, c: (b, c, C_MLO // wv)),
                  pl.BlockSpec((1, L, LANE), lambda b, c: (b, c, 0)),
                  pl.BlockSpec((ML_CONV, w), lambda b, c: (0, 0)),
                  pl.BlockSpec((1, w), lambda b, c: (0, 0)),
                  pl.BlockSpec((1, wv), lambda b, c: (0, 0))],
        out_specs=(pl.BlockSpec((1, L, wv), lambda b, c: (b, c, 0)),
                   pl.BlockSpec((1, L, LANE), lambda b, c: (b, c, 0)),
                   pl.BlockSpec((1, 8, L), lambda b, c: (b, 0, c))),
        scratch_shapes=[pltpu.VMEM((L + 8, w), F32),
                        pltpu.VMEM((ML_HEADS, ML_QK, ML_V), F32),
                        pltpu.VMEM((ML_HEADS, 1, ML_QK), F32),
                        pltpu.VMEM((8, LANE), F32),
                        pltpu.VMEM((8, LANE), F32)],
        compiler_params=_params(("parallel", "arbitrary")),
        name="mlstm",
    )(proj, proj, proj, misc, conv_w, conv_b.reshape(1, w), norm_g.reshape(1, wv))


def _fox_kernel(q_ref, k_ref, v_ref, fcol_ref, frow_ref, y_ref, k_scr, vt_scr, *, T, S):
    qi = pl.program_id(1)
    d = FOX_DIM

    @pl.when(qi == 0)
    def _():
        for h in range(FOX_HEADS):
            k_scr[h] = k_ref[0, :, h * d:(h + 1) * d].astype(BF16)
            vt_scr[h] = v_ref[0, :, h * d:(h + 1) * d].astype(F32).T.astype(BF16)

    q_t = (q_ref[0].astype(F32) * (d ** -0.5 * LOG2E)).T

    def body(ext):
        kpos = lax.broadcasted_iota(jnp.int32, (ext, 1), 0)
        qpos = (ext - T) + lax.broadcasted_iota(jnp.int32, (1, T), 1)
        causal = kpos <= qpos
        for h in range(FOX_HEADS):
            s = _dot(k_scr[h, 0:ext, :], q_t[h * d:(h + 1) * d, :].astype(BF16))
            s = s + (frow_ref[0, h:h + 1, :] * LOG2E - fcol_ref[0, 0:ext, M_FXF + h:M_FXF + h + 1] * LOG2E)
            s = jnp.where(causal, s, -jnp.inf)
            p = jnp.exp2(s - jnp.max(s, axis=0, keepdims=True))
            l = jnp.sum(p, axis=0, keepdims=True)
            out_t = _dot(vt_scr[h, :, 0:ext], p.astype(BF16)) / l
            y_ref[0, :, h * d:(h + 1) * d] = out_t.T.astype(BF16)

    for c in range(S // T):
        pl.when(qi == c)(functools.partial(body, (c + 1) * T))


def _fox(proj, fcol, frow):
    bsz, s, _ = proj.shape
    T = min(256, s)
    w = FOX_HEADS * FOX_DIM
    return pl.pallas_call(
        functools.partial(_fox_kernel, T=T, S=s),
        out_shape=jax.ShapeDtypeStruct((bsz, s, w), BF16),
        grid=(bsz, s // T),
        in_specs=[pl.BlockSpec((1, T, w), lambda b, i: (b, i, C_FOX // w)),
                  pl.BlockSpec((1, s, w), lambda b, i: (b, 0, C_FOX // w + 1)),
                  pl.BlockSpec((1, s, w), lambda b, i: (b, 0, C_FOX // w + 2)),
                  pl.BlockSpec((1, s, LANE), lambda b, i: (b, 0, 0)),
                  pl.BlockSpec((1, 8, T), lambda b, i: (b, 0, i))],
        out_specs=pl.BlockSpec((1, T, w), lambda b, i: (b, i, 0)),
        scratch_shapes=[pltpu.VMEM((FOX_HEADS, s, FOX_DIM), BF16),
                        pltpu.VMEM((FOX_HEADS, FOX_DIM, s), BF16)],
        compiler_params=_params(("parallel", "arbitrary")),
        name="fox_attention",
    )(proj, proj, proj, fcol, frow)


def _dsa_body(ext, qi, q_ref, qidx_ref, misc_q_ref, wuk_ref, wuv_ref, y_ref,
              ckvn_scr, ckvnt_scr, kidx_scr, sel_scr, *, T, n_sel, rank_tile):
    dh = DSA_DIM
    q_t = q_ref[0].astype(F32).T
    qidx_t = qidx_ref[0].astype(F32).T.astype(BF16)
    w_t = misc_q_ref[0].T[M_IW:M_IW + IDX_HEADS, :] * (IDX_HEADS ** -0.5)
    kidx = kidx_scr[0:ext, :]
    score = jnp.zeros((ext, T), F32)
    for h in range(IDX_HEADS):
        lg = _dot(kidx, qidx_t[h * IDX_DIM:(h + 1) * IDX_DIM, :]) * (IDX_DIM ** -0.5)
        score = score + w_t[h:h + 1, :] * jnp.maximum(lg, 0.0)
    kpos = lax.broadcasted_iota(jnp.int32, (ext, 1), 0)
    qchunk = (qi * T + lax.broadcasted_iota(jnp.int32, (1, T), 1)) // CHUNK
    score = jnp.where((kpos // CHUNK) <= qchunk, score, -jnp.inf)

    def as_float(c):
        return pltpu.bitcast(jnp.where(c < 0, c ^ jnp.int32(0x7FFFFFFF), c), F32)

    def count_ge(c):
        return jnp.sum(jnp.where(score >= as_float(c), 1.0, 0.0), axis=0, keepdims=True)

    t0 = jnp.where(count_ge(jnp.zeros((1, T), jnp.int32)) >= n_sel, jnp.int32(0), jnp.int32(INT_MIN))

    def bis(i, t):
        cand = t + jnp.left_shift(jnp.int32(1), jnp.int32(30) - i)
        return jnp.where(count_ge(cand) >= n_sel, cand, t)

    t = lax.fori_loop(0, 31, bis, t0)
    n_vis = ((qchunk + 1) * CHUNK).astype(F32)
    thr = jnp.where(n_vis <= n_sel, -3.0e38, as_float(t))
    need = n_sel - jnp.sum(jnp.where(score > thr, 1.0, 0.0), axis=0, keepdims=True)
    r_i = lax.broadcasted_iota(jnp.int32, (rank_tile, rank_tile), 0)
    c_i = lax.broadcasted_iota(jnp.int32, (rank_tile, rank_tile), 1)
    lower = jnp.where(c_i < r_i, 1.0, 0.0).astype(BF16)
    carry = jnp.zeros((1, T), F32)
    for j in range(ext // rank_tile):
        sc = score[j * rank_tile:(j + 1) * rank_tile, :]
        eq = jnp.where(sc == thr, 1.0, 0.0)
        rank = _dot(lower, eq.astype(BF16)) + carry
        carry = carry + jnp.sum(eq, axis=0, keepdims=True)
        sel_scr[j * rank_tile:(j + 1) * rank_tile, :] = jnp.where(
            sc > thr, 1.0, jnp.where(rank < need, eq, 0.0))

    sel = sel_scr[0:ext, :] > 0.5
    ckvn = ckvn_scr[0:ext, :]
    ckvn_t = ckvnt_scr[:, 0:ext]
    for h in range(DSA_HEADS):
        qa_t = _dot(wuk_ref[h], q_t[h * dh:(h + 1) * dh, :].astype(BF16)) * (dh ** -0.5 * LOG2E)
        lg = _dot(ckvn, qa_t.astype(BF16))
        lg = jnp.where(sel, lg, -jnp.inf)
        p = jnp.exp2(lg - jnp.max(lg, axis=0, keepdims=True))
        l = jnp.sum(p, axis=0, keepdims=True)
        lat_t = _dot(ckvn_t, p.astype(BF16)) / l
        out_t = _dot(wuv_ref[h], lat_t.astype(BF16))
        y_ref[0, :, h * dh:(h + 1) * dh] = out_t.T.astype(BF16)


def _dsa_kernel(q_ref, ckv_ref, qidx_ref, misc_all_ref, misc_q_ref, g_ref, wuk_ref, wuv_ref, y_ref,
                ckvn_scr, ckvnt_scr, kidx_scr, sel_scr, *, T, S, n_sel, n_cls, rank_tile):
    qi = pl.program_id(1)

    @pl.when(qi == 0)
    def _():
        ckvn = _rms(ckv_ref[0].astype(F32)) * g_ref[...]
        ckvn_scr[...] = ckvn.astype(BF16)
        ckvnt_scr[...] = ckvn.T.astype(BF16)
        kidx_scr[...] = misc_all_ref[0, :, M_IK:M_IK + IDX_DIM].astype(BF16)

    per = (S // T) // n_cls
    for c in range(n_cls):
        ext = (c + 1) * per * T

        @pl.when(qi // per == c)
        def _(ext=ext):
            _dsa_body(ext, qi, q_ref, qidx_ref, misc_q_ref, wuk_ref, wuv_ref, y_ref,
                      ckvn_scr, ckvnt_scr, kidx_scr, sel_scr, T=T, n_sel=n_sel, rank_tile=rank_tile)


def _dsa(proj, misc, kv_g, wuk, wuv_t):
    bsz, s, _ = proj.shape
    T = min(256, s)
    n_sel = min(DSA_TOPK, s // 4)
    n_cls = max(1, min(4, s // 512))
    rank_tile = 256
    w = DSA_HEADS * DSA_DIM
    wi = IDX_HEADS * IDX_DIM
    return pl.pallas_call(
        functools.partial(_dsa_kernel, T=T, S=s, n_sel=float(n_sel), n_cls=n_cls, rank_tile=rank_tile),
        out_shape=jax.ShapeDtypeStruct((bsz, s, w), BF16),
        grid=(bsz, s // T),
        in_specs=[pl.BlockSpec((1, T, w), lambda b, i: (b, i, C_DQ // w)),
                  pl.BlockSpec((1, s, DSA_LATENT), lambda b, i: (b, 0, C_CKV // DSA_LATENT)),
                  pl.BlockSpec((1, T, wi), lambda b, i: (b, i, C_DIQ // wi)),
                  pl.BlockSpec((1, s, LANE), lambda b, i: (b, 0, 0)),
                  pl.BlockSpec((1, T, LANE), lambda b, i: (b, i, 0)),
                  pl.BlockSpec((1, DSA_LATENT), lambda b, i: (0, 0)),
                  pl.BlockSpec((DSA_HEADS, DSA_LATENT, DSA_DIM), lambda b, i: (0, 0, 0)),
                  pl.BlockSpec((DSA_HEADS, DSA_DIM, DSA_LATENT), lambda b, i: (0, 0, 0))],
        out_specs=pl.BlockSpec((1, T, w), lambda b, i: (b, i, 0)),
        scratch_shapes=[pltpu.VMEM((s, DSA_LATENT), BF16),
                        pltpu.VMEM((DSA_LATENT, s), BF16),
                        pltpu.VMEM((s, IDX_DIM), BF16),
                        pltpu.VMEM((s, T), F32)],
        compiler_params=_params(("parallel", "arbitrary")),
        name="dsa_attention",
    )(proj, proj, proj, misc, misc, kv_g.reshape(1, DSA_LATENT), wuk, wuv_t)


def _merge_kernel(yml_ref, ydsa_ref, yfox_ref, g0_ref, g1_ref, g2_ref, x_ref, mod_ref,
                  wml_ref, wdsa_ref, wfox_ref, wout_ref, o_ref):
    merged = (_sigmoid(g0_ref[0].astype(F32)) * _dot(yml_ref[0], wml_ref[...])
              + _sigmoid(g1_ref[0].astype(F32)) * _dot(ydsa_ref[0], wdsa_ref[...])
              + _sigmoid(g2_ref[0].astype(F32)) * _dot(yfox_ref[0], wfox_ref[...]))
    out = _dot(merged.astype(BF16), wout_ref[...])
    o_ref[0] = x_ref[0] + mod_ref[0, 2:3, :] * out


def _merge(x, proj, y_ml, y_dsa, y_fox, mod, w_ml, w_dsa, w_fox, w_out):
    bsz, s, d = x.shape
    tm = min(512, s)
    wb = y_ml.shape[-1]
    gb = C_GATE // d
    yspec = pl.BlockSpec((1, tm, wb), lambda b, i: (b, i, 0))
    wspec = pl.BlockSpec((wb, d), lambda b, i: (0, 0))
    return pl.pallas_call(
        _merge_kernel,
        out_shape=jax.ShapeDtypeStruct((bsz, s, d), F32),
        grid=(bsz, s // tm),
        in_specs=[yspec, yspec, yspec,
                  pl.BlockSpec((1, tm, d), lambda b, i: (b, i, gb)),
                  pl.BlockSpec((1, tm, d), lambda b, i: (b, i, gb + 1)),
                  pl.BlockSpec((1, tm, d), lambda b, i: (b, i, gb + 2)),
                  pl.BlockSpec((1, tm, d), lambda b, i: (b, i, 0)),
                  pl.BlockSpec((1, 6, d), lambda b, i: (b, 0, 0)),
                  wspec, wspec, wspec,
                  pl.BlockSpec((d, d), lambda b, i: (0, 0))],
        out_specs=pl.BlockSpec((1, tm, d), lambda b, i: (b, i, 0)),
        compiler_params=_params(("parallel", "parallel")),
        name="merge_out",
    )(y_ml, y_dsa, y_fox, proj, proj, proj, x, mod, w_ml, w_dsa, w_fox, w_out)


def _router_kernel(x_ref, mod_ref, wr_ref, br_ref, h_ref, route_ref, stat_ref, *, T):
    h = _rms(x_ref[0]) * (1.0 + mod_ref[0, 4:5, :]) + mod_ref[0, 3:4, :]
    h_ref[0] = h
    lg = _dot(h, wr_ref[...], HIGHEST) + br_ref[...]
    lane = lax.broadcasted_iota(jnp.int32, (T, LANE), 1).astype(F32)
    vals, hots = [], []
    for _ in range(TOP_K):
        mx = jnp.max(lg, axis=-1, keepdims=True)
        idx = jnp.min(jnp.where(lg == mx, lane, float(LANE)), axis=-1, keepdims=True)
        hot = lane == idx
        vals.append(mx)
        hots.append(hot)
        lg = jnp.where(hot, -jnp.inf, lg)
    exps = [jnp.exp(v - vals[0]) for v in vals]
    tot = exps[0] + exps[1] + exps[2] + exps[3]
    multi = jnp.zeros((T, LANE), F32)
    for hot in hots:
        multi = multi + jnp.where(hot, 1.0, 0.0)
    r_i = lax.broadcasted_iota(jnp.int32, (T, T), 0)
    c_i = lax.broadcasted_iota(jnp.int32, (T, T), 1)
    lower = jnp.where(c_i < r_i, 1.0, 0.0).astype(BF16)
    before = _dot(lower, multi.astype(BF16))
    cnt = jnp.broadcast_to(jnp.sum(multi, axis=0, keepdims=True), (8, LANE))
    cnt = jnp.floor((cnt + (ROW_ALIGN - 1.0)) * (1.0 / ROW_ALIGN)) * ROW_ALIGN
    a_i = lax.broadcasted_iota(jnp.int32, (LANE, LANE), 0)
    b_i = lax.broadcasted_iota(jnp.int32, (LANE, LANE), 1)
    start = _dot(cnt, jnp.where(a_i < b_i, 1.0, 0.0).astype(F32), HIGHEST)
    sub = lax.broadcasted_iota(jnp.int32, (8, LANE), 0)
    stat_ref[0] = jnp.where(sub == 0, cnt, jnp.where(sub == 1, start, 0.0))
    where_to = before + start[0:1, :]
    slab = jnp.zeros((T, LANE), F32)
    for k in range(TOP_K):
        e_k = jnp.sum(jnp.where(hots[k], lane, 0.0), axis=-1, keepdims=True)
        p_k = jnp.sum(jnp.where(hots[k], where_to, 0.0), axis=-1, keepdims=True)
        slab = slab + jnp.where(lane == float(k), e_k, 0.0)
        slab = slab + jnp.where(lane == float(TOP_K + k), p_k, 0.0)
        slab = slab + jnp.where(lane == float(2 * TOP_K + k), exps[k] / tot, 0.0)
    route_ref[0] = slab


def _router(x, mod, w_router, b_router):
    bsz, s, d = x.shape
    T = MOE_TILE
    nt = s // T
    ne = w_router.shape[1]
    wr = jnp.zeros((d, LANE), F32).at[:, :ne].set(w_router)
    br = jnp.full((1, LANE), -1e30, F32).at[0, :ne].set(b_router)
    return pl.pallas_call(
        functools.partial(_router_kernel, T=T),
        out_shape=(jax.ShapeDtypeStruct((bsz, s, d), F32),
                   jax.ShapeDtypeStruct((bsz, s, LANE), F32),
                   jax.ShapeDtypeStruct((bsz * nt, 8, LANE), F32)),
        grid=(bsz, nt),
        in_specs=[pl.BlockSpec((1, T, d), lambda b, i: (b, i, 0)),
                  pl.BlockSpec((1, 6, d), lambda b, i: (b, 0, 0)),
                  pl.BlockSpec((d, LANE), lambda b, i: (0, 0)),
                  pl.BlockSpec((1, LANE), lambda b, i: (0, 0))],
        out_specs=(pl.BlockSpec((1, T, d), lambda b, i: (b, i, 0)),
                   pl.BlockSpec((1, T, LANE), lambda b, i: (b, i, 0)),
                   pl.BlockSpec((1, 8, LANE), lambda b, i: (b * nt + i, 0, 0))),
        compiler_params=_params(("parallel", "parallel")),
        name="moe_router",
    )(x, mod, wr, br)


def _run_copies(tile, cnt_ref, start_ref, row_ref, make_copy):
    for r in range(N_RUNS):
        c = cnt_ref[tile * N_RUNS + r]
        src = start_ref[tile * N_RUNS + r]
        dst = row_ref[tile * N_RUNS + r]
        size = MOE_TILE
        while size >= ROW_ALIGN:
            done = c & (-2 * size)
            cp = make_copy(pl.multiple_of(src + done, ROW_ALIGN), pl.multiple_of(dst + done, ROW_ALIGN), size)
            pl.when((c & size) != 0)(cp.start)
            size //= 2


def _dispatch_kernel(cnt_ref, start_ref, row_ref, pad_end_ref, cnt_end_ref, h_ref, route_ref, xs_hbm,
                     loc0, loc1, zbuf, zsem, sem, *, BM, N_TAIL, N_T):
    t = pl.program_id(0)
    loc = (loc0, loc1)

    @pl.when(t == 0)
    def _():
        zbuf[...] = jnp.zeros(zbuf.shape, F32)
        n_rows = xs_hbm.shape[0]

        def pad_copy(e, g):
            start = pl.multiple_of(pad_end_ref[e] - g * ZERO_ROWS, ZERO_ROWS)
            return (start + ZERO_ROWS > cnt_end_ref[e],
                    pltpu.make_async_copy(zbuf.at[pl.ds(0, ZERO_ROWS)], xs_hbm.at[pl.ds(start, ZERO_ROWS)], zsem))

        def tail_copy(e):
            start = pl.multiple_of(pad_end_ref[N_EXPERTS - 1] + e * BM, BM)
            safe = pl.multiple_of(jnp.minimum(start, n_rows - BM), BM)
            return start < n_rows, pltpu.make_async_copy(zbuf, xs_hbm.at[pl.ds(safe, BM)], zsem)

        copies = [pad_copy(e, g) for e in range(N_EXPERTS) for g in range(1, BM // ZERO_ROWS + 1)]
        copies += [tail_copy(e) for e in range(N_TAIL)]
        for cond, cp in copies:
            pl.when(cond)(cp.start)
        for cond, cp in copies:
            pl.when(cond)(cp.wait)

    pos_t = route_ref[...].T[TOP_K:2 * TOP_K, :]
    slot = lax.broadcasted_iota(jnp.int32, (LOC_ROWS, 1), 0).astype(F32)
    pick = jnp.zeros((LOC_ROWS, MOE_TILE), F32)
    for k in range(TOP_K):
        pick = pick + jnp.where(slot == pos_t[k:k + 1, :], 1.0, 0.0)
    pick = pick.astype(BF16)

    def drain(s):
        pltpu.make_async_copy(loc[s], xs_hbm.at[pl.ds(0, LOC_ROWS)], sem.at[s]).wait()

    for s in range(2):
        @pl.when(t % 2 == s)
        def _(s=s):
            pl.when(t >= 2)(functools.partial(drain, s))
            loc[s][...] = _dot(pick, h_ref[...].astype(BF16))
            _run_copies(t, cnt_ref, start_ref, row_ref,
                        lambda a, b, n: pltpu.make_async_copy(loc[s].at[pl.ds(a, n)], xs_hbm.at[pl.ds(b, n)],
                                                              sem.at[s]))

    @pl.when(t == N_T - 1)
    def _():
        if N_T >= 2:
            drain(N_T % 2)
        drain((N_T - 1) % 2)


def _dispatch(plan, h2, route, n_rows, bm):
    n, d = h2.shape
    pairs = LOC_ROWS
    return pl.pallas_call(
        functools.partial(_dispatch_kernel, BM=bm, N_TAIL=n_rows // bm - (n * TOP_K) // bm, N_T=n // MOE_TILE),
        out_shape=jax.ShapeDtypeStruct((n_rows, d), F32),
        grid_spec=pltpu.PrefetchScalarGridSpec(
            num_scalar_prefetch=5,
            grid=(n // MOE_TILE,),
            in_specs=[pl.BlockSpec((MOE_TILE, d), lambda t, *_: (t, 0)),
                      pl.BlockSpec((MOE_TILE, LANE), lambda t, *_: (t, 0))],
            out_specs=pl.BlockSpec(memory_space=pl.ANY),
            scratch_shapes=[pltpu.VMEM((pairs, d), F32),
                            pltpu.VMEM((pairs, d), F32),
                            pltpu.VMEM((bm, d), F32),
                            pltpu.SemaphoreType.DMA(()),
                            pltpu.SemaphoreType.DMA((2,))]),
        compiler_params=pltpu.CompilerParams(dimension_semantics=("arbitrary",), vmem_limit_bytes=VMEM_LIMIT,
                                             disable_bounds_checks=True),
        name="moe_dispatch",
    )(plan["cnt"], plan["start"], plan["row"], plan["pad_end"], plan["cnt_end"], h2, route)


def _expert_kernel(be_ref, nu_ref, xs_ref, w1_ref, b1_ref, w2_ref, b2_ref, o_ref, w1b, w2b):
    i = pl.program_id(0)
    de = w2_ref.shape[0]
    prev = be_ref[jnp.maximum(i - 1, 0)]
    fresh = (i == 0) | (be_ref[i] != prev)

    @pl.when(fresh & (i < nu_ref[0]))
    def _():
        w1b[...] = w1_ref[...].astype(BF16)
        w2b[...] = w2_ref[...].astype(BF16)

    @pl.when(i < nu_ref[0])
    def _():
        hdn = _dot(xs_ref[...].astype(BF16), w1b[...]) + b1_ref[...]
        glu = jnp.minimum(hdn[:, :de], SWIGLU_LIMIT)
        lin = jnp.clip(hdn[:, de:], -SWIGLU_LIMIT, SWIGLU_LIMIT)
        act = glu * _sigmoid(SWIGLU_ALPHA * glu) * (lin + 1.0)
        o_ref[...] = _dot(act.astype(BF16), w2b[...]) + b2_ref[...]

    @pl.when(i >= nu_ref[0])
    def _():
        o_ref[...] = jnp.zeros(o_ref.shape, F32)


def _experts(blk_expert, n_used, xs, w1, b1, w2, b2, layer, bm):
    n_rows, d = xs.shape
    depth, ne, _, dh2 = w1.shape
    de = w2.shape[2]
    n_blocks = n_rows // bm

    def row_map(i, be, nu):
        return (jnp.minimum(i, nu[0] - 1), 0)

    def w_map(i, be, nu):
        return (layer, be[i], 0, 0)

    return pl.pallas_call(
        _expert_kernel,
        out_shape=jax.ShapeDtypeStruct((n_rows, d), F32),
        grid_spec=pltpu.PrefetchScalarGridSpec(
            num_scalar_prefetch=2,
            grid=(n_blocks,),
            in_specs=[pl.BlockSpec((bm, d), row_map),
                      pl.BlockSpec((None, None, d, dh2), w_map),
                      pl.BlockSpec((None, None, 1, dh2), w_map),
                      pl.BlockSpec((None, None, de, d), w_map),
                      pl.BlockSpec((None, None, 1, d), w_map)],
            out_specs=pl.BlockSpec((bm, d), lambda i, be, nu: (i, 0)),
            scratch_shapes=[pltpu.VMEM((d, dh2), BF16), pltpu.VMEM((de, d), BF16)]),
        compiler_params=_params(("arbitrary",)),
        name="moe_experts",
    )(blk_expert, n_used, xs, w1, b1.reshape(depth, ne, 1, dh2), w2, b2.reshape(depth, ne, 1, d))


def _combine_kernel(cnt_ref, start_ref, row_ref, yb_hbm, x_ref, route_ref, mod_ref, fg_ref, o_ref,
                    loc0, loc1, sem, *, nt, n_tiles, final):
    t = pl.program_id(0) * nt + pl.program_id(1)
    pairs = LOC_ROWS
    loc = (loc0, loc1)

    def fetch(tile, s):
        _run_copies(tile, cnt_ref, start_ref, row_ref,
                    lambda a, b, n: pltpu.make_async_copy(yb_hbm.at[pl.ds(b, n)], loc[s].at[pl.ds(a, n)],
                                                          sem.at[s]))

    pl.when(t == 0)(functools.partial(fetch, 0, 0))
    route = route_ref[0]
    slot = lax.broadcasted_iota(jnp.int32, (1, pairs), 1).astype(F32)
    wgt = jnp.zeros((MOE_TILE, pairs), F32)
    for k in range(TOP_K):
        wgt = wgt + jnp.where(slot == route[:, TOP_K + k:TOP_K + k + 1],
                              route[:, 2 * TOP_K + k:2 * TOP_K + k + 1], 0.0)
    hi = wgt.astype(BF16)
    lo = (wgt - hi.astype(F32)).astype(BF16)

    for s in range(2):
        @pl.when(t % 2 == s)
        def _(s=s):
            pl.when(t + 1 < n_tiles)(functools.partial(fetch, t + 1, 1 - s))
            pltpu.make_async_copy(yb_hbm.at[pl.ds(0, pairs)], loc[s], sem.at[s]).wait()
            rows = loc[s][...].astype(BF16)
            y = _dot(hi, rows) + _dot(lo, rows)
            out = x_ref[0] + mod_ref[0, 5:6, :] * y
            if final:
                out = _rms(out) * fg_ref[...]
            o_ref[0] = out


def _combine(plan, yb, x, route, mod, final_g, final):
    bsz, s, d = x.shape
    nt = s // MOE_TILE
    pairs = LOC_ROWS
    return pl.pallas_call(
        functools.partial(_combine_kernel, nt=nt, n_tiles=bsz * nt, final=final),
        out_shape=jax.ShapeDtypeStruct((bsz, s, d), F32),
        grid_spec=pltpu.PrefetchScalarGridSpec(
            num_scalar_prefetch=3,
            grid=(bsz, nt),
            in_specs=[pl.BlockSpec(memory_space=pl.ANY),
                      pl.BlockSpec((1, MOE_TILE, d), lambda b, i, *_: (b, i, 0)),
                      pl.BlockSpec((1, MOE_TILE, LANE), lambda b, i, *_: (b, i, 0)),
                      pl.BlockSpec((1, 6, d), lambda b, i, *_: (b, 0, 0)),
                      pl.BlockSpec((1, d), lambda b, i, *_: (0, 0))],
            out_specs=pl.BlockSpec((1, MOE_TILE, d), lambda b, i, *_: (b, i, 0)),
            scratch_shapes=[pltpu.VMEM((pairs, d), F32), pltpu.VMEM((pairs, d), F32),
                            pltpu.SemaphoreType.DMA((2,))]),
        compiler_params=pltpu.CompilerParams(dimension_semantics=("arbitrary", "arbitrary"),
                                             vmem_limit_bytes=VMEM_LIMIT, disable_bounds_checks=True),
        name="moe_combine",
    )(plan["cnt"], plan["start"], plan["row"], yb, x, route, mod, final_g.reshape(1, d))


def _rearranged_in_proj(w_in, b_in):
    sizes = (ML_HEADS * ML_QK, ML_HEADS * ML_QK, ML_HEADS * ML_V, ML_HEADS * ML_V, ML_HEADS, ML_HEADS,
             DSA_HEADS * DSA_DIM, DSA_LATENT, IDX_HEADS * IDX_DIM, IDX_DIM, IDX_HEADS,
             3 * FOX_HEADS * FOX_DIM, FOX_HEADS, N_BRANCH * w_in.shape[0])
    offs = [0]
    for sz in sizes:
        offs.append(offs[-1] + sz)
    (o_mq, o_mk, o_mv, o_mo, o_mi, o_mf, o_dq, o_ckv, o_iq, o_ik, o_iw, o_fx, o_ff, o_g, o_end) = offs
    pad = LANE - (IDX_DIM + 2 * ML_HEADS + IDX_HEADS + FOX_HEADS)

    def cols(a):
        parts = [a[..., o_mq:o_mi],
                 a[..., o_dq:o_ckv],
                 a[..., o_iq:o_ik],
                 a[..., o_ckv:o_iq],
                 a[..., o_ik:o_iw],
                 a[..., o_mi:o_dq],
                 a[..., o_iw:o_fx],
                 a[..., o_ff:o_g],
                 jnp.zeros(a.shape[:-1] + (pad,), a.dtype),
                 a[..., o_fx:o_ff],
                 a[..., o_g:o_end]]
        return jnp.concatenate(parts, axis=-1)

    return cols(w_in).astype(BF16), cols(b_in.reshape(1, -1))


def _moe_plan(stats, bm, n_blocks):
    cnt = stats[:, 0, :N_EXPERTS].astype(jnp.int32)
    start = stats[:, 1, :N_EXPERTS].astype(jnp.int32)
    total = jnp.sum(cnt, axis=0)
    padded = (total + bm - 1) // bm * bm
    pad_end = jnp.cumsum(padded)
    pad_start = pad_end - padded
    row = pad_start[None, :] + jnp.cumsum(cnt, axis=0) - cnt
    used = jnp.sum(cnt, axis=1, keepdims=True)
    cnt = jnp.concatenate([cnt, LOC_ROWS - used], axis=1)
    start = jnp.concatenate([start, used], axis=1)
    parity = jnp.arange(cnt.shape[0], dtype=jnp.int32)[:, None] % 2
    row = jnp.concatenate([row, n_blocks * bm + parity * MOE_TILE], axis=1)
    blk_row = jnp.arange(n_blocks + 1, dtype=jnp.int32) * bm
    blk_expert = jnp.minimum(jnp.sum((pad_end[None, :] <= blk_row[:, None]).astype(jnp.int32), axis=1),
                             N_EXPERTS - 1)
    n_used = (pad_end[-1:] // bm).astype(jnp.int32)
    plan = dict(cnt=cnt.reshape(-1), start=start.reshape(-1), row=row.reshape(-1).astype(jnp.int32),
                pad_end=pad_end.astype(jnp.int32), cnt_end=(pad_start + total).astype(jnp.int32))
    return plan, blk_expert, n_used


def kernel(x, c, w_ada, b_ada, w_in, b_in, conv_w, conv_b, ml_norm_g, kv_norm_g, w_uk, w_uv,
           w_br_ml, w_br_dsa, w_br_fox, w_out, w_router, b_router, w1, b1, w2, b2, final_g):
    bsz, s, d = x.shape
    depth = w_in.shape[0]
    n = bsz * s
    bm = 2 * MOE_TILE
    n_blocks = -(-(n * TOP_K + (ROW_ALIGN - 1) * N_EXPERTS * (n // MOE_TILE)) // bm) + N_EXPERTS
    mods = _ada_mod(c, w_ada, b_ada).reshape(depth, bsz, 6, d)
    for l in range(depth):
        mod = mods[l]
        w_r, b_r = _rearranged_in_proj(w_in[l], b_in[l])
        proj, misc = _in_proj(x, mod, w_r, b_r)
        y_ml, fcol, frow = _mlstm(proj, misc, conv_w[l], conv_b[l], ml_norm_g[l])
        y_fox = _fox(proj, fcol, frow)
        y_dsa = _dsa(proj, misc, kv_norm_g[l], w_uk[l].astype(BF16), jnp.swapaxes(w_uv[l], 1, 2).astype(BF16))
        x = _merge(x, proj, y_ml, y_dsa, y_fox, mod, w_br_ml[l].astype(BF16), w_br_dsa[l].astype(BF16),
                   w_br_fox[l].astype(BF16), w_out[l].astype(BF16))
        h2, route, stats = _router(x, mod, w_router[l], b_router[l])
        plan, blk_expert, n_used = _moe_plan(stats, bm, n_blocks)
        xs = _dispatch(plan, h2.reshape(n, d), route.reshape(n, LANE), (n_blocks + 1) * bm, bm)
        yb = _experts(blk_expert, n_used, xs, w1, b1, w2, b2, l, bm)
        x = _combine(plan, yb, x, route, mod, final_g, final=(l == depth - 1))
    return x
```

```python
import functools

import jax
import jax.numpy as jnp
from jax import lax
from jax.experimental import pallas as pl
from jax.experimental.pallas import tpu as pltpu

F32 = jnp.float32
BF16 = jnp.bfloat16
HIGHEST = lax.Precision.HIGHEST

EPS = 1e-6
LOG2E = 1.4426950408889634
CHUNK = 64

ML_HEADS, ML_QK, ML_V, ML_CONV = 4, 64, 128, 4
DSA_HEADS, DSA_DIM, DSA_LATENT = 4, 128, 128
IDX_HEADS, IDX_DIM, DSA_TOPK = 4, 64, 256
FOX_HEADS, FOX_DIM = 4, 128
N_BRANCH = 3
N_EXPERTS, TOP_K = 32, 4
SWIGLU_LIMIT, SWIGLU_ALPHA = 7.0, 1.702

LANE = 128
ROW_ALIGN = 8
INT_MIN = -2 ** 31

MOE_TILE = 256
N_RUNS = N_EXPERTS + 1
LOC_ROWS = TOP_K * MOE_TILE + MOE_TILE
ZERO_ROWS = 128

C_MLQK = 0
C_MLV = 512
C_MLO = 1024
C_DQ = 1536
C_DIQ = 2048
C_CKV = 2304
C_MISC = 2432
C_FOX = 2560
C_GATE = 4096
M_IK, M_MLI, M_MLF, M_IW, M_FXF = 0, 64, 68, 72, 76

VMEM_LIMIT = 56 * 1024 * 1024


def _dot(a, b, prec=None):
    return jnp.dot(a, b, preferred_element_type=F32, precision=prec)


def _dot_nt(a, b, prec=None):
    return lax.dot_general(a, b, (((1,), (1,)), ((), ())), preferred_element_type=F32, precision=prec)


def _dot_tn(a, b):
    return lax.dot_general(a, b, (((0,), (0,)), ((), ())), preferred_element_type=F32)


def _sigmoid(x):
    return 1.0 / (1.0 + jnp.exp(-x))


def _log_sigmoid(x):
    return jnp.minimum(x, 0.0) - jnp.log1p(jnp.exp(-jnp.abs(x)))


def _rms(x):
    return x * lax.rsqrt(jnp.mean(x * x, axis=-1, keepdims=True) + EPS)


def _params(sem, vmem=VMEM_LIMIT):
    return pltpu.CompilerParams(dimension_semantics=sem, vmem_limit_bytes=vmem)


def _ada_kernel(c_ref, w_ref, b_ref, o_ref):
    c = c_ref[...]
    o_ref[0] = _dot(c * _sigmoid(c), w_ref[0], HIGHEST) + b_ref[0]


def _ada_mod(c, w_ada, b_ada):
    depth, d, n = w_ada.shape
    bsz = c.shape[0]
    tn = 1536
    return pl.pallas_call(
        _ada_kernel,
        out_shape=jax.ShapeDtypeStruct((depth, bsz, n), F32),
        grid=(depth, n // tn),
        in_specs=[pl.BlockSpec((bsz, d), lambda l, j: (0, 0)),
                  pl.BlockSpec((1, d, tn), lambda l, j: (l, 0, j)),
                  pl.BlockSpec((1, 1, tn), lambda l, j: (l, 0, j))],
        out_specs=pl.BlockSpec((1, bsz, tn), lambda l, j: (l, 0, j)),
        compiler_params=_params(("parallel", "parallel")),
        name="ada_mod",
    )(c, w_ada, b_ada.reshape(depth, 1, n))


def _inproj_kernel(x_ref, mod_ref, w_ref, b_ref, o_ref, misc_ref, h_scr, *, tn):
    j = pl.program_id(2)

    @pl.when(j == 0)
    def _():
        h = _rms(x_ref[0]) * (1.0 + mod_ref[0, 1:2, :]) + mod_ref[0, 0:1, :]
        h_scr[...] = h.astype(BF16)

    acc = _dot(h_scr[...], w_ref[...]) + b_ref[...]
    o_ref[0] = acc.astype(BF16)

    @pl.when(j == C_MISC // tn)
    def _():
        misc_ref[0] = acc[:, C_MISC % tn:C_MISC % tn + LANE]


def _in_proj(x, mod, w, b):
    bsz, s, d = x.shape
    n = w.shape[1]
    tm = min(2048, s)
    tn = 1024
    return pl.pallas_call(
        functools.partial(_inproj_kernel, tn=tn),
        out_shape=(jax.ShapeDtypeStruct((bsz, s, n), BF16), jax.ShapeDtypeStruct((bsz, s, LANE), F32)),
        grid=(bsz, s // tm, n // tn),
        in_specs=[pl.BlockSpec((1, tm, d), lambda bi, i, j: (bi, i, 0)),
                  pl.BlockSpec((1, 6, d), lambda bi, i, j: (bi, 0, 0)),
                  pl.BlockSpec((d, tn), lambda bi, i, j: (0, j)),
                  pl.BlockSpec((1, tn), lambda bi, i, j: (0, j))],
        out_specs=(pl.BlockSpec((1, tm, tn), lambda bi, i, j: (bi, i, j)),
                   pl.BlockSpec((1, tm, LANE), lambda bi, i, j: (bi, i, 0))),
        scratch_shapes=[pltpu.VMEM((tm, d), BF16)],
        compiler_params=_params(("parallel", "parallel", "arbitrary")),
        name="in_proj",
    )(x, mod, w, b)


def _mlstm_kernel(qk_ref, v_ref, o_ref, misc_ref, cw_ref, cb_ref, g_ref,
                  y_ref, fcol_ref, frow_ref,
                  xext, ct_scr, n_scr, m_scr, carry_scr, *, L):
    c = pl.program_id(1)
    nqk = ML_HEADS * ML_QK

    @pl.when(c == 0)
    def _():
        xext[0:8, :] = jnp.zeros((8, 2 * nqk), F32)
        ct_scr[...] = jnp.zeros(ct_scr.shape, F32)
        n_scr[...] = jnp.zeros(n_scr.shape, F32)
        m_scr[...] = jnp.full(m_scr.shape, -jnp.inf, F32)
        carry_scr[...] = jnp.zeros(carry_scr.shape, F32)

    @pl.when(c > 0)
    def _():
        xext[0:8, :] = xext[L:L + 8, :]

    xext[8:8 + L, :] = qk_ref[0].astype(F32)
    cw = cw_ref[...]
    conv = (cb_ref[...] + cw[3:4, :] * xext[8:8 + L, :] + cw[2:3, :] * xext[7:7 + L, :]
            + cw[1:2, :] * xext[6:6 + L, :] + cw[0:1, :] * xext[5:5 + L, :])
    qk = conv * _sigmoid(conv)

    misc = misc_ref[0]
    ls = _log_sigmoid(misc)
    row = lax.broadcasted_iota(jnp.int32, (L, L), 0)
    col = lax.broadcasted_iota(jnp.int32, (L, L), 1)
    causal = row >= col
    tri = jnp.where(causal, 1.0, 0.0).astype(F32)
    cs = _dot(tri, ls, HIGHEST)
    cs_t = cs.T
    misc_t = misc.T
    carry = carry_scr[0:1, :]
    fcol_ref[0] = cs + carry
    for h in range(FOX_HEADS):
        frow_ref[0, h:h + 1, :] = cs_t[M_FXF + h:M_FXF + h + 1, :] + carry[:, M_FXF + h:M_FXF + h + 1]
    frow_ref[0, 4:8, :] = jnp.zeros((4, L), F32)
    carry_scr[0:1, :] = carry + cs[L - 1:L, :]

    for h in range(ML_HEADS):
        qh = qk[:, h * ML_QK:(h + 1) * ML_QK] * (ML_QK ** -0.5)
        kh = qk[:, nqk + h * ML_QK:nqk + (h + 1) * ML_QK]
        vb = v_ref[0, :, h * ML_V:(h + 1) * ML_V].astype(BF16)
        i_col = misc[:, M_MLI + h:M_MLI + h + 1]
        i_row = misc_t[M_MLI + h:M_MLI + h + 1, :]
        b_col = cs[:, M_MLF + h:M_MLF + h + 1]
        b_row = cs_t[M_MLF + h:M_MLF + h + 1, :]
        b_last = b_col[L - 1:L, :]
        m_prev = m_scr[h:h + 1, 0:1]

        d_log = jnp.where(causal, b_col - b_row + i_row, -jnp.inf)
        inter_log = b_col + m_prev
        m_out = jnp.maximum(inter_log, jnp.max(d_log, axis=-1, keepdims=True))
        qb = qh.astype(BF16)
        kb = kh.astype(BF16)
        s = _dot_nt(qb, kb) * jnp.exp(d_log - m_out)
        a_inter = jnp.exp(inter_log - m_out)
        ct = ct_scr[h]
        n_row = n_scr[h]
        num = _dot(s.astype(BF16), vb) + a_inter * _dot(qb, ct.astype(BF16))
        den = jnp.sum(s, axis=-1, keepdims=True) + a_inter * jnp.sum(qh * n_row, axis=-1, keepdims=True)
        hid = num / jnp.maximum(jnp.abs(den), jnp.exp(-m_out))

        w_state = b_last - b_col + i_col
        m_loc = jnp.max(w_state, axis=0, keepdims=True)
        ke = kh * jnp.exp(w_state - m_loc)
        c_loc = _dot_tn(ke.astype(BF16), vb)
        n_loc = jnp.sum(ke, axis=0, keepdims=True)
        m_new = jnp.maximum(b_last + m_prev, m_loc)
        decay = jnp.exp(b_last + m_prev - m_new)
        scale = jnp.exp(m_loc - m_new)
        ct_scr[h] = decay * ct + scale * c_loc
        n_scr[h] = decay * n_row + scale * n_loc
        m_scr[h:h + 1, :] = jnp.broadcast_to(m_new, (1, LANE))

        y = (_rms(hid) * g_ref[:, h * ML_V:(h + 1) * ML_V]
             * _sigmoid(o_ref[0, :, h * ML_V:(h + 1) * ML_V].astype(F32)))
        y_ref[0, :, h * ML_V:(h + 1) * ML_V] = y.astype(BF16)


def _mlstm(proj, misc, conv_w, conv_b, norm_g):
    bsz, s, _ = proj.shape
    L = min(256, s)
    w = 2 * ML_HEADS * ML_QK
    wv = ML_HEADS * ML_V
    return pl.pallas_call(
        functools.partial(_mlstm_kernel, L=L),
        out_shape=(jax.ShapeDtypeStruct((bsz, s, wv), BF16),
                   jax.ShapeDtypeStruct((bsz, s, LANE), F32),
                   jax.ShapeDtypeStruct((bsz, 8, s), F32)),
        grid=(bsz, s // L),
        in_specs=[pl.BlockSpec((1, L, w), lambda b, c: (b, c, C_MLQK // w)),
                  pl.BlockSpec((1, L, wv), lambda b, c: (b, c, C_MLV // wv)),
                  pl.BlockSpec((1, L, wv), lambda b, c: (b, c, C_MLO // wv)),
                  pl.BlockSpec((1, L, LANE), lambda b, c: (b, c, 0)),
                  pl.BlockSpec((ML_CONV, w), lambda b, c: (0, 0)),
                  pl.BlockSpec((1, w), lambda b, c: (0, 0)),
                  pl.BlockSpec((1, wv), lambda b, c: (0, 0))],
        out_specs=(pl.BlockSpec((1, L, wv), lambda b, c: (b, c, 0)),
                   pl.BlockSpec((1, L, LANE), lambda b, c: (b, c, 0)),
                   pl.BlockSpec((1, 8, L), lambda b, c: (b, 0, c))),
        scratch_shapes=[pltpu.VMEM((L + 8, w), F32),
                        pltpu.VMEM((ML_HEADS, ML_QK, ML_V), F32),
                        pltpu.VMEM((ML_HEADS, 1, ML_QK), F32),
                        pltpu.VMEM((8, LANE), F32),
                        pltpu.VMEM((8, LANE), F32)],
        compiler_params=_params(("parallel", "arbitrary")),
        name="mlstm",
    )(proj, proj, proj, misc, conv_w, conv_b.reshape(1, w), norm_g.reshape(1, wv))


def _fox_kernel(q_ref, k_ref, v_ref, fcol_ref, frow_ref, y_ref, k_scr, vt_scr, *, T, S):
    qi = pl.program_id(1)
    d = FOX_DIM

    @pl.when(qi == 0)
    def _():
        for h in range(FOX_HEADS):
            k_scr[h] = k_ref[0, :, h * d:(h + 1) * d].astype(BF16)
            vt_scr[h] = v_ref[0, :, h * d:(h + 1) * d].astype(F32).T.astype(BF16)

    q_t = (q_ref[0].astype(F32) * (d ** -0.5 * LOG2E)).T

    def body(ext):
        kpos = lax.broadcasted_iota(jnp.int32, (ext, 1), 0)
        qpos = (ext - T) + lax.broadcasted_iota(jnp.int32, (1, T), 1)
        causal = kpos <= qpos
        for h in range(FOX_HEADS):
            s = _dot(k_scr[h, 0:ext, :], q_t[h * d:(h + 1) * d, :].astype(BF16))
            s = s + (frow_ref[0, h:h + 1, :] * LOG2E - fcol_ref[0, 0:ext, M_FXF + h:M_FXF + h + 1] * LOG2E)
            s = jnp.where(causal, s, -jnp.inf)
            p = jnp.exp2(s - jnp.max(s, axis=0, keepdims=True))
            l = jnp.sum(p, axis=0, keepdims=True)
            out_t = _dot(vt_scr[h, :, 0:ext], p.astype(BF16)) / l
            y_ref[0, :, h * d:(h + 1) * d] = out_t.T.astype(BF16)

    for c in range(S // T):
        pl.when(qi == c)(functools.partial(body, (c + 1) * T))


def _fox(proj, fcol, frow):
    bsz, s, _ = proj.shape
    T = min(256, s)
    w = FOX_HEADS * FOX_DIM
    return pl.pallas_call(
        functools.partial(_fox_kernel, T=T, S=s),
        out_shape=jax.ShapeDtypeStruct((bsz, s, w), BF16),
        grid=(bsz, s // T),
        in_specs=[pl.BlockSpec((1, T, w), lambda b, i: (b, i, C_FOX // w)),
                  pl.BlockSpec((1, s, w), lambda b, i: (b, 0, C_FOX // w + 1)),
                  pl.BlockSpec((1, s, w), lambda b, i: (b, 0, C_FOX // w + 2)),
                  pl.BlockSpec((1, s, LANE), lambda b, i: (b, 0, 0)),
                  pl.BlockSpec((1, 8, T), lambda b, i: (b, 0, i))],
        out_specs=pl.BlockSpec((1, T, w), lambda b, i: (b, i, 0)),
        scratch_shapes=[pltpu.VMEM((FOX_HEADS, s, FOX_DIM), BF16),
                        pltpu.VMEM((FOX_HEADS, FOX_DIM, s), BF16)],
        compiler_params=_params(("parallel", "arbitrary")),
        name="fox_attention",
    )(proj, proj, proj, fcol, frow)


def _dsa_body(ext, qi, q_ref, qidx_ref, misc_q_ref, wuk_ref, wuv_ref, y_ref,
              ckvn_scr, ckvnt_scr, kidx_scr, sel_scr, *, T, n_sel, rank_tile):
    dh = DSA_DIM
    q_t = q_ref[0].astype(F32).T
    qidx_t = qidx_ref[0].astype(F32).T.astype(BF16)
    w_t = misc_q_ref[0].T[M_IW:M_IW + IDX_HEADS, :] * (IDX_HEADS ** -0.5)
    kidx = kidx_scr[0:ext, :]
    score = jnp.zeros((ext, T), F32)
    for h in range(IDX_HEADS):
        lg = _dot(kidx, qidx_t[h * IDX_DIM:(h + 1) * IDX_DIM, :]) * (IDX_DIM ** -0.5)
        score = score + w_t[h:h + 1, :] * jnp.maximum(lg, 0.0)
    kpos = lax.broadcasted_iota(jnp.int32, (ext, 1), 0)
    qchunk = (qi * T + lax.broadcasted_iota(jnp.int32, (1, T), 1)) // CHUNK
    score = jnp.where((kpos // CHUNK) <= qchunk, score, -jnp.inf)

    def as_float(c):
        return pltpu.bitcast(jnp.where(c < 0, c ^ jnp.int32(0x7FFFFFFF), c), F32)

    def count_ge(c):
        return jnp.sum(jnp.where(score >= as_float(c), 1.0, 0.0), axis=0, keepdims=True)

    t0 = jnp.where(count_ge(jnp.zeros((1, T), jnp.int32)) >= n_sel, jnp.int32(0), jnp.int32(INT_MIN))

    def bis(i, t):
        cand = t + jnp.left_shift(jnp.int32(1), jnp.int32(30) - i)
        return jnp.where(count_ge(cand) >= n_sel, cand, t)

    t = lax.fori_loop(0, 31, bis, t0)
    n_vis = ((qchunk + 1) * CHUNK).astype(F32)
    thr = jnp.where(n_vis <= n_sel, -3.0e38, as_float(t))
    need = n_sel - jnp.sum(jnp.where(score > thr, 1.0, 0.0), axis=0, keepdims=True)
    r_i = lax.broadcasted_iota(jnp.int32, (rank_tile, rank_tile), 0)
    c_i = lax.broadcasted_iota(jnp.int32, (rank_tile, rank_tile), 1)
    lower = jnp.where(c_i < r_i, 1.0, 0.0).astype(BF16)
    carry = jnp.zeros((1, T), F32)
    for j in range(ext // rank_tile):
        sc = score[j * rank_tile:(j + 1) * rank_tile, :]
        eq = jnp.where(sc == thr, 1.0, 0.0)
        rank = _dot(lower, eq.astype(BF16)) + carry
        carry = carry + jnp.sum(eq, axis=0, keepdims=True)
        sel_scr[j * rank_tile:(j + 1) * rank_tile, :] = jnp.where(
            sc > thr, 1.0, jnp.where(rank < need, eq, 0.0))

    sel = sel_scr[0:ext, :] > 0.5
    ckvn = ckvn_scr[0:ext, :]
    ckvn_t = ckvnt_scr[:, 0:ext]
    for h in range(DSA_HEADS):
        qa_t = _dot(wuk_ref[h], q_t[h * dh:(h + 1) * dh, :].astype(BF16)) * (dh ** -0.5 * LOG2E)
        lg = _dot(ckvn, qa_t.astype(BF16))
        lg = jnp.where(sel, lg, -jnp.inf)
        p = jnp.exp2(lg - jnp.max(lg, axis=0, keepdims=True))
        l = jnp.sum(p, axis=0, keepdims=True)
        lat_t = _dot(ckvn_t, p.astype(BF16)) / l
        out_t = _dot(wuv_ref[h], lat_t.astype(BF16))
        y_ref[0, :, h * dh:(h + 1) * dh] = out_t.T.astype(BF16)


def _dsa_kernel(q_ref, ckv_ref, qidx_ref, misc_all_ref, misc_q_ref, g_ref, wuk_ref, wuv_ref, y_ref,
                ckvn_scr, ckvnt_scr, kidx_scr, sel_scr, *, T, S, n_sel, n_cls, rank_tile):
    qi = pl.program_id(1)

    @pl.when(qi == 0)
    def _():
        ckvn = _rms(ckv_ref[0].astype(F32)) * g_ref[...]
        ckvn_scr[...] = ckvn.astype(BF16)
        ckvnt_scr[...] = ckvn.T.astype(BF16)
        kidx_scr[...] = misc_all_ref[0, :, M_IK:M_IK + IDX_DIM].astype(BF16)

    per = (S // T) // n_cls
    for c in range(n_cls):
        ext = (c + 1) * per * T

        @pl.when(qi // per == c)
        def _(ext=ext):
            _dsa_body(ext, qi, q_ref, qidx_ref, misc_q_ref, wuk_ref, wuv_ref, y_ref,
                      ckvn_scr, ckvnt_scr, kidx_scr, sel_scr, T=T, n_sel=n_sel, rank_tile=rank_tile)


def _dsa(proj, misc, kv_g, wuk, wuv_t):
    bsz, s, _ = proj.shape
    T = min(256, s)
    n_sel = min(DSA_TOPK, s // 4)
    n_cls = max(1, min(4, s // 512))
    rank_tile = 256
    w = DSA_HEADS * DSA_DIM
    wi = IDX_HEADS * IDX_DIM
    return pl.pallas_call(
        functools.partial(_dsa_kernel, T=T, S=s, n_sel=float(n_sel), n_cls=n_cls, rank_tile=rank_tile),
        out_shape=jax.ShapeDtypeStruct((bsz, s, w), BF16),
        grid=(bsz, s // T),
        in_specs=[pl.BlockSpec((1, T, w), lambda b, i: (b, i, C_DQ // w)),
                  pl.BlockSpec((1, s, DSA_LATENT), lambda b, i: (b, 0, C_CKV // DSA_LATENT)),
                  pl.BlockSpec((1, T, wi), lambda b, i: (b, i, C_DIQ // wi)),
                  pl.BlockSpec((1, s, LANE), lambda b, i: (b, 0, 0)),
                  pl.BlockSpec((1, T, LANE), lambda b, i: (b, i, 0)),
                  pl.BlockSpec((1, DSA_LATENT), lambda b, i: (0, 0)),
                  pl.BlockSpec((DSA_HEADS, DSA_LATENT, DSA_DIM), lambda b, i: (0, 0, 0)),
                  pl.BlockSpec((DSA_HEADS, DSA_DIM, DSA_LATENT), lambda b, i: (0, 0, 0))],
        out_specs=pl.BlockSpec((1, T, w), lambda b, i: (b, i, 0)),
        scratch_shapes=[pltpu.VMEM((s, DSA_LATENT), BF16),
                        pltpu.VMEM((DSA_LATENT, s), BF16),
                        pltpu.VMEM((s, IDX_DIM), BF16),
                        pltpu.VMEM((s, T), F32)],
        compiler_params=_params(("parallel", "arbitrary")),
        name="dsa_attention",
    )(proj, proj, proj, misc, misc, kv_g.reshape(1, DSA_LATENT), wuk, wuv_t)


def _merge_kernel(yml_ref, ydsa_ref, yfox_ref, g0_ref, g1_ref, g2_ref, x_ref, mod_ref,
                  wml_ref, wdsa_ref, wfox_ref, wout_ref, o_ref):
    merged = (_sigmoid(g0_ref[0].astype(F32)) * _dot(yml_ref[0], wml_ref[...])
              + _sigmoid(g1_ref[0].astype(F32)) * _dot(ydsa_ref[0], wdsa_ref[...])
              + _sigmoid(g2_ref[0].astype(F32)) * _dot(yfox_ref[0], wfox_ref[...]))
    out = _dot(merged.astype(BF16), wout_ref[...])
    o_ref[0] = x_ref[0] + mod_ref[0, 2:3, :] * out


def _merge(x, proj, y_ml, y_dsa, y_fox, mod, w_ml, w_dsa, w_fox, w_out):
    bsz, s, d = x.shape
    tm = min(512, s)
    wb = y_ml.shape[-1]
    gb = C_GATE // d
    yspec = pl.BlockSpec((1, tm, wb), lambda b, i: (b, i, 0))
    wspec = pl.BlockSpec((wb, d), lambda b, i: (0, 0))
    return pl.pallas_call(
        _merge_kernel,
        out_shape=jax.ShapeDtypeStruct((bsz, s, d), F32),
        grid=(bsz, s // tm),
        in_specs=[yspec, yspec, yspec,
                  pl.BlockSpec((1, tm, d), lambda b, i: (b, i, gb)),
                  pl.BlockSpec((1, tm, d), lambda b, i: (b, i, gb + 1)),
                  pl.BlockSpec((1, tm, d), lambda b, i: (b, i, gb + 2)),
                  pl.BlockSpec((1, tm, d), lambda b, i: (b, i, 0)),
                  pl.BlockSpec((1, 6, d), lambda b, i: (b, 0, 0)),
                  wspec, wspec, wspec,
                  pl.BlockSpec((d, d), lambda b, i: (0, 0))],
        out_specs=pl.BlockSpec((1, tm, d), lambda b, i: (b, i, 0)),
        compiler_params=_params(("parallel", "parallel")),
        name="merge_out",
    )(y_ml, y_dsa, y_fox, proj, proj, proj, x, mod, w_ml, w_dsa, w_fox, w_out)


def _router_kernel(x_ref, mod_ref, wr_ref, br_ref, h_ref, route_ref, stat_ref, *, T):
    h = _rms(x_ref[0]) * (1.0 + mod_ref[0, 4:5, :]) + mod_ref[0, 3:4, :]
    h_ref[0] = h
    lg = _dot(h, wr_ref[...], HIGHEST) + br_ref[...]
    lg = lg.T[0:N_EXPERTS, :]
    sub = lax.broadcasted_iota(jnp.int32, (N_EXPERTS, 1), 0).astype(F32)
    vals, idxs, hots = [], [], []
    for _ in range(TOP_K):
        mx = jnp.max(lg, axis=0, keepdims=True)
        idx = jnp.min(jnp.where(lg == mx, sub, float(N_EXPERTS)), axis=0, keepdims=True)
        hot = sub == idx
        vals.append(mx)
        idxs.append(idx)
        hots.append(hot)
        lg = jnp.where(hot, -jnp.inf, lg)
    exps = [jnp.exp(v - vals[0]) for v in vals]
    tot = exps[0] + exps[1] + exps[2] + exps[3]
    multi = jnp.zeros((N_EXPERTS, T), F32)
    for hot in hots:
        multi = multi + jnp.where(hot, 1.0, 0.0)
    r_i = lax.broadcasted_iota(jnp.int32, (T, T), 0)
    c_i = lax.broadcasted_iota(jnp.int32, (T, T), 1)
    upper = jnp.where(r_i < c_i, 1.0, 0.0).astype(BF16)
    before = _dot(multi.astype(BF16), upper)
    cnt = jnp.broadcast_to(jnp.sum(multi, axis=1, keepdims=True), (N_EXPERTS, LANE))
    cnt = jnp.floor((cnt + (ROW_ALIGN - 1.0)) * (1.0 / ROW_ALIGN)) * ROW_ALIGN
    a_i = lax.broadcasted_iota(jnp.int32, (N_EXPERTS, N_EXPERTS), 0)
    b_i = lax.broadcasted_iota(jnp.int32, (N_EXPERTS, N_EXPERTS), 1)
    start = _dot(jnp.where(b_i < a_i, 1.0, 0.0).astype(F32), cnt, HIGHEST)
    stat_ref[0, 0:N_EXPERTS, :] = cnt
    stat_ref[0, N_EXPERTS:2 * N_EXPERTS, :] = start
    where_to = before + start[:, 0:1]
    rows = list(idxs)
    rows += [jnp.sum(jnp.where(hot, where_to, 0.0), axis=0, keepdims=True) for hot in hots]
    rows += [e / tot for e in exps]
    rows.append(jnp.zeros((LANE - 3 * TOP_K, T), F32))
    route_ref[0] = jnp.concatenate(rows, axis=0).T


def _router(x, mod, w_router, b_router):
    bsz, s, d = x.shape
    T = MOE_TILE
    nt = s // T
    ne = w_router.shape[1]
    wr = jnp.zeros((d, LANE), F32).at[:, :ne].set(w_router)
    br = jnp.full((1, LANE), -1e30, F32).at[0, :ne].set(b_router)
    return pl.pallas_call(
        functools.partial(_router_kernel, T=T),
        out_shape=(jax.ShapeDtypeStruct((bsz, s, d), F32),
                   jax.ShapeDtypeStruct((bsz, s, LANE), F32),
                   jax.ShapeDtypeStruct((bsz * nt, 2 * N_EXPERTS, LANE), F32)),
        grid=(bsz, nt),
        in_specs=[pl.BlockSpec((1, T, d), lambda b, i: (b, i, 0)),
                  pl.BlockSpec((1, 6, d), lambda b, i: (b, 0, 0)),
                  pl.BlockSpec((d, LANE), lambda b, i: (0, 0)),
                  pl.BlockSpec((1, LANE), lambda b, i: (0, 0))],
        out_specs=(pl.BlockSpec((1, T, d), lambda b, i: (b, i, 0)),
                   pl.BlockSpec((1, T, LANE), lambda b, i: (b, i, 0)),
                   pl.BlockSpec((1, 2 * N_EXPERTS, LANE), lambda b, i: (b * nt + i, 0, 0))),
        compiler_params=_params(("parallel", "parallel")),
        name="moe_router",
    )(x, mod, wr, br)


def _run_copies(tile, cnt_ref, start_ref, row_ref, make_copy):
    for r in range(N_RUNS):
        c = cnt_ref[tile * N_RUNS + r]
        src = start_ref[tile * N_RUNS + r]
        dst = row_ref[tile * N_RUNS + r]
        size = MOE_TILE
        while size >= ROW_ALIGN:
            done = c & (-2 * size)
            cp = make_copy(pl.multiple_of(src + done, ROW_ALIGN), pl.multiple_of(dst + done, ROW_ALIGN), size)
            pl.when((c & size) != 0)(cp.start)
            size //= 2


def _dispatch_kernel(cnt_ref, start_ref, row_ref, pad_end_ref, cnt_end_ref, h_ref, route_ref, xs_hbm,
                     loc0, loc1, zbuf, zsem, sem, *, BM, N_TAIL, N_T):
    t = pl.program_id(0)
    loc = (loc0, loc1)

    @pl.when(t == 0)
    def _():
        zbuf[...] = jnp.zeros(zbuf.shape, F32)
        n_rows = xs_hbm.shape[0]

        def pad_copy(e, g):
            start = pl.multiple_of(pad_end_ref[e] - g * ZERO_ROWS, ZERO_ROWS)
            return (start + ZERO_ROWS > cnt_end_ref[e],
                    pltpu.make_async_copy(zbuf.at[pl.ds(0, ZERO_ROWS)], xs_hbm.at[pl.ds(start, ZERO_ROWS)], zsem))

        def tail_copy(e):
            start = pl.multiple_of(pad_end_ref[N_EXPERTS - 1] + e * BM, BM)
            safe = pl.multiple_of(jnp.minimum(start, n_rows - BM), BM)
            return start < n_rows, pltpu.make_async_copy(zbuf, xs_hbm.at[pl.ds(safe, BM)], zsem)

        copies = [pad_copy(e, g) for e in range(N_EXPERTS) for g in range(1, BM // ZERO_ROWS + 1)]
        copies += [tail_copy(e) for e in range(N_TAIL)]
        for cond, cp in copies:
            pl.when(cond)(cp.start)
        for cond, cp in copies:
            pl.when(cond)(cp.wait)

    pos_t = route_ref[...].T[TOP_K:2 * TOP_K, :]
    slot = lax.broadcasted_iota(jnp.int32, (LOC_ROWS, 1), 0).astype(F32)
    pick = jnp.zeros((LOC_ROWS, MOE_TILE), F32)
    for k in range(TOP_K):
        pick = pick + jnp.where(slot == pos_t[k:k + 1, :], 1.0, 0.0)
    pick = pick.astype(BF16)

    def drain(s):
        pltpu.make_async_copy(loc[s], xs_hbm.at[pl.ds(0, LOC_ROWS)], sem.at[s]).wait()

    for s in range(2):
        @pl.when(t % 2 == s)
        def _(s=s):
            pl.when(t >= 2)(functools.partial(drain, s))
            loc[s][...] = _dot(pick, h_ref[...].astype(BF16))
            _run_copies(t, cnt_ref, start_ref, row_ref,
                        lambda a, b, n: pltpu.make_async_copy(loc[s].at[pl.ds(a, n)], xs_hbm.at[pl.ds(b, n)],
                                                              sem.at[s]))

    @pl.when(t == N_T - 1)
    def _():
        if N_T >= 2:
            drain(N_T % 2)
        drain((N_T - 1) % 2)


def _dispatch(plan, h2, route, n_rows, bm):
    n, d = h2.shape
    pairs = LOC_ROWS
    return pl.pallas_call(
        functools.partial(_dispatch_kernel, BM=bm, N_TAIL=n_rows // bm - (n * TOP_K) // bm, N_T=n // MOE_TILE),
        out_shape=jax.ShapeDtypeStruct((n_rows, d), F32),
        grid_spec=pltpu.PrefetchScalarGridSpec(
            num_scalar_prefetch=5,
            grid=(n // MOE_TILE,),
            in_specs=[pl.BlockSpec((MOE_TILE, d), lambda t, *_: (t, 0)),
                      pl.BlockSpec((MOE_TILE, LANE), lambda t, *_: (t, 0))],
            out_specs=pl.BlockSpec(memory_space=pl.ANY),
            scratch_shapes=[pltpu.VMEM((pairs, d), F32),
                            pltpu.VMEM((pairs, d), F32),
                            pltpu.VMEM((bm, d), F32),
                            pltpu.SemaphoreType.DMA(()),
                            pltpu.SemaphoreType.DMA((2,))]),
        compiler_params=pltpu.CompilerParams(dimension_semantics=("arbitrary",), vmem_limit_bytes=VMEM_LIMIT,
                                             disable_bounds_checks=True),
        name="moe_dispatch",
    )(plan["cnt"], plan["start"], plan["row"], plan["pad_end"], plan["cnt_end"], h2, route)


def _expert_kernel(be_ref, nu_ref, nxt_ref, slot_ref, xs_ref, w1_hbm, b1_ref, w2_hbm, b2_ref, o_ref,
                   w1f, w2f, w1b, w2b, sem, *, layer):
    i = pl.program_id(0)
    de = w2b.shape[0]
    e = be_ref[i]
    prev = be_ref[jnp.maximum(i - 1, 0)]
    fresh = (i == 0) | (e != prev)

    def copies(expert, s):
        return (pltpu.make_async_copy(w1_hbm.at[layer, expert], w1f.at[s], sem.at[s, 0]),
                pltpu.make_async_copy(w2_hbm.at[layer, expert], w2f.at[s], sem.at[s, 1]))

    @pl.when(i == 0)
    def _():
        for cp in copies(e, slot_ref[e]):
            cp.start()

    @pl.when(fresh & (i < nu_ref[0]))
    def _():
        s = slot_ref[e]
        for cp in copies(e, s):
            cp.wait()
        w1b[...] = w1f[s].astype(BF16)
        w2b[...] = w2f[s].astype(BF16)
        nxt = nxt_ref[e]

        @pl.when(nxt >= 0)
        def _():
            for cp in copies(nxt, 1 - s):
                cp.start()

    @pl.when(i < nu_ref[0])
    def _():
        hdn = _dot(xs_ref[...].astype(BF16), w1b[...]) + b1_ref[...]
        glu = jnp.minimum(hdn[:, :de], SWIGLU_LIMIT)
        lin = jnp.clip(hdn[:, de:], -SWIGLU_LIMIT, SWIGLU_LIMIT)
        act = glu * _sigmoid(SWIGLU_ALPHA * glu) * (lin + 1.0)
        o_ref[...] = _dot(act.astype(BF16), w2b[...]) + b2_ref[...]

    @pl.when(i >= nu_ref[0])
    def _():
        o_ref[...] = jnp.zeros(o_ref.shape, F32)


def _experts(sched, xs, w1, b1, w2, b2, layer, bm):
    n_rows, d = xs.shape
    depth, ne, _, dh2 = w1.shape
    de = w2.shape[2]
    n_blocks = n_rows // bm

    def row_map(i, be, nu, *_):
        return (jnp.minimum(i, nu[0] - 1), 0)

    def b_map(i, be, *_):
        return (layer, be[i], 0, 0)

    return pl.pallas_call(
        functools.partial(_expert_kernel, layer=layer),
        out_shape=jax.ShapeDtypeStruct((n_rows, d), F32),
        grid_spec=pltpu.PrefetchScalarGridSpec(
            num_scalar_prefetch=4,
            grid=(n_blocks,),
            in_specs=[pl.BlockSpec((bm, d), row_map),
                      pl.BlockSpec(memory_space=pl.ANY),
                      pl.BlockSpec((None, None, 1, dh2), b_map),
                      pl.BlockSpec(memory_space=pl.ANY),
                      pl.BlockSpec((None, None, 1, d), b_map)],
            out_specs=pl.BlockSpec((bm, d), lambda i, *_: (i, 0)),
            scratch_shapes=[pltpu.VMEM((2, d, dh2), F32), pltpu.VMEM((2, de, d), F32),
                            pltpu.VMEM((d, dh2), BF16), pltpu.VMEM((de, d), BF16),
                            pltpu.SemaphoreType.DMA((2, 2))]),
        compiler_params=_params(("arbitrary",)),
        name="moe_experts",
    )(sched["blk_expert"], sched["n_used"], sched["next_expert"], sched["slot"], xs,
      w1, b1.reshape(depth, ne, 1, dh2), w2, b2.reshape(depth, ne, 1, d))


def _combine_kernel(cnt_ref, start_ref, row_ref, yb_hbm, x_ref, route_ref, mod_ref, fg_ref, o_ref,
                    loc0, loc1, sem, *, nt, n_tiles, final):
    t = pl.program_id(0) * nt + pl.program_id(1)
    pairs = LOC_ROWS
    loc = (loc0, loc1)

    def fetch(tile, s):
        _run_copies(tile, cnt_ref, start_ref, row_ref,
                    lambda a, b, n: pltpu.make_async_copy(yb_hbm.at[pl.ds(b, n)], loc[s].at[pl.ds(a, n)],
                                                          sem.at[s]))

    pl.when(t == 0)(functools.partial(fetch, 0, 0))
    route = route_ref[0]
    slot = lax.broadcasted_iota(jnp.int32, (1, pairs), 1).astype(F32)
    wgt = jnp.zeros((MOE_TILE, pairs), F32)
    for k in range(TOP_K):
        wgt = wgt + jnp.where(slot == route[:, TOP_K + k:TOP_K + k + 1],
                              route[:, 2 * TOP_K + k:2 * TOP_K + k + 1], 0.0)
    hi = wgt.astype(BF16)
    lo = (wgt - hi.astype(F32)).astype(BF16)

    for s in range(2):
        @pl.when(t % 2 == s)
        def _(s=s):
            pl.when(t + 1 < n_tiles)(functools.partial(fetch, t + 1, 1 - s))
            pltpu.make_async_copy(yb_hbm.at[pl.ds(0, pairs)], loc[s], sem.at[s]).wait()
            rows = loc[s][...].astype(BF16)
            y = _dot(hi, rows) + _dot(lo, rows)
            out = x_ref[0] + mod_ref[0, 5:6, :] * y
            if final:
                out = _rms(out) * fg_ref[...]
            o_ref[0] = out


def _combine(plan, yb, x, route, mod, final_g, final):
    bsz, s, d = x.shape
    nt = s // MOE_TILE
    pairs = LOC_ROWS
    return pl.pallas_call(
        functools.partial(_combine_kernel, nt=nt, n_tiles=bsz * nt, final=final),
        out_shape=jax.ShapeDtypeStruct((bsz, s, d), F32),
        grid_spec=pltpu.PrefetchScalarGridSpec(
            num_scalar_prefetch=3,
            grid=(bsz, nt),
            in_specs=[pl.BlockSpec(memory_space=pl.ANY),
                      pl.BlockSpec((1, MOE_TILE, d), lambda b, i, *_: (b, i, 0)),
                      pl.BlockSpec((1, MOE_TILE, LANE), lambda b, i, *_: (b, i, 0)),
                      pl.BlockSpec((1, 6, d), lambda b, i, *_: (b, 0, 0)),
                      pl.BlockSpec((1, d), lambda b, i, *_: (0, 0))],
            out_specs=pl.BlockSpec((1, MOE_TILE, d), lambda b, i, *_: (b, i, 0)),
            scratch_shapes=[pltpu.VMEM((pairs, d), F32), pltpu.VMEM((pairs, d), F32),
                            pltpu.SemaphoreType.DMA((2,))]),
        compiler_params=pltpu.CompilerParams(dimension_semantics=("arbitrary", "arbitrary"),
                                             vmem_limit_bytes=VMEM_LIMIT, disable_bounds_checks=True),
        name="moe_combine",
    )(plan["cnt"], plan["start"], plan["row"], yb, x, route, mod, final_g.reshape(1, d))


def _rearranged_in_proj(w_in, b_in):
    sizes = (ML_HEADS * ML_QK, ML_HEADS * ML_QK, ML_HEADS * ML_V, ML_HEADS * ML_V, ML_HEADS, ML_HEADS,
             DSA_HEADS * DSA_DIM, DSA_LATENT, IDX_HEADS * IDX_DIM, IDX_DIM, IDX_HEADS,
             3 * FOX_HEADS * FOX_DIM, FOX_HEADS, N_BRANCH * w_in.shape[0])
    offs = [0]
    for sz in sizes:
        offs.append(offs[-1] + sz)
    (o_mq, o_mk, o_mv, o_mo, o_mi, o_mf, o_dq, o_ckv, o_iq, o_ik, o_iw, o_fx, o_ff, o_g, o_end) = offs
    pad = LANE - (IDX_DIM + 2 * ML_HEADS + IDX_HEADS + FOX_HEADS)

    def cols(a):
        parts = [a[..., o_mq:o_mi],
                 a[..., o_dq:o_ckv],
                 a[..., o_iq:o_ik],
                 a[..., o_ckv:o_iq],
                 a[..., o_ik:o_iw],
                 a[..., o_mi:o_dq],
                 a[..., o_iw:o_fx],
                 a[..., o_ff:o_g],
                 jnp.zeros(a.shape[:-1] + (pad,), a.dtype),
                 a[..., o_fx:o_ff],
                 a[..., o_g:o_end]]
        return jnp.concatenate(parts, axis=-1)

    return cols(w_in).astype(BF16), cols(b_in.reshape(1, -1))


def _moe_plan(stats, bm, n_blocks):
    cnt = stats[:, :N_EXPERTS, 0].astype(jnp.int32)
    start = stats[:, N_EXPERTS:, 0].astype(jnp.int32)
    total = jnp.sum(cnt, axis=0)
    padded = (total + bm - 1) // bm * bm
    pad_end = jnp.cumsum(padded)
    pad_start = pad_end - padded
    row = pad_start[None, :] + jnp.cumsum(cnt, axis=0) - cnt
    used = jnp.sum(cnt, axis=1, keepdims=True)
    cnt = jnp.concatenate([cnt, LOC_ROWS - used], axis=1)
    start = jnp.concatenate([start, used], axis=1)
    parity = jnp.arange(cnt.shape[0], dtype=jnp.int32)[:, None] % 2
    row = jnp.concatenate([row, n_blocks * bm + parity * MOE_TILE], axis=1)
    blk_row = jnp.arange(n_blocks + 1, dtype=jnp.int32) * bm
    blk_expert = jnp.minimum(jnp.sum((pad_end[None, :] <= blk_row[:, None]).astype(jnp.int32), axis=1),
                             N_EXPERTS - 1)
    n_used = (pad_end[-1:] // bm).astype(jnp.int32)
    present = padded > 0
    ids = jnp.arange(N_EXPERTS, dtype=jnp.int32)
    later = jnp.where(present[None, :] & (ids[None, :] > ids[:, None]), ids[None, :], N_EXPERTS)
    nxt = jnp.min(later, axis=1)
    sched = dict(blk_expert=blk_expert, n_used=n_used,
                 next_expert=jnp.where(nxt < N_EXPERTS, nxt, -1).astype(jnp.int32),
                 slot=((jnp.cumsum(present) - present) % 2).astype(jnp.int32))
    plan = dict(cnt=cnt.reshape(-1), start=start.reshape(-1), row=row.reshape(-1).astype(jnp.int32),
                pad_end=pad_end.astype(jnp.int32), cnt_end=(pad_start + total).astype(jnp.int32))
    return plan, sched


def kernel(x, c, w_ada, b_ada, w_in, b_in, conv_w, conv_b, ml_norm_g, kv_norm_g, w_uk, w_uv,
           w_br_ml, w_br_dsa, w_br_fox, w_out, w_router, b_router, w1, b1, w2, b2, final_g):
    bsz, s, d = x.shape
    depth = w_in.shape[0]
    n = bsz * s
    bm = 2 * MOE_TILE
    n_blocks = -(-(n * TOP_K + (ROW_ALIGN - 1) * N_EXPERTS * (n // MOE_TILE)) // bm) + N_EXPERTS
    mods = _ada_mod(c, w_ada, b_ada).reshape(depth, bsz, 6, d)
    for l in range(depth):
        mod = mods[l]
        w_r, b_r = _rearranged_in_proj(w_in[l], b_in[l])
        proj, misc = _in_proj(x, mod, w_r, b_r)
        y_ml, fcol, frow = _mlstm(proj, misc, conv_w[l], conv_b[l], ml_norm_g[l])
        y_fox = _fox(proj, fcol, frow)
        y_dsa = _dsa(proj, misc, kv_norm_g[l], w_uk[l].astype(BF16), jnp.swapaxes(w_uv[l], 1, 2).astype(BF16))
        x = _merge(x, proj, y_ml, y_dsa, y_fox, mod, w_br_ml[l].astype(BF16), w_br_dsa[l].astype(BF16),
                   w_br_fox[l].astype(BF16), w_out[l].astype(BF16))
        h2, route, stats = _router(x, mod, w_router[l], b_router[l])
        plan, sched = _moe_plan(stats, bm, n_blocks)
        xs = _dispatch(plan, h2.reshape(n, d), route.reshape(n, LANE), (n_blocks + 1) * bm, bm)
        yb = _experts(sched, xs, w1, b1, w2, b2, l, bm)
        x = _combine(plan, yb, x, route, mod, final_g, final=(l == depth - 1))
    return x
```

```python
import functools

import jax
import jax.numpy as jnp
from jax import lax
from jax.experimental import pallas as pl
from jax.experimental.pallas import tpu as pltpu

F32 = jnp.float32
BF16 = jnp.bfloat16
HIGHEST = lax.Precision.HIGHEST

EPS = 1e-6
LOG2E = 1.4426950408889634
CHUNK = 64

ML_HEADS, ML_QK, ML_V, ML_CONV = 4, 64, 128, 4
DSA_HEADS, DSA_DIM, DSA_LATENT = 4, 128, 128
IDX_HEADS, IDX_DIM, DSA_TOPK = 4, 64, 256
FOX_HEADS, FOX_DIM = 4, 128
N_BRANCH = 3
N_EXPERTS, TOP_K = 32, 4
SWIGLU_LIMIT, SWIGLU_ALPHA = 7.0, 1.702

LANE = 128
ROW_ALIGN = 8
INT_MIN = -2 ** 31

MOE_TILE = 256
N_RUNS = N_EXPERTS + 1
LOC_ROWS = TOP_K * MOE_TILE + MOE_TILE
ZERO_ROWS = 128
COUNT_ROWS = 64

C_MLQK = 0
C_MLV = 512
C_MLO = 1024
C_DQ = 1536
C_DIQ = 2048
C_CKV = 2304
C_MISC = 2432
C_FOX = 2560
C_GATE = 4096
M_IK, M_MLI, M_MLF, M_IW, M_FXF = 0, 64, 68, 72, 76

VMEM_LIMIT = 56 * 1024 * 1024


def _dot(a, b, prec=None):
    return jnp.dot(a, b, preferred_element_type=F32, precision=prec)


def _dot_nt(a, b, prec=None):
    return lax.dot_general(a, b, (((1,), (1,)), ((), ())), preferred_element_type=F32, precision=prec)


def _dot_tn(a, b):
    return lax.dot_general(a, b, (((0,), (0,)), ((), ())), preferred_element_type=F32)


def _sigmoid(x):
    return 1.0 / (1.0 + jnp.exp(-x))


def _log_sigmoid(x):
    return jnp.minimum(x, 0.0) - jnp.log1p(jnp.exp(-jnp.abs(x)))


def _rms(x):
    return x * lax.rsqrt(jnp.mean(x * x, axis=-1, keepdims=True) + EPS)


def _col_reduce(op, x):
    rows = x.shape[0]
    if rows > COUNT_ROWS and rows % COUNT_ROWS == 0:
        x = op(x.reshape(rows // COUNT_ROWS, COUNT_ROWS, x.shape[1]), axis=0)
    return op(x, axis=0, keepdims=True)


def _params(sem, vmem=VMEM_LIMIT):
    return pltpu.CompilerParams(dimension_semantics=sem, vmem_limit_bytes=vmem)


def _ada_kernel(c_ref, w_ref, b_ref, o_ref):
    c = c_ref[...]
    o_ref[0] = _dot(c * _sigmoid(c), w_ref[0], HIGHEST) + b_ref[0]


def _ada_mod(c, w_ada, b_ada):
    depth, d, n = w_ada.shape
    bsz = c.shape[0]
    tn = 1536
    return pl.pallas_call(
        _ada_kernel,
        out_shape=jax.ShapeDtypeStruct((depth, bsz, n), F32),
        grid=(depth, n // tn),
        in_specs=[pl.BlockSpec((bsz, d), lambda l, j: (0, 0)),
                  pl.BlockSpec((1, d, tn), lambda l, j: (l, 0, j)),
                  pl.BlockSpec((1, 1, tn), lambda l, j: (l, 0, j))],
        out_specs=pl.BlockSpec((1, bsz, tn), lambda l, j: (l, 0, j)),
        compiler_params=_params(("parallel", "parallel")),
        name="ada_mod",
    )(c, w_ada, b_ada.reshape(depth, 1, n))


def _inproj_kernel(x_ref, mod_ref, w_ref, b_ref, o_ref, misc_ref, h_scr, *, tn):
    j = pl.program_id(2)

    @pl.when(j == 0)
    def _():
        h = _rms(x_ref[0]) * (1.0 + mod_ref[0, 1:2, :]) + mod_ref[0, 0:1, :]
        h_scr[...] = h.astype(BF16)

    acc = _dot(h_scr[...], w_ref[...]) + b_ref[...]
    o_ref[0] = acc.astype(BF16)

    @pl.when(j == C_MISC // tn)
    def _():
        misc_ref[0] = acc[:, C_MISC % tn:C_MISC % tn + LANE]


def _in_proj(x, mod, w, b):
    bsz, s, d = x.shape
    n = w.shape[1]
    tm = min(2048, s)
    tn = 1024
    return pl.pallas_call(
        functools.partial(_inproj_kernel, tn=tn),
        out_shape=(jax.ShapeDtypeStruct((bsz, s, n), BF16), jax.ShapeDtypeStruct((bsz, s, LANE), F32)),
        grid=(bsz, s // tm, n // tn),
        in_specs=[pl.BlockSpec((1, tm, d), lambda bi, i, j: (bi, i, 0)),
                  pl.BlockSpec((1, 6, d), lambda bi, i, j: (bi, 0, 0)),
                  pl.BlockSpec((d, tn), lambda bi, i, j: (0, j)),
                  pl.BlockSpec((1, tn), lambda bi, i, j: (0, j))],
        out_specs=(pl.BlockSpec((1, tm, tn), lambda bi, i, j: (bi, i, j)),
                   pl.BlockSpec((1, tm, LANE), lambda bi, i, j: (bi, i, 0))),
        scratch_shapes=[pltpu.VMEM((tm, d), BF16)],
        compiler_params=_params(("parallel", "parallel", "arbitrary")),
        name="in_proj",
    )(x, mod, w, b)


def _mlstm_kernel(qk_ref, v_ref, o_ref, misc_ref, cw_ref, cb_ref, g_ref,
                  y_ref, fcol_ref, frow_ref,
                  xext, ct_scr, n_scr, m_scr, carry_scr, *, L):
    c = pl.program_id(1)
    nqk = ML_HEADS * ML_QK

    @pl.when(c == 0)
    def _():
        xext[0:8, :] = jnp.zeros((8, 2 * nqk), F32)
        ct_scr[...] = jnp.zeros(ct_scr.shape, F32)
        n_scr[...] = jnp.zeros(n_scr.shape, F32)
        m_scr[...] = jnp.full(m_scr.shape, -jnp.inf, F32)
        carry_scr[...] = jnp.zeros(carry_scr.shape, F32)

    @pl.when(c > 0)
    def _():
        xext[0:8, :] = xext[L:L + 8, :]

    xext[8:8 + L, :] = qk_ref[0].astype(F32)
    cw = cw_ref[...]
    conv = (cb_ref[...] + cw[3:4, :] * xext[8:8 + L, :] + cw[2:3, :] * xext[7:7 + L, :]
            + cw[1:2, :] * xext[6:6 + L, :] + cw[0:1, :] * xext[5:5 + L, :])
    qk = conv * _sigmoid(conv)

    misc = misc_ref[0]
    ls = _log_sigmoid(misc)
    row = lax.broadcasted_iota(jnp.int32, (L, L), 0)
    col = lax.broadcasted_iota(jnp.int32, (L, L), 1)
    causal = row >= col
    tri = jnp.where(causal, 1.0, 0.0).astype(F32)
    cs = _dot(tri, ls, HIGHEST)
    cs_t = cs.T
    misc_t = misc.T
    carry = carry_scr[0:1, :]
    fcol_ref[0] = cs + carry
    for h in range(FOX_HEADS):
        frow_ref[0, h:h + 1, :] = cs_t[M_FXF + h:M_FXF + h + 1, :] + carry[:, M_FXF + h:M_FXF + h + 1]
    frow_ref[0, 4:8, :] = jnp.zeros((4, L), F32)
    carry_scr[0:1, :] = carry + cs[L - 1:L, :]

    for h in range(ML_HEADS):
        qh = qk[:, h * ML_QK:(h + 1) * ML_QK] * (ML_QK ** -0.5)
        kh = qk[:, nqk + h * ML_QK:nqk + (h + 1) * ML_QK]
        vb = v_ref[0, :, h * ML_V:(h + 1) * ML_V].astype(BF16)
        i_col = misc[:, M_MLI + h:M_MLI + h + 1]
        i_row = misc_t[M_MLI + h:M_MLI + h + 1, :]
        b_col = cs[:, M_MLF + h:M_MLF + h + 1]
        b_row = cs_t[M_MLF + h:M_MLF + h + 1, :]
        b_last = b_col[L - 1:L, :]
        m_prev = m_scr[h:h + 1, 0:1]

        d_log = jnp.where(causal, b_col - b_row + i_row, -jnp.inf)
        inter_log = b_col + m_prev
        m_out = jnp.maximum(inter_log, jnp.max(d_log, axis=-1, keepdims=True))
        qb = qh.astype(BF16)
        kb = kh.astype(BF16)
        s = _dot_nt(qb, kb) * jnp.exp(d_log - m_out)
        a_inter = jnp.exp(inter_log - m_out)
        ct = ct_scr[h]
        n_row = n_scr[h]
        num = _dot(s.astype(BF16), vb) + a_inter * _dot(qb, ct.astype(BF16))
        den = jnp.sum(s, axis=-1, keepdims=True) + a_inter * jnp.sum(qh * n_row, axis=-1, keepdims=True)
        hid = num / jnp.maximum(jnp.abs(den), jnp.exp(-m_out))

        w_state = b_last - b_col + i_col
        m_loc = jnp.max(w_state, axis=0, keepdims=True)
        ke = kh * jnp.exp(w_state - m_loc)
        c_loc = _dot_tn(ke.astype(BF16), vb)
        n_loc = jnp.sum(ke, axis=0, keepdims=True)
        m_new = jnp.maximum(b_last + m_prev, m_loc)
        decay = jnp.exp(b_last + m_prev - m_new)
        scale = jnp.exp(m_loc - m_new)
        ct_scr[h] = decay * ct + scale * c_loc
        n_scr[h] = decay * n_row + scale * n_loc
        m_scr[h:h + 1, :] = jnp.broadcast_to(m_new, (1, LANE))

        y = (_rms(hid) * g_ref[:, h * ML_V:(h + 1) * ML_V]
             * _sigmoid(o_ref[0, :, h * ML_V:(h + 1) * ML_V].astype(F32)))
        y_ref[0, :, h * ML_V:(h + 1) * ML_V] = y.astype(BF16)


def _mlstm(proj, misc, conv_w, conv_b, norm_g):
    bsz, s, _ = proj.shape
    L = min(256, s)
    w = 2 * ML_HEADS * ML_QK
    wv = ML_HEADS * ML_V
    return pl.pallas_call(
        functools.partial(_mlstm_kernel, L=L),
        out_shape=(jax.ShapeDtypeStruct((bsz, s, wv), BF16),
                   jax.ShapeDtypeStruct((bsz, s, LANE), F32),
                   jax.ShapeDtypeStruct((bsz, 8, s), F32)),
        grid=(bsz, s // L),
        in_specs=[pl.BlockSpec((1, L, w), lambda b, c: (b, c, C_MLQK // w)),
                  pl.BlockSpec((1, L, wv), lambda b, c: (b, c, C_MLV // wv)),
                  pl.BlockSpec((1, L, wv), lambda b, c: (b, c, C_MLO // wv)),
                  pl.BlockSpec((1, L, LANE), lambda b, c: (b, c, 0)),
                  pl.BlockSpec((ML_CONV, w), lambda b, c: (0, 0)),
                  pl.BlockSpec((1, w), lambda b, c: (0, 0)),
                  pl.BlockSpec((1, wv), lambda b, c: (0, 0))],
        out_specs=(pl.BlockSpec((1, L, wv), lambda b, c: (b, c, 0)),
                   pl.BlockSpec((1, L, LANE), lambda b, c: (b, c, 0)),
                   pl.BlockSpec((1, 8, L), lambda b, c: (b, 0, c))),
        scratch_shapes=[pltpu.VMEM((L + 8, w), F32),
                        pltpu.VMEM((ML_HEADS, ML_QK, ML_V), F32),
                        pltpu.VMEM((ML_HEADS, 1, ML_QK), F32),
                        pltpu.VMEM((8, LANE), F32),
                        pltpu.VMEM((8, LANE), F32)],
        compiler_params=_params(("parallel", "arbitrary")),
        name="mlstm",
    )(proj, proj, proj, misc, conv_w, conv_b.reshape(1, w), norm_g.reshape(1, wv))


def _fox_kernel(q_ref, k_ref, v_ref, fcol_ref, frow_ref, y_ref, k_scr, vt_scr, *, T, S):
    qi = pl.program_id(1)
    d = FOX_DIM

    @pl.when(qi == 0)
    def _():
        for h in range(FOX_HEADS):
            k_scr[h] = k_ref[0, :, h * d:(h + 1) * d].astype(BF16)
            vt_scr[h] = v_ref[0, :, h * d:(h + 1) * d].astype(F32).T.astype(BF16)

    q_t = (q_ref[0].astype(F32) * (d ** -0.5 * LOG2E)).T

    def body(ext):
        kpos = lax.broadcasted_iota(jnp.int32, (ext, 1), 0)
        qpos = (ext - T) + lax.broadcasted_iota(jnp.int32, (1, T), 1)
        causal = kpos <= qpos
        for h in range(FOX_HEADS):
            s = _dot(k_scr[h, 0:ext, :], q_t[h * d:(h + 1) * d, :].astype(BF16))
            s = s + (frow_ref[0, h:h + 1, :] * LOG2E - fcol_ref[0, 0:ext, M_FXF + h:M_FXF + h + 1] * LOG2E)
            s = jnp.where(causal, s, -jnp.inf)
            p = jnp.exp2(s - _col_reduce(jnp.max, s))
            l = _col_reduce(jnp.sum, p)
            out_t = _dot(vt_scr[h, :, 0:ext], p.astype(BF16)) / l
            y_ref[0, :, h * d:(h + 1) * d] = out_t.T.astype(BF16)

    for c in range(S // T):
        pl.when(qi == c)(functools.partial(body, (c + 1) * T))


def _fox(proj, fcol, frow):
    bsz, s, _ = proj.shape
    T = min(256, s)
    w = FOX_HEADS * FOX_DIM
    return pl.pallas_call(
        functools.partial(_fox_kernel, T=T, S=s),
        out_shape=jax.ShapeDtypeStruct((bsz, s, w), BF16),
        grid=(bsz, s // T),
        in_specs=[pl.BlockSpec((1, T, w), lambda b, i: (b, i, C_FOX // w)),
                  pl.BlockSpec((1, s, w), lambda b, i: (b, 0, C_FOX // w + 1)),
                  pl.BlockSpec((1, s, w), lambda b, i: (b, 0, C_FOX // w + 2)),
                  pl.BlockSpec((1, s, LANE), lambda b, i: (b, 0, 0)),
                  pl.BlockSpec((1, 8, T), lambda b, i: (b, 0, i))],
        out_specs=pl.BlockSpec((1, T, w), lambda b, i: (b, i, 0)),
        scratch_shapes=[pltpu.VMEM((FOX_HEADS, s, FOX_DIM), BF16),
                        pltpu.VMEM((FOX_HEADS, FOX_DIM, s), BF16)],
        compiler_params=_params(("parallel", "arbitrary")),
        name="fox_attention",
    )(proj, proj, proj, fcol, frow)


def _dsa_body(ext, qi, q_ref, qidx_ref, misc_q_ref, wuk_ref, wuv_ref, y_ref,
              ckvn_scr, ckvnt_scr, kidx_scr, sel_scr, *, T, n_sel, rank_tile):
    dh = DSA_DIM
    q_t = q_ref[0].astype(F32).T
    qidx_t = qidx_ref[0].astype(F32).T.astype(BF16)
    w_t = misc_q_ref[0].T[M_IW:M_IW + IDX_HEADS, :] * (IDX_HEADS ** -0.5)
    kidx = kidx_scr[0:ext, :]
    score = jnp.zeros((ext, T), F32)
    for h in range(IDX_HEADS):
        lg = _dot(kidx, qidx_t[h * IDX_DIM:(h + 1) * IDX_DIM, :]) * (IDX_DIM ** -0.5)
        score = score + w_t[h:h + 1, :] * jnp.maximum(lg, 0.0)
    kpos = lax.broadcasted_iota(jnp.int32, (ext, 1), 0)
    qchunk = (qi * T + lax.broadcasted_iota(jnp.int32, (1, T), 1)) // CHUNK
    score = jnp.where((kpos // CHUNK) <= qchunk, score, -jnp.inf)

    def as_float(c):
        return pltpu.bitcast(jnp.where(c < 0, c ^ jnp.int32(0x7FFFFFFF), c), F32)

    def count_ge(c):
        return _col_reduce(jnp.sum, jnp.where(score >= as_float(c), 1.0, 0.0))

    t0 = jnp.where(count_ge(jnp.zeros((1, T), jnp.int32)) >= n_sel, jnp.int32(0), jnp.int32(INT_MIN))

    def bis(i, t):
        cand = t + jnp.left_shift(jnp.int32(1), jnp.int32(30) - i)
        return jnp.where(count_ge(cand) >= n_sel, cand, t)

    t = lax.fori_loop(0, 31, bis, t0)
    n_vis = ((qchunk + 1) * CHUNK).astype(F32)
    thr = jnp.where(n_vis <= n_sel, -3.0e38, as_float(t))
    need = n_sel - _col_reduce(jnp.sum, jnp.where(score > thr, 1.0, 0.0))
    r_i = lax.broadcasted_iota(jnp.int32, (rank_tile, rank_tile), 0)
    c_i = lax.broadcasted_iota(jnp.int32, (rank_tile, rank_tile), 1)
    lower = jnp.where(c_i < r_i, 1.0, 0.0).astype(BF16)
    carry = jnp.zeros((1, T), F32)
    for j in range(ext // rank_tile):
        sc = score[j * rank_tile:(j + 1) * rank_tile, :]
        eq = jnp.where(sc == thr, 1.0, 0.0)
        rank = _dot(lower, eq.astype(BF16)) + carry
        carry = carry + _col_reduce(jnp.sum, eq)
        sel_scr[j * rank_tile:(j + 1) * rank_tile, :] = jnp.where(
            sc > thr, 1.0, jnp.where(rank < need, eq, 0.0))

    sel = sel_scr[0:ext, :] > 0.5
    ckvn = ckvn_scr[0:ext, :]
    ckvn_t = ckvnt_scr[:, 0:ext]
    for h in range(DSA_HEADS):
        qa_t = _dot(wuk_ref[h], q_t[h * dh:(h + 1) * dh, :].astype(BF16)) * (dh ** -0.5 * LOG2E)
        lg = _dot(ckvn, qa_t.astype(BF16))
        lg = jnp.where(sel, lg, -jnp.inf)
        p = jnp.exp2(lg - _col_reduce(jnp.max, lg))
        l = _col_reduce(jnp.sum, p)
        lat_t = _dot(ckvn_t, p.astype(BF16)) / l
        out_t = _dot(wuv_ref[h], lat_t.astype(BF16))
        y_ref[0, :, h * dh:(h + 1) * dh] = out_t.T.astype(BF16)


def _dsa_kernel(q_ref, ckv_ref, qidx_ref, misc_all_ref, misc_q_ref, g_ref, wuk_ref, wuv_ref, y_ref,
                ckvn_scr, ckvnt_scr, kidx_scr, sel_scr, *, T, S, n_sel, n_cls, rank_tile):
    qi = pl.program_id(1)

    @pl.when(qi == 0)
    def _():
        ckvn = _rms(ckv_ref[0].astype(F32)) * g_ref[...]
        ckvn_scr[...] = ckvn.astype(BF16)
        ckvnt_scr[...] = ckvn.T.astype(BF16)
        kidx_scr[...] = misc_all_ref[0, :, M_IK:M_IK + IDX_DIM].astype(BF16)

    per = (S // T) // n_cls
    for c in range(n_cls):
        ext = (c + 1) * per * T

        @pl.when(qi // per == c)
        def _(ext=ext):
            _dsa_body(ext, qi, q_ref, qidx_ref, misc_q_ref, wuk_ref, wuv_ref, y_ref,
                      ckvn_scr, ckvnt_scr, kidx_scr, sel_scr, T=T, n_sel=n_sel, rank_tile=rank_tile)


def _dsa(proj, misc, kv_g, wuk, wuv_t):
    bsz, s, _ = proj.shape
    T = min(256, s)
    n_sel = min(DSA_TOPK, s // 4)
    n_cls = max(1, min(4, s // 512))
    rank_tile = 256
    w = DSA_HEADS * DSA_DIM
    wi = IDX_HEADS * IDX_DIM
    return pl.pallas_call(
        functools.partial(_dsa_kernel, T=T, S=s, n_sel=float(n_sel), n_cls=n_cls, rank_tile=rank_tile),
        out_shape=jax.ShapeDtypeStruct((bsz, s, w), BF16),
        grid=(bsz, s // T),
        in_specs=[pl.BlockSpec((1, T, w), lambda b, i: (b, i, C_DQ // w)),
                  pl.BlockSpec((1, s, DSA_LATENT), lambda b, i: (b, 0, C_CKV // DSA_LATENT)),
                  pl.BlockSpec((1, T, wi), lambda b, i: (b, i, C_DIQ // wi)),
                  pl.BlockSpec((1, s, LANE), lambda b, i: (b, 0, 0)),
                  pl.BlockSpec((1, T, LANE), lambda b, i: (b, i, 0)),
                  pl.BlockSpec((1, DSA_LATENT), lambda b, i: (0, 0)),
                  pl.BlockSpec((DSA_HEADS, DSA_LATENT, DSA_DIM), lambda b, i: (0, 0, 0)),
                  pl.BlockSpec((DSA_HEADS, DSA_DIM, DSA_LATENT), lambda b, i: (0, 0, 0))],
        out_specs=pl.BlockSpec((1, T, w), lambda b, i: (b, i, 0)),
        scratch_shapes=[pltpu.VMEM((s, DSA_LATENT), BF16),
                        pltpu.VMEM((DSA_LATENT, s), BF16),
                        pltpu.VMEM((s, IDX_DIM), BF16),
                        pltpu.VMEM((s, T), F32)],
        compiler_params=_params(("parallel", "arbitrary")),
        name="dsa_attention",
    )(proj, proj, proj, misc, misc, kv_g.reshape(1, DSA_LATENT), wuk, wuv_t)


def _merge_kernel(yml_ref, ydsa_ref, yfox_ref, g0_ref, g1_ref, g2_ref, x_ref, mod_ref,
                  wml_ref, wdsa_ref, wfox_ref, wout_ref, o_ref):
    merged = (_sigmoid(g0_ref[0].astype(F32)) * _dot(yml_ref[0], wml_ref[...])
              + _sigmoid(g1_ref[0].astype(F32)) * _dot(ydsa_ref[0], wdsa_ref[...])
              + _sigmoid(g2_ref[0].astype(F32)) * _dot(yfox_ref[0], wfox_ref[...]))
    out = _dot(merged.astype(BF16), wout_ref[...])
    o_ref[0] = x_ref[0] + mod_ref[0, 2:3, :] * out


def _merge(x, proj, y_ml, y_dsa, y_fox, mod, w_ml, w_dsa, w_fox, w_out):
    bsz, s, d = x.shape
    tm = min(512, s)
    wb = y_ml.shape[-1]
    gb = C_GATE // d
    yspec = pl.BlockSpec((1, tm, wb), lambda b, i: (b, i, 0))
    wspec = pl.BlockSpec((wb, d), lambda b, i: (0, 0))
    return pl.pallas_call(
        _merge_kernel,
        out_shape=jax.ShapeDtypeStruct((bsz, s, d), F32),
        grid=(bsz, s // tm),
        in_specs=[yspec, yspec, yspec,
                  pl.BlockSpec((1, tm, d), lambda b, i: (b, i, gb)),
                  pl.BlockSpec((1, tm, d), lambda b, i: (b, i, gb + 1)),
                  pl.BlockSpec((1, tm, d), lambda b, i: (b, i, gb + 2)),
                  pl.BlockSpec((1, tm, d), lambda b, i: (b, i, 0)),
                  pl.BlockSpec((1, 6, d), lambda b, i: (b, 0, 0)),
                  wspec, wspec, wspec,
                  pl.BlockSpec((d, d), lambda b, i: (0, 0))],
        out_specs=pl.BlockSpec((1, tm, d), lambda b, i: (b, i, 0)),
        compiler_params=_params(("parallel", "parallel")),
        name="merge_out",
    )(y_ml, y_dsa, y_fox, proj, proj, proj, x, mod, w_ml, w_dsa, w_fox, w_out)


def _router_kernel(x_ref, mod_ref, wr_ref, br_ref, h_ref, route_ref, stat_ref, *, T):
    h = _rms(x_ref[0]) * (1.0 + mod_ref[0, 4:5, :]) + mod_ref[0, 3:4, :]
    h_ref[0] = h
    lg = _dot(h, wr_ref[...], HIGHEST) + br_ref[...]
    lg = lg.T[0:N_EXPERTS, :]
    sub = lax.broadcasted_iota(jnp.int32, (N_EXPERTS, 1), 0).astype(F32)
    vals, idxs, hots = [], [], []
    for _ in range(TOP_K):
        mx = jnp.max(lg, axis=0, keepdims=True)
        idx = jnp.min(jnp.where(lg == mx, sub, float(N_EXPERTS)), axis=0, keepdims=True)
        hot = sub == idx
        vals.append(mx)
        idxs.append(idx)
        hots.append(hot)
        lg = jnp.where(hot, -jnp.inf, lg)
    exps = [jnp.exp(v - vals[0]) for v in vals]
    tot = exps[0] + exps[1] + exps[2] + exps[3]
    multi = jnp.zeros((N_EXPERTS, T), F32)
    for hot in hots:
        multi = multi + jnp.where(hot, 1.0, 0.0)
    r_i = lax.broadcasted_iota(jnp.int32, (T, T), 0)
    c_i = lax.broadcasted_iota(jnp.int32, (T, T), 1)
    upper = jnp.where(r_i < c_i, 1.0, 0.0).astype(BF16)
    before = _dot(multi.astype(BF16), upper)
    cnt = jnp.broadcast_to(jnp.sum(multi, axis=1, keepdims=True), (N_EXPERTS, LANE))
    cnt = jnp.floor((cnt + (ROW_ALIGN - 1.0)) * (1.0 / ROW_ALIGN)) * ROW_ALIGN
    a_i = lax.broadcasted_iota(jnp.int32, (N_EXPERTS, N_EXPERTS), 0)
    b_i = lax.broadcasted_iota(jnp.int32, (N_EXPERTS, N_EXPERTS), 1)
    start = _dot(jnp.where(b_i < a_i, 1.0, 0.0).astype(F32), cnt, HIGHEST)
    stat_ref[0, 0:N_EXPERTS, :] = cnt
    stat_ref[0, N_EXPERTS:2 * N_EXPERTS, :] = start
    where_to = before + start[:, 0:1]
    rows = list(idxs)
    rows += [jnp.sum(jnp.where(hot, where_to, 0.0), axis=0, keepdims=True) for hot in hots]
    rows += [e / tot for e in exps]
    rows.append(jnp.zeros((LANE - 3 * TOP_K, T), F32))
    route_ref[0] = jnp.concatenate(rows, axis=0).T


def _router(x, mod, w_router, b_router):
    bsz, s, d = x.shape
    T = MOE_TILE
    nt = s // T
    ne = w_router.shape[1]
    wr = jnp.zeros((d, LANE), F32).at[:, :ne].set(w_router)
    br = jnp.full((1, LANE), -1e30, F32).at[0, :ne].set(b_router)
    return pl.pallas_call(
        functools.partial(_router_kernel, T=T),
        out_shape=(jax.ShapeDtypeStruct((bsz, s, d), F32),
                   jax.ShapeDtypeStruct((bsz, s, LANE), F32),
                   jax.ShapeDtypeStruct((bsz * nt, 2 * N_EXPERTS, LANE), F32)),
        grid=(bsz, nt),
        in_specs=[pl.BlockSpec((1, T, d), lambda b, i: (b, i, 0)),
                  pl.BlockSpec((1, 6, d), lambda b, i: (b, 0, 0)),
                  pl.BlockSpec((d, LANE), lambda b, i: (0, 0)),
                  pl.BlockSpec((1, LANE), lambda b, i: (0, 0))],
        out_specs=(pl.BlockSpec((1, T, d), lambda b, i: (b, i, 0)),
                   pl.BlockSpec((1, T, LANE), lambda b, i: (b, i, 0)),
                   pl.BlockSpec((1, 2 * N_EXPERTS, LANE), lambda b, i: (b * nt + i, 0, 0))),
        compiler_params=_params(("parallel", "parallel")),
        name="moe_router",
    )(x, mod, wr, br)


def _run_copies(tile, cnt_ref, start_ref, row_ref, make_copy):
    for r in range(N_RUNS):
        c = cnt_ref[tile * N_RUNS + r]
        src = start_ref[tile * N_RUNS + r]
        dst = row_ref[tile * N_RUNS + r]
        size = MOE_TILE
        while size >= ROW_ALIGN:
            done = c & (-2 * size)
            cp = make_copy(pl.multiple_of(src + done, ROW_ALIGN), pl.multiple_of(dst + done, ROW_ALIGN), size)
            pl.when((c & size) != 0)(cp.start)
            size //= 2


def _dispatch_kernel(cnt_ref, start_ref, row_ref, pad_end_ref, cnt_end_ref, h_ref, route_ref, xs_hbm,
                     loc0, loc1, zbuf, zsem, sem, *, BM, N_TAIL, N_T):
    t = pl.program_id(0)
    loc = (loc0, loc1)

    @pl.when(t == 0)
    def _():
        zbuf[...] = jnp.zeros(zbuf.shape, F32)
        n_rows = xs_hbm.shape[0]

        def pad_copy(e, g):
            start = pl.multiple_of(pad_end_ref[e] - g * ZERO_ROWS, ZERO_ROWS)
            return (start + ZERO_ROWS > cnt_end_ref[e],
                    pltpu.make_async_copy(zbuf.at[pl.ds(0, ZERO_ROWS)], xs_hbm.at[pl.ds(start, ZERO_ROWS)], zsem))

        def tail_copy(e):
            start = pl.multiple_of(pad_end_ref[N_EXPERTS - 1] + e * BM, BM)
            safe = pl.multiple_of(jnp.minimum(start, n_rows - BM), BM)
            return start < n_rows, pltpu.make_async_copy(zbuf, xs_hbm.at[pl.ds(safe, BM)], zsem)

        copies = [pad_copy(e, g) for e in range(N_EXPERTS) for g in range(1, BM // ZERO_ROWS + 1)]
        copies += [tail_copy(e) for e in range(N_TAIL)]
        for cond, cp in copies:
            pl.when(cond)(cp.start)
        for cond, cp in copies:
            pl.when(cond)(cp.wait)

    pos_t = route_ref[...].T[TOP_K:2 * TOP_K, :]
    slot = lax.broadcasted_iota(jnp.int32, (LOC_ROWS, 1), 0).astype(F32)
    pick = jnp.zeros((LOC_ROWS, MOE_TILE), F32)
    for k in range(TOP_K):
        pick = pick + jnp.where(slot == pos_t[k:k + 1, :], 1.0, 0.0)
    pick = pick.astype(BF16)

    def drain(s):
        pltpu.make_async_copy(loc[s], xs_hbm.at[pl.ds(0, LOC_ROWS)], sem.at[s]).wait()

    for s in range(2):
        @pl.when(t % 2 == s)
        def _(s=s):
            pl.when(t >= 2)(functools.partial(drain, s))
            loc[s][...] = _dot(pick, h_ref[...].astype(BF16))
            _run_copies(t, cnt_ref, start_ref, row_ref,
                        lambda a, b, n: pltpu.make_async_copy(loc[s].at[pl.ds(a, n)], xs_hbm.at[pl.ds(b, n)],
                                                              sem.at[s]))

    @pl.when(t == N_T - 1)
    def _():
        if N_T >= 2:
            drain(N_T % 2)
        drain((N_T - 1) % 2)


def _dispatch(plan, h2, route, n_rows, bm):
    n, d = h2.shape
    pairs = LOC_ROWS
    return pl.pallas_call(
        functools.partial(_dispatch_kernel, BM=bm, N_TAIL=n_rows // bm - (n * TOP_K) // bm, N_T=n // MOE_TILE),
        out_shape=jax.ShapeDtypeStruct((n_rows, d), F32),
        grid_spec=pltpu.PrefetchScalarGridSpec(
            num_scalar_prefetch=5,
            grid=(n // MOE_TILE,),
            in_specs=[pl.BlockSpec((MOE_TILE, d), lambda t, *_: (t, 0)),
                      pl.BlockSpec((MOE_TILE, LANE), lambda t, *_: (t, 0))],
            out_specs=pl.BlockSpec(memory_space=pl.ANY),
            scratch_shapes=[pltpu.VMEM((pairs, d), F32),
                            pltpu.VMEM((pairs, d), F32),
                            pltpu.VMEM((bm, d), F32),
                            pltpu.SemaphoreType.DMA(()),
                            pltpu.SemaphoreType.DMA((2,))]),
        compiler_params=pltpu.CompilerParams(dimension_semantics=("arbitrary",), vmem_limit_bytes=VMEM_LIMIT,
                                             disable_bounds_checks=True),
        name="moe_dispatch",
    )(plan["cnt"], plan["start"], plan["row"], plan["pad_end"], plan["cnt_end"], h2, route)


def _expert_kernel(be_ref, nu_ref, nxt_ref, slot_ref, xs_ref, w1_hbm, b1_ref, w2_hbm, b2_ref, o_ref,
                   w1f, w2f, w1b, w2b, sem, *, layer):
    i = pl.program_id(0)
    de = w2b.shape[0]
    e = be_ref[i]
    prev = be_ref[jnp.maximum(i - 1, 0)]
    fresh = (i == 0) | (e != prev)

    def copies(expert, s):
        return (pltpu.make_async_copy(w1_hbm.at[layer, expert], w1f.at[s], sem.at[s, 0]),
                pltpu.make_async_copy(w2_hbm.at[layer, expert], w2f.at[s], sem.at[s, 1]))

    @pl.when(i == 0)
    def _():
        for cp in copies(e, slot_ref[e]):
            cp.start()

    @pl.when(fresh & (i < nu_ref[0]))
    def _():
        s = slot_ref[e]
        for cp in copies(e, s):
            cp.wait()
        w1b[...] = w1f[s].astype(BF16)
        w2b[...] = w2f[s].astype(BF16)
        nxt = nxt_ref[e]

        @pl.when(nxt >= 0)
        def _():
            for cp in copies(nxt, 1 - s):
                cp.start()

    @pl.when(i < nu_ref[0])
    def _():
        hdn = _dot(xs_ref[...].astype(BF16), w1b[...]) + b1_ref[...]
        glu = jnp.minimum(hdn[:, :de], SWIGLU_LIMIT)
        lin = jnp.clip(hdn[:, de:], -SWIGLU_LIMIT, SWIGLU_LIMIT)
        act = glu * _sigmoid(SWIGLU_ALPHA * glu) * (lin + 1.0)
        o_ref[...] = _dot(act.astype(BF16), w2b[...]) + b2_ref[...]

    @pl.when(i >= nu_ref[0])
    def _():
        o_ref[...] = jnp.zeros(o_ref.shape, F32)


def _experts(sched, xs, w1, b1, w2, b2, layer, bm):
    n_rows, d = xs.shape
    depth, ne, _, dh2 = w1.shape
    de = w2.shape[2]
    n_blocks = n_rows // bm

    def row_map(i, be, nu, *_):
        return (jnp.minimum(i, nu[0] - 1), 0)

    def b_map(i, be, *_):
        return (layer, be[i], 0, 0)

    return pl.pallas_call(
        functools.partial(_expert_kernel, layer=layer),
        out_shape=jax.ShapeDtypeStruct((n_rows, d), F32),
        grid_spec=pltpu.PrefetchScalarGridSpec(
            num_scalar_prefetch=4,
            grid=(n_blocks,),
            in_specs=[pl.BlockSpec((bm, d), row_map),
                      pl.BlockSpec(memory_space=pl.ANY),
                      pl.BlockSpec((None, None, 1, dh2), b_map),
                      pl.BlockSpec(memory_space=pl.ANY),
                      pl.BlockSpec((None, None, 1, d), b_map)],
            out_specs=pl.BlockSpec((bm, d), lambda i, *_: (i, 0)),
            scratch_shapes=[pltpu.VMEM((2, d, dh2), F32), pltpu.VMEM((2, de, d), F32),
                            pltpu.VMEM((d, dh2), BF16), pltpu.VMEM((de, d), BF16),
                            pltpu.SemaphoreType.DMA((2, 2))]),
        compiler_params=_params(("arbitrary",)),
        name="moe_experts",
    )(sched["blk_expert"], sched["n_used"], sched["next_expert"], sched["slot"], xs,
      w1, b1.reshape(depth, ne, 1, dh2), w2, b2.reshape(depth, ne, 1, d))


def _combine_kernel(cnt_ref, start_ref, row_ref, yb_hbm, x_ref, route_ref, mod_ref, fg_ref, o_ref,
                    loc0, loc1, sem, *, nt, n_tiles, final):
    t = pl.program_id(0) * nt + pl.program_id(1)
    pairs = LOC_ROWS
    loc = (loc0, loc1)

    def fetch(tile, s):
        _run_copies(tile, cnt_ref, start_ref, row_ref,
                    lambda a, b, n: pltpu.make_async_copy(yb_hbm.at[pl.ds(b, n)], loc[s].at[pl.ds(a, n)],
                                                          sem.at[s]))

    pl.when(t == 0)(functools.partial(fetch, 0, 0))
    route = route_ref[0]
    slot = lax.broadcasted_iota(jnp.int32, (1, pairs), 1).astype(F32)
    wgt = jnp.zeros((MOE_TILE, pairs), F32)
    for k in range(TOP_K):
        wgt = wgt + jnp.where(slot == route[:, TOP_K + k:TOP_K + k + 1],
                              route[:, 2 * TOP_K + k:2 * TOP_K + k + 1], 0.0)
    hi = wgt.astype(BF16)
    lo = (wgt - hi.astype(F32)).astype(BF16)

    for s in range(2):
        @pl.when(t % 2 == s)
        def _(s=s):
            pl.when(t + 1 < n_tiles)(functools.partial(fetch, t + 1, 1 - s))
            pltpu.make_async_copy(yb_hbm.at[pl.ds(0, pairs)], loc[s], sem.at[s]).wait()
            rows = loc[s][...].astype(BF16)
            y = _dot(hi, rows) + _dot(lo, rows)
            out = x_ref[0] + mod_ref[0, 5:6, :] * y
            if final:
                out = _rms(out) * fg_ref[...]
            o_ref[0] = out


def _combine(plan, yb, x, route, mod, final_g, final):
    bsz, s, d = x.shape
    nt = s // MOE_TILE
    pairs = LOC_ROWS
    return pl.pallas_call(
        functools.partial(_combine_kernel, nt=nt, n_tiles=bsz * nt, final=final),
        out_shape=jax.ShapeDtypeStruct((bsz, s, d), F32),
        grid_spec=pltpu.PrefetchScalarGridSpec(
            num_scalar_prefetch=3,
            grid=(bsz, nt),
            in_specs=[pl.BlockSpec(memory_space=pl.ANY),
                      pl.BlockSpec((1, MOE_TILE, d), lambda b, i, *_: (b, i, 0)),
                      pl.BlockSpec((1, MOE_TILE, LANE), lambda b, i, *_: (b, i, 0)),
                      pl.BlockSpec((1, 6, d), lambda b, i, *_: (b, 0, 0)),
                      pl.BlockSpec((1, d), lambda b, i, *_: (0, 0))],
            out_specs=pl.BlockSpec((1, MOE_TILE, d), lambda b, i, *_: (b, i, 0)),
            scratch_shapes=[pltpu.VMEM((pairs, d), F32), pltpu.VMEM((pairs, d), F32),
                            pltpu.SemaphoreType.DMA((2,))]),
        compiler_params=pltpu.CompilerParams(dimension_semantics=("arbitrary", "arbitrary"),
                                             vmem_limit_bytes=VMEM_LIMIT, disable_bounds_checks=True),
        name="moe_combine",
    )(plan["cnt"], plan["start"], plan["row"], yb, x, route, mod, final_g.reshape(1, d))


def _rearranged_in_proj(w_in, b_in):
    sizes = (ML_HEADS * ML_QK, ML_HEADS * ML_QK, ML_HEADS * ML_V, ML_HEADS * ML_V, ML_HEADS, ML_HEADS,
             DSA_HEADS * DSA_DIM, DSA_LATENT, IDX_HEADS * IDX_DIM, IDX_DIM, IDX_HEADS,
             3 * FOX_HEADS * FOX_DIM, FOX_HEADS, N_BRANCH * w_in.shape[0])
    offs = [0]
    for sz in sizes:
        offs.append(offs[-1] + sz)
    (o_mq, o_mk, o_mv, o_mo, o_mi, o_mf, o_dq, o_ckv, o_iq, o_ik, o_iw, o_fx, o_ff, o_g, o_end) = offs
    pad = LANE - (IDX_DIM + 2 * ML_HEADS + IDX_HEADS + FOX_HEADS)

    def cols(a):
        parts = [a[..., o_mq:o_mi],
                 a[..., o_dq:o_ckv],
                 a[..., o_iq:o_ik],
                 a[..., o_ckv:o_iq],
                 a[..., o_ik:o_iw],
                 a[..., o_mi:o_dq],
                 a[..., o_iw:o_fx],
                 a[..., o_ff:o_g],
                 jnp.zeros(a.shape[:-1] + (pad,), a.dtype),
                 a[..., o_fx:o_ff],
                 a[..., o_g:o_end]]
        return jnp.concatenate(parts, axis=-1)

    return cols(w_in).astype(BF16), cols(b_in.reshape(1, -1))


def _moe_plan(stats, bm, n_blocks):
    cnt = stats[:, :N_EXPERTS, 0].astype(jnp.int32)
    start = stats[:, N_EXPERTS:, 0].astype(jnp.int32)
    total = jnp.sum(cnt, axis=0)
    padded = (total + bm - 1) // bm * bm
    pad_end = jnp.cumsum(padded)
    pad_start = pad_end - padded
    row = pad_start[None, :] + jnp.cumsum(cnt, axis=0) - cnt
    used = jnp.sum(cnt, axis=1, keepdims=True)
    cnt = jnp.concatenate([cnt, LOC_ROWS - used], axis=1)
    start = jnp.concatenate([start, used], axis=1)
    parity = jnp.arange(cnt.shape[0], dtype=jnp.int32)[:, None] % 2
    row = jnp.concatenate([row, n_blocks * bm + parity * MOE_TILE], axis=1)
    blk_row = jnp.arange(n_blocks + 1, dtype=jnp.int32) * bm
    blk_expert = jnp.minimum(jnp.sum((pad_end[None, :] <= blk_row[:, None]).astype(jnp.int32), axis=1),
                             N_EXPERTS - 1)
    n_used = (pad_end[-1:] // bm).astype(jnp.int32)
    present = padded > 0
    ids = jnp.arange(N_EXPERTS, dtype=jnp.int32)
    later = jnp.where(present[None, :] & (ids[None, :] > ids[:, None]), ids[None, :], N_EXPERTS)
    nxt = jnp.min(later, axis=1)
    sched = dict(blk_expert=blk_expert, n_used=n_used,
                 next_expert=jnp.where(nxt < N_EXPERTS, nxt, -1).astype(jnp.int32),
                 slot=((jnp.cumsum(present) - present) % 2).astype(jnp.int32))
    plan = dict(cnt=cnt.reshape(-1), start=start.reshape(-1), row=row.reshape(-1).astype(jnp.int32),
                pad_end=pad_end.astype(jnp.int32), cnt_end=(pad_start + total).astype(jnp.int32))
    return plan, sched


def kernel(x, c, w_ada, b_ada, w_in, b_in, conv_w, conv_b, ml_norm_g, kv_norm_g, w_uk, w_uv,
           w_br_ml, w_br_dsa, w_br_fox, w_out, w_router, b_router, w1, b1, w2, b2, final_g):
    bsz, s, d = x.shape
    depth = w_in.shape[0]
    n = bsz * s
    bm = 2 * MOE_TILE
    n_blocks = -(-(n * TOP_K + (ROW_ALIGN - 1) * N_EXPERTS * (n // MOE_TILE)) // bm) + N_EXPERTS
    mods = _ada_mod(c, w_ada, b_ada).reshape(depth, bsz, 6, d)
    for l in range(depth):
        mod = mods[l]
        w_r, b_r = _rearranged_in_proj(w_in[l], b_in[l])
        proj, misc = _in_proj(x, mod, w_r, b_r)
        y_ml, fcol, frow = _mlstm(proj, misc, conv_w[l], conv_b[l], ml_norm_g[l])
        y_fox = _fox(proj, fcol, frow)
        y_dsa = _dsa(proj, misc, kv_norm_g[l], w_uk[l].astype(BF16), jnp.swapaxes(w_uv[l], 1, 2).astype(BF16))
        x = _merge(x, proj, y_ml, y_dsa, y_fox, mod, w_br_ml[l].astype(BF16), w_br_dsa[l].astype(BF16),
                   w_br_fox[l].astype(BF16), w_out[l].astype(BF16))
        h2, route, stats = _router(x, mod, w_router[l], b_router[l])
        plan, sched = _moe_plan(stats, bm, n_blocks)
        xs = _dispatch(plan, h2.reshape(n, d), route.reshape(n, LANE), (n_blocks + 1) * bm, bm)
        yb = _experts(sched, xs, w1, b1, w2, b2, l, bm)
        x = _combine(plan, yb, x, route, mod, final_g, final=(l == depth - 1))
    return x
```

```python
import functools

import jax
import jax.numpy as jnp
from jax import lax
from jax.experimental import pallas as pl
from jax.experimental.pallas import tpu as pltpu

F32 = jnp.float32
BF16 = jnp.bfloat16
HIGHEST = lax.Precision.HIGHEST

EPS = 1e-6
LOG2E = 1.4426950408889634
CHUNK = 64

ML_HEADS, ML_QK, ML_V, ML_CONV = 4, 64, 128, 4
DSA_HEADS, DSA_DIM, DSA_LATENT = 4, 128, 128
IDX_HEADS, IDX_DIM, DSA_TOPK = 4, 64, 256
FOX_HEADS, FOX_DIM = 4, 128
N_BRANCH = 3
N_EXPERTS, TOP_K = 32, 4
SWIGLU_LIMIT, SWIGLU_ALPHA = 7.0, 1.702

LANE = 128
ROW_ALIGN = 8
INT_MIN = -2 ** 31

MOE_TILE = 256
N_RUNS = N_EXPERTS + 1
LOC_ROWS = TOP_K * MOE_TILE + MOE_TILE
ZERO_ROWS = 128
COUNT_ROWS = 64

C_MLQK = 0
C_MLV = 512
C_MLO = 1024
C_DQ = 1536
C_DIQ = 2048
C_CKV = 2304
C_MISC = 2432
C_FOX = 2560
C_GATE = 4096
M_IK, M_MLI, M_MLF, M_IW, M_FXF = 0, 64, 68, 72, 76

VMEM_LIMIT = 56 * 1024 * 1024


def _dot(a, b, prec=None):
    return jnp.dot(a, b, preferred_element_type=F32, precision=prec)


def _dot_nt(a, b, prec=None):
    return lax.dot_general(a, b, (((1,), (1,)), ((), ())), preferred_element_type=F32, precision=prec)


def _dot_tn(a, b):
    return lax.dot_general(a, b, (((0,), (0,)), ((), ())), preferred_element_type=F32)


def _sigmoid(x):
    return 1.0 / (1.0 + jnp.exp(-x))


def _log_sigmoid(x):
    return jnp.minimum(x, 0.0) - jnp.log1p(jnp.exp(-jnp.abs(x)))


def _rms(x):
    return x * lax.rsqrt(jnp.mean(x * x, axis=-1, keepdims=True) + EPS)


def _col_reduce(op, x):
    rows = x.shape[0]
    if rows > COUNT_ROWS and rows % COUNT_ROWS == 0:
        x = op(x.reshape(rows // COUNT_ROWS, COUNT_ROWS, x.shape[1]), axis=0)
    return op(x, axis=0, keepdims=True)


def _params(sem, vmem=VMEM_LIMIT):
    return pltpu.CompilerParams(dimension_semantics=sem, vmem_limit_bytes=vmem)


def _ada_kernel(c_ref, w_ref, b_ref, o_ref):
    c = c_ref[...]
    o_ref[0] = _dot(c * _sigmoid(c), w_ref[0], HIGHEST) + b_ref[0]


def _ada_mod(c, w_ada, b_ada):
    depth, d, n = w_ada.shape
    bsz = c.shape[0]
    tn = 1536
    return pl.pallas_call(
        _ada_kernel,
        out_shape=jax.ShapeDtypeStruct((depth, bsz, n), F32),
        grid=(depth, n // tn),
        in_specs=[pl.BlockSpec((bsz, d), lambda l, j: (0, 0)),
                  pl.BlockSpec((1, d, tn), lambda l, j: (l, 0, j)),
                  pl.BlockSpec((1, 1, tn), lambda l, j: (l, 0, j))],
        out_specs=pl.BlockSpec((1, bsz, tn), lambda l, j: (l, 0, j)),
        compiler_params=_params(("parallel", "parallel")),
        name="ada_mod",
    )(c, w_ada, b_ada.reshape(depth, 1, n))


def _inproj_kernel(x_ref, mod_ref, w_ref, b_ref, o_ref, misc_ref, h_scr, *, tn):
    j = pl.program_id(2)

    @pl.when(j == 0)
    def _():
        h = _rms(x_ref[0]) * (1.0 + mod_ref[0, 1:2, :]) + mod_ref[0, 0:1, :]
        h_scr[...] = h.astype(BF16)

    acc = _dot(h_scr[...], w_ref[...]) + b_ref[...]
    o_ref[0] = acc.astype(BF16)

    @pl.when(j == C_MISC // tn)
    def _():
        misc_ref[0] = acc[:, C_MISC % tn:C_MISC % tn + LANE]


def _in_proj(x, mod, w, b):
    bsz, s, d = x.shape
    n = w.shape[1]
    tm = min(2048, s)
    tn = 1024
    return pl.pallas_call(
        functools.partial(_inproj_kernel, tn=tn),
        out_shape=(jax.ShapeDtypeStruct((bsz, s, n), BF16), jax.ShapeDtypeStruct((bsz, s, LANE), F32)),
        grid=(bsz, s // tm, n // tn),
        in_specs=[pl.BlockSpec((1, tm, d), lambda bi, i, j: (bi, i, 0)),
                  pl.BlockSpec((1, 6, d), lambda bi, i, j: (bi, 0, 0)),
                  pl.BlockSpec((d, tn), lambda bi, i, j: (0, j)),
                  pl.BlockSpec((1, tn), lambda bi, i, j: (0, j))],
        out_specs=(pl.BlockSpec((1, tm, tn), lambda bi, i, j: (bi, i, j)),
                   pl.BlockSpec((1, tm, LANE), lambda bi, i, j: (bi, i, 0))),
        scratch_shapes=[pltpu.VMEM((tm, d), BF16)],
        compiler_params=_params(("parallel", "parallel", "arbitrary")),
        name="in_proj",
    )(x, mod, w, b)


def _mlstm_kernel(qk_ref, v_ref, o_ref, misc_ref, cw_ref, cb_ref, g_ref,
                  y_ref, fcol_ref, frow_ref,
                  xext, ct_scr, n_scr, m_scr, carry_scr, *, L):
    c = pl.program_id(1)
    nqk = ML_HEADS * ML_QK

    @pl.when(c == 0)
    def _():
        xext[0:8, :] = jnp.zeros((8, 2 * nqk), F32)
        ct_scr[...] = jnp.zeros(ct_scr.shape, F32)
        n_scr[...] = jnp.zeros(n_scr.shape, F32)
        m_scr[...] = jnp.full(m_scr.shape, -jnp.inf, F32)
        carry_scr[...] = jnp.zeros(carry_scr.shape, F32)

    @pl.when(c > 0)
    def _():
        xext[0:8, :] = xext[L:L + 8, :]

    xext[8:8 + L, :] = qk_ref[0].astype(F32)
    cw = cw_ref[...]
    conv = (cb_ref[...] + cw[3:4, :] * xext[8:8 + L, :] + cw[2:3, :] * xext[7:7 + L, :]
            + cw[1:2, :] * xext[6:6 + L, :] + cw[0:1, :] * xext[5:5 + L, :])
    qk = conv * _sigmoid(conv)

    misc = misc_ref[0]
    ls = _log_sigmoid(misc)
    row = lax.broadcasted_iota(jnp.int32, (L, L), 0)
    col = lax.broadcasted_iota(jnp.int32, (L, L), 1)
    causal = row >= col
    tri = jnp.where(causal, 1.0, 0.0).astype(F32)
    cs = _dot(tri, ls, HIGHEST)
    cs_t = cs.T
    misc_t = misc.T
    carry = carry_scr[0:1, :]
    fcol_ref[0] = cs + carry
    for h in range(FOX_HEADS):
        frow_ref[0, h:h + 1, :] = cs_t[M_FXF + h:M_FXF + h + 1, :] + carry[:, M_FXF + h:M_FXF + h + 1]
    frow_ref[0, 4:8, :] = jnp.zeros((4, L), F32)
    carry_scr[0:1, :] = carry + cs[L - 1:L, :]

    for h in range(ML_HEADS):
        qh = qk[:, h * ML_QK:(h + 1) * ML_QK] * (ML_QK ** -0.5)
        kh = qk[:, nqk + h * ML_QK:nqk + (h + 1) * ML_QK]
        vb = v_ref[0, :, h * ML_V:(h + 1) * ML_V].astype(BF16)
        i_col = misc[:, M_MLI + h:M_MLI + h + 1]
        i_row = misc_t[M_MLI + h:M_MLI + h + 1, :]
        b_col = cs[:, M_MLF + h:M_MLF + h + 1]
        b_row = cs_t[M_MLF + h:M_MLF + h + 1, :]
        b_last = b_col[L - 1:L, :]
        m_prev = m_scr[h:h + 1, 0:1]

        d_log = jnp.where(causal, b_col - b_row + i_row, -jnp.inf)
        inter_log = b_col + m_prev
        m_out = jnp.maximum(inter_log, jnp.max(d_log, axis=-1, keepdims=True))
        qb = qh.astype(BF16)
        kb = kh.astype(BF16)
        s = _dot_nt(qb, kb) * jnp.exp(d_log - m_out)
        a_inter = jnp.exp(inter_log - m_out)
        ct = ct_scr[h]
        n_row = n_scr[h]
        num = _dot(s.astype(BF16), vb) + a_inter * _dot(qb, ct.astype(BF16))
        den = jnp.sum(s, axis=-1, keepdims=True) + a_inter * jnp.sum(qh * n_row, axis=-1, keepdims=True)
        hid = num / jnp.maximum(jnp.abs(den), jnp.exp(-m_out))

        w_state = b_last - b_col + i_col
        m_loc = jnp.max(w_state, axis=0, keepdims=True)
        ke = kh * jnp.exp(w_state - m_loc)
        c_loc = _dot_tn(ke.astype(BF16), vb)
        n_loc = jnp.sum(ke, axis=0, keepdims=True)
        m_new = jnp.maximum(b_last + m_prev, m_loc)
        decay = jnp.exp(b_last + m_prev - m_new)
        scale = jnp.exp(m_loc - m_new)
        ct_scr[h] = decay * ct + scale * c_loc
        n_scr[h] = decay * n_row + scale * n_loc
        m_scr[h:h + 1, :] = jnp.broadcast_to(m_new, (1, LANE))

        y = (_rms(hid) * g_ref[:, h * ML_V:(h + 1) * ML_V]
             * _sigmoid(o_ref[0, :, h * ML_V:(h + 1) * ML_V].astype(F32)))
        y_ref[0, :, h * ML_V:(h + 1) * ML_V] = y.astype(BF16)


def _mlstm(proj, misc, conv_w, conv_b, norm_g):
    bsz, s, _ = proj.shape
    L = min(256, s)
    w = 2 * ML_HEADS * ML_QK
    wv = ML_HEADS * ML_V
    return pl.pallas_call(
        functools.partial(_mlstm_kernel, L=L),
        out_shape=(jax.ShapeDtypeStruct((bsz, s, wv), BF16),
                   jax.ShapeDtypeStruct((bsz, s, LANE), F32),
                   jax.ShapeDtypeStruct((bsz, 8, s), F32)),
        grid=(bsz, s // L),
        in_specs=[pl.BlockSpec((1, L, w), lambda b, c: (b, c, C_MLQK // w)),
                  pl.BlockSpec((1, L, wv), lambda b, c: (b, c, C_MLV // wv)),
                  pl.BlockSpec((1, L, wv), lambda b, c: (b, c, C_MLO // wv)),
                  pl.BlockSpec((1, L, LANE), lambda b, c: (b, c, 0)),
                  pl.BlockSpec((ML_CONV, w), lambda b, c: (0, 0)),
                  pl.BlockSpec((1, w), lambda b, c: (0, 0)),
                  pl.BlockSpec((1, wv), lambda b, c: (0, 0))],
        out_specs=(pl.BlockSpec((1, L, wv), lambda b, c: (b, c, 0)),
                   pl.BlockSpec((1, L, LANE), lambda b, c: (b, c, 0)),
                   pl.BlockSpec((1, 8, L), lambda b, c: (b, 0, c))),
        scratch_shapes=[pltpu.VMEM((L + 8, w), F32),
                        pltpu.VMEM((ML_HEADS, ML_QK, ML_V), F32),
                        pltpu.VMEM((ML_HEADS, 1, ML_QK), F32),
                        pltpu.VMEM((8, LANE), F32),
                        pltpu.VMEM((8, LANE), F32)],
        compiler_params=_params(("parallel", "arbitrary")),
        name="mlstm",
    )(proj, proj, proj, misc, conv_w, conv_b.reshape(1, w), norm_g.reshape(1, wv))


def _fox_kernel(q_ref, k_ref, v_ref, fcol_ref, frow_ref, y_ref, k_scr, vt_scr, *, T, S):
    qi = pl.program_id(1)
    d = FOX_DIM

    @pl.when(qi == 0)
    def _():
        for h in range(FOX_HEADS):
            k_scr[h] = k_ref[0, :, h * d:(h + 1) * d].astype(BF16)
            vt_scr[h] = v_ref[0, :, h * d:(h + 1) * d].astype(F32).T.astype(BF16)

    q_t = (q_ref[0].astype(F32) * (d ** -0.5 * LOG2E)).T

    def body(ext):
        kpos = lax.broadcasted_iota(jnp.int32, (ext, 1), 0)
        qpos = (ext - T) + lax.broadcasted_iota(jnp.int32, (1, T), 1)
        causal = kpos <= qpos
        for h in range(FOX_HEADS):
            s = _dot(k_scr[h, 0:ext, :], q_t[h * d:(h + 1) * d, :].astype(BF16))
            s = s + (frow_ref[0, h:h + 1, :] * LOG2E - fcol_ref[0, 0:ext, M_FXF + h:M_FXF + h + 1] * LOG2E)
            s = jnp.where(causal, s, -jnp.inf)
            p = jnp.exp2(s - _col_reduce(jnp.max, s))
            l = _col_reduce(jnp.sum, p)
            out_t = _dot(vt_scr[h, :, 0:ext], p.astype(BF16)) / l
            y_ref[0, :, h * d:(h + 1) * d] = out_t.T.astype(BF16)

    for c in range(S // T):
        pl.when(qi == c)(functools.partial(body, (c + 1) * T))


def _fox(proj, fcol, frow):
    bsz, s, _ = proj.shape
    T = min(256, s)
    w = FOX_HEADS * FOX_DIM
    return pl.pallas_call(
        functools.partial(_fox_kernel, T=T, S=s),
        out_shape=jax.ShapeDtypeStruct((bsz, s, w), BF16),
        grid=(bsz, s // T),
        in_specs=[pl.BlockSpec((1, T, w), lambda b, i: (b, i, C_FOX // w)),
                  pl.BlockSpec((1, s, w), lambda b, i: (b, 0, C_FOX // w + 1)),
                  pl.BlockSpec((1, s, w), lambda b, i: (b, 0, C_FOX // w + 2)),
                  pl.BlockSpec((1, s, LANE), lambda b, i: (b, 0, 0)),
                  pl.BlockSpec((1, 8, T), lambda b, i: (b, 0, i))],
        out_specs=pl.BlockSpec((1, T, w), lambda b, i: (b, i, 0)),
        scratch_shapes=[pltpu.VMEM((FOX_HEADS, s, FOX_DIM), BF16),
                        pltpu.VMEM((FOX_HEADS, FOX_DIM, s), BF16)],
        compiler_params=_params(("parallel", "arbitrary")),
        name="fox_attention",
    )(proj, proj, proj, fcol, frow)


def _dsa_body(ext, qi, q_ref, qidx_ref, misc_q_ref, wuk_ref, wuv_ref, y_ref,
              ckvn_scr, ckvnt_scr, kidx_scr, sel_scr, *, T, n_sel, rank_tile):
    dh = DSA_DIM
    q_t = q_ref[0].astype(F32).T
    qidx_t = qidx_ref[0].astype(F32).T.astype(BF16)
    w_t = misc_q_ref[0].T[M_IW:M_IW + IDX_HEADS, :] * (IDX_HEADS ** -0.5)
    kidx = kidx_scr[0:ext, :]
    score = jnp.zeros((ext, T), F32)
    for h in range(IDX_HEADS):
        lg = _dot(kidx, qidx_t[h * IDX_DIM:(h + 1) * IDX_DIM, :]) * (IDX_DIM ** -0.5)
        score = score + w_t[h:h + 1, :] * jnp.maximum(lg, 0.0)
    kpos = lax.broadcasted_iota(jnp.int32, (ext, 1), 0)
    qchunk = (qi * T + lax.broadcasted_iota(jnp.int32, (1, T), 1)) // CHUNK
    score = jnp.where((kpos // CHUNK) <= qchunk, score, -jnp.inf)

    def as_float(c):
        return pltpu.bitcast(jnp.where(c < 0, c ^ jnp.int32(0x7FFFFFFF), c), F32)

    def count_ge(c):
        return _col_reduce(jnp.sum, jnp.where(score >= as_float(c), 1.0, 0.0))

    t0 = jnp.where(count_ge(jnp.zeros((1, T), jnp.int32)) >= n_sel, jnp.int32(0), jnp.int32(INT_MIN))

    def bis(i, t):
        cand = t + jnp.left_shift(jnp.int32(1), jnp.int32(30) - i)
        return jnp.where(count_ge(cand) >= n_sel, cand, t)

    t = lax.fori_loop(0, 31, bis, t0)
    n_vis = ((qchunk + 1) * CHUNK).astype(F32)
    thr = jnp.where(n_vis <= n_sel, -3.0e38, as_float(t))
    need = n_sel - _col_reduce(jnp.sum, jnp.where(score > thr, 1.0, 0.0))
    r_i = lax.broadcasted_iota(jnp.int32, (rank_tile, rank_tile), 0)
    c_i = lax.broadcasted_iota(jnp.int32, (rank_tile, rank_tile), 1)
    lower = jnp.where(c_i < r_i, 1.0, 0.0).astype(BF16)
    carry = jnp.zeros((1, T), F32)
    for j in range(ext // rank_tile):
        sc = score[j * rank_tile:(j + 1) * rank_tile, :]
        eq = jnp.where(sc == thr, 1.0, 0.0)
        rank = _dot(lower, eq.astype(BF16)) + carry
        carry = carry + _col_reduce(jnp.sum, eq)
        sel_scr[j * rank_tile:(j + 1) * rank_tile, :] = jnp.where(
            sc > thr, 1.0, jnp.where(rank < need, eq, 0.0))

    sel = sel_scr[0:ext, :] > 0.5
    ckvn = ckvn_scr[0:ext, :]
    ckvn_t = ckvnt_scr[:, 0:ext]
    for h in range(DSA_HEADS):
        qa_t = _dot(wuk_ref[h], q_t[h * dh:(h + 1) * dh, :].astype(BF16)) * (dh ** -0.5 * LOG2E)
        lg = _dot(ckvn, qa_t.astype(BF16))
        lg = jnp.where(sel, lg, -jnp.inf)
        p = jnp.exp2(lg - _col_reduce(jnp.max, lg))
        l = _col_reduce(jnp.sum, p)
        lat_t = _dot(ckvn_t, p.astype(BF16)) / l
        out_t = _dot(wuv_ref[h], lat_t.astype(BF16))
        y_ref[0, :, h * dh:(h + 1) * dh] = out_t.T.astype(BF16)


def _dsa_kernel(q_ref, ckv_ref, qidx_ref, misc_all_ref, misc_q_ref, g_ref, wuk_ref, wuv_ref, y_ref,
                ckvn_scr, ckvnt_scr, kidx_scr, sel_scr, *, T, S, n_sel, n_cls, rank_tile):
    qi = pl.program_id(1)

    @pl.when(qi == 0)
    def _():
        ckvn = _rms(ckv_ref[0].astype(F32)) * g_ref[...]
        ckvn_scr[...] = ckvn.astype(BF16)
        ckvnt_scr[...] = ckvn.T.astype(BF16)
        kidx_scr[...] = misc_all_ref[0, :, M_IK:M_IK + IDX_DIM].astype(BF16)

    per = (S // T) // n_cls
    for c in range(n_cls):
        ext = (c + 1) * per * T

        @pl.when(qi // per == c)
        def _(ext=ext):
            _dsa_body(ext, qi, q_ref, qidx_ref, misc_q_ref, wuk_ref, wuv_ref, y_ref,
                      ckvn_scr, ckvnt_scr, kidx_scr, sel_scr, T=T, n_sel=n_sel, rank_tile=rank_tile)


def _dsa(proj, misc, kv_g, wuk, wuv_t):
    bsz, s, _ = proj.shape
    T = min(256, s)
    n_sel = min(DSA_TOPK, s // 4)
    n_cls = max(1, min(4, s // 512))
    rank_tile = 256
    w = DSA_HEADS * DSA_DIM
    wi = IDX_HEADS * IDX_DIM
    return pl.pallas_call(
        functools.partial(_dsa_kernel, T=T, S=s, n_sel=float(n_sel), n_cls=n_cls, rank_tile=rank_tile),
        out_shape=jax.ShapeDtypeStruct((bsz, s, w), BF16),
        grid=(bsz, s // T),
        in_specs=[pl.BlockSpec((1, T, w), lambda b, i: (b, i, C_DQ // w)),
                  pl.BlockSpec((1, s, DSA_LATENT), lambda b, i: (b, 0, C_CKV // DSA_LATENT)),
                  pl.BlockSpec((1, T, wi), lambda b, i: (b, i, C_DIQ // wi)),
                  pl.BlockSpec((1, s, LANE), lambda b, i: (b, 0, 0)),
                  pl.BlockSpec((1, T, LANE), lambda b, i: (b, i, 0)),
                  pl.BlockSpec((1, DSA_LATENT), lambda b, i: (0, 0)),
                  pl.BlockSpec((DSA_HEADS, DSA_LATENT, DSA_DIM), lambda b, i: (0, 0, 0)),
                  pl.BlockSpec((DSA_HEADS, DSA_DIM, DSA_LATENT), lambda b, i: (0, 0, 0))],
        out_specs=pl.BlockSpec((1, T, w), lambda b, i: (b, i, 0)),
        scratch_shapes=[pltpu.VMEM((s, DSA_LATENT), BF16),
                        pltpu.VMEM((DSA_LATENT, s), BF16),
                        pltpu.VMEM((s, IDX_DIM), BF16),
                        pltpu.VMEM((s, T), F32)],
        compiler_params=_params(("parallel", "arbitrary")),
        name="dsa_attention",
    )(proj, proj, proj, misc, misc, kv_g.reshape(1, DSA_LATENT), wuk, wuv_t)


def _merge_kernel(yml_ref, ydsa_ref, yfox_ref, g0_ref, g1_ref, g2_ref, x_ref, mod_ref,
                  wml_ref, wdsa_ref, wfox_ref, wout_ref, o_ref):
    merged = (_sigmoid(g0_ref[0].astype(F32)) * _dot(yml_ref[0], wml_ref[...])
              + _sigmoid(g1_ref[0].astype(F32)) * _dot(ydsa_ref[0], wdsa_ref[...])
              + _sigmoid(g2_ref[0].astype(F32)) * _dot(yfox_ref[0], wfox_ref[...]))
    out = _dot(merged.astype(BF16), wout_ref[...])
    o_ref[0] = x_ref[0] + mod_ref[0, 2:3, :] * out


def _merge(x, proj, y_ml, y_dsa, y_fox, mod, w_ml, w_dsa, w_fox, w_out):
    bsz, s, d = x.shape
    tm = min(512, s)
    wb = y_ml.shape[-1]
    gb = C_GATE // d
    yspec = pl.BlockSpec((1, tm, wb), lambda b, i: (b, i, 0))
    wspec = pl.BlockSpec((wb, d), lambda b, i: (0, 0))
    return pl.pallas_call(
        _merge_kernel,
        out_shape=jax.ShapeDtypeStruct((bsz, s, d), F32),
        grid=(bsz, s // tm),
        in_specs=[yspec, yspec, yspec,
                  pl.BlockSpec((1, tm, d), lambda b, i: (b, i, gb)),
                  pl.BlockSpec((1, tm, d), lambda b, i: (b, i, gb + 1)),
                  pl.BlockSpec((1, tm, d), lambda b, i: (b, i, gb + 2)),
                  pl.BlockSpec((1, tm, d), lambda b, i: (b, i, 0)),
                  pl.BlockSpec((1, 6, d), lambda b, i: (b, 0, 0)),
                  wspec, wspec, wspec,
                  pl.BlockSpec((d, d), lambda b, i: (0, 0))],
        out_specs=pl.BlockSpec((1, tm, d), lambda b, i: (b, i, 0)),
        compiler_params=_params(("parallel", "parallel")),
        name="merge_out",
    )(y_ml, y_dsa, y_fox, proj, proj, proj, x, mod, w_ml, w_dsa, w_fox, w_out)


def _router_kernel(x_ref, mod_ref, wr_ref, br_ref, h_ref, route_ref, stat_ref, *, T):
    h = _rms(x_ref[0]) * (1.0 + mod_ref[0, 4:5, :]) + mod_ref[0, 3:4, :]
    h_ref[0] = h.astype(BF16)
    lg = _dot(h, wr_ref[...], HIGHEST) + br_ref[...]
    lg = lg.T[0:N_EXPERTS, :]
    sub = lax.broadcasted_iota(jnp.int32, (N_EXPERTS, 1), 0).astype(F32)
    vals, idxs, hots = [], [], []
    for _ in range(TOP_K):
        mx = jnp.max(lg, axis=0, keepdims=True)
        idx = jnp.min(jnp.where(lg == mx, sub, float(N_EXPERTS)), axis=0, keepdims=True)
        hot = sub == idx
        vals.append(mx)
        idxs.append(idx)
        hots.append(hot)
        lg = jnp.where(hot, -jnp.inf, lg)
    exps = [jnp.exp(v - vals[0]) for v in vals]
    tot = exps[0] + exps[1] + exps[2] + exps[3]
    multi = jnp.zeros((N_EXPERTS, T), F32)
    for hot in hots:
        multi = multi + jnp.where(hot, 1.0, 0.0)
    r_i = lax.broadcasted_iota(jnp.int32, (T, T), 0)
    c_i = lax.broadcasted_iota(jnp.int32, (T, T), 1)
    upper = jnp.where(r_i < c_i, 1.0, 0.0).astype(BF16)
    before = _dot(multi.astype(BF16), upper)
    cnt = jnp.broadcast_to(jnp.sum(multi, axis=1, keepdims=True), (N_EXPERTS, LANE))
    cnt = jnp.floor((cnt + (ROW_ALIGN - 1.0)) * (1.0 / ROW_ALIGN)) * ROW_ALIGN
    a_i = lax.broadcasted_iota(jnp.int32, (N_EXPERTS, N_EXPERTS), 0)
    b_i = lax.broadcasted_iota(jnp.int32, (N_EXPERTS, N_EXPERTS), 1)
    start = _dot(jnp.where(b_i < a_i, 1.0, 0.0).astype(F32), cnt, HIGHEST)
    stat_ref[0, 0:N_EXPERTS, :] = cnt
    stat_ref[0, N_EXPERTS:2 * N_EXPERTS, :] = start
    where_to = before + start[:, 0:1]
    rows = list(idxs)
    rows += [jnp.sum(jnp.where(hot, where_to, 0.0), axis=0, keepdims=True) for hot in hots]
    rows += [e / tot for e in exps]
    rows.append(jnp.zeros((LANE - 3 * TOP_K, T), F32))
    route_ref[0] = jnp.concatenate(rows, axis=0).T


def _router(x, mod, w_router, b_router):
    bsz, s, d = x.shape
    T = MOE_TILE
    nt = s // T
    ne = w_router.shape[1]
    wr = jnp.zeros((d, LANE), F32).at[:, :ne].set(w_router)
    br = jnp.full((1, LANE), -1e30, F32).at[0, :ne].set(b_router)
    return pl.pallas_call(
        functools.partial(_router_kernel, T=T),
        out_shape=(jax.ShapeDtypeStruct((bsz, s, d), BF16),
                   jax.ShapeDtypeStruct((bsz, s, LANE), F32),
                   jax.ShapeDtypeStruct((bsz * nt, 2 * N_EXPERTS, LANE), F32)),
        grid=(bsz, nt),
        in_specs=[pl.BlockSpec((1, T, d), lambda b, i: (b, i, 0)),
                  pl.BlockSpec((1, 6, d), lambda b, i: (b, 0, 0)),
                  pl.BlockSpec((d, LANE), lambda b, i: (0, 0)),
                  pl.BlockSpec((1, LANE), lambda b, i: (0, 0))],
        out_specs=(pl.BlockSpec((1, T, d), lambda b, i: (b, i, 0)),
                   pl.BlockSpec((1, T, LANE), lambda b, i: (b, i, 0)),
                   pl.BlockSpec((1, 2 * N_EXPERTS, LANE), lambda b, i: (b * nt + i, 0, 0))),
        compiler_params=_params(("parallel", "parallel")),
        name="moe_router",
    )(x, mod, wr, br)


def _run_copies(tile, cnt_ref, start_ref, row_ref, make_copy):
    for r in range(N_RUNS):
        c = cnt_ref[tile * N_RUNS + r]
        src = start_ref[tile * N_RUNS + r]
        dst = row_ref[tile * N_RUNS + r]
        size = MOE_TILE
        while size >= ROW_ALIGN:
            done = c & (-2 * size)
            cp = make_copy(pl.multiple_of(src + done, ROW_ALIGN), pl.multiple_of(dst + done, ROW_ALIGN), size)
            pl.when((c & size) != 0)(cp.start)
            size //= 2


def _dispatch_kernel(cnt_ref, start_ref, row_ref, pad_end_ref, cnt_end_ref, h_ref, route_ref, xs_hbm,
                     loc0, loc1, zbuf, zsem, sem, *, BM, N_TAIL, N_T):
    t = pl.program_id(0)
    loc = (loc0, loc1)

    @pl.when(t == 0)
    def _():
        zbuf[...] = jnp.zeros(zbuf.shape, F32)
        n_rows = xs_hbm.shape[0]

        def pad_copy(e, g):
            start = pl.multiple_of(pad_end_ref[e] - g * ZERO_ROWS, ZERO_ROWS)
            return (start + ZERO_ROWS > cnt_end_ref[e],
                    pltpu.make_async_copy(zbuf.at[pl.ds(0, ZERO_ROWS)], xs_hbm.at[pl.ds(start, ZERO_ROWS)], zsem))

        def tail_copy(e):
            start = pl.multiple_of(pad_end_ref[N_EXPERTS - 1] + e * BM, BM)
            safe = pl.multiple_of(jnp.minimum(start, n_rows - BM), BM)
            return start < n_rows, pltpu.make_async_copy(zbuf, xs_hbm.at[pl.ds(safe, BM)], zsem)

        copies = [pad_copy(e, g) for e in range(N_EXPERTS) for g in range(1, BM // ZERO_ROWS + 1)]
        copies += [tail_copy(e) for e in range(N_TAIL)]
        for cond, cp in copies:
            pl.when(cond)(cp.start)
        for cond, cp in copies:
            pl.when(cond)(cp.wait)

    pos_t = route_ref[...].T[TOP_K:2 * TOP_K, :]
    slot = lax.broadcasted_iota(jnp.int32, (LOC_ROWS, 1), 0).astype(F32)
    pick = jnp.zeros((LOC_ROWS, MOE_TILE), F32)
    for k in range(TOP_K):
        pick = pick + jnp.where(slot == pos_t[k:k + 1, :], 1.0, 0.0)
    pick = pick.astype(BF16)

    def drain(s):
        pltpu.make_async_copy(loc[s], xs_hbm.at[pl.ds(0, LOC_ROWS)], sem.at[s]).wait()

    for s in range(2):
        @pl.when(t % 2 == s)
        def _(s=s):
            pl.when(t >= 2)(functools.partial(drain, s))
            loc[s][...] = _dot(pick, h_ref[...])
            _run_copies(t, cnt_ref, start_ref, row_ref,
                        lambda a, b, n: pltpu.make_async_copy(loc[s].at[pl.ds(a, n)], xs_hbm.at[pl.ds(b, n)],
                                                              sem.at[s]))

    @pl.when(t == N_T - 1)
    def _():
        if N_T >= 2:
            drain(N_T % 2)
        drain((N_T - 1) % 2)


def _dispatch(plan, h2, route, n_rows, bm):
    n, d = h2.shape
    pairs = LOC_ROWS
    return pl.pallas_call(
        functools.partial(_dispatch_kernel, BM=bm, N_TAIL=n_rows // bm - (n * TOP_K) // bm, N_T=n // MOE_TILE),
        out_shape=jax.ShapeDtypeStruct((n_rows, d), F32),
        grid_spec=pltpu.PrefetchScalarGridSpec(
            num_scalar_prefetch=5,
            grid=(n // MOE_TILE,),
            in_specs=[pl.BlockSpec((MOE_TILE, d), lambda t, *_: (t, 0)),
                      pl.BlockSpec((MOE_TILE, LANE), lambda t, *_: (t, 0))],
            out_specs=pl.BlockSpec(memory_space=pl.ANY),
            scratch_shapes=[pltpu.VMEM((pairs, d), F32),
                            pltpu.VMEM((pairs, d), F32),
                            pltpu.VMEM((bm, d), F32),
                            pltpu.SemaphoreType.DMA(()),
                            pltpu.SemaphoreType.DMA((2,))]),
        compiler_params=pltpu.CompilerParams(dimension_semantics=("arbitrary",), vmem_limit_bytes=VMEM_LIMIT,
                                             disable_bounds_checks=True),
        name="moe_dispatch",
    )(plan["cnt"], plan["start"], plan["row"], plan["pad_end"], plan["cnt_end"], h2, route)


def _expert_kernel(be_ref, nu_ref, nxt_ref, slot_ref, xs_ref, w1_hbm, b1_ref, w2_hbm, b2_ref, o_ref,
                   w1f, w2f, w1b, w2b, sem, *, layer):
    i = pl.program_id(0)
    de = w2b.shape[0]
    e = be_ref[i]
    prev = be_ref[jnp.maximum(i - 1, 0)]
    fresh = (i == 0) | (e != prev)

    def copies(expert, s):
        return (pltpu.make_async_copy(w1_hbm.at[layer, expert], w1f.at[s], sem.at[s, 0]),
                pltpu.make_async_copy(w2_hbm.at[layer, expert], w2f.at[s], sem.at[s, 1]))

    @pl.when(i == 0)
    def _():
        for cp in copies(e, slot_ref[e]):
            cp.start()

    @pl.when(fresh & (i < nu_ref[0]))
    def _():
        s = slot_ref[e]
        for cp in copies(e, s):
            cp.wait()
        w1b[...] = w1f[s].astype(BF16)
        w2b[...] = w2f[s].astype(BF16)
        nxt = nxt_ref[e]

        @pl.when(nxt >= 0)
        def _():
            for cp in copies(nxt, 1 - s):
                cp.start()

    @pl.when(i < nu_ref[0])
    def _():
        hdn = _dot(xs_ref[...].astype(BF16), w1b[...]) + b1_ref[...]
        glu = jnp.minimum(hdn[:, :de], SWIGLU_LIMIT)
        lin = jnp.clip(hdn[:, de:], -SWIGLU_LIMIT, SWIGLU_LIMIT)
        act = glu * _sigmoid(SWIGLU_ALPHA * glu) * (lin + 1.0)
        o_ref[...] = _dot(act.astype(BF16), w2b[...]) + b2_ref[...]

    @pl.when(i >= nu_ref[0])
    def _():
        o_ref[...] = jnp.zeros(o_ref.shape, F32)


def _experts(sched, xs, w1, b1, w2, b2, layer, bm):
    n_rows, d = xs.shape
    depth, ne, _, dh2 = w1.shape
    de = w2.shape[2]
    n_blocks = n_rows // bm

    def row_map(i, be, nu, *_):
        return (jnp.minimum(i, nu[0] - 1), 0)

    def b_map(i, be, *_):
        return (layer, be[i], 0, 0)

    return pl.pallas_call(
        functools.partial(_expert_kernel, layer=layer),
        out_shape=jax.ShapeDtypeStruct((n_rows, d), F32),
        grid_spec=pltpu.PrefetchScalarGridSpec(
            num_scalar_prefetch=4,
            grid=(n_blocks,),
            in_specs=[pl.BlockSpec((bm, d), row_map),
                      pl.BlockSpec(memory_space=pl.ANY),
                      pl.BlockSpec((None, None, 1, dh2), b_map),
                      pl.BlockSpec(memory_space=pl.ANY),
                      pl.BlockSpec((None, None, 1, d), b_map)],
            out_specs=pl.BlockSpec((bm, d), lambda i, *_: (i, 0)),
            scratch_shapes=[pltpu.VMEM((2, d, dh2), F32), pltpu.VMEM((2, de, d), F32),
                            pltpu.VMEM((d, dh2), BF16), pltpu.VMEM((de, d), BF16),
                            pltpu.SemaphoreType.DMA((2, 2))]),
        compiler_params=_params(("arbitrary",)),
        name="moe_experts",
    )(sched["blk_expert"], sched["n_used"], sched["next_expert"], sched["slot"], xs,
      w1, b1.reshape(depth, ne, 1, dh2), w2, b2.reshape(depth, ne, 1, d))


def _combine_kernel(cnt_ref, start_ref, row_ref, yb_hbm, x_ref, route_ref, mod_ref, fg_ref, o_ref,
                    loc0, loc1, sem, *, nt, n_tiles, final):
    t = pl.program_id(0) * nt + pl.program_id(1)
    pairs = LOC_ROWS
    loc = (loc0, loc1)

    def fetch(tile, s):
        _run_copies(tile, cnt_ref, start_ref, row_ref,
                    lambda a, b, n: pltpu.make_async_copy(yb_hbm.at[pl.ds(b, n)], loc[s].at[pl.ds(a, n)],
                                                          sem.at[s]))

    pl.when(t == 0)(functools.partial(fetch, 0, 0))
    route = route_ref[0]
    slot = lax.broadcasted_iota(jnp.int32, (1, pairs), 1).astype(F32)
    wgt = jnp.zeros((MOE_TILE, pairs), F32)
    for k in range(TOP_K):
        wgt = wgt + jnp.where(slot == route[:, TOP_K + k:TOP_K + k + 1],
                              route[:, 2 * TOP_K + k:2 * TOP_K + k + 1], 0.0)
    hi = wgt.astype(BF16)
    lo = (wgt - hi.astype(F32)).astype(BF16)

    for s in range(2):
        @pl.when(t % 2 == s)
        def _(s=s):
            pl.when(t + 1 < n_tiles)(functools.partial(fetch, t + 1, 1 - s))
            pltpu.make_async_copy(yb_hbm.at[pl.ds(0, pairs)], loc[s], sem.at[s]).wait()
            rows = loc[s][...].astype(BF16)
            y = _dot(hi, rows) + _dot(lo, rows)
            out = x_ref[0] + mod_ref[0, 5:6, :] * y
            if final:
                out = _rms(out) * fg_ref[...]
            o_ref[0] = out


def _combine(plan, yb, x, route, mod, final_g, final):
    bsz, s, d = x.shape
    nt = s // MOE_TILE
    pairs = LOC_ROWS
    return pl.pallas_call(
        functools.partial(_combine_kernel, nt=nt, n_tiles=bsz * nt, final=final),
        out_shape=jax.ShapeDtypeStruct((bsz, s, d), F32),
        grid_spec=pltpu.PrefetchScalarGridSpec(
            num_scalar_prefetch=3,
            grid=(bsz, nt),
            in_specs=[pl.BlockSpec(memory_space=pl.ANY),
                      pl.BlockSpec((1, MOE_TILE, d), lambda b, i, *_: (b, i, 0)),
                      pl.BlockSpec((1, MOE_TILE, LANE), lambda b, i, *_: (b, i, 0)),
                      pl.BlockSpec((1, 6, d), lambda b, i, *_: (b, 0, 0)),
                      pl.BlockSpec((1, d), lambda b, i, *_: (0, 0))],
            out_specs=pl.BlockSpec((1, MOE_TILE, d), lambda b, i, *_: (b, i, 0)),
            scratch_shapes=[pltpu.VMEM((pairs, d), F32), pltpu.VMEM((pairs, d), F32),
                            pltpu.SemaphoreType.DMA((2,))]),
        compiler_params=pltpu.CompilerParams(dimension_semantics=("arbitrary", "arbitrary"),
                                             vmem_limit_bytes=VMEM_LIMIT, disable_bounds_checks=True),
        name="moe_combine",
    )(plan["cnt"], plan["start"], plan["row"], yb, x, route, mod, final_g.reshape(1, d))


def _rearranged_in_proj(w_in, b_in):
    sizes = (ML_HEADS * ML_QK, ML_HEADS * ML_QK, ML_HEADS * ML_V, ML_HEADS * ML_V, ML_HEADS, ML_HEADS,
             DSA_HEADS * DSA_DIM, DSA_LATENT, IDX_HEADS * IDX_DIM, IDX_DIM, IDX_HEADS,
             3 * FOX_HEADS * FOX_DIM, FOX_HEADS, N_BRANCH * w_in.shape[0])
    offs = [0]
    for sz in sizes:
        offs.append(offs[-1] + sz)
    (o_mq, o_mk, o_mv, o_mo, o_mi, o_mf, o_dq, o_ckv, o_iq, o_ik, o_iw, o_fx, o_ff, o_g, o_end) = offs
    pad = LANE - (IDX_DIM + 2 * ML_HEADS + IDX_HEADS + FOX_HEADS)

    def cols(a):
        parts = [a[..., o_mq:o_mi],
                 a[..., o_dq:o_ckv],
                 a[..., o_iq:o_ik],
                 a[..., o_ckv:o_iq],
                 a[..., o_ik:o_iw],
                 a[..., o_mi:o_dq],
                 a[..., o_iw:o_fx],
                 a[..., o_ff:o_g],
                 jnp.zeros(a.shape[:-1] + (pad,), a.dtype),
                 a[..., o_fx:o_ff],
                 a[..., o_g:o_end]]
        return jnp.concatenate(parts, axis=-1)

    return cols(w_in).astype(BF16), cols(b_in.reshape(1, -1))


def _moe_plan(stats, bm, n_blocks):
    cnt = stats[:, :N_EXPERTS, 0].astype(jnp.int32)
    start = stats[:, N_EXPERTS:, 0].astype(jnp.int32)
    total = jnp.sum(cnt, axis=0)
    padded = (total + bm - 1) // bm * bm
    pad_end = jnp.cumsum(padded)
    pad_start = pad_end - padded
    row = pad_start[None, :] + jnp.cumsum(cnt, axis=0) - cnt
    used = jnp.sum(cnt, axis=1, keepdims=True)
    cnt = jnp.concatenate([cnt, LOC_ROWS - used], axis=1)
    start = jnp.concatenate([start, used], axis=1)
    parity = jnp.arange(cnt.shape[0], dtype=jnp.int32)[:, None] % 2
    row = jnp.concatenate([row, n_blocks * bm + parity * MOE_TILE], axis=1)
    blk_row = jnp.arange(n_blocks + 1, dtype=jnp.int32) * bm
    blk_expert = jnp.minimum(jnp.sum((pad_end[None, :] <= blk_row[:, None]).astype(jnp.int32), axis=1),
                             N_EXPERTS - 1)
    n_used = (pad_end[-1:] // bm).astype(jnp.int32)
    present = padded > 0
    ids = jnp.arange(N_EXPERTS, dtype=jnp.int32)
    later = jnp.where(present[None, :] & (ids[None, :] > ids[:, None]), ids[None, :], N_EXPERTS)
    nxt = jnp.min(later, axis=1)
    sched = dict(blk_expert=blk_expert, n_used=n_used,
                 next_expert=jnp.where(nxt < N_EXPERTS, nxt, -1).astype(jnp.int32),
                 slot=((jnp.cumsum(present) - present) % 2).astype(jnp.int32))
    plan = dict(cnt=cnt.reshape(-1), start=start.reshape(-1), row=row.reshape(-1).astype(jnp.int32),
                pad_end=pad_end.astype(jnp.int32), cnt_end=(pad_start + total).astype(jnp.int32))
    return plan, sched


def kernel(x, c, w_ada, b_ada, w_in, b_in, conv_w, conv_b, ml_norm_g, kv_norm_g, w_uk, w_uv,
           w_br_ml, w_br_dsa, w_br_fox, w_out, w_router, b_router, w1, b1, w2, b2, final_g):
    bsz, s, d = x.shape
    depth = w_in.shape[0]
    n = bsz * s
    bm = 2 * MOE_TILE
    n_blocks = -(-(n * TOP_K + (ROW_ALIGN - 1) * N_EXPERTS * (n // MOE_TILE)) // bm) + N_EXPERTS
    mods = _ada_mod(c, w_ada, b_ada).reshape(depth, bsz, 6, d)
    for l in range(depth):
        mod = mods[l]
        w_r, b_r = _rearranged_in_proj(w_in[l], b_in[l])
        proj, misc = _in_proj(x, mod, w_r, b_r)
        y_ml, fcol, frow = _mlstm(proj, misc, conv_w[l], conv_b[l], ml_norm_g[l])
        y_fox = _fox(proj, fcol, frow)
        y_dsa = _dsa(proj, misc, kv_norm_g[l], w_uk[l].astype(BF16), jnp.swapaxes(w_uv[l], 1, 2).astype(BF16))
        x = _merge(x, proj, y_ml, y_dsa, y_fox, mod, w_br_ml[l].astype(BF16), w_br_dsa[l].astype(BF16),
                   w_br_fox[l].astype(BF16), w_out[l].astype(BF16))
        h2, route, stats = _router(x, mod, w_router[l], b_router[l])
        plan, sched = _moe_plan(stats, bm, n_blocks)
        xs = _dispatch(plan, h2.reshape(n, d), route.reshape(n, LANE), (n_blocks + 1) * bm, bm)
        yb = _experts(sched, xs, w1, b1, w2, b2, l, bm)
        x = _combine(plan, yb, x, route, mod, final_g, final=(l == depth - 1))
    return x
```

```python
import functools

import jax
import jax.numpy as jnp
from jax import lax
from jax.experimental import pallas as pl
from jax.experimental.pallas import tpu as pltpu

F32 = jnp.float32
BF16 = jnp.bfloat16
HIGHEST = lax.Precision.HIGHEST

EPS = 1e-6
LOG2E = 1.4426950408889634
CHUNK = 64

ML_HEADS, ML_QK, ML_V, ML_CONV = 4, 64, 128, 4
DSA_HEADS, DSA_DIM, DSA_LATENT = 4, 128, 128
IDX_HEADS, IDX_DIM, DSA_TOPK = 4, 64, 256
FOX_HEADS, FOX_DIM = 4, 128
N_BRANCH = 3
N_EXPERTS, TOP_K = 32, 4
SWIGLU_LIMIT, SWIGLU_ALPHA = 7.0, 1.702

LANE = 128
ROW_ALIGN = 8
INT_MIN = -2 ** 31

MOE_TILE = 256
N_RUNS = N_EXPERTS + 1
LOC_ROWS = TOP_K * MOE_TILE + MOE_TILE
ZERO_ROWS = 128
COUNT_ROWS = 64

C_MLQK = 0
C_MLV = 512
C_MLO = 1024
C_DQ = 1536
C_DIQ = 2048
C_CKV = 2304
C_MISC = 2432
C_FOX = 2560
C_GATE = 4096
M_IK, M_MLI, M_MLF, M_IW, M_FXF = 0, 64, 68, 72, 76

VMEM_LIMIT = 56 * 1024 * 1024


def _dot(a, b, prec=None):
    return jnp.dot(a, b, preferred_element_type=F32, precision=prec)


def _dot_nt(a, b, prec=None):
    return lax.dot_general(a, b, (((1,), (1,)), ((), ())), preferred_element_type=F32, precision=prec)


def _dot_tn(a, b):
    return lax.dot_general(a, b, (((0,), (0,)), ((), ())), preferred_element_type=F32)


def _sigmoid(x):
    return 1.0 / (1.0 + jnp.exp(-x))


def _log_sigmoid(x):
    return jnp.minimum(x, 0.0) - jnp.log1p(jnp.exp(-jnp.abs(x)))


def _rms(x):
    return x * lax.rsqrt(jnp.mean(x * x, axis=-1, keepdims=True) + EPS)


def _col_reduce(op, x):
    rows = x.shape[0]
    if rows > COUNT_ROWS and rows % COUNT_ROWS == 0:
        x = op(x.reshape(rows // COUNT_ROWS, COUNT_ROWS, x.shape[1]), axis=0)
    return op(x, axis=0, keepdims=True)


def _params(sem, vmem=VMEM_LIMIT):
    return pltpu.CompilerParams(dimension_semantics=sem, vmem_limit_bytes=vmem)


def _ada_kernel(c_ref, w_ref, b_ref, o_ref):
    c = c_ref[...]
    o_ref[0] = _dot(c * _sigmoid(c), w_ref[0], HIGHEST) + b_ref[0]


def _ada_mod(c, w_ada, b_ada):
    depth, d, n = w_ada.shape
    bsz = c.shape[0]
    tn = 1536
    return pl.pallas_call(
        _ada_kernel,
        out_shape=jax.ShapeDtypeStruct((depth, bsz, n), F32),
        grid=(depth, n // tn),
        in_specs=[pl.BlockSpec((bsz, d), lambda l, j: (0, 0)),
                  pl.BlockSpec((1, d, tn), lambda l, j: (l, 0, j)),
                  pl.BlockSpec((1, 1, tn), lambda l, j: (l, 0, j))],
        out_specs=pl.BlockSpec((1, bsz, tn), lambda l, j: (l, 0, j)),
        compiler_params=_params(("parallel", "parallel")),
        name="ada_mod",
    )(c, w_ada, b_ada.reshape(depth, 1, n))


def _inproj_kernel(x_ref, mod_ref, w_ref, b_ref, o_ref, misc_ref, h_scr, *, tn):
    j = pl.program_id(2)

    @pl.when(j == 0)
    def _():
        h = _rms(x_ref[0]) * (1.0 + mod_ref[0, 1:2, :]) + mod_ref[0, 0:1, :]
        h_scr[...] = h.astype(BF16)

    acc = _dot(h_scr[...], w_ref[...]) + b_ref[...]
    o_ref[0] = acc.astype(BF16)

    @pl.when(j == C_MISC // tn)
    def _():
        misc_ref[0] = acc[:, C_MISC % tn:C_MISC % tn + LANE]


def _in_proj(x, mod, w, b):
    bsz, s, d = x.shape
    n = w.shape[1]
    tm = min(2048, s)
    tn = 1024
    return pl.pallas_call(
        functools.partial(_inproj_kernel, tn=tn),
        out_shape=(jax.ShapeDtypeStruct((bsz, s, n), BF16), jax.ShapeDtypeStruct((bsz, s, LANE), F32)),
        grid=(bsz, s // tm, n // tn),
        in_specs=[pl.BlockSpec((1, tm, d), lambda bi, i, j: (bi, i, 0)),
                  pl.BlockSpec((1, 6, d), lambda bi, i, j: (bi, 0, 0)),
                  pl.BlockSpec((d, tn), lambda bi, i, j: (0, j)),
                  pl.BlockSpec((1, tn), lambda bi, i, j: (0, j))],
        out_specs=(pl.BlockSpec((1, tm, tn), lambda bi, i, j: (bi, i, j)),
                   pl.BlockSpec((1, tm, LANE), lambda bi, i, j: (bi, i, 0))),
        scratch_shapes=[pltpu.VMEM((tm, d), BF16)],
        compiler_params=_params(("parallel", "parallel", "arbitrary")),
        name="in_proj",
    )(x, mod, w, b)


def _mlstm_kernel(qk_ref, v_ref, o_ref, misc_ref, cw_ref, cb_ref, g_ref,
                  y_ref, fcol_ref, frow_ref,
                  xext, ct_scr, n_scr, m_scr, carry_scr, *, L):
    c = pl.program_id(1)
    nqk = ML_HEADS * ML_QK

    @pl.when(c == 0)
    def _():
        xext[0:8, :] = jnp.zeros((8, 2 * nqk), F32)
        ct_scr[...] = jnp.zeros(ct_scr.shape, F32)
        n_scr[...] = jnp.zeros(n_scr.shape, F32)
        m_scr[...] = jnp.full(m_scr.shape, -jnp.inf, F32)
        carry_scr[...] = jnp.zeros(carry_scr.shape, F32)

    @pl.when(c > 0)
    def _():
        xext[0:8, :] = xext[L:L + 8, :]

    xext[8:8 + L, :] = qk_ref[0].astype(F32)
    cw = cw_ref[...]
    conv = (cb_ref[...] + cw[3:4, :] * xext[8:8 + L, :] + cw[2:3, :] * xext[7:7 + L, :]
            + cw[1:2, :] * xext[6:6 + L, :] + cw[0:1, :] * xext[5:5 + L, :])
    qk = conv * _sigmoid(conv)

    misc = misc_ref[0]
    ls = _log_sigmoid(misc)
    row = lax.broadcasted_iota(jnp.int32, (L, L), 0)
    col = lax.broadcasted_iota(jnp.int32, (L, L), 1)
    causal = row >= col
    tri = jnp.where(causal, 1.0, 0.0).astype(F32)
    cs = _dot(tri, ls, HIGHEST)
    cs_t = cs.T
    misc_t = misc.T
    carry = carry_scr[0:1, :]
    fcol_ref[0] = cs + carry
    for h in range(FOX_HEADS):
        frow_ref[0, h:h + 1, :] = cs_t[M_FXF + h:M_FXF + h + 1, :] + carry[:, M_FXF + h:M_FXF + h + 1]
    frow_ref[0, 4:8, :] = jnp.zeros((4, L), F32)
    carry_scr[0:1, :] = carry + cs[L - 1:L, :]

    for h in range(ML_HEADS):
        qh = qk[:, h * ML_QK:(h + 1) * ML_QK] * (ML_QK ** -0.5)
        kh = qk[:, nqk + h * ML_QK:nqk + (h + 1) * ML_QK]
        vb = v_ref[0, :, h * ML_V:(h + 1) * ML_V].astype(BF16)
        i_col = misc[:, M_MLI + h:M_MLI + h + 1]
        i_row = misc_t[M_MLI + h:M_MLI + h + 1, :]
        b_col = cs[:, M_MLF + h:M_MLF + h + 1]
        b_row = cs_t[M_MLF + h:M_MLF + h + 1, :]
        b_last = b_col[L - 1:L, :]
        m_prev = m_scr[h:h + 1, 0:1]

        d_log = jnp.where(causal, b_col - b_row + i_row, -jnp.inf)
        inter_log = b_col + m_prev
        m_out = jnp.maximum(inter_log, jnp.max(d_log, axis=-1, keepdims=True))
        qb = qh.astype(BF16)
        kb = kh.astype(BF16)
        s = _dot_nt(qb, kb) * jnp.exp(d_log - m_out)
        a_inter = jnp.exp(inter_log - m_out)
        ct = ct_scr[h]
        n_row = n_scr[h]
        num = _dot(s.astype(BF16), vb) + a_inter * _dot(qb, ct.astype(BF16))
        den = jnp.sum(s, axis=-1, keepdims=True) + a_inter * jnp.sum(qh * n_row, axis=-1, keepdims=True)
        hid = num / jnp.maximum(jnp.abs(den), jnp.exp(-m_out))

        w_state = b_last - b_col + i_col
        m_loc = jnp.max(w_state, axis=0, keepdims=True)
        ke = kh * jnp.exp(w_state - m_loc)
        c_loc = _dot_tn(ke.astype(BF16), vb)
        n_loc = jnp.sum(ke, axis=0, keepdims=True)
        m_new = jnp.maximum(b_last + m_prev, m_loc)
        decay = jnp.exp(b_last + m_prev - m_new)
        scale = jnp.exp(m_loc - m_new)
        ct_scr[h] = decay * ct + scale * c_loc
        n_scr[h] = decay * n_row + scale * n_loc
        m_scr[h:h + 1, :] = jnp.broadcast_to(m_new, (1, LANE))

        y = (_rms(hid) * g_ref[:, h * ML_V:(h + 1) * ML_V]
             * _sigmoid(o_ref[0, :, h * ML_V:(h + 1) * ML_V].astype(F32)))
        y_ref[0, :, h * ML_V:(h + 1) * ML_V] = y.astype(BF16)


def _mlstm(proj, misc, conv_w, conv_b, norm_g):
    bsz, s, _ = proj.shape
    L = min(256, s)
    w = 2 * ML_HEADS * ML_QK
    wv = ML_HEADS * ML_V
    return pl.pallas_call(
        functools.partial(_mlstm_kernel, L=L),
        out_shape=(jax.ShapeDtypeStruct((bsz, s, wv), BF16),
                   jax.ShapeDtypeStruct((bsz, s, LANE), F32),
                   jax.ShapeDtypeStruct((bsz, 8, s), F32)),
        grid=(bsz, s // L),
        in_specs=[pl.BlockSpec((1, L, w), lambda b, c: (b, c, C_MLQK // w)),
                  pl.BlockSpec((1, L, wv), lambda b, c: (b, c, C_MLV // wv)),
                  pl.BlockSpec((1, L, wv), lambda b, c: (b, c, C_MLO // wv)),
                  pl.BlockSpec((1, L, LANE), lambda b, c: (b, c, 0)),
                  pl.BlockSpec((ML_CONV, w), lambda b, c: (0, 0)),
                  pl.BlockSpec((1, w), lambda b, c: (0, 0)),
                  pl.BlockSpec((1, wv), lambda b, c: (0, 0))],
        out_specs=(pl.BlockSpec((1, L, wv), lambda b, c: (b, c, 0)),
                   pl.BlockSpec((1, L, LANE), lambda b, c: (b, c, 0)),
                   pl.BlockSpec((1, 8, L), lambda b, c: (b, 0, c))),
        scratch_shapes=[pltpu.VMEM((L + 8, w), F32),
                        pltpu.VMEM((ML_HEADS, ML_QK, ML_V), F32),
                        pltpu.VMEM((ML_HEADS, 1, ML_QK), F32),
                        pltpu.VMEM((8, LANE), F32),
                        pltpu.VMEM((8, LANE), F32)],
        compiler_params=_params(("parallel", "arbitrary")),
        name="mlstm",
    )(proj, proj, proj, misc, conv_w, conv_b.reshape(1, w), norm_g.reshape(1, wv))


def _fox_kernel(q_ref, k_ref, v_ref, fcol_ref, frow_ref, y_ref, k_scr, vt_scr, *, T, S):
    qi = pl.program_id(1)
    d = FOX_DIM

    @pl.when(qi == 0)
    def _():
        for h in range(FOX_HEADS):
            k_scr[h] = k_ref[0, :, h * d:(h + 1) * d].astype(BF16)
            vt_scr[h] = v_ref[0, :, h * d:(h + 1) * d].astype(F32).T.astype(BF16)

    q_t = (q_ref[0].astype(F32) * (d ** -0.5 * LOG2E)).T

    def body(ext):
        kpos = lax.broadcasted_iota(jnp.int32, (ext, 1), 0)
        qpos = (ext - T) + lax.broadcasted_iota(jnp.int32, (1, T), 1)
        causal = kpos <= qpos
        for h in range(FOX_HEADS):
            s = _dot(k_scr[h, 0:ext, :], q_t[h * d:(h + 1) * d, :].astype(BF16))
            s = s + (frow_ref[0, h:h + 1, :] * LOG2E - fcol_ref[0, 0:ext, M_FXF + h:M_FXF + h + 1] * LOG2E)
            s = jnp.where(causal, s, -jnp.inf)
            p = jnp.exp2(s - _col_reduce(jnp.max, s))
            l = _col_reduce(jnp.sum, p)
            out_t = _dot(vt_scr[h, :, 0:ext], p.astype(BF16)) / l
            y_ref[0, :, h * d:(h + 1) * d] = out_t.T.astype(BF16)

    for c in range(S // T):
        pl.when(qi == c)(functools.partial(body, (c + 1) * T))


def _fox(proj, fcol, frow):
    bsz, s, _ = proj.shape
    T = min(256, s)
    w = FOX_HEADS * FOX_DIM
    return pl.pallas_call(
        functools.partial(_fox_kernel, T=T, S=s),
        out_shape=jax.ShapeDtypeStruct((bsz, s, w), BF16),
        grid=(bsz, s // T),
        in_specs=[pl.BlockSpec((1, T, w), lambda b, i: (b, i, C_FOX // w)),
                  pl.BlockSpec((1, s, w), lambda b, i: (b, 0, C_FOX // w + 1)),
                  pl.BlockSpec((1, s, w), lambda b, i: (b, 0, C_FOX // w + 2)),
                  pl.BlockSpec((1, s, LANE), lambda b, i: (b, 0, 0)),
                  pl.BlockSpec((1, 8, T), lambda b, i: (b, 0, i))],
        out_specs=pl.BlockSpec((1, T, w), lambda b, i: (b, i, 0)),
        scratch_shapes=[pltpu.VMEM((FOX_HEADS, s, FOX_DIM), BF16),
                        pltpu.VMEM((FOX_HEADS, FOX_DIM, s), BF16)],
        compiler_params=_params(("parallel", "arbitrary")),
        name="fox_attention",
    )(proj, proj, proj, fcol, frow)


def _dsa_body(ext, qi, q_ref, qidx_ref, misc_q_ref, wuk_ref, wuv_ref, y_ref,
              ckvn_scr, ckvnt_scr, kidx_scr, sel_scr, *, T, n_sel, rank_tile):
    dh = DSA_DIM
    q_t = q_ref[0].astype(F32).T
    qidx_t = qidx_ref[0].astype(F32).T.astype(BF16)
    w_t = misc_q_ref[0].T[M_IW:M_IW + IDX_HEADS, :] * (IDX_HEADS ** -0.5)
    kidx = kidx_scr[0:ext, :]
    score = jnp.zeros((ext, T), F32)
    for h in range(IDX_HEADS):
        lg = _dot(kidx, qidx_t[h * IDX_DIM:(h + 1) * IDX_DIM, :]) * (IDX_DIM ** -0.5)
        score = score + w_t[h:h + 1, :] * jnp.maximum(lg, 0.0)
    kpos = lax.broadcasted_iota(jnp.int32, (ext, 1), 0)
    qchunk = (qi * T + lax.broadcasted_iota(jnp.int32, (1, T), 1)) // CHUNK
    score = jnp.where((kpos // CHUNK) <= qchunk, score, -jnp.inf)

    def as_float(c):
        return pltpu.bitcast(jnp.where(c < 0, c ^ jnp.int32(0x7FFFFFFF), c), F32)

    def count_ge(c):
        return _col_reduce(jnp.sum, jnp.where(score >= as_float(c), 1.0, 0.0))

    t0 = jnp.where(count_ge(jnp.zeros((1, T), jnp.int32)) >= n_sel, jnp.int32(0), jnp.int32(INT_MIN))

    def bis(i, t):
        cand = t + jnp.left_shift(jnp.int32(1), jnp.int32(30) - i)
        return jnp.where(count_ge(cand) >= n_sel, cand, t)

    t = lax.fori_loop(0, 31, bis, t0)
    n_vis = ((qchunk + 1) * CHUNK).astype(F32)
    thr = jnp.where(n_vis <= n_sel, -3.0e38, as_float(t))
    need = n_sel - _col_reduce(jnp.sum, jnp.where(score > thr, 1.0, 0.0))
    r_i = lax.broadcasted_iota(jnp.int32, (rank_tile, rank_tile), 0)
    c_i = lax.broadcasted_iota(jnp.int32, (rank_tile, rank_tile), 1)
    lower = jnp.where(c_i < r_i, 1.0, 0.0).astype(BF16)
    carry = jnp.zeros((1, T), F32)
    for j in range(ext // rank_tile):
        sc = score[j * rank_tile:(j + 1) * rank_tile, :]
        eq = jnp.where(sc == thr, 1.0, 0.0)
        rank = _dot(lower, eq.astype(BF16)) + carry
        carry = carry + _col_reduce(jnp.sum, eq)
        sel_scr[j * rank_tile:(j + 1) * rank_tile, :] = jnp.where(
            sc > thr, 1.0, jnp.where(rank < need, eq, 0.0))

    sel = sel_scr[0:ext, :] > 0.5
    ckvn = ckvn_scr[0:ext, :]
    ckvn_t = ckvnt_scr[:, 0:ext]
    for h in range(DSA_HEADS):
        qa_t = _dot(wuk_ref[h], q_t[h * dh:(h + 1) * dh, :].astype(BF16)) * (dh ** -0.5 * LOG2E)
        lg = _dot(ckvn, qa_t.astype(BF16))
        lg = jnp.where(sel, lg, -jnp.inf)
        p = jnp.exp2(lg - _col_reduce(jnp.max, lg))
        l = _col_reduce(jnp.sum, p)
        lat_t = _dot(ckvn_t, p.astype(BF16)) / l
        out_t = _dot(wuv_ref[h], lat_t.astype(BF16))
        y_ref[0, :, h * dh:(h + 1) * dh] = out_t.T.astype(BF16)


def _dsa_kernel(q_ref, ckv_ref, qidx_ref, misc_all_ref, misc_q_ref, g_ref, wuk_ref, wuv_ref, y_ref,
                ckvn_scr, ckvnt_scr, kidx_scr, sel_scr, *, T, S, n_sel, n_cls, rank_tile):
    qi = pl.program_id(1)

    @pl.when(qi == 0)
    def _():
        ckvn = _rms(ckv_ref[0].astype(F32)) * g_ref[...]
        ckvn_scr[...] = ckvn.astype(BF16)
        ckvnt_scr[...] = ckvn.T.astype(BF16)
        kidx_scr[...] = misc_all_ref[0, :, M_IK:M_IK + IDX_DIM].astype(BF16)

    per = (S // T) // n_cls
    for c in range(n_cls):
        ext = (c + 1) * per * T

        @pl.when(qi // per == c)
        def _(ext=ext):
            _dsa_body(ext, qi, q_ref, qidx_ref, misc_q_ref, wuk_ref, wuv_ref, y_ref,
                      ckvn_scr, ckvnt_scr, kidx_scr, sel_scr, T=T, n_sel=n_sel, rank_tile=rank_tile)


def _dsa(proj, misc, kv_g, wuk, wuv_t):
    bsz, s, _ = proj.shape
    T = min(256, s)
    n_sel = min(DSA_TOPK, s // 4)
    n_cls = max(1, min(4, s // 512))
    rank_tile = 256
    w = DSA_HEADS * DSA_DIM
    wi = IDX_HEADS * IDX_DIM
    return pl.pallas_call(
        functools.partial(_dsa_kernel, T=T, S=s, n_sel=float(n_sel), n_cls=n_cls, rank_tile=rank_tile),
        out_shape=jax.ShapeDtypeStruct((bsz, s, w), BF16),
        grid=(bsz, s // T),
        in_specs=[pl.BlockSpec((1, T, w), lambda b, i: (b, i, C_DQ // w)),
                  pl.BlockSpec((1, s, DSA_LATENT), lambda b, i: (b, 0, C_CKV // DSA_LATENT)),
                  pl.BlockSpec((1, T, wi), lambda b, i: (b, i, C_DIQ // wi)),
                  pl.BlockSpec((1, s, LANE), lambda b, i: (b, 0, 0)),
                  pl.BlockSpec((1, T, LANE), lambda b, i: (b, i, 0)),
                  pl.BlockSpec((1, DSA_LATENT), lambda b, i: (0, 0)),
                  pl.BlockSpec((DSA_HEADS, DSA_LATENT, DSA_DIM), lambda b, i: (0, 0, 0)),
                  pl.BlockSpec((DSA_HEADS, DSA_DIM, DSA_LATENT), lambda b, i: (0, 0, 0))],
        out_specs=pl.BlockSpec((1, T, w), lambda b, i: (b, i, 0)),
        scratch_shapes=[pltpu.VMEM((s, DSA_LATENT), BF16),
                        pltpu.VMEM((DSA_LATENT, s), BF16),
                        pltpu.VMEM((s, IDX_DIM), BF16),
                        pltpu.VMEM((s, T), F32)],
        compiler_params=_params(("parallel", "arbitrary")),
        name="dsa_attention",
    )(proj, proj, proj, misc, misc, kv_g.reshape(1, DSA_LATENT), wuk, wuv_t)


def _merge_kernel(yml_ref, ydsa_ref, yfox_ref, g0_ref, g1_ref, g2_ref, x_ref, mod_ref,
                  wml_ref, wdsa_ref, wfox_ref, wout_ref, o_ref):
    merged = (_sigmoid(g0_ref[0].astype(F32)) * _dot(yml_ref[0], wml_ref[...])
              + _sigmoid(g1_ref[0].astype(F32)) * _dot(ydsa_ref[0], wdsa_ref[...])
              + _sigmoid(g2_ref[0].astype(F32)) * _dot(yfox_ref[0], wfox_ref[...]))
    out = _dot(merged.astype(BF16), wout_ref[...])
    o_ref[0] = x_ref[0] + mod_ref[0, 2:3, :] * out


def _merge(x, proj, y_ml, y_dsa, y_fox, mod, w_ml, w_dsa, w_fox, w_out):
    bsz, s, d = x.shape
    tm = min(512, s)
    wb = y_ml.shape[-1]
    gb = C_GATE // d
    yspec = pl.BlockSpec((1, tm, wb), lambda b, i: (b, i, 0))
    wspec = pl.BlockSpec((wb, d), lambda b, i: (0, 0))
    return pl.pallas_call(
        _merge_kernel,
        out_shape=jax.ShapeDtypeStruct((bsz, s, d), F32),
        grid=(bsz, s // tm),
        in_specs=[yspec, yspec, yspec,
                  pl.BlockSpec((1, tm, d), lambda b, i: (b, i, gb)),
                  pl.BlockSpec((1, tm, d), lambda b, i: (b, i, gb + 1)),
                  pl.BlockSpec((1, tm, d), lambda b, i: (b, i, gb + 2)),
                  pl.BlockSpec((1, tm, d), lambda b, i: (b, i, 0)),
                  pl.BlockSpec((1, 6, d), lambda b, i: (b, 0, 0)),
                  wspec, wspec, wspec,
                  pl.BlockSpec((d, d), lambda b, i: (0, 0))],
        out_specs=pl.BlockSpec((1, tm, d), lambda b, i: (b, i, 0)),
        compiler_params=_params(("parallel", "parallel")),
        name="merge_out",
    )(y_ml, y_dsa, y_fox, proj, proj, proj, x, mod, w_ml, w_dsa, w_fox, w_out)


def _router_kernel(x_ref, mod_ref, wr_ref, br_ref, h_ref, route_ref, stat_ref, *, T):
    h = _rms(x_ref[0]) * (1.0 + mod_ref[0, 4:5, :]) + mod_ref[0, 3:4, :]
    h_ref[0] = h
    lg = _dot(h, wr_ref[...], HIGHEST) + br_ref[...]
    lg = lg.T[0:N_EXPERTS, :]
    sub = lax.broadcasted_iota(jnp.int32, (N_EXPERTS, 1), 0).astype(F32)
    vals, idxs, hots = [], [], []
    for _ in range(TOP_K):
        mx = jnp.max(lg, axis=0, keepdims=True)
        idx = jnp.min(jnp.where(lg == mx, sub, float(N_EXPERTS)), axis=0, keepdims=True)
        hot = sub == idx
        vals.append(mx)
        idxs.append(idx)
        hots.append(hot)
        lg = jnp.where(hot, -jnp.inf, lg)
    exps = [jnp.exp(v - vals[0]) for v in vals]
    tot = exps[0] + exps[1] + exps[2] + exps[3]
    multi = jnp.zeros((N_EXPERTS, T), F32)
    for hot in hots:
        multi = multi + jnp.where(hot, 1.0, 0.0)
    r_i = lax.broadcasted_iota(jnp.int32, (T, T), 0)
    c_i = lax.broadcasted_iota(jnp.int32, (T, T), 1)
    upper = jnp.where(r_i < c_i, 1.0, 0.0).astype(BF16)
    before = _dot(multi.astype(BF16), upper)
    cnt = jnp.broadcast_to(jnp.sum(multi, axis=1, keepdims=True), (N_EXPERTS, LANE))
    cnt = jnp.floor((cnt + (ROW_ALIGN - 1.0)) * (1.0 / ROW_ALIGN)) * ROW_ALIGN
    a_i = lax.broadcasted_iota(jnp.int32, (N_EXPERTS, N_EXPERTS), 0)
    b_i = lax.broadcasted_iota(jnp.int32, (N_EXPERTS, N_EXPERTS), 1)
    start = _dot(jnp.where(b_i < a_i, 1.0, 0.0).astype(F32), cnt, HIGHEST)
    stat_ref[0, 0:N_EXPERTS, :] = cnt
    stat_ref[0, N_EXPERTS:2 * N_EXPERTS, :] = start
    where_to = before + start[:, 0:1]
    rows = list(idxs)
    rows += [jnp.sum(jnp.where(hot, where_to, 0.0), axis=0, keepdims=True) for hot in hots]
    rows += [e / tot for e in exps]
    rows.append(jnp.zeros((LANE - 3 * TOP_K, T), F32))
    route_ref[0] = jnp.concatenate(rows, axis=0).T


def _router(x, mod, w_router, b_router):
    bsz, s, d = x.shape
    T = MOE_TILE
    nt = s // T
    ne = w_router.shape[1]
    wr = jnp.zeros((d, LANE), F32).at[:, :ne].set(w_router)
    br = jnp.full((1, LANE), -1e30, F32).at[0, :ne].set(b_router)
    return pl.pallas_call(
        functools.partial(_router_kernel, T=T),
        out_shape=(jax.ShapeDtypeStruct((bsz, s, d), F32),
                   jax.ShapeDtypeStruct((bsz, s, LANE), F32),
                   jax.ShapeDtypeStruct((bsz * nt, 2 * N_EXPERTS, LANE), F32)),
        grid=(bsz, nt),
        in_specs=[pl.BlockSpec((1, T, d), lambda b, i: (b, i, 0)),
                  pl.BlockSpec((1, 6, d), lambda b, i: (b, 0, 0)),
                  pl.BlockSpec((d, LANE), lambda b, i: (0, 0)),
                  pl.BlockSpec((1, LANE), lambda b, i: (0, 0))],
        out_specs=(pl.BlockSpec((1, T, d), lambda b, i: (b, i, 0)),
                   pl.BlockSpec((1, T, LANE), lambda b, i: (b, i, 0)),
                   pl.BlockSpec((1, 2 * N_EXPERTS, LANE), lambda b, i: (b * nt + i, 0, 0))),
        compiler_params=_params(("parallel", "parallel")),
        name="moe_router",
    )(x, mod, wr, br)


def _run_copies(tile, cnt_ref, start_ref, row_ref, make_copy):
    for r in range(N_RUNS):
        c = cnt_ref[tile * N_RUNS + r]
        src = start_ref[tile * N_RUNS + r]
        dst = row_ref[tile * N_RUNS + r]
        size = MOE_TILE
        while size >= ROW_ALIGN:
            done = c & (-2 * size)
            cp = make_copy(pl.multiple_of(src + done, ROW_ALIGN), pl.multiple_of(dst + done, ROW_ALIGN), size)
            pl.when((c & size) != 0)(cp.start)
            size //= 2


def _dispatch_kernel(cnt_ref, start_ref, row_ref, pad_end_ref, cnt_end_ref, h_ref, route_ref, xs_hbm,
                     loc0, loc1, zbuf, zsem, sem, *, BM, N_TAIL, N_T):
    t = pl.program_id(0)
    loc = (loc0, loc1)

    @pl.when(t == 0)
    def _():
        zbuf[...] = jnp.zeros(zbuf.shape, F32)
        n_rows = xs_hbm.shape[0]

        def pad_copy(e, g):
            start = pl.multiple_of(pad_end_ref[e] - g * ZERO_ROWS, ZERO_ROWS)
            return (start + ZERO_ROWS > cnt_end_ref[e],
                    pltpu.make_async_copy(zbuf.at[pl.ds(0, ZERO_ROWS)], xs_hbm.at[pl.ds(start, ZERO_ROWS)], zsem))

        def tail_copy(e):
            start = pl.multiple_of(pad_end_ref[N_EXPERTS - 1] + e * BM, BM)
            safe = pl.multiple_of(jnp.minimum(start, n_rows - BM), BM)
            return start < n_rows, pltpu.make_async_copy(zbuf, xs_hbm.at[pl.ds(safe, BM)], zsem)

        copies = [pad_copy(e, g) for e in range(N_EXPERTS) for g in range(1, BM // ZERO_ROWS + 1)]
        copies += [tail_copy(e) for e in range(N_TAIL)]
        for cond, cp in copies:
            pl.when(cond)(cp.start)
        for cond, cp in copies:
            pl.when(cond)(cp.wait)

    pos_t = route_ref[...].T[TOP_K:2 * TOP_K, :]
    slot = lax.broadcasted_iota(jnp.int32, (LOC_ROWS, 1), 0).astype(F32)
    pick = jnp.zeros((LOC_ROWS, MOE_TILE), F32)
    for k in range(TOP_K):
        pick = pick + jnp.where(slot == pos_t[k:k + 1, :], 1.0, 0.0)
    pick = pick.astype(BF16)

    def drain(s):
        pltpu.make_async_copy(loc[s], xs_hbm.at[pl.ds(0, LOC_ROWS)], sem.at[s]).wait()

    for s in range(2):
        @pl.when(t % 2 == s)
        def _(s=s):
            pl.when(t >= 2)(functools.partial(drain, s))
            loc[s][...] = _dot(pick, h_ref[...].astype(BF16))
            _run_copies(t, cnt_ref, start_ref, row_ref,
                        lambda a, b, n: pltpu.make_async_copy(loc[s].at[pl.ds(a, n)], xs_hbm.at[pl.ds(b, n)],
                                                              sem.at[s]))

    @pl.when(t == N_T - 1)
    def _():
        if N_T >= 2:
            drain(N_T % 2)
        drain((N_T - 1) % 2)


def _dispatch(plan, h2, route, n_rows, bm):
    n, d = h2.shape
    pairs = LOC_ROWS
    return pl.pallas_call(
        functools.partial(_dispatch_kernel, BM=bm, N_TAIL=n_rows // bm - (n * TOP_K) // bm, N_T=n // MOE_TILE),
        out_shape=jax.ShapeDtypeStruct((n_rows, d), F32),
        grid_spec=pltpu.PrefetchScalarGridSpec(
            num_scalar_prefetch=5,
            grid=(n // MOE_TILE,),
            in_specs=[pl.BlockSpec((MOE_TILE, d), lambda t, *_: (t, 0)),
                      pl.BlockSpec((MOE_TILE, LANE), lambda t, *_: (t, 0))],
            out_specs=pl.BlockSpec(memory_space=pl.ANY),
            scratch_shapes=[pltpu.VMEM((pairs, d), F32),
                            pltpu.VMEM((pairs, d), F32),
                            pltpu.VMEM((bm, d), F32),
                            pltpu.SemaphoreType.DMA(()),
                            pltpu.SemaphoreType.DMA((2,))]),
        compiler_params=pltpu.CompilerParams(dimension_semantics=("arbitrary",), vmem_limit_bytes=VMEM_LIMIT,
                                             disable_bounds_checks=True),
        name="moe_dispatch",
    )(plan["cnt"], plan["start"], plan["row"], plan["pad_end"], plan["cnt_end"], h2, route)


def _expert_kernel(be_ref, nu_ref, nxt_ref, slot_ref, rows_ref, xs_ref, w1_hbm, b1_ref, w2_hbm, b2_ref, o_ref,
                   w1f, w2f, w1b, w2b, sem, *, layer):
    i = pl.program_id(0)
    de = w2b.shape[0]
    e = be_ref[i]
    prev = be_ref[jnp.maximum(i - 1, 0)]
    fresh = (i == 0) | (e != prev)

    def copies(expert, s):
        return (pltpu.make_async_copy(w1_hbm.at[layer, expert], w1f.at[s], sem.at[s, 0]),
                pltpu.make_async_copy(w2_hbm.at[layer, expert], w2f.at[s], sem.at[s, 1]))

    @pl.when(i == 0)
    def _():
        for cp in copies(e, slot_ref[e]):
            cp.start()

    @pl.when(fresh & (i < nu_ref[0]))
    def _():
        s = slot_ref[e]
        for cp in copies(e, s):
            cp.wait()
        w1b[...] = w1f[s].astype(BF16)
        w2b[...] = w2f[s].astype(BF16)
        nxt = nxt_ref[e]

        @pl.when(nxt >= 0)
        def _():
            for cp in copies(nxt, 1 - s):
                cp.start()

    def ffn(x):
        hdn = _dot(x.astype(BF16), w1b[...]) + b1_ref[...]
        glu = jnp.minimum(hdn[:, :de], SWIGLU_LIMIT)
        lin = jnp.clip(hdn[:, de:], -SWIGLU_LIMIT, SWIGLU_LIMIT)
        act = glu * _sigmoid(SWIGLU_ALPHA * glu) * (lin + 1.0)
        return _dot(act.astype(BF16), w2b[...]) + b2_ref[...]

    half = o_ref.shape[0] // 2
    few = rows_ref[i] <= half

    @pl.when((i < nu_ref[0]) & jnp.logical_not(few))
    def _():
        o_ref[...] = ffn(xs_ref[...])

    @pl.when((i < nu_ref[0]) & few)
    def _():
        o_ref[0:half, :] = ffn(xs_ref[0:half, :])
        o_ref[half:, :] = jnp.zeros((half, o_ref.shape[1]), F32)

    @pl.when(i >= nu_ref[0])
    def _():
        o_ref[...] = jnp.zeros(o_ref.shape, F32)


def _experts(sched, xs, w1, b1, w2, b2, layer, bm):
    n_rows, d = xs.shape
    depth, ne, _, dh2 = w1.shape
    de = w2.shape[2]
    n_blocks = n_rows // bm

    def row_map(i, be, nu, *_):
        return (jnp.minimum(i, nu[0] - 1), 0)

    def b_map(i, be, *_):
        return (layer, be[i], 0, 0)

    return pl.pallas_call(
        functools.partial(_expert_kernel, layer=layer),
        out_shape=jax.ShapeDtypeStruct((n_rows, d), F32),
        grid_spec=pltpu.PrefetchScalarGridSpec(
            num_scalar_prefetch=5,
            grid=(n_blocks,),
            in_specs=[pl.BlockSpec((bm, d), row_map),
                      pl.BlockSpec(memory_space=pl.ANY),
                      pl.BlockSpec((None, None, 1, dh2), b_map),
                      pl.BlockSpec(memory_space=pl.ANY),
                      pl.BlockSpec((None, None, 1, d), b_map)],
            out_specs=pl.BlockSpec((bm, d), lambda i, *_: (i, 0)),
            scratch_shapes=[pltpu.VMEM((2, d, dh2), F32), pltpu.VMEM((2, de, d), F32),
                            pltpu.VMEM((d, dh2), BF16), pltpu.VMEM((de, d), BF16),
                            pltpu.SemaphoreType.DMA((2, 2))]),
        compiler_params=_params(("arbitrary",)),
        name="moe_experts",
    )(sched["blk_expert"], sched["n_used"], sched["next_expert"], sched["slot"], sched["blk_rows"], xs,
      w1, b1.reshape(depth, ne, 1, dh2), w2, b2.reshape(depth, ne, 1, d))


def _combine_kernel(cnt_ref, start_ref, row_ref, yb_hbm, x_ref, route_ref, mod_ref, fg_ref, o_ref,
                    loc0, loc1, sem, *, nt, n_tiles, final):
    t = pl.program_id(0) * nt + pl.program_id(1)
    pairs = LOC_ROWS
    loc = (loc0, loc1)

    def fetch(tile, s):
        _run_copies(tile, cnt_ref, start_ref, row_ref,
                    lambda a, b, n: pltpu.make_async_copy(yb_hbm.at[pl.ds(b, n)], loc[s].at[pl.ds(a, n)],
                                                          sem.at[s]))

    pl.when(t == 0)(functools.partial(fetch, 0, 0))
    route = route_ref[0]
    slot = lax.broadcasted_iota(jnp.int32, (1, pairs), 1).astype(F32)
    wgt = jnp.zeros((MOE_TILE, pairs), F32)
    for k in range(TOP_K):
        wgt = wgt + jnp.where(slot == route[:, TOP_K + k:TOP_K + k + 1],
                              route[:, 2 * TOP_K + k:2 * TOP_K + k + 1], 0.0)
    hi = wgt.astype(BF16)
    lo = (wgt - hi.astype(F32)).astype(BF16)

    for s in range(2):
        @pl.when(t % 2 == s)
        def _(s=s):
            pl.when(t + 1 < n_tiles)(functools.partial(fetch, t + 1, 1 - s))
            pltpu.make_async_copy(yb_hbm.at[pl.ds(0, pairs)], loc[s], sem.at[s]).wait()
            rows = loc[s][...].astype(BF16)
            y = _dot(hi, rows) + _dot(lo, rows)
            out = x_ref[0] + mod_ref[0, 5:6, :] * y
            if final:
                out = _rms(out) * fg_ref[...]
            o_ref[0] = out


def _combine(plan, yb, x, route, mod, final_g, final):
    bsz, s, d = x.shape
    nt = s // MOE_TILE
    pairs = LOC_ROWS
    return pl.pallas_call(
        functools.partial(_combine_kernel, nt=nt, n_tiles=bsz * nt, final=final),
        out_shape=jax.ShapeDtypeStruct((bsz, s, d), F32),
        grid_spec=pltpu.PrefetchScalarGridSpec(
            num_scalar_prefetch=3,
            grid=(bsz, nt),
            in_specs=[pl.BlockSpec(memory_space=pl.ANY),
                      pl.BlockSpec((1, MOE_TILE, d), lambda b, i, *_: (b, i, 0)),
                      pl.BlockSpec((1, MOE_TILE, LANE), lambda b, i, *_: (b, i, 0)),
                      pl.BlockSpec((1, 6, d), lambda b, i, *_: (b, 0, 0)),
                      pl.BlockSpec((1, d), lambda b, i, *_: (0, 0))],
            out_specs=pl.BlockSpec((1, MOE_TILE, d), lambda b, i, *_: (b, i, 0)),
            scratch_shapes=[pltpu.VMEM((pairs, d), F32), pltpu.VMEM((pairs, d), F32),
                            pltpu.SemaphoreType.DMA((2,))]),
        compiler_params=pltpu.CompilerParams(dimension_semantics=("arbitrary", "arbitrary"),
                                             vmem_limit_bytes=VMEM_LIMIT, disable_bounds_checks=True),
        name="moe_combine",
    )(plan["cnt"], plan["start"], plan["row"], yb, x, route, mod, final_g.reshape(1, d))


def _rearranged_in_proj(w_in, b_in):
    sizes = (ML_HEADS * ML_QK, ML_HEADS * ML_QK, ML_HEADS * ML_V, ML_HEADS * ML_V, ML_HEADS, ML_HEADS,
             DSA_HEADS * DSA_DIM, DSA_LATENT, IDX_HEADS * IDX_DIM, IDX_DIM, IDX_HEADS,
             3 * FOX_HEADS * FOX_DIM, FOX_HEADS, N_BRANCH * w_in.shape[0])
    offs = [0]
    for sz in sizes:
        offs.append(offs[-1] + sz)
    (o_mq, o_mk, o_mv, o_mo, o_mi, o_mf, o_dq, o_ckv, o_iq, o_ik, o_iw, o_fx, o_ff, o_g, o_end) = offs
    pad = LANE - (IDX_DIM + 2 * ML_HEADS + IDX_HEADS + FOX_HEADS)

    def cols(a):
        parts = [a[..., o_mq:o_mi],
                 a[..., o_dq:o_ckv],
                 a[..., o_iq:o_ik],
                 a[..., o_ckv:o_iq],
                 a[..., o_ik:o_iw],
                 a[..., o_mi:o_dq],
                 a[..., o_iw:o_fx],
                 a[..., o_ff:o_g],
                 jnp.zeros(a.shape[:-1] + (pad,), a.dtype),
                 a[..., o_fx:o_ff],
                 a[..., o_g:o_end]]
        return jnp.concatenate(parts, axis=-1)

    return cols(w_in).astype(BF16), cols(b_in.reshape(1, -1))


def _moe_plan(stats, bm, n_blocks):
    cnt = stats[:, :N_EXPERTS, 0].astype(jnp.int32)
    start = stats[:, N_EXPERTS:, 0].astype(jnp.int32)
    total = jnp.sum(cnt, axis=0)
    padded = (total + bm - 1) // bm * bm
    pad_end = jnp.cumsum(padded)
    pad_start = pad_end - padded
    row = pad_start[None, :] + jnp.cumsum(cnt, axis=0) - cnt
    used = jnp.sum(cnt, axis=1, keepdims=True)
    cnt = jnp.concatenate([cnt, LOC_ROWS - used], axis=1)
    start = jnp.concatenate([start, used], axis=1)
    parity = jnp.arange(cnt.shape[0], dtype=jnp.int32)[:, None] % 2
    row = jnp.concatenate([row, n_blocks * bm + parity * MOE_TILE], axis=1)
    blk_row = jnp.arange(n_blocks + 1, dtype=jnp.int32) * bm
    blk_expert = jnp.minimum(jnp.sum((pad_end[None, :] <= blk_row[:, None]).astype(jnp.int32), axis=1),
                             N_EXPERTS - 1)
    n_used = (pad_end[-1:] // bm).astype(jnp.int32)
    present = padded > 0
    ids = jnp.arange(N_EXPERTS, dtype=jnp.int32)
    later = jnp.where(present[None, :] & (ids[None, :] > ids[:, None]), ids[None, :], N_EXPERTS)
    nxt = jnp.min(later, axis=1)
    blk_rows = jnp.clip((pad_start + total)[blk_expert] - blk_row, 0, bm).astype(jnp.int32)
    sched = dict(blk_expert=blk_expert, n_used=n_used, blk_rows=blk_rows,
                 next_expert=jnp.where(nxt < N_EXPERTS, nxt, -1).astype(jnp.int32),
                 slot=((jnp.cumsum(present) - present) % 2).astype(jnp.int32))
    plan = dict(cnt=cnt.reshape(-1), start=start.reshape(-1), row=row.reshape(-1).astype(jnp.int32),
                pad_end=pad_end.astype(jnp.int32), cnt_end=(pad_start + total).astype(jnp.int32))
    return plan, sched


def kernel(x, c, w_ada, b_ada, w_in, b_in, conv_w, conv_b, ml_norm_g, kv_norm_g, w_uk, w_uv,
           w_br_ml, w_br_dsa, w_br_fox, w_out, w_router, b_router, w1, b1, w2, b2, final_g):
    bsz, s, d = x.shape
    depth = w_in.shape[0]
    n = bsz * s
    bm = 2 * MOE_TILE
    n_blocks = -(-(n * TOP_K + (ROW_ALIGN - 1) * N_EXPERTS * (n // MOE_TILE)) // bm) + N_EXPERTS
    mods = _ada_mod(c, w_ada, b_ada).reshape(depth, bsz, 6, d)
    for l in range(depth):
        mod = mods[l]
        w_r, b_r = _rearranged_in_proj(w_in[l], b_in[l])
        proj, misc = _in_proj(x, mod, w_r, b_r)
        y_ml, fcol, frow = _mlstm(proj, misc, conv_w[l], conv_b[l], ml_norm_g[l])
        y_fox = _fox(proj, fcol, frow)
        y_dsa = _dsa(proj, misc, kv_norm_g[l], w_uk[l].astype(BF16), jnp.swapaxes(w_uv[l], 1, 2).astype(BF16))
        x = _merge(x, proj, y_ml, y_dsa, y_fox, mod, w_br_ml[l].astype(BF16), w_br_dsa[l].astype(BF16),
                   w_br_fox[l].astype(BF16), w_out[l].astype(BF16))
        h2, route, stats = _router(x, mod, w_router[l], b_router[l])
        plan, sched = _moe_plan(stats, bm, n_blocks)
        xs = _dispatch(plan, h2.reshape(n, d), route.reshape(n, LANE), (n_blocks + 1) * bm, bm)
        yb = _experts(sched, xs, w1, b1, w2, b2, l, bm)
        x = _combine(plan, yb, x, route, mod, final_g, final=(l == depth - 1))
    return x
```

```python
import functools

import jax
import jax.numpy as jnp
from jax import lax
from jax.experimental import pallas as pl
from jax.experimental.pallas import tpu as pltpu

F32 = jnp.float32
BF16 = jnp.bfloat16
HIGHEST = lax.Precision.HIGHEST

EPS = 1e-6
LOG2E = 1.4426950408889634
CHUNK = 64

ML_HEADS, ML_QK, ML_V, ML_CONV = 4, 64, 128, 4
DSA_HEADS, DSA_DIM, DSA_LATENT = 4, 128, 128
IDX_HEADS, IDX_DIM, DSA_TOPK = 4, 64, 256
FOX_HEADS, FOX_DIM = 4, 128
N_BRANCH = 3
N_EXPERTS, TOP_K = 32, 4
SWIGLU_LIMIT, SWIGLU_ALPHA = 7.0, 1.702

LANE = 128
ROW_ALIGN = 8
INT_MIN = -2 ** 31

MOE_TILE = 256
N_RUNS = N_EXPERTS + 1
LOC_ROWS = TOP_K * MOE_TILE + MOE_TILE
ZERO_ROWS = 128
COUNT_ROWS = 64

C_MLQK = 0
C_MLV = 512
C_MLO = 1024
C_DQ = 1536
C_DIQ = 2048
C_CKV = 2304
C_MISC = 2432
C_FOX = 2560
C_GATE = 4096
M_IK, M_MLI, M_MLF, M_IW, M_FXF = 0, 64, 68, 72, 76

VMEM_LIMIT = 56 * 1024 * 1024


def _dot(a, b, prec=None):
    return jnp.dot(a, b, preferred_element_type=F32, precision=prec)


def _dot_nt(a, b, prec=None):
    return lax.dot_general(a, b, (((1,), (1,)), ((), ())), preferred_element_type=F32, precision=prec)


def _dot_tn(a, b):
    return lax.dot_general(a, b, (((0,), (0,)), ((), ())), preferred_element_type=F32)


def _sigmoid(x):
    return 1.0 / (1.0 + jnp.exp(-x))


def _log_sigmoid(x):
    return jnp.minimum(x, 0.0) - jnp.log1p(jnp.exp(-jnp.abs(x)))


def _rms(x):
    return x * lax.rsqrt(jnp.mean(x * x, axis=-1, keepdims=True) + EPS)


def _col_reduce(op, x):
    rows = x.shape[0]
    if rows > COUNT_ROWS and rows % COUNT_ROWS == 0:
        x = op(x.reshape(rows // COUNT_ROWS, COUNT_ROWS, x.shape[1]), axis=0)
    return op(x, axis=0, keepdims=True)


def _params(sem, vmem=VMEM_LIMIT):
    return pltpu.CompilerParams(dimension_semantics=sem, vmem_limit_bytes=vmem)


def _ada_kernel(c_ref, w_ref, b_ref, o_ref):
    c = c_ref[...]
    o_ref[0] = _dot(c * _sigmoid(c), w_ref[0], HIGHEST) + b_ref[0]


def _ada_mod(c, w_ada, b_ada):
    depth, d, n = w_ada.shape
    bsz = c.shape[0]
    tn = 1536
    return pl.pallas_call(
        _ada_kernel,
        out_shape=jax.ShapeDtypeStruct((depth, bsz, n), F32),
        grid=(depth, n // tn),
        in_specs=[pl.BlockSpec((bsz, d), lambda l, j: (0, 0)),
                  pl.BlockSpec((1, d, tn), lambda l, j: (l, 0, j)),
                  pl.BlockSpec((1, 1, tn), lambda l, j: (l, 0, j))],
        out_specs=pl.BlockSpec((1, bsz, tn), lambda l, j: (l, 0, j)),
        compiler_params=_params(("parallel", "parallel")),
        name="ada_mod",
    )(c, w_ada, b_ada.reshape(depth, 1, n))


def _inproj_kernel(x_ref, mod_ref, w_ref, b_ref, o_ref, misc_ref, h_scr, *, tn):
    j = pl.program_id(2)

    @pl.when(j == 0)
    def _():
        h = _rms(x_ref[0]) * (1.0 + mod_ref[0, 1:2, :]) + mod_ref[0, 0:1, :]
        h_scr[...] = h.astype(BF16)

    acc = _dot(h_scr[...], w_ref[...]) + b_ref[...]
    o_ref[0] = acc.astype(BF16)

    @pl.when(j == C_MISC // tn)
    def _():
        misc_ref[0] = acc[:, C_MISC % tn:C_MISC % tn + LANE]


def _in_proj(x, mod, w, b):
    bsz, s, d = x.shape
    n = w.shape[1]
    tm = min(2048, s)
    tn = 1024
    return pl.pallas_call(
        functools.partial(_inproj_kernel, tn=tn),
        out_shape=(jax.ShapeDtypeStruct((bsz, s, n), BF16), jax.ShapeDtypeStruct((bsz, s, LANE), F32)),
        grid=(bsz, s // tm, n // tn),
        in_specs=[pl.BlockSpec((1, tm, d), lambda bi, i, j: (bi, i, 0)),
                  pl.BlockSpec((1, 6, d), lambda bi, i, j: (bi, 0, 0)),
                  pl.BlockSpec((d, tn), lambda bi, i, j: (0, j)),
                  pl.BlockSpec((1, tn), lambda bi, i, j: (0, j))],
        out_specs=(pl.BlockSpec((1, tm, tn), lambda bi, i, j: (bi, i, j)),
                   pl.BlockSpec((1, tm, LANE), lambda bi, i, j: (bi, i, 0))),
        scratch_shapes=[pltpu.VMEM((tm, d), BF16)],
        compiler_params=_params(("parallel", "parallel", "arbitrary")),
        name="in_proj",
    )(x, mod, w, b)


def _mlstm_kernel(qk_ref, v_ref, o_ref, misc_ref, cw_ref, cb_ref, g_ref,
                  y_ref, fcol_ref, frow_ref,
                  xext, ct_scr, n_scr, m_scr, carry_scr, *, L):
    c = pl.program_id(1)
    nqk = ML_HEADS * ML_QK

    @pl.when(c == 0)
    def _():
        xext[0:8, :] = jnp.zeros((8, 2 * nqk), F32)
        ct_scr[...] = jnp.zeros(ct_scr.shape, F32)
        n_scr[...] = jnp.zeros(n_scr.shape, F32)
        m_scr[...] = jnp.full(m_scr.shape, -jnp.inf, F32)
        carry_scr[...] = jnp.zeros(carry_scr.shape, F32)

    @pl.when(c > 0)
    def _():
        xext[0:8, :] = xext[L:L + 8, :]

    xext[8:8 + L, :] = qk_ref[0].astype(F32)
    cw = cw_ref[...]
    conv = (cb_ref[...] + cw[3:4, :] * xext[8:8 + L, :] + cw[2:3, :] * xext[7:7 + L, :]
            + cw[1:2, :] * xext[6:6 + L, :] + cw[0:1, :] * xext[5:5 + L, :])
    qk = conv * _sigmoid(conv)

    misc = misc_ref[0]
    ls = _log_sigmoid(misc)
    row = lax.broadcasted_iota(jnp.int32, (L, L), 0)
    col = lax.broadcasted_iota(jnp.int32, (L, L), 1)
    causal = row >= col
    tri = jnp.where(causal, 1.0, 0.0).astype(F32)
    cs = _dot(tri, ls, HIGHEST)
    cs_t = cs.T
    misc_t = misc.T
    carry = carry_scr[0:1, :]
    fcol_ref[0] = cs + carry
    for h in range(FOX_HEADS):
        frow_ref[0, h:h + 1, :] = cs_t[M_FXF + h:M_FXF + h + 1, :] + carry[:, M_FXF + h:M_FXF + h + 1]
    frow_ref[0, 4:8, :] = jnp.zeros((4, L), F32)
    carry_scr[0:1, :] = carry + cs[L - 1:L, :]

    for h in range(ML_HEADS):
        qh = qk[:, h * ML_QK:(h + 1) * ML_QK] * (ML_QK ** -0.5)
        kh = qk[:, nqk + h * ML_QK:nqk + (h + 1) * ML_QK]
        vb = v_ref[0, :, h * ML_V:(h + 1) * ML_V].astype(BF16)
        i_col = misc[:, M_MLI + h:M_MLI + h + 1]
        i_row = misc_t[M_MLI + h:M_MLI + h + 1, :]
        b_col = cs[:, M_MLF + h:M_MLF + h + 1]
        b_row = cs_t[M_MLF + h:M_MLF + h + 1, :]
        b_last = b_col[L - 1:L, :]
        m_prev = m_scr[h:h + 1, 0:1]

        d_log = jnp.where(causal, b_col - b_row + i_row, -jnp.inf)
        inter_log = b_col + m_prev
        m_out = jnp.maximum(inter_log, jnp.max(d_log, axis=-1, keepdims=True))
        qb = qh.astype(BF16)
        kb = kh.astype(BF16)
        s = _dot_nt(qb, kb) * jnp.exp(d_log - m_out)
        a_inter = jnp.exp(inter_log - m_out)
        ct = ct_scr[h]
        n_row = n_scr[h]
        num = _dot(s.astype(BF16), vb) + a_inter * _dot(qb, ct.astype(BF16))
        den = jnp.sum(s, axis=-1, keepdims=True) + a_inter * jnp.sum(qh * n_row, axis=-1, keepdims=True)
        hid = num / jnp.maximum(jnp.abs(den), jnp.exp(-m_out))

        w_state = b_last - b_col + i_col
        m_loc = jnp.max(w_state, axis=0, keepdims=True)
        ke = kh * jnp.exp(w_state - m_loc)
        c_loc = _dot_tn(ke.astype(BF16), vb)
        n_loc = jnp.sum(ke, axis=0, keepdims=True)
        m_new = jnp.maximum(b_last + m_prev, m_loc)
        decay = jnp.exp(b_last + m_prev - m_new)
        scale = jnp.exp(m_loc - m_new)
        ct_scr[h] = decay * ct + scale * c_loc
        n_scr[h] = decay * n_row + scale * n_loc
        m_scr[h:h + 1, :] = jnp.broadcast_to(m_new, (1, LANE))

        y = (_rms(hid) * g_ref[:, h * ML_V:(h + 1) * ML_V]
             * _sigmoid(o_ref[0, :, h * ML_V:(h + 1) * ML_V].astype(F32)))
        y_ref[0, :, h * ML_V:(h + 1) * ML_V] = y.astype(BF16)


def _mlstm(proj, misc, conv_w, conv_b, norm_g):
    bsz, s, _ = proj.shape
    L = min(256, s)
    w = 2 * ML_HEADS * ML_QK
    wv = ML_HEADS * ML_V
    return pl.pallas_call(
        functools.partial(_mlstm_kernel, L=L),
        out_shape=(jax.ShapeDtypeStruct((bsz, s, wv), BF16),
                   jax.ShapeDtypeStruct((bsz, s, LANE), F32),
                   jax.ShapeDtypeStruct((bsz, 8, s), F32)),
        grid=(bsz, s // L),
        in_specs=[pl.BlockSpec((1, L, w), lambda b, c: (b, c, C_MLQK // w)),
                  pl.BlockSpec((1, L, wv), lambda b, c: (b, c, C_MLV // wv)),
                  pl.BlockSpec((1, L, wv), lambda b, c: (b, c, C_MLO // wv)),
                  pl.BlockSpec((1, L, LANE), lambda b, c: (b, c, 0)),
                  pl.BlockSpec((ML_CONV, w), lambda b, c: (0, 0)),
                  pl.BlockSpec((1, w), lambda b, c: (0, 0)),
                  pl.BlockSpec((1, wv), lambda b, c: (0, 0))],
        out_specs=(pl.BlockSpec((1, L, wv), lambda b, c: (b, c, 0)),
                   pl.BlockSpec((1, L, LANE), lambda b, c: (b, c, 0)),
                   pl.BlockSpec((1, 8, L), lambda b, c: (b, 0, c))),
        scratch_shapes=[pltpu.VMEM((L + 8, w), F32),
                        pltpu.VMEM((ML_HEADS, ML_QK, ML_V), F32),
                        pltpu.VMEM((ML_HEADS, 1, ML_QK), F32),
                        pltpu.VMEM((8, LANE), F32),
                        pltpu.VMEM((8, LANE), F32)],
        compiler_params=_params(("parallel", "arbitrary")),
        name="mlstm",
    )(proj, proj, proj, misc, conv_w, conv_b.reshape(1, w), norm_g.reshape(1, wv))


def _fox_kernel(q_ref, k_ref, v_ref, fcol_ref, frow_ref, y_ref, k_scr, vt_scr, *, T, S):
    qi = pl.program_id(1)
    d = FOX_DIM

    @pl.when(qi == 0)
    def _():
        for h in range(FOX_HEADS):
            k_scr[h] = k_ref[0, :, h * d:(h + 1) * d].astype(BF16)
            vt_scr[h] = v_ref[0, :, h * d:(h + 1) * d].astype(F32).T.astype(BF16)

    q_t = (q_ref[0].astype(F32) * (d ** -0.5 * LOG2E)).T

    def body(ext):
        kpos = lax.broadcasted_iota(jnp.int32, (ext, 1), 0)
        qpos = (ext - T) + lax.broadcasted_iota(jnp.int32, (1, T), 1)
        causal = kpos <= qpos
        for h in range(FOX_HEADS):
            s = _dot(k_scr[h, 0:ext, :], q_t[h * d:(h + 1) * d, :].astype(BF16))
            s = s + (frow_ref[0, h:h + 1, :] * LOG2E - fcol_ref[0, 0:ext, M_FXF + h:M_FXF + h + 1] * LOG2E)
            s = jnp.where(causal, s, -jnp.inf)
            p = jnp.exp2(s - _col_reduce(jnp.max, s))
            l = _col_reduce(jnp.sum, p)
            out_t = _dot(vt_scr[h, :, 0:ext], p.astype(BF16)) / l
            y_ref[0, :, h * d:(h + 1) * d] = out_t.T.astype(BF16)

    for c in range(S // T):
        pl.when(qi == c)(functools.partial(body, (c + 1) * T))


def _fox(proj, fcol, frow):
    bsz, s, _ = proj.shape
    T = min(256, s)
    w = FOX_HEADS * FOX_DIM
    return pl.pallas_call(
        functools.partial(_fox_kernel, T=T, S=s),
        out_shape=jax.ShapeDtypeStruct((bsz, s, w), BF16),
        grid=(bsz, s // T),
        in_specs=[pl.BlockSpec((1, T, w), lambda b, i: (b, i, C_FOX // w)),
                  pl.BlockSpec((1, s, w), lambda b, i: (b, 0, C_FOX // w + 1)),
                  pl.BlockSpec((1, s, w), lambda b, i: (b, 0, C_FOX // w + 2)),
                  pl.BlockSpec((1, s, LANE), lambda b, i: (b, 0, 0)),
                  pl.BlockSpec((1, 8, T), lambda b, i: (b, 0, i))],
        out_specs=pl.BlockSpec((1, T, w), lambda b, i: (b, i, 0)),
        scratch_shapes=[pltpu.VMEM((FOX_HEADS, s, FOX_DIM), BF16),
                        pltpu.VMEM((FOX_HEADS, FOX_DIM, s), BF16)],
        compiler_params=_params(("parallel", "arbitrary")),
        name="fox_attention",
    )(proj, proj, proj, fcol, frow)


def _dsa_body(ext, qi, q_ref, qidx_ref, misc_q_ref, wuk_ref, wuv_ref, y_ref,
              ckvn_scr, ckvnt_scr, kidx_scr, sel_scr, *, T, n_sel, rank_tile):
    dh = DSA_DIM
    q_t = q_ref[0].astype(F32).T
    qidx_t = qidx_ref[0].astype(F32).T.astype(BF16)
    w_t = misc_q_ref[0].T[M_IW:M_IW + IDX_HEADS, :] * (IDX_HEADS ** -0.5)
    kidx = kidx_scr[0:ext, :]
    score = jnp.zeros((ext, T), F32)
    for h in range(IDX_HEADS):
        lg = _dot(kidx, qidx_t[h * IDX_DIM:(h + 1) * IDX_DIM, :]) * (IDX_DIM ** -0.5)
        score = score + w_t[h:h + 1, :] * jnp.maximum(lg, 0.0)
    kpos = lax.broadcasted_iota(jnp.int32, (ext, 1), 0)
    qchunk = (qi * T + lax.broadcasted_iota(jnp.int32, (1, T), 1)) // CHUNK
    score = jnp.where((kpos // CHUNK) <= qchunk, score, -jnp.inf)

    def as_float(c):
        return pltpu.bitcast(jnp.where(c < 0, c ^ jnp.int32(0x7FFFFFFF), c), F32)

    def count_ge(c):
        return _col_reduce(jnp.sum, jnp.where(score >= as_float(c), 1.0, 0.0))

    t0 = jnp.where(count_ge(jnp.zeros((1, T), jnp.int32)) >= n_sel, jnp.int32(0), jnp.int32(INT_MIN))

    def bis(i, t):
        cand = t + jnp.left_shift(jnp.int32(1), jnp.int32(30) - i)
        return jnp.where(count_ge(cand) >= n_sel, cand, t)

    t = lax.fori_loop(0, 31, bis, t0)
    n_vis = ((qchunk + 1) * CHUNK).astype(F32)
    thr = jnp.where(n_vis <= n_sel, -3.0e38, as_float(t))
    need = n_sel - _col_reduce(jnp.sum, jnp.where(score > thr, 1.0, 0.0))
    r_i = lax.broadcasted_iota(jnp.int32, (rank_tile, rank_tile), 0)
    c_i = lax.broadcasted_iota(jnp.int32, (rank_tile, rank_tile), 1)
    lower = jnp.where(c_i < r_i, 1.0, 0.0).astype(BF16)
    carry = jnp.zeros((1, T), F32)
    for j in range(ext // rank_tile):
        sc = score[j * rank_tile:(j + 1) * rank_tile, :]
        eq = jnp.where(sc == thr, 1.0, 0.0)
        rank = _dot(lower, eq.astype(BF16)) + carry
        carry = carry + _col_reduce(jnp.sum, eq)
        sel_scr[j * rank_tile:(j + 1) * rank_tile, :] = jnp.where(
            sc > thr, 1.0, jnp.where(rank < need, eq, 0.0))

    sel = sel_scr[0:ext, :] > 0.5
    ckvn = ckvn_scr[0:ext, :]
    ckvn_t = ckvnt_scr[:, 0:ext]
    for h in range(DSA_HEADS):
        qa_t = _dot(wuk_ref[h], q_t[h * dh:(h + 1) * dh, :].astype(BF16)) * (dh ** -0.5 * LOG2E)
        lg = _dot(ckvn, qa_t.astype(BF16))
        lg = jnp.where(sel, lg, -jnp.inf)
        p = jnp.exp2(lg - _col_reduce(jnp.max, lg))
        l = _col_reduce(jnp.sum, p)
        lat_t = _dot(ckvn_t, p.astype(BF16)) / l
        out_t = _dot(wuv_ref[h], lat_t.astype(BF16))
        y_ref[0, :, h * dh:(h + 1) * dh] = out_t.T.astype(BF16)


def _dsa_kernel(q_ref, ckv_ref, qidx_ref, misc_all_ref, misc_q_ref, g_ref, wuk_ref, wuv_ref, y_ref,
                ckvn_scr, ckvnt_scr, kidx_scr, sel_scr, *, T, S, n_sel, n_cls, rank_tile):
    qi = pl.program_id(1)

    @pl.when(qi == 0)
    def _():
        ckvn = _rms(ckv_ref[0].astype(F32)) * g_ref[...]
        ckvn_scr[...] = ckvn.astype(BF16)
        ckvnt_scr[...] = ckvn.T.astype(BF16)
        kidx_scr[...] = misc_all_ref[0, :, M_IK:M_IK + IDX_DIM].astype(BF16)

    per = (S // T) // n_cls
    for c in range(n_cls):
        ext = (c + 1) * per * T

        @pl.when(qi // per == c)
        def _(ext=ext):
            _dsa_body(ext, qi, q_ref, qidx_ref, misc_q_ref, wuk_ref, wuv_ref, y_ref,
                      ckvn_scr, ckvnt_scr, kidx_scr, sel_scr, T=T, n_sel=n_sel, rank_tile=rank_tile)


def _dsa(proj, misc, kv_g, wuk, wuv_t):
    bsz, s, _ = proj.shape
    T = min(256, s)
    n_sel = min(DSA_TOPK, s // 4)
    n_cls = max(1, min(4, s // 512))
    rank_tile = 256
    w = DSA_HEADS * DSA_DIM
    wi = IDX_HEADS * IDX_DIM
    return pl.pallas_call(
        functools.partial(_dsa_kernel, T=T, S=s, n_sel=float(n_sel), n_cls=n_cls, rank_tile=rank_tile),
        out_shape=jax.ShapeDtypeStruct((bsz, s, w), BF16),
        grid=(bsz, s // T),
        in_specs=[pl.BlockSpec((1, T, w), lambda b, i: (b, i, C_DQ // w)),
                  pl.BlockSpec((1, s, DSA_LATENT), lambda b, i: (b, 0, C_CKV // DSA_LATENT)),
                  pl.BlockSpec((1, T, wi), lambda b, i: (b, i, C_DIQ // wi)),
                  pl.BlockSpec((1, s, LANE), lambda b, i: (b, 0, 0)),
                  pl.BlockSpec((1, T, LANE), lambda b, i: (b, i, 0)),
                  pl.BlockSpec((1, DSA_LATENT), lambda b, i: (0, 0)),
                  pl.BlockSpec((DSA_HEADS, DSA_LATENT, DSA_DIM), lambda b, i: (0, 0, 0)),
                  pl.BlockSpec((DSA_HEADS, DSA_DIM, DSA_LATENT), lambda b, i: (0, 0, 0))],
        out_specs=pl.BlockSpec((1, T, w), lambda b, i: (b, i, 0)),
        scratch_shapes=[pltpu.VMEM((s, DSA_LATENT), BF16),
                        pltpu.VMEM((DSA_LATENT, s), BF16),
                        pltpu.VMEM((s, IDX_DIM), BF16),
                        pltpu.VMEM((s, T), F32)],
        compiler_params=_params(("parallel", "arbitrary")),
        name="dsa_attention",
    )(proj, proj, proj, misc, misc, kv_g.reshape(1, DSA_LATENT), wuk, wuv_t)


def _merge_kernel(yml_ref, ydsa_ref, yfox_ref, g0_ref, g1_ref, g2_ref, x_ref, mod_ref,
                  wml_ref, wdsa_ref, wfox_ref, wout_ref, o_ref):
    merged = (_sigmoid(g0_ref[0].astype(F32)) * _dot(yml_ref[0], wml_ref[...])
              + _sigmoid(g1_ref[0].astype(F32)) * _dot(ydsa_ref[0], wdsa_ref[...])
              + _sigmoid(g2_ref[0].astype(F32)) * _dot(yfox_ref[0], wfox_ref[...]))
    out = _dot(merged.astype(BF16), wout_ref[...])
    o_ref[0] = x_ref[0] + mod_ref[0, 2:3, :] * out


def _merge(x, proj, y_ml, y_dsa, y_fox, mod, w_ml, w_dsa, w_fox, w_out):
    bsz, s, d = x.shape
    tm = min(512, s)
    wb = y_ml.shape[-1]
    gb = C_GATE // d
    yspec = pl.BlockSpec((1, tm, wb), lambda b, i: (b, i, 0))
    wspec = pl.BlockSpec((wb, d), lambda b, i: (0, 0))
    return pl.pallas_call(
        _merge_kernel,
        out_shape=jax.ShapeDtypeStruct((bsz, s, d), F32),
        grid=(bsz, s // tm),
        in_specs=[yspec, yspec, yspec,
                  pl.BlockSpec((1, tm, d), lambda b, i: (b, i, gb)),
                  pl.BlockSpec((1, tm, d), lambda b, i: (b, i, gb + 1)),
                  pl.BlockSpec((1, tm, d), lambda b, i: (b, i, gb + 2)),
                  pl.BlockSpec((1, tm, d), lambda b, i: (b, i, 0)),
                  pl.BlockSpec((1, 6, d), lambda b, i: (b, 0, 0)),
                  wspec, wspec, wspec,
                  pl.BlockSpec((d, d), lambda b, i: (0, 0))],
        out_specs=pl.BlockSpec((1, tm, d), lambda b, i: (b, i, 0)),
        compiler_params=_params(("parallel", "parallel")),
        name="merge_out",
    )(y_ml, y_dsa, y_fox, proj, proj, proj, x, mod, w_ml, w_dsa, w_fox, w_out)


def _router_kernel(x_ref, mod_ref, wr_ref, br_ref, h_ref, route_ref, stat_ref, *, T):
    h = _rms(x_ref[0]) * (1.0 + mod_ref[0, 4:5, :]) + mod_ref[0, 3:4, :]
    h_ref[0] = h
    lg = _dot(h, wr_ref[...], HIGHEST) + br_ref[...]
    lg = lg.T[0:N_EXPERTS, :]
    sub = lax.broadcasted_iota(jnp.int32, (N_EXPERTS, 1), 0).astype(F32)
    vals, idxs, hots = [], [], []
    for _ in range(TOP_K):
        mx = jnp.max(lg, axis=0, keepdims=True)
        idx = jnp.min(jnp.where(lg == mx, sub, float(N_EXPERTS)), axis=0, keepdims=True)
        hot = sub == idx
        vals.append(mx)
        idxs.append(idx)
        hots.append(hot)
        lg = jnp.where(hot, -jnp.inf, lg)
    exps = [jnp.exp(v - vals[0]) for v in vals]
    tot = exps[0] + exps[1] + exps[2] + exps[3]
    multi = jnp.zeros((N_EXPERTS, T), F32)
    for hot in hots:
        multi = multi + jnp.where(hot, 1.0, 0.0)
    r_i = lax.broadcasted_iota(jnp.int32, (T, T), 0)
    c_i = lax.broadcasted_iota(jnp.int32, (T, T), 1)
    upper = jnp.where(r_i < c_i, 1.0, 0.0).astype(BF16)
    before = _dot(multi.astype(BF16), upper)
    cnt = jnp.broadcast_to(jnp.sum(multi, axis=1, keepdims=True), (N_EXPERTS, LANE))
    cnt = jnp.floor((cnt + (ROW_ALIGN - 1.0)) * (1.0 / ROW_ALIGN)) * ROW_ALIGN
    a_i = lax.broadcasted_iota(jnp.int32, (N_EXPERTS, N_EXPERTS), 0)
    b_i = lax.broadcasted_iota(jnp.int32, (N_EXPERTS, N_EXPERTS), 1)
    start = _dot(jnp.where(b_i < a_i, 1.0, 0.0).astype(F32), cnt, HIGHEST)
    stat_ref[0, 0:N_EXPERTS, :] = cnt
    stat_ref[0, N_EXPERTS:2 * N_EXPERTS, :] = start
    where_to = before + start[:, 0:1]
    rows = list(idxs)
    rows += [jnp.sum(jnp.where(hot, where_to, 0.0), axis=0, keepdims=True) for hot in hots]
    rows += [e / tot for e in exps]
    rows.append(jnp.zeros((LANE - 3 * TOP_K, T), F32))
    route_ref[0] = jnp.concatenate(rows, axis=0).T


def _router(x, mod, w_router, b_router):
    bsz, s, d = x.shape
    T = MOE_TILE
    nt = s // T
    ne = w_router.shape[1]
    wr = jnp.zeros((d, LANE), F32).at[:, :ne].set(w_router)
    br = jnp.full((1, LANE), -1e30, F32).at[0, :ne].set(b_router)
    return pl.pallas_call(
        functools.partial(_router_kernel, T=T),
        out_shape=(jax.ShapeDtypeStruct((bsz, s, d), F32),
                   jax.ShapeDtypeStruct((bsz, s, LANE), F32),
                   jax.ShapeDtypeStruct((bsz * nt, 2 * N_EXPERTS, LANE), F32)),
        grid=(bsz, nt),
        in_specs=[pl.BlockSpec((1, T, d), lambda b, i: (b, i, 0)),
                  pl.BlockSpec((1, 6, d), lambda b, i: (b, 0, 0)),
                  pl.BlockSpec((d, LANE), lambda b, i: (0, 0)),
                  pl.BlockSpec((1, LANE), lambda b, i: (0, 0))],
        out_specs=(pl.BlockSpec((1, T, d), lambda b, i: (b, i, 0)),
                   pl.BlockSpec((1, T, LANE), lambda b, i: (b, i, 0)),
                   pl.BlockSpec((1, 2 * N_EXPERTS, LANE), lambda b, i: (b * nt + i, 0, 0))),
        compiler_params=_params(("parallel", "parallel")),
        name="moe_router",
    )(x, mod, wr, br)


def _run_copies(tile, cnt_ref, start_ref, row_ref, make_copy):
    for r in range(N_RUNS):
        c = cnt_ref[tile * N_RUNS + r]
        src = start_ref[tile * N_RUNS + r]
        dst = row_ref[tile * N_RUNS + r]
        size = MOE_TILE
        while size >= ROW_ALIGN:
            done = c & (-2 * size)
            cp = make_copy(pl.multiple_of(src + done, ROW_ALIGN), pl.multiple_of(dst + done, ROW_ALIGN), size)
            pl.when((c & size) != 0)(cp.start)
            size //= 2


def _dispatch_kernel(cnt_ref, start_ref, row_ref, pad_end_ref, cnt_end_ref, h_ref, route_ref, xs_hbm,
                     loc0, loc1, zbuf, zsem, sem, *, BM, N_TAIL, N_T):
    t = pl.program_id(0)
    loc = (loc0, loc1)

    @pl.when(t == 0)
    def _():
        zbuf[...] = jnp.zeros(zbuf.shape, F32)
        n_rows = xs_hbm.shape[0]

        def pad_copy(e, g):
            start = pl.multiple_of(pad_end_ref[e] - g * ZERO_ROWS, ZERO_ROWS)
            return (start + ZERO_ROWS > cnt_end_ref[e],
                    pltpu.make_async_copy(zbuf.at[pl.ds(0, ZERO_ROWS)], xs_hbm.at[pl.ds(start, ZERO_ROWS)], zsem))

        def tail_copy(e):
            start = pl.multiple_of(pad_end_ref[N_EXPERTS - 1] + e * BM, BM)
            safe = pl.multiple_of(jnp.minimum(start, n_rows - BM), BM)
            return start < n_rows, pltpu.make_async_copy(zbuf, xs_hbm.at[pl.ds(safe, BM)], zsem)

        copies = [pad_copy(e, g) for e in range(N_EXPERTS) for g in range(1, BM // ZERO_ROWS + 1)]
        copies += [tail_copy(e) for e in range(N_TAIL)]
        for cond, cp in copies:
            pl.when(cond)(cp.start)
        for cond, cp in copies:
            pl.when(cond)(cp.wait)

    pos_t = route_ref[...].T[TOP_K:2 * TOP_K, :]
    slot = lax.broadcasted_iota(jnp.int32, (LOC_ROWS, 1), 0).astype(F32)
    pick = jnp.zeros((LOC_ROWS, MOE_TILE), F32)
    for k in range(TOP_K):
        pick = pick + jnp.where(slot == pos_t[k:k + 1, :], 1.0, 0.0)
    pick = pick.astype(BF16)

    def drain(s):
        pltpu.make_async_copy(loc[s], xs_hbm.at[pl.ds(0, LOC_ROWS)], sem.at[s]).wait()

    for s in range(2):
        @pl.when(t % 2 == s)
        def _(s=s):
            pl.when(t >= 2)(functools.partial(drain, s))
            loc[s][...] = _dot(pick, h_ref[...].astype(BF16))
            _run_copies(t, cnt_ref, start_ref, row_ref,
                        lambda a, b, n: pltpu.make_async_copy(loc[s].at[pl.ds(a, n)], xs_hbm.at[pl.ds(b, n)],
                                                              sem.at[s]))

    @pl.when(t == N_T - 1)
    def _():
        if N_T >= 2:
            drain(N_T % 2)
        drain((N_T - 1) % 2)


def _dispatch(plan, h2, route, n_rows, bm):
    n, d = h2.shape
    pairs = LOC_ROWS
    return pl.pallas_call(
        functools.partial(_dispatch_kernel, BM=bm, N_TAIL=n_rows // bm - (n * TOP_K) // bm, N_T=n // MOE_TILE),
        out_shape=jax.ShapeDtypeStruct((n_rows, d), F32),
        grid_spec=pltpu.PrefetchScalarGridSpec(
            num_scalar_prefetch=5,
            grid=(n // MOE_TILE,),
            in_specs=[pl.BlockSpec((MOE_TILE, d), lambda t, *_: (t, 0)),
                      pl.BlockSpec((MOE_TILE, LANE), lambda t, *_: (t, 0))],
            out_specs=pl.BlockSpec(memory_space=pl.ANY),
            scratch_shapes=[pltpu.VMEM((pairs, d), F32),
                            pltpu.VMEM((pairs, d), F32),
                            pltpu.VMEM((bm, d), F32),
                            pltpu.SemaphoreType.DMA(()),
                            pltpu.SemaphoreType.DMA((2,))]),
        compiler_params=pltpu.CompilerParams(dimension_semantics=("arbitrary",), vmem_limit_bytes=VMEM_LIMIT,
                                             disable_bounds_checks=True),
        name="moe_dispatch",
    )(plan["cnt"], plan["start"], plan["row"], plan["pad_end"], plan["cnt_end"], h2, route)


def _expert_kernel(be_ref, nu_ref, nxt_ref, slot_ref, rows_ref, xs_ref, w1_hbm, b1_ref, w2_hbm, b2_ref, o_ref,
                   w1f, w2f, w1b, w2b, sem, *, layer):
    i = pl.program_id(0)
    de = w2b.shape[0]
    e = be_ref[i]
    prev = be_ref[jnp.maximum(i - 1, 0)]
    fresh = (i == 0) | (e != prev)

    def copies(expert, s):
        return (pltpu.make_async_copy(w1_hbm.at[layer, expert], w1f.at[s], sem.at[s, 0]),
                pltpu.make_async_copy(w2_hbm.at[layer, expert], w2f.at[s], sem.at[s, 1]))

    @pl.when(i == 0)
    def _():
        for cp in copies(e, slot_ref[e]):
            cp.start()

    @pl.when(fresh & (i < nu_ref[0]))
    def _():
        s = slot_ref[e]
        for cp in copies(e, s):
            cp.wait()
        w1b[...] = w1f[s].astype(BF16)
        w2b[...] = w2f[s].astype(BF16)
        nxt = nxt_ref[e]

        @pl.when(nxt >= 0)
        def _():
            for cp in copies(nxt, 1 - s):
                cp.start()

    def ffn(x):
        hdn = _dot(x.astype(BF16), w1b[...]) + b1_ref[...]
        glu = jnp.minimum(hdn[:, :de], SWIGLU_LIMIT)
        lin = jnp.clip(hdn[:, de:], -SWIGLU_LIMIT, SWIGLU_LIMIT)
        act = glu * _sigmoid(SWIGLU_ALPHA * glu) * (lin + 1.0)
        return _dot(act.astype(BF16), w2b[...]) + b2_ref[...]

    half = o_ref.shape[0] // 2
    few = rows_ref[i] <= half

    @pl.when((i < nu_ref[0]) & jnp.logical_not(few))
    def _():
        o_ref[...] = ffn(xs_ref[...])

    @pl.when((i < nu_ref[0]) & few)
    def _():
        o_ref[0:half, :] = ffn(xs_ref[0:half, :])
        o_ref[half:, :] = jnp.zeros((half, o_ref.shape[1]), F32)

    @pl.when(i >= nu_ref[0])
    def _():
        o_ref[...] = jnp.zeros(o_ref.shape, F32)


def _experts(sched, xs, w1, b1, w2, b2, layer, bm):
    n_rows, d = xs.shape
    depth, ne, _, dh2 = w1.shape
    de = w2.shape[2]
    n_blocks = n_rows // bm

    def row_map(i, be, nu, *_):
        return (jnp.minimum(i, nu[0] - 1), 0)

    def b_map(i, be, *_):
        return (layer, be[i], 0, 0)

    return pl.pallas_call(
        functools.partial(_expert_kernel, layer=layer),
        out_shape=jax.ShapeDtypeStruct((n_rows, d), F32),
        grid_spec=pltpu.PrefetchScalarGridSpec(
            num_scalar_prefetch=5,
            grid=(n_blocks,),
            in_specs=[pl.BlockSpec((bm, d), row_map),
                      pl.BlockSpec(memory_space=pl.ANY),
                      pl.BlockSpec((None, None, 1, dh2), b_map),
                      pl.BlockSpec(memory_space=pl.ANY),
                      pl.BlockSpec((None, None, 1, d), b_map)],
            out_specs=pl.BlockSpec((bm, d), lambda i, *_: (i, 0)),
            scratch_shapes=[pltpu.VMEM((2, d, dh2), F32), pltpu.VMEM((2, de, d), F32),
                            pltpu.VMEM((d, dh2), BF16), pltpu.VMEM((de, d), BF16),
                            pltpu.SemaphoreType.DMA((2, 2))]),
        compiler_params=_params(("arbitrary",)),
        name="moe_experts",
    )(sched["blk_expert"], sched["n_used"], sched["next_expert"], sched["slot"], sched["blk_rows"], xs,
      w1, b1.reshape(depth, ne, 1, dh2), w2, b2.reshape(depth, ne, 1, d))


def _combine_kernel(cnt_ref, start_ref, row_ref, yb_hbm, x_ref, route_ref, mod_ref, fg_ref, o_ref,
                    loc0, loc1, sem, *, nt, n_tiles, final):
    t = pl.program_id(0) * nt + pl.program_id(1)
    pairs = LOC_ROWS
    loc = (loc0, loc1)

    def fetch(tile, s):
        _run_copies(tile, cnt_ref, start_ref, row_ref,
                    lambda a, b, n: pltpu.make_async_copy(yb_hbm.at[pl.ds(b, n)], loc[s].at[pl.ds(a, n)],
                                                          sem.at[s]))

    pl.when(t == 0)(functools.partial(fetch, 0, 0))
    route = route_ref[0]
    slot = lax.broadcasted_iota(jnp.int32, (1, pairs), 1).astype(F32)
    wgt = jnp.zeros((MOE_TILE, pairs), F32)
    for k in range(TOP_K):
        wgt = wgt + jnp.where(slot == route[:, TOP_K + k:TOP_K + k + 1],
                              route[:, 2 * TOP_K + k:2 * TOP_K + k + 1], 0.0)
    hi = wgt.astype(BF16)
    lo = (wgt - hi.astype(F32)).astype(BF16)

    for s in range(2):
        @pl.when(t % 2 == s)
        def _(s=s):
            pl.when(t + 1 < n_tiles)(functools.partial(fetch, t + 1, 1 - s))
            pltpu.make_async_copy(yb_hbm.at[pl.ds(0, pairs)], loc[s], sem.at[s]).wait()
            rows = loc[s][...].astype(BF16)
            y = _dot(hi, rows) + _dot(lo, rows)
            out = x_ref[0] + mod_ref[0, 5:6, :] * y
            if final:
                out = _rms(out) * fg_ref[...]
            o_ref[0] = out


def _combine(plan, yb, x, route, mod, final_g, final):
    bsz, s, d = x.shape
    nt = s // MOE_TILE
    pairs = LOC_ROWS
    return pl.pallas_call(
        functools.partial(_combine_kernel, nt=nt, n_tiles=bsz * nt, final=final),
        out_shape=jax.ShapeDtypeStruct((bsz, s, d), F32),
        grid_spec=pltpu.PrefetchScalarGridSpec(
            num_scalar_prefetch=3,
            grid=(bsz, nt),
            in_specs=[pl.BlockSpec(memory_space=pl.ANY),
                      pl.BlockSpec((1, MOE_TILE, d), lambda b, i, *_: (b, i, 0)),
                      pl.BlockSpec((1, MOE_TILE, LANE), lambda b, i, *_: (b, i, 0)),
                      pl.BlockSpec((1, 6, d), lambda b, i, *_: (b, 0, 0)),
                      pl.BlockSpec((1, d), lambda b, i, *_: (0, 0))],
            out_specs=pl.BlockSpec((1, MOE_TILE, d), lambda b, i, *_: (b, i, 0)),
            scratch_shapes=[pltpu.VMEM((pairs, d), F32), pltpu.VMEM((pairs, d), F32),
                            pltpu.SemaphoreType.DMA((2,))]),
        compiler_params=pltpu.CompilerParams(dimension_semantics=("arbitrary", "arbitrary"),
                                             vmem_limit_bytes=VMEM_LIMIT, disable_bounds_checks=True),
        name="moe_combine",
    )(plan["cnt"], plan["start"], plan["row"], yb, x, route, mod, final_g.reshape(1, d))


def _rearranged_in_proj(w_in, b_in):
    sizes = (ML_HEADS * ML_QK, ML_HEADS * ML_QK, ML_HEADS * ML_V, ML_HEADS * ML_V, ML_HEADS, ML_HEADS,
             DSA_HEADS * DSA_DIM, DSA_LATENT, IDX_HEADS * IDX_DIM, IDX_DIM, IDX_HEADS,
             3 * FOX_HEADS * FOX_DIM, FOX_HEADS, N_BRANCH * w_in.shape[0])
    offs = [0]
    for sz in sizes:
        offs.append(offs[-1] + sz)
    (o_mq, o_mk, o_mv, o_mo, o_mi, o_mf, o_dq, o_ckv, o_iq, o_ik, o_iw, o_fx, o_ff, o_g, o_end) = offs
    pad = LANE - (IDX_DIM + 2 * ML_HEADS + IDX_HEADS + FOX_HEADS)

    def cols(a):
        parts = [a[..., o_mq:o_mi],
                 a[..., o_dq:o_ckv],
                 a[..., o_iq:o_ik],
                 a[..., o_ckv:o_iq],
                 a[..., o_ik:o_iw],
                 a[..., o_mi:o_dq],
                 a[..., o_iw:o_fx],
                 a[..., o_ff:o_g],
                 jnp.zeros(a.shape[:-1] + (pad,), a.dtype),
                 a[..., o_fx:o_ff],
                 a[..., o_g:o_end]]
        return jnp.concatenate(parts, axis=-1)

    return cols(w_in).astype(BF16), cols(b_in.reshape(1, -1))


def _moe_plan(stats, bm, n_blocks):
    cnt = stats[:, :N_EXPERTS, 0].astype(jnp.int32)
    start = stats[:, N_EXPERTS:, 0].astype(jnp.int32)
    total = jnp.sum(cnt, axis=0)
    padded = (total + bm - 1) // bm * bm
    pad_end = jnp.cumsum(padded)
    pad_start = pad_end - padded
    row = pad_start[None, :] + jnp.cumsum(cnt, axis=0) - cnt
    used = jnp.sum(cnt, axis=1, keepdims=True)
    cnt = jnp.concatenate([cnt, LOC_ROWS - used], axis=1)
    start = jnp.concatenate([start, used], axis=1)
    parity = jnp.arange(cnt.shape[0], dtype=jnp.int32)[:, None] % 2
    row = jnp.concatenate([row, n_blocks * bm + parity * MOE_TILE], axis=1)
    blk_row = jnp.arange(n_blocks + 1, dtype=jnp.int32) * bm
    blk_expert = jnp.minimum(jnp.sum((pad_end[None, :] <= blk_row[:, None]).astype(jnp.int32), axis=1),
                             N_EXPERTS - 1)
    n_used = (pad_end[-1:] // bm).astype(jnp.int32)
    present = padded > 0
    ids = jnp.arange(N_EXPERTS, dtype=jnp.int32)
    later = jnp.where(present[None, :] & (ids[None, :] > ids[:, None]), ids[None, :], N_EXPERTS)
    nxt = jnp.min(later, axis=1)
    run_end = jnp.sum(jnp.where(ids[None, :] == blk_expert[:, None], (pad_start + total)[None, :], 0), axis=1)
    blk_rows = jnp.clip(run_end - blk_row, 0, bm).astype(jnp.int32)
    sched = dict(blk_expert=blk_expert, n_used=n_used, blk_rows=blk_rows,
                 next_expert=jnp.where(nxt < N_EXPERTS, nxt, -1).astype(jnp.int32),
                 slot=((jnp.cumsum(present) - present) % 2).astype(jnp.int32))
    plan = dict(cnt=cnt.reshape(-1), start=start.reshape(-1), row=row.reshape(-1).astype(jnp.int32),
                pad_end=pad_end.astype(jnp.int32), cnt_end=(pad_start + total).astype(jnp.int32))
    return plan, sched


def kernel(x, c, w_ada, b_ada, w_in, b_in, conv_w, conv_b, ml_norm_g, kv_norm_g, w_uk, w_uv,
           w_br_ml, w_br_dsa, w_br_fox, w_out, w_router, b_router, w1, b1, w2, b2, final_g):
    bsz, s, d = x.shape
    depth = w_in.shape[0]
    n = bsz * s
    bm = 2 * MOE_TILE
    n_blocks = -(-(n * TOP_K + (ROW_ALIGN - 1) * N_EXPERTS * (n // MOE_TILE)) // bm) + N_EXPERTS
    mods = _ada_mod(c, w_ada, b_ada).reshape(depth, bsz, 6, d)
    for l in range(depth):
        mod = mods[l]
        w_r, b_r = _rearranged_in_proj(w_in[l], b_in[l])
        proj, misc = _in_proj(x, mod, w_r, b_r)
        y_ml, fcol, frow = _mlstm(proj, misc, conv_w[l], conv_b[l], ml_norm_g[l])
        y_fox = _fox(proj, fcol, frow)
        y_dsa = _dsa(proj, misc, kv_norm_g[l], w_uk[l].astype(BF16), jnp.swapaxes(w_uv[l], 1, 2).astype(BF16))
        x = _merge(x, proj, y_ml, y_dsa, y_fox, mod, w_br_ml[l].astype(BF16), w_br_dsa[l].astype(BF16),
                   w_br_fox[l].astype(BF16), w_out[l].astype(BF16))
        h2, route, stats = _router(x, mod, w_router[l], b_router[l])
        plan, sched = _moe_plan(stats, bm, n_blocks)
        xs = _dispatch(plan, h2.reshape(n, d), route.reshape(n, LANE), (n_blocks + 1) * bm, bm)
        yb = _experts(sched, xs, w1, b1, w2, b2, l, bm)
        x = _combine(plan, yb, x, route, mod, final_g, final=(l == depth - 1))
    return x
```

```python
import functools

import jax
import jax.numpy as jnp
from jax import lax
from jax.experimental import pallas as pl
from jax.experimental.pallas import tpu as pltpu

F32 = jnp.float32
BF16 = jnp.bfloat16
HIGHEST = lax.Precision.HIGHEST

EPS = 1e-6
LOG2E = 1.4426950408889634
CHUNK = 64

ML_HEADS, ML_QK, ML_V, ML_CONV = 4, 64, 128, 4
DSA_HEADS, DSA_DIM, DSA_LATENT = 4, 128, 128
IDX_HEADS, IDX_DIM, DSA_TOPK = 4, 64, 256
FOX_HEADS, FOX_DIM = 4, 128
N_BRANCH = 3
N_EXPERTS, TOP_K = 32, 4
SWIGLU_LIMIT, SWIGLU_ALPHA = 7.0, 1.702

LANE = 128
ROW_ALIGN = 8
INT_MIN = -2 ** 31

MOE_TILE = 256
N_RUNS = N_EXPERTS + 1
LOC_ROWS = TOP_K * MOE_TILE + MOE_TILE
ZERO_ROWS = 128
COUNT_ROWS = 64

C_MLQK = 0
C_MLV = 512
C_MLO = 1024
C_DQ = 1536
C_DIQ = 2048
C_CKV = 2304
C_MISC = 2432
C_FOX = 2560
C_GATE = 4096
M_IK, M_MLI, M_MLF, M_IW, M_FXF = 0, 64, 68, 72, 76

VMEM_LIMIT = 56 * 1024 * 1024


def _dot(a, b, prec=None):
    return jnp.dot(a, b, preferred_element_type=F32, precision=prec)


def _dot_nt(a, b, prec=None):
    return lax.dot_general(a, b, (((1,), (1,)), ((), ())), preferred_element_type=F32, precision=prec)


def _dot_tn(a, b):
    return lax.dot_general(a, b, (((0,), (0,)), ((), ())), preferred_element_type=F32)


def _sigmoid(x):
    return 1.0 / (1.0 + jnp.exp(-x))


def _log_sigmoid(x):
    return jnp.minimum(x, 0.0) - jnp.log1p(jnp.exp(-jnp.abs(x)))


def _rms(x):
    return x * lax.rsqrt(jnp.mean(x * x, axis=-1, keepdims=True) + EPS)


def _col_reduce(op, x):
    rows = x.shape[0]
    if rows > COUNT_ROWS and rows % COUNT_ROWS == 0:
        x = op(x.reshape(rows // COUNT_ROWS, COUNT_ROWS, x.shape[1]), axis=0)
    return op(x, axis=0, keepdims=True)


def _params(sem, vmem=VMEM_LIMIT):
    return pltpu.CompilerParams(dimension_semantics=sem, vmem_limit_bytes=vmem)


def _ada_kernel(c_ref, w_ref, b_ref, o_ref):
    c = c_ref[...]
    o_ref[0] = _dot(c * _sigmoid(c), w_ref[0], HIGHEST) + b_ref[0]


def _ada_mod(c, w_ada, b_ada):
    depth, d, n = w_ada.shape
    bsz = c.shape[0]
    tn = 1536
    return pl.pallas_call(
        _ada_kernel,
        out_shape=jax.ShapeDtypeStruct((depth, bsz, n), F32),
        grid=(depth, n // tn),
        in_specs=[pl.BlockSpec((bsz, d), lambda l, j: (0, 0)),
                  pl.BlockSpec((1, d, tn), lambda l, j: (l, 0, j)),
                  pl.BlockSpec((1, 1, tn), lambda l, j: (l, 0, j))],
        out_specs=pl.BlockSpec((1, bsz, tn), lambda l, j: (l, 0, j)),
        compiler_params=_params(("parallel", "parallel")),
        name="ada_mod",
    )(c, w_ada, b_ada.reshape(depth, 1, n))


def _inproj_kernel(x_ref, mod_ref, w_ref, b_ref, o_ref, misc_ref, h_scr, *, tn):
    j = pl.program_id(2)

    @pl.when(j == 0)
    def _():
        h = _rms(x_ref[0]) * (1.0 + mod_ref[0, 1:2, :]) + mod_ref[0, 0:1, :]
        h_scr[...] = h.astype(BF16)

    acc = _dot(h_scr[...], w_ref[...]) + b_ref[...]
    o_ref[0] = acc.astype(BF16)

    @pl.when(j == C_MISC // tn)
    def _():
        misc_ref[0] = acc[:, C_MISC % tn:C_MISC % tn + LANE]


def _in_proj(x, mod, w, b):
    bsz, s, d = x.shape
    n = w.shape[1]
    tm = min(2048, s)
    tn = 1024
    return pl.pallas_call(
        functools.partial(_inproj_kernel, tn=tn),
        out_shape=(jax.ShapeDtypeStruct((bsz, s, n), BF16), jax.ShapeDtypeStruct((bsz, s, LANE), F32)),
        grid=(bsz, s // tm, n // tn),
        in_specs=[pl.BlockSpec((1, tm, d), lambda bi, i, j: (bi, i, 0)),
                  pl.BlockSpec((1, 6, d), lambda bi, i, j: (bi, 0, 0)),
                  pl.BlockSpec((d, tn), lambda bi, i, j: (0, j)),
                  pl.BlockSpec((1, tn), lambda bi, i, j: (0, j))],
        out_specs=(pl.BlockSpec((1, tm, tn), lambda bi, i, j: (bi, i, j)),
                   pl.BlockSpec((1, tm, LANE), lambda bi, i, j: (bi, i, 0))),
        scratch_shapes=[pltpu.VMEM((tm, d), BF16)],
        compiler_params=_params(("parallel", "parallel", "arbitrary")),
        name="in_proj",
    )(x, mod, w, b)


def _mlstm_kernel(qk_ref, v_ref, o_ref, misc_ref, cw_ref, cb_ref, g_ref,
                  y_ref, fcol_ref, frow_ref,
                  xext, ct_scr, n_scr, m_scr, carry_scr, *, L):
    c = pl.program_id(1)
    nqk = ML_HEADS * ML_QK

    @pl.when(c == 0)
    def _():
        xext[0:8, :] = jnp.zeros((8, 2 * nqk), F32)
        ct_scr[...] = jnp.zeros(ct_scr.shape, F32)
        n_scr[...] = jnp.zeros(n_scr.shape, F32)
        m_scr[...] = jnp.full(m_scr.shape, -jnp.inf, F32)
        carry_scr[...] = jnp.zeros(carry_scr.shape, F32)

    @pl.when(c > 0)
    def _():
        xext[0:8, :] = xext[L:L + 8, :]

    xext[8:8 + L, :] = qk_ref[0].astype(F32)
    cw = cw_ref[...]
    conv = (cb_ref[...] + cw[3:4, :] * xext[8:8 + L, :] + cw[2:3, :] * xext[7:7 + L, :]
            + cw[1:2, :] * xext[6:6 + L, :] + cw[0:1, :] * xext[5:5 + L, :])
    qk = conv * _sigmoid(conv)

    misc = misc_ref[0]
    ls = _log_sigmoid(misc)
    row = lax.broadcasted_iota(jnp.int32, (L, L), 0)
    col = lax.broadcasted_iota(jnp.int32, (L, L), 1)
    causal = row >= col
    tri = jnp.where(causal, 1.0, 0.0).astype(F32)
    cs = _dot(tri, ls, HIGHEST)
    cs_t = cs.T
    misc_t = misc.T
    carry = carry_scr[0:1, :]
    fcol_ref[0] = cs + carry
    for h in range(FOX_HEADS):
        frow_ref[0, h:h + 1, :] = cs_t[M_FXF + h:M_FXF + h + 1, :] + carry[:, M_FXF + h:M_FXF + h + 1]
    frow_ref[0, 4:8, :] = jnp.zeros((4, L), F32)
    carry_scr[0:1, :] = carry + cs[L - 1:L, :]

    for h in range(ML_HEADS):
        qh = qk[:, h * ML_QK:(h + 1) * ML_QK] * (ML_QK ** -0.5)
        kh = qk[:, nqk + h * ML_QK:nqk + (h + 1) * ML_QK]
        vb = v_ref[0, :, h * ML_V:(h + 1) * ML_V].astype(BF16)
        i_col = misc[:, M_MLI + h:M_MLI + h + 1]
        i_row = misc_t[M_MLI + h:M_MLI + h + 1, :]
        b_col = cs[:, M_MLF + h:M_MLF + h + 1]
        b_row = cs_t[M_MLF + h:M_MLF + h + 1, :]
        b_last = b_col[L - 1:L, :]
        m_prev = m_scr[h:h + 1, 0:1]

        d_log = jnp.where(causal, b_col - b_row + i_row, -jnp.inf)
        inter_log = b_col + m_prev
        m_out = jnp.maximum(inter_log, jnp.max(d_log, axis=-1, keepdims=True))
        qb = qh.astype(BF16)
        kb = kh.astype(BF16)
        s = _dot_nt(qb, kb) * jnp.exp(d_log - m_out)
        a_inter = jnp.exp(inter_log - m_out)
        ct = ct_scr[h]
        n_row = n_scr[h]
        num = _dot(s.astype(BF16), vb) + a_inter * _dot(qb, ct.astype(BF16))
        den = jnp.sum(s, axis=-1, keepdims=True) + a_inter * jnp.sum(qh * n_row, axis=-1, keepdims=True)
        hid = num / jnp.maximum(jnp.abs(den), jnp.exp(-m_out))

        w_state = b_last - b_col + i_col
        m_loc = jnp.max(w_state, axis=0, keepdims=True)
        ke = kh * jnp.exp(w_state - m_loc)
        c_loc = _dot_tn(ke.astype(BF16), vb)
        n_loc = jnp.sum(ke, axis=0, keepdims=True)
        m_new = jnp.maximum(b_last + m_prev, m_loc)
        decay = jnp.exp(b_last + m_prev - m_new)
        scale = jnp.exp(m_loc - m_new)
        ct_scr[h] = decay * ct + scale * c_loc
        n_scr[h] = decay * n_row + scale * n_loc
        m_scr[h:h + 1, :] = jnp.broadcast_to(m_new, (1, LANE))

        y = (_rms(hid) * g_ref[:, h * ML_V:(h + 1) * ML_V]
             * _sigmoid(o_ref[0, :, h * ML_V:(h + 1) * ML_V].astype(F32)))
        y_ref[0, :, h * ML_V:(h + 1) * ML_V] = y.astype(BF16)


def _mlstm(proj, misc, conv_w, conv_b, norm_g):
    bsz, s, _ = proj.shape
    L = min(256, s)
    w = 2 * ML_HEADS * ML_QK
    wv = ML_HEADS * ML_V
    return pl.pallas_call(
        functools.partial(_mlstm_kernel, L=L),
        out_shape=(jax.ShapeDtypeStruct((bsz, s, wv), BF16),
                   jax.ShapeDtypeStruct((bsz, s, LANE), F32),
                   jax.ShapeDtypeStruct((bsz, 8, s), F32)),
        grid=(bsz, s // L),
        in_specs=[pl.BlockSpec((1, L, w), lambda b, c: (b, c, C_MLQK // w)),
                  pl.BlockSpec((1, L, wv), lambda b, c: (b, c, C_MLV // wv)),
                  pl.BlockSpec((1, L, wv), lambda b, c: (b, c, C_MLO // wv)),
                  pl.BlockSpec((1, L, LANE), lambda b, c: (b, c, 0)),
                  pl.BlockSpec((ML_CONV, w), lambda b, c: (0, 0)),
                  pl.BlockSpec((1, w), lambda b, c: (0, 0)),
                  pl.BlockSpec((1, wv), lambda b, c: (0, 0))],
        out_specs=(pl.BlockSpec((1, L, wv), lambda b, c: (b, c, 0)),
                   pl.BlockSpec((1, L, LANE), lambda b, c: (b, c, 0)),
                   pl.BlockSpec((1, 8, L), lambda b, c: (b, 0, c))),
        scratch_shapes=[pltpu.VMEM((L + 8, w), F32),
                        pltpu.VMEM((ML_HEADS, ML_QK, ML_V), F32),
                        pltpu.VMEM((ML_HEADS, 1, ML_QK), F32),
                        pltpu.VMEM((8, LANE), F32),
                        pltpu.VMEM((8, LANE), F32)],
        compiler_params=_params(("parallel", "arbitrary")),
        name="mlstm",
    )(proj, proj, proj, misc, conv_w, conv_b.reshape(1, w), norm_g.reshape(1, wv))


def _fox_kernel(q_ref, k_ref, v_ref, fcol_ref, frow_ref, y_ref, k_scr, vt_scr, *, T, S):
    qi = pl.program_id(1)
    d = FOX_DIM

    @pl.when(qi == 0)
    def _():
        for h in range(FOX_HEADS):
            k_scr[h] = k_ref[0, :, h * d:(h + 1) * d].astype(BF16)
            vt_scr[h] = v_ref[0, :, h * d:(h + 1) * d].astype(F32).T.astype(BF16)

    q_t = (q_ref[0].astype(F32) * (d ** -0.5 * LOG2E)).T

    def body(ext):
        kpos = lax.broadcasted_iota(jnp.int32, (ext, 1), 0)
        qpos = (ext - T) + lax.broadcasted_iota(jnp.int32, (1, T), 1)
        causal = kpos <= qpos
        for h in range(FOX_HEADS):
            s = _dot(k_scr[h, 0:ext, :], q_t[h * d:(h + 1) * d, :].astype(BF16))
            s = s + (frow_ref[0, h:h + 1, :] * LOG2E - fcol_ref[0, 0:ext, M_FXF + h:M_FXF + h + 1] * LOG2E)
            s = jnp.where(causal, s, -jnp.inf)
            p = jnp.exp2(s - _col_reduce(jnp.max, s))
            l = _col_reduce(jnp.sum, p)
            out_t = _dot(vt_scr[h, :, 0:ext], p.astype(BF16)) / l
            y_ref[0, :, h * d:(h + 1) * d] = out_t.T.astype(BF16)

    for c in range(S // T):
        pl.when(qi == c)(functools.partial(body, (c + 1) * T))


def _fox(proj, fcol, frow):
    bsz, s, _ = proj.shape
    T = min(256, s)
    w = FOX_HEADS * FOX_DIM
    return pl.pallas_call(
        functools.partial(_fox_kernel, T=T, S=s),
        out_shape=jax.ShapeDtypeStruct((bsz, s, w), BF16),
        grid=(bsz, s // T),
        in_specs=[pl.BlockSpec((1, T, w), lambda b, i: (b, i, C_FOX // w)),
                  pl.BlockSpec((1, s, w), lambda b, i: (b, 0, C_FOX // w + 1)),
                  pl.BlockSpec((1, s, w), lambda b, i: (b, 0, C_FOX // w + 2)),
                  pl.BlockSpec((1, s, LANE), lambda b, i: (b, 0, 0)),
                  pl.BlockSpec((1, 8, T), lambda b, i: (b, 0, i))],
        out_specs=pl.BlockSpec((1, T, w), lambda b, i: (b, i, 0)),
        scratch_shapes=[pltpu.VMEM((FOX_HEADS, s, FOX_DIM), BF16),
                        pltpu.VMEM((FOX_HEADS, FOX_DIM, s), BF16)],
        compiler_params=_params(("parallel", "arbitrary")),
        name="fox_attention",
    )(proj, proj, proj, fcol, frow)


def _dsa_body(ext, qi, q_ref, qidx_ref, misc_q_ref, wuk_ref, wuv_ref, y_ref,
              ckvn_scr, ckvnt_scr, kidx_scr, sel_scr, *, T, n_sel, rank_tile):
    dh = DSA_DIM
    q_t = q_ref[0].astype(F32).T
    qidx_t = qidx_ref[0].astype(F32).T.astype(BF16)
    w_t = misc_q_ref[0].T[M_IW:M_IW + IDX_HEADS, :] * (IDX_HEADS ** -0.5)
    kidx = kidx_scr[0:ext, :]
    score = jnp.zeros((ext, T), F32)
    for h in range(IDX_HEADS):
        lg = _dot(kidx, qidx_t[h * IDX_DIM:(h + 1) * IDX_DIM, :]) * (IDX_DIM ** -0.5)
        score = score + w_t[h:h + 1, :] * jnp.maximum(lg, 0.0)
    kpos = lax.broadcasted_iota(jnp.int32, (ext, 1), 0)
    qchunk = (qi * T + lax.broadcasted_iota(jnp.int32, (1, T), 1)) // CHUNK
    score = jnp.where((kpos // CHUNK) <= qchunk, score, -jnp.inf)

    def as_float(c):
        return pltpu.bitcast(jnp.where(c < 0, c ^ jnp.int32(0x7FFFFFFF), c), F32)

    def count_ge(c):
        return _col_reduce(jnp.sum, jnp.where(score >= as_float(c), 1.0, 0.0))

    t0 = jnp.where(count_ge(jnp.zeros((1, T), jnp.int32)) >= n_sel, jnp.int32(0), jnp.int32(INT_MIN))

    def bis(i, t):
        cand = t + jnp.left_shift(jnp.int32(1), jnp.int32(30) - i)
        return jnp.where(count_ge(cand) >= n_sel, cand, t)

    t = lax.fori_loop(0, 31, bis, t0)
    n_vis = ((qchunk + 1) * CHUNK).astype(F32)
    thr = jnp.where(n_vis <= n_sel, -3.0e38, as_float(t))
    need = n_sel - _col_reduce(jnp.sum, jnp.where(score > thr, 1.0, 0.0))
    r_i = lax.broadcasted_iota(jnp.int32, (rank_tile, rank_tile), 0)
    c_i = lax.broadcasted_iota(jnp.int32, (rank_tile, rank_tile), 1)
    lower = jnp.where(c_i < r_i, 1.0, 0.0).astype(BF16)
    carry = jnp.zeros((1, T), F32)
    for j in range(ext // rank_tile):
        sc = score[j * rank_tile:(j + 1) * rank_tile, :]
        eq = jnp.where(sc == thr, 1.0, 0.0)
        rank = _dot(lower, eq.astype(BF16)) + carry
        carry = carry + _col_reduce(jnp.sum, eq)
        sel_scr[j * rank_tile:(j + 1) * rank_tile, :] = jnp.where(
            sc > thr, 1.0, jnp.where(rank < need, eq, 0.0))

    sel = sel_scr[0:ext, :] > 0.5
    ckvn = ckvn_scr[0:ext, :]
    ckvn_t = ckvnt_scr[:, 0:ext]
    for h in range(DSA_HEADS):
        qa_t = _dot(wuk_ref[h], q_t[h * dh:(h + 1) * dh, :].astype(BF16)) * (dh ** -0.5 * LOG2E)
        lg = _dot(ckvn, qa_t.astype(BF16))
        lg = jnp.where(sel, lg, -jnp.inf)
        p = jnp.exp2(lg - _col_reduce(jnp.max, lg))
        l = _col_reduce(jnp.sum, p)
        lat_t = _dot(ckvn_t, p.astype(BF16)) / l
        out_t = _dot(wuv_ref[h], lat_t.astype(BF16))
        y_ref[0, :, h * dh:(h + 1) * dh] = out_t.T.astype(BF16)


def _dsa_kernel(q_ref, ckv_ref, qidx_ref, misc_all_ref, misc_q_ref, g_ref, wuk_ref, wuv_ref, y_ref,
                ckvn_scr, ckvnt_scr, kidx_scr, sel_scr, *, T, n_sel, extents, rank_tile):
    qi = pl.program_id(1)

    @pl.when(qi == 0)
    def _():
        ckvn = _rms(ckv_ref[0].astype(F32)) * g_ref[...]
        ckvn_scr[...] = ckvn.astype(BF16)
        ckvnt_scr[...] = ckvn.T.astype(BF16)
        kidx_scr[...] = misc_all_ref[0, :, M_IK:M_IK + IDX_DIM].astype(BF16)

    lo = 0
    for tiles in extents:
        @pl.when((qi >= lo) & (qi < tiles))
        def _(ext=tiles * T):
            _dsa_body(ext, qi, q_ref, qidx_ref, misc_q_ref, wuk_ref, wuv_ref, y_ref,
                      ckvn_scr, ckvnt_scr, kidx_scr, sel_scr, T=T, n_sel=n_sel, rank_tile=rank_tile)
        lo = tiles


def _dsa(proj, misc, kv_g, wuk, wuv_t):
    bsz, s, _ = proj.shape
    T = min(256, s)
    n_sel = min(DSA_TOPK, s // 4)
    nq = s // T
    extents = tuple(range(1, nq // 2 + 1)) + tuple(range(nq // 2 + 2, nq + 1, 2)) if nq > 4 else tuple(range(1, nq + 1))
    if extents[-1] != nq:
        extents += (nq,)
    rank_tile = 256
    w = DSA_HEADS * DSA_DIM
    wi = IDX_HEADS * IDX_DIM
    return pl.pallas_call(
        functools.partial(_dsa_kernel, T=T, n_sel=float(n_sel), extents=extents, rank_tile=rank_tile),
        out_shape=jax.ShapeDtypeStruct((bsz, s, w), BF16),
        grid=(bsz, s // T),
        in_specs=[pl.BlockSpec((1, T, w), lambda b, i: (b, i, C_DQ // w)),
                  pl.BlockSpec((1, s, DSA_LATENT), lambda b, i: (b, 0, C_CKV // DSA_LATENT)),
                  pl.BlockSpec((1, T, wi), lambda b, i: (b, i, C_DIQ // wi)),
                  pl.BlockSpec((1, s, LANE), lambda b, i: (b, 0, 0)),
                  pl.BlockSpec((1, T, LANE), lambda b, i: (b, i, 0)),
                  pl.BlockSpec((1, DSA_LATENT), lambda b, i: (0, 0)),
                  pl.BlockSpec((DSA_HEADS, DSA_LATENT, DSA_DIM), lambda b, i: (0, 0, 0)),
                  pl.BlockSpec((DSA_HEADS, DSA_DIM, DSA_LATENT), lambda b, i: (0, 0, 0))],
        out_specs=pl.BlockSpec((1, T, w), lambda b, i: (b, i, 0)),
        scratch_shapes=[pltpu.VMEM((s, DSA_LATENT), BF16),
                        pltpu.VMEM((DSA_LATENT, s), BF16),
                        pltpu.VMEM((s, IDX_DIM), BF16),
                        pltpu.VMEM((s, T), F32)],
        compiler_params=_params(("parallel", "arbitrary")),
        name="dsa_attention",
    )(proj, proj, proj, misc, misc, kv_g.reshape(1, DSA_LATENT), wuk, wuv_t)


def _merge_kernel(yml_ref, ydsa_ref, yfox_ref, g0_ref, g1_ref, g2_ref, x_ref, mod_ref,
                  wml_ref, wdsa_ref, wfox_ref, wout_ref, o_ref):
    merged = (_sigmoid(g0_ref[0].astype(F32)) * _dot(yml_ref[0], wml_ref[...])
              + _sigmoid(g1_ref[0].astype(F32)) * _dot(ydsa_ref[0], wdsa_ref[...])
              + _sigmoid(g2_ref[0].astype(F32)) * _dot(yfox_ref[0], wfox_ref[...]))
    out = _dot(merged.astype(BF16), wout_ref[...])
    o_ref[0] = x_ref[0] + mod_ref[0, 2:3, :] * out


def _merge(x, proj, y_ml, y_dsa, y_fox, mod, w_ml, w_dsa, w_fox, w_out):
    bsz, s, d = x.shape
    tm = min(512, s)
    wb = y_ml.shape[-1]
    gb = C_GATE // d
    yspec = pl.BlockSpec((1, tm, wb), lambda b, i: (b, i, 0))
    wspec = pl.BlockSpec((wb, d), lambda b, i: (0, 0))
    return pl.pallas_call(
        _merge_kernel,
        out_shape=jax.ShapeDtypeStruct((bsz, s, d), F32),
        grid=(bsz, s // tm),
        in_specs=[yspec, yspec, yspec,
                  pl.BlockSpec((1, tm, d), lambda b, i: (b, i, gb)),
                  pl.BlockSpec((1, tm, d), lambda b, i: (b, i, gb + 1)),
                  pl.BlockSpec((1, tm, d), lambda b, i: (b, i, gb + 2)),
                  pl.BlockSpec((1, tm, d), lambda b, i: (b, i, 0)),
                  pl.BlockSpec((1, 6, d), lambda b, i: (b, 0, 0)),
                  wspec, wspec, wspec,
                  pl.BlockSpec((d, d), lambda b, i: (0, 0))],
        out_specs=pl.BlockSpec((1, tm, d), lambda b, i: (b, i, 0)),
        compiler_params=_params(("parallel", "parallel")),
        name="merge_out",
    )(y_ml, y_dsa, y_fox, proj, proj, proj, x, mod, w_ml, w_dsa, w_fox, w_out)


def _router_kernel(x_ref, mod_ref, wr_ref, br_ref, h_ref, route_ref, stat_ref, *, T):
    h = _rms(x_ref[0]) * (1.0 + mod_ref[0, 4:5, :]) + mod_ref[0, 3:4, :]
    h_ref[0] = h
    lg = _dot(h, wr_ref[...], HIGHEST) + br_ref[...]
    lg = lg.T[0:N_EXPERTS, :]
    sub = lax.broadcasted_iota(jnp.int32, (N_EXPERTS, 1), 0).astype(F32)
    vals, idxs, hots = [], [], []
    for _ in range(TOP_K):
        mx = jnp.max(lg, axis=0, keepdims=True)
        idx = jnp.min(jnp.where(lg == mx, sub, float(N_EXPERTS)), axis=0, keepdims=True)
        hot = sub == idx
        vals.append(mx)
        idxs.append(idx)
        hots.append(hot)
        lg = jnp.where(hot, -jnp.inf, lg)
    exps = [jnp.exp(v - vals[0]) for v in vals]
    tot = exps[0] + exps[1] + exps[2] + exps[3]
    multi = jnp.zeros((N_EXPERTS, T), F32)
    for hot in hots:
        multi = multi + jnp.where(hot, 1.0, 0.0)
    r_i = lax.broadcasted_iota(jnp.int32, (T, T), 0)
    c_i = lax.broadcasted_iota(jnp.int32, (T, T), 1)
    upper = jnp.where(r_i < c_i, 1.0, 0.0).astype(BF16)
    before = _dot(multi.astype(BF16), upper)
    cnt = jnp.broadcast_to(jnp.sum(multi, axis=1, keepdims=True), (N_EXPERTS, LANE))
    cnt = jnp.floor((cnt + (ROW_ALIGN - 1.0)) * (1.0 / ROW_ALIGN)) * ROW_ALIGN
    a_i = lax.broadcasted_iota(jnp.int32, (N_EXPERTS, N_EXPERTS), 0)
    b_i = lax.broadcasted_iota(jnp.int32, (N_EXPERTS, N_EXPERTS), 1)
    start = _dot(jnp.where(b_i < a_i, 1.0, 0.0).astype(F32), cnt, HIGHEST)
    stat_ref[0, 0:N_EXPERTS, :] = cnt
    stat_ref[0, N_EXPERTS:2 * N_EXPERTS, :] = start
    where_to = before + start[:, 0:1]
    rows = list(idxs)
    rows += [jnp.sum(jnp.where(hot, where_to, 0.0), axis=0, keepdims=True) for hot in hots]
    rows += [e / tot for e in exps]
    rows.append(jnp.zeros((LANE - 3 * TOP_K, T), F32))
    route_ref[0] = jnp.concatenate(rows, axis=0).T


def _router(x, mod, w_router, b_router):
    bsz, s, d = x.shape
    T = MOE_TILE
    nt = s // T
    ne = w_router.shape[1]
    wr = jnp.zeros((d, LANE), F32).at[:, :ne].set(w_router)
    br = jnp.full((1, LANE), -1e30, F32).at[0, :ne].set(b_router)
    return pl.pallas_call(
        functools.partial(_router_kernel, T=T),
        out_shape=(jax.ShapeDtypeStruct((bsz, s, d), F32),
                   jax.ShapeDtypeStruct((bsz, s, LANE), F32),
                   jax.ShapeDtypeStruct((bsz * nt, 2 * N_EXPERTS, LANE), F32)),
        grid=(bsz, nt),
        in_specs=[pl.BlockSpec((1, T, d), lambda b, i: (b, i, 0)),
                  pl.BlockSpec((1, 6, d), lambda b, i: (b, 0, 0)),
                  pl.BlockSpec((d, LANE), lambda b, i: (0, 0)),
                  pl.BlockSpec((1, LANE), lambda b, i: (0, 0))],
        out_specs=(pl.BlockSpec((1, T, d), lambda b, i: (b, i, 0)),
                   pl.BlockSpec((1, T, LANE), lambda b, i: (b, i, 0)),
                   pl.BlockSpec((1, 2 * N_EXPERTS, LANE), lambda b, i: (b * nt + i, 0, 0))),
        compiler_params=_params(("parallel", "parallel")),
        name="moe_router",
    )(x, mod, wr, br)


def _run_copies(tile, cnt_ref, start_ref, row_ref, make_copy):
    for r in range(N_RUNS):
        c = cnt_ref[tile * N_RUNS + r]
        src = start_ref[tile * N_RUNS + r]
        dst = row_ref[tile * N_RUNS + r]
        size = MOE_TILE
        while size >= ROW_ALIGN:
            done = c & (-2 * size)
            cp = make_copy(pl.multiple_of(src + done, ROW_ALIGN), pl.multiple_of(dst + done, ROW_ALIGN), size)
            pl.when((c & size) != 0)(cp.start)
            size //= 2


def _dispatch_kernel(cnt_ref, start_ref, row_ref, pad_end_ref, cnt_end_ref, h_ref, route_ref, xs_hbm,
                     loc0, loc1, zbuf, zsem, sem, *, BM, N_TAIL, N_T):
    t = pl.program_id(0)
    loc = (loc0, loc1)

    @pl.when(t == 0)
    def _():
        zbuf[...] = jnp.zeros(zbuf.shape, F32)
        n_rows = xs_hbm.shape[0]

        def pad_copy(e, g):
            start = pl.multiple_of(pad_end_ref[e] - g * ZERO_ROWS, ZERO_ROWS)
            return (start + ZERO_ROWS > cnt_end_ref[e],
                    pltpu.make_async_copy(zbuf.at[pl.ds(0, ZERO_ROWS)], xs_hbm.at[pl.ds(start, ZERO_ROWS)], zsem))

        def tail_copy(e):
            start = pl.multiple_of(pad_end_ref[N_EXPERTS - 1] + e * BM, BM)
            safe = pl.multiple_of(jnp.minimum(start, n_rows - BM), BM)
            return start < n_rows, pltpu.make_async_copy(zbuf, xs_hbm.at[pl.ds(safe, BM)], zsem)

        copies = [pad_copy(e, g) for e in range(N_EXPERTS) for g in range(1, BM // ZERO_ROWS + 1)]
        copies += [tail_copy(e) for e in range(N_TAIL)]
        for cond, cp in copies:
            pl.when(cond)(cp.start)
        for cond, cp in copies:
            pl.when(cond)(cp.wait)

    pos_t = route_ref[...].T[TOP_K:2 * TOP_K, :]
    slot = lax.broadcasted_iota(jnp.int32, (LOC_ROWS, 1), 0).astype(F32)
    pick = jnp.zeros((LOC_ROWS, MOE_TILE), F32)
    for k in range(TOP_K):
        pick = pick + jnp.where(slot == pos_t[k:k + 1, :], 1.0, 0.0)
    pick = pick.astype(BF16)

    def drain(s):
        pltpu.make_async_copy(loc[s], xs_hbm.at[pl.ds(0, LOC_ROWS)], sem.at[s]).wait()

    for s in range(2):
        @pl.when(t % 2 == s)
        def _(s=s):
            pl.when(t >= 2)(functools.partial(drain, s))
            loc[s][...] = _dot(pick, h_ref[...].astype(BF16))
            _run_copies(t, cnt_ref, start_ref, row_ref,
                        lambda a, b, n: pltpu.make_async_copy(loc[s].at[pl.ds(a, n)], xs_hbm.at[pl.ds(b, n)],
                                                              sem.at[s]))

    @pl.when(t == N_T - 1)
    def _():
        if N_T >= 2:
            drain(N_T % 2)
        drain((N_T - 1) % 2)


def _dispatch(plan, h2, route, n_rows, bm):
    n, d = h2.shape
    pairs = LOC_ROWS
    return pl.pallas_call(
        functools.partial(_dispatch_kernel, BM=bm, N_TAIL=n_rows // bm - (n * TOP_K) // bm, N_T=n // MOE_TILE),
        out_shape=jax.ShapeDtypeStruct((n_rows, d), F32),
        grid_spec=pltpu.PrefetchScalarGridSpec(
            num_scalar_prefetch=5,
            grid=(n // MOE_TILE,),
            in_specs=[pl.BlockSpec((MOE_TILE, d), lambda t, *_: (t, 0)),
                      pl.BlockSpec((MOE_TILE, LANE), lambda t, *_: (t, 0))],
            out_specs=pl.BlockSpec(memory_space=pl.ANY),
            scratch_shapes=[pltpu.VMEM((pairs, d), F32),
                            pltpu.VMEM((pairs, d), F32),
                            pltpu.VMEM((bm, d), F32),
                            pltpu.SemaphoreType.DMA(()),
                            pltpu.SemaphoreType.DMA((2,))]),
        compiler_params=pltpu.CompilerParams(dimension_semantics=("arbitrary",), vmem_limit_bytes=VMEM_LIMIT,
                                             disable_bounds_checks=True),
        name="moe_dispatch",
    )(plan["cnt"], plan["start"], plan["row"], plan["pad_end"], plan["cnt_end"], h2, route)


def _expert_kernel(be_ref, nu_ref, nxt_ref, slot_ref, rows_ref, xs_ref, w1_hbm, b1_ref, w2_hbm, b2_ref, o_ref,
                   w1f, w2f, w1b, w2b, sem, *, layer):
    i = pl.program_id(0)
    de = w2b.shape[0]
    e = be_ref[i]
    prev = be_ref[jnp.maximum(i - 1, 0)]
    fresh = (i == 0) | (e != prev)

    def copies(expert, s):
        return (pltpu.make_async_copy(w1_hbm.at[layer, expert], w1f.at[s], sem.at[s, 0]),
                pltpu.make_async_copy(w2_hbm.at[layer, expert], w2f.at[s], sem.at[s, 1]))

    @pl.when(i == 0)
    def _():
        for cp in copies(e, slot_ref[e]):
            cp.start()

    @pl.when(fresh & (i < nu_ref[0]))
    def _():
        s = slot_ref[e]
        for cp in copies(e, s):
            cp.wait()
        w1b[...] = w1f[s].astype(BF16)
        w2b[...] = w2f[s].astype(BF16)
        nxt = nxt_ref[e]

        @pl.when(nxt >= 0)
        def _():
            for cp in copies(nxt, 1 - s):
                cp.start()

    def ffn(x):
        hdn = _dot(x.astype(BF16), w1b[...]) + b1_ref[...]
        glu = jnp.minimum(hdn[:, :de], SWIGLU_LIMIT)
        lin = jnp.clip(hdn[:, de:], -SWIGLU_LIMIT, SWIGLU_LIMIT)
        act = glu * _sigmoid(SWIGLU_ALPHA * glu) * (lin + 1.0)
        return _dot(act.astype(BF16), w2b[...]) + b2_ref[...]

    half = o_ref.shape[0] // 2
    few = rows_ref[i] <= half

    @pl.when((i < nu_ref[0]) & jnp.logical_not(few))
    def _():
        o_ref[...] = ffn(xs_ref[...])

    @pl.when((i < nu_ref[0]) & few)
    def _():
        o_ref[0:half, :] = ffn(xs_ref[0:half, :])
        o_ref[half:, :] = jnp.zeros((half, o_ref.shape[1]), F32)

    @pl.when(i >= nu_ref[0])
    def _():
        o_ref[...] = jnp.zeros(o_ref.shape, F32)


def _experts(sched, xs, w1, b1, w2, b2, layer, bm):
    n_rows, d = xs.shape
    depth, ne, _, dh2 = w1.shape
    de = w2.shape[2]
    n_blocks = n_rows // bm

    def row_map(i, be, nu, *_):
        return (jnp.minimum(i, nu[0] - 1), 0)

    def b_map(i, be, *_):
        return (layer, be[i], 0, 0)

    return pl.pallas_call(
        functools.partial(_expert_kernel, layer=layer),
        out_shape=jax.ShapeDtypeStruct((n_rows, d), F32),
        grid_spec=pltpu.PrefetchScalarGridSpec(
            num_scalar_prefetch=5,
            grid=(n_blocks,),
            in_specs=[pl.BlockSpec((bm, d), row_map),
                      pl.BlockSpec(memory_space=pl.ANY),
                      pl.BlockSpec((None, None, 1, dh2), b_map),
                      pl.BlockSpec(memory_space=pl.ANY),
                      pl.BlockSpec((None, None, 1, d), b_map)],
            out_specs=pl.BlockSpec((bm, d), lambda i, *_: (i, 0)),
            scratch_shapes=[pltpu.VMEM((2, d, dh2), F32), pltpu.VMEM((2, de, d), F32),
                            pltpu.VMEM((d, dh2), BF16), pltpu.VMEM((de, d), BF16),
                            pltpu.SemaphoreType.DMA((2, 2))]),
        compiler_params=_params(("arbitrary",)),
        name="moe_experts",
    )(sched["blk_expert"], sched["n_used"], sched["next_expert"], sched["slot"], sched["blk_rows"], xs,
      w1, b1.reshape(depth, ne, 1, dh2), w2, b2.reshape(depth, ne, 1, d))


def _combine_kernel(cnt_ref, start_ref, row_ref, yb_hbm, x_ref, route_ref, mod_ref, fg_ref, o_ref,
                    loc0, loc1, sem, *, nt, n_tiles, final):
    t = pl.program_id(0) * nt + pl.program_id(1)
    pairs = LOC_ROWS
    loc = (loc0, loc1)

    def fetch(tile, s):
        _run_copies(tile, cnt_ref, start_ref, row_ref,
                    lambda a, b, n: pltpu.make_async_copy(yb_hbm.at[pl.ds(b, n)], loc[s].at[pl.ds(a, n)],
                                                          sem.at[s]))

    pl.when(t == 0)(functools.partial(fetch, 0, 0))
    route = route_ref[0]
    slot = lax.broadcasted_iota(jnp.int32, (1, pairs), 1).astype(F32)
    wgt = jnp.zeros((MOE_TILE, pairs), F32)
    for k in range(TOP_K):
        wgt = wgt + jnp.where(slot == route[:, TOP_K + k:TOP_K + k + 1],
                              route[:, 2 * TOP_K + k:2 * TOP_K + k + 1], 0.0)
    hi = wgt.astype(BF16)
    lo = (wgt - hi.astype(F32)).astype(BF16)

    for s in range(2):
        @pl.when(t % 2 == s)
        def _(s=s):
            pl.when(t + 1 < n_tiles)(functools.partial(fetch, t + 1, 1 - s))
            pltpu.make_async_copy(yb_hbm.at[pl.ds(0, pairs)], loc[s], sem.at[s]).wait()
            rows = loc[s][...].astype(BF16)
            y = _dot(hi, rows) + _dot(lo, rows)
            out = x_ref[0] + mod_ref[0, 5:6, :] * y
            if final:
                out = _rms(out) * fg_ref[...]
            o_ref[0] = out


def _combine(plan, yb, x, route, mod, final_g, final):
    bsz, s, d = x.shape
    nt = s // MOE_TILE
    pairs = LOC_ROWS
    return pl.pallas_call(
        functools.partial(_combine_kernel, nt=nt, n_tiles=bsz * nt, final=final),
        out_shape=jax.ShapeDtypeStruct((bsz, s, d), F32),
        grid_spec=pltpu.PrefetchScalarGridSpec(
            num_scalar_prefetch=3,
            grid=(bsz, nt),
            in_specs=[pl.BlockSpec(memory_space=pl.ANY),
                      pl.BlockSpec((1, MOE_TILE, d), lambda b, i, *_: (b, i, 0)),
                      pl.BlockSpec((1, MOE_TILE, LANE), lambda b, i, *_: (b, i, 0)),
                      pl.BlockSpec((1, 6, d), lambda b, i, *_: (b, 0, 0)),
                      pl.BlockSpec((1, d), lambda b, i, *_: (0, 0))],
            out_specs=pl.BlockSpec((1, MOE_TILE, d), lambda b, i, *_: (b, i, 0)),
            scratch_shapes=[pltpu.VMEM((pairs, d), F32), pltpu.VMEM((pairs, d), F32),
                            pltpu.SemaphoreType.DMA((2,))]),
        compiler_params=pltpu.CompilerParams(dimension_semantics=("arbitrary", "arbitrary"),
                                             vmem_limit_bytes=VMEM_LIMIT, disable_bounds_checks=True),
        name="moe_combine",
    )(plan["cnt"], plan["start"], plan["row"], yb, x, route, mod, final_g.reshape(1, d))


def _rearranged_in_proj(w_in, b_in):
    sizes = (ML_HEADS * ML_QK, ML_HEADS * ML_QK, ML_HEADS * ML_V, ML_HEADS * ML_V, ML_HEADS, ML_HEADS,
             DSA_HEADS * DSA_DIM, DSA_LATENT, IDX_HEADS * IDX_DIM, IDX_DIM, IDX_HEADS,
             3 * FOX_HEADS * FOX_DIM, FOX_HEADS, N_BRANCH * w_in.shape[0])
    offs = [0]
    for sz in sizes:
        offs.append(offs[-1] + sz)
    (o_mq, o_mk, o_mv, o_mo, o_mi, o_mf, o_dq, o_ckv, o_iq, o_ik, o_iw, o_fx, o_ff, o_g, o_end) = offs
    pad = LANE - (IDX_DIM + 2 * ML_HEADS + IDX_HEADS + FOX_HEADS)

    def cols(a):
        parts = [a[..., o_mq:o_mi],
                 a[..., o_dq:o_ckv],
                 a[..., o_iq:o_ik],
                 a[..., o_ckv:o_iq],
                 a[..., o_ik:o_iw],
                 a[..., o_mi:o_dq],
                 a[..., o_iw:o_fx],
                 a[..., o_ff:o_g],
                 jnp.zeros(a.shape[:-1] + (pad,), a.dtype),
                 a[..., o_fx:o_ff],
                 a[..., o_g:o_end]]
        return jnp.concatenate(parts, axis=-1)

    return cols(w_in).astype(BF16), cols(b_in.reshape(1, -1))


def _moe_plan(stats, bm, n_blocks):
    cnt = stats[:, :N_EXPERTS, 0].astype(jnp.int32)
    start = stats[:, N_EXPERTS:, 0].astype(jnp.int32)
    total = jnp.sum(cnt, axis=0)
    padded = (total + bm - 1) // bm * bm
    pad_end = jnp.cumsum(padded)
    pad_start = pad_end - padded
    row = pad_start[None, :] + jnp.cumsum(cnt, axis=0) - cnt
    used = jnp.sum(cnt, axis=1, keepdims=True)
    cnt = jnp.concatenate([cnt, LOC_ROWS - used], axis=1)
    start = jnp.concatenate([start, used], axis=1)
    parity = jnp.arange(cnt.shape[0], dtype=jnp.int32)[:, None] % 2
    row = jnp.concatenate([row, n_blocks * bm + parity * MOE_TILE], axis=1)
    blk_row = jnp.arange(n_blocks + 1, dtype=jnp.int32) * bm
    blk_expert = jnp.minimum(jnp.sum((pad_end[None, :] <= blk_row[:, None]).astype(jnp.int32), axis=1),
                             N_EXPERTS - 1)
    n_used = (pad_end[-1:] // bm).astype(jnp.int32)
    present = padded > 0
    ids = jnp.arange(N_EXPERTS, dtype=jnp.int32)
    later = jnp.where(present[None, :] & (ids[None, :] > ids[:, None]), ids[None, :], N_EXPERTS)
    nxt = jnp.min(later, axis=1)
    run_end = jnp.sum(jnp.where(ids[None, :] == blk_expert[:, None], (pad_start + total)[None, :], 0), axis=1)
    blk_rows = jnp.clip(run_end - blk_row, 0, bm).astype(jnp.int32)
    sched = dict(blk_expert=blk_expert, n_used=n_used, blk_rows=blk_rows,
                 next_expert=jnp.where(nxt < N_EXPERTS, nxt, -1).astype(jnp.int32),
                 slot=((jnp.cumsum(present) - present) % 2).astype(jnp.int32))
    plan = dict(cnt=cnt.reshape(-1), start=start.reshape(-1), row=row.reshape(-1).astype(jnp.int32),
                pad_end=pad_end.astype(jnp.int32), cnt_end=(pad_start + total).astype(jnp.int32))
    return plan, sched


def kernel(x, c, w_ada, b_ada, w_in, b_in, conv_w, conv_b, ml_norm_g, kv_norm_g, w_uk, w_uv,
           w_br_ml, w_br_dsa, w_br_fox, w_out, w_router, b_router, w1, b1, w2, b2, final_g):
    bsz, s, d = x.shape
    depth = w_in.shape[0]
    n = bsz * s
    bm = 2 * MOE_TILE
    n_blocks = -(-(n * TOP_K + (ROW_ALIGN - 1) * N_EXPERTS * (n // MOE_TILE)) // bm) + N_EXPERTS
    mods = _ada_mod(c, w_ada, b_ada).reshape(depth, bsz, 6, d)
    for l in range(depth):
        mod = mods[l]
        w_r, b_r = _rearranged_in_proj(w_in[l], b_in[l])
        proj, misc = _in_proj(x, mod, w_r, b_r)
        y_ml, fcol, frow = _mlstm(proj, misc, conv_w[l], conv_b[l], ml_norm_g[l])
        y_fox = _fox(proj, fcol, frow)
        y_dsa = _dsa(proj, misc, kv_norm_g[l], w_uk[l].astype(BF16), jnp.swapaxes(w_uv[l], 1, 2).astype(BF16))
        x = _merge(x, proj, y_ml, y_dsa, y_fox, mod, w_br_ml[l].astype(BF16), w_br_dsa[l].astype(BF16),
                   w_br_fox[l].astype(BF16), w_out[l].astype(BF16))
        h2, route, stats = _router(x, mod, w_router[l], b_router[l])
        plan, sched = _moe_plan(stats, bm, n_blocks)
        xs = _dispatch(plan, h2.reshape(n, d), route.reshape(n, LANE), (n_blocks + 1) * bm, bm)
        yb = _experts(sched, xs, w1, b1, w2, b2, l, bm)
        x = _combine(plan, yb, x, route, mod, final_g, final=(l == depth - 1))
    return x
```

```python
import functools

import jax
import jax.numpy as jnp
from jax import lax
from jax.experimental import pallas as pl
from jax.experimental.pallas import tpu as pltpu

F32 = jnp.float32
BF16 = jnp.bfloat16
HIGHEST = lax.Precision.HIGHEST

EPS = 1e-6
LOG2E = 1.4426950408889634
CHUNK = 64

ML_HEADS, ML_QK, ML_V, ML_CONV = 4, 64, 128, 4
DSA_HEADS, DSA_DIM, DSA_LATENT = 4, 128, 128
IDX_HEADS, IDX_DIM, DSA_TOPK = 4, 64, 256
FOX_HEADS, FOX_DIM = 4, 128
N_BRANCH = 3
N_EXPERTS, TOP_K = 32, 4
SWIGLU_LIMIT, SWIGLU_ALPHA = 7.0, 1.702

LANE = 128
ROW_ALIGN = 8
INT_MIN = -2 ** 31

MOE_TILE = 256
N_RUNS = N_EXPERTS + 1
LOC_ROWS = TOP_K * MOE_TILE + MOE_TILE
ZERO_ROWS = 128
COUNT_ROWS = 64

C_MLQK = 0
C_MLV = 512
C_MLO = 1024
C_DQ = 1536
C_DIQ = 2048
C_CKV = 2304
C_MISC = 2432
C_FOX = 2560
C_GATE = 4096
M_IK, M_MLI, M_MLF, M_IW, M_FXF = 0, 64, 68, 72, 76

VMEM_LIMIT = 56 * 1024 * 1024


def _dot(a, b, prec=None):
    return jnp.dot(a, b, preferred_element_type=F32, precision=prec)


def _dot_nt(a, b, prec=None):
    return lax.dot_general(a, b, (((1,), (1,)), ((), ())), preferred_element_type=F32, precision=prec)


def _dot_tn(a, b):
    return lax.dot_general(a, b, (((0,), (0,)), ((), ())), preferred_element_type=F32)


def _sigmoid(x):
    return 1.0 / (1.0 + jnp.exp(-x))


def _log_sigmoid(x):
    return jnp.minimum(x, 0.0) - jnp.log1p(jnp.exp(-jnp.abs(x)))


def _rms(x):
    return x * lax.rsqrt(jnp.mean(x * x, axis=-1, keepdims=True) + EPS)


def _col_reduce(op, x):
    rows = x.shape[0]
    if rows > COUNT_ROWS and rows % COUNT_ROWS == 0:
        x = op(x.reshape(rows // COUNT_ROWS, COUNT_ROWS, x.shape[1]), axis=0)
    return op(x, axis=0, keepdims=True)


def _params(sem, vmem=VMEM_LIMIT):
    return pltpu.CompilerParams(dimension_semantics=sem, vmem_limit_bytes=vmem)


def _ada_kernel(c_ref, w_ref, b_ref, o_ref):
    c = c_ref[...]
    o_ref[0] = _dot(c * _sigmoid(c), w_ref[0], HIGHEST) + b_ref[0]


def _ada_mod(c, w_ada, b_ada):
    depth, d, n = w_ada.shape
    bsz = c.shape[0]
    tn = 1536
    return pl.pallas_call(
        _ada_kernel,
        out_shape=jax.ShapeDtypeStruct((depth, bsz, n), F32),
        grid=(depth, n // tn),
        in_specs=[pl.BlockSpec((bsz, d), lambda l, j: (0, 0)),
                  pl.BlockSpec((1, d, tn), lambda l, j: (l, 0, j)),
                  pl.BlockSpec((1, 1, tn), lambda l, j: (l, 0, j))],
        out_specs=pl.BlockSpec((1, bsz, tn), lambda l, j: (l, 0, j)),
        compiler_params=_params(("parallel", "parallel")),
        name="ada_mod",
    )(c, w_ada, b_ada.reshape(depth, 1, n))


def _inproj_kernel(x_ref, mod_ref, w_ref, b_ref, o_ref, misc_ref, h_scr, *, tn):
    j = pl.program_id(2)

    @pl.when(j == 0)
    def _():
        h = _rms(x_ref[0]) * (1.0 + mod_ref[0, 1:2, :]) + mod_ref[0, 0:1, :]
        h_scr[...] = h.astype(BF16)

    acc = _dot(h_scr[...], w_ref[...]) + b_ref[...]
    o_ref[0] = acc.astype(BF16)

    @pl.when(j == C_MISC // tn)
    def _():
        misc_ref[0] = acc[:, C_MISC % tn:C_MISC % tn + LANE]


def _in_proj(x, mod, w, b):
    bsz, s, d = x.shape
    n = w.shape[1]
    tm = min(2048, s)
    tn = 1024
    return pl.pallas_call(
        functools.partial(_inproj_kernel, tn=tn),
        out_shape=(jax.ShapeDtypeStruct((bsz, s, n), BF16), jax.ShapeDtypeStruct((bsz, s, LANE), F32)),
        grid=(bsz, s // tm, n // tn),
        in_specs=[pl.BlockSpec((1, tm, d), lambda bi, i, j: (bi, i, 0)),
                  pl.BlockSpec((1, 6, d), lambda bi, i, j: (bi, 0, 0)),
                  pl.BlockSpec((d, tn), lambda bi, i, j: (0, j)),
                  pl.BlockSpec((1, tn), lambda bi, i, j: (0, j))],
        out_specs=(pl.BlockSpec((1, tm, tn), lambda bi, i, j: (bi, i, j)),
                   pl.BlockSpec((1, tm, LANE), lambda bi, i, j: (bi, i, 0))),
        scratch_shapes=[pltpu.VMEM((tm, d), BF16)],
        compiler_params=_params(("parallel", "parallel", "arbitrary")),
        name="in_proj",
    )(x, mod, w, b)


def _mlstm_kernel(qk_ref, v_ref, o_ref, misc_ref, cw_ref, cb_ref, g_ref,
                  y_ref, fcol_ref, frow_ref,
                  xext, ct_scr, n_scr, m_scr, carry_scr, *, L):
    c = pl.program_id(1)
    nqk = ML_HEADS * ML_QK

    @pl.when(c == 0)
    def _():
        xext[0:8, :] = jnp.zeros((8, 2 * nqk), F32)
        ct_scr[...] = jnp.zeros(ct_scr.shape, F32)
        n_scr[...] = jnp.zeros(n_scr.shape, F32)
        m_scr[...] = jnp.full(m_scr.shape, -jnp.inf, F32)
        carry_scr[...] = jnp.zeros(carry_scr.shape, F32)

    @pl.when(c > 0)
    def _():
        xext[0:8, :] = xext[L:L + 8, :]

    xext[8:8 + L, :] = qk_ref[0].astype(F32)
    cw = cw_ref[...]
    conv = (cb_ref[...] + cw[3:4, :] * xext[8:8 + L, :] + cw[2:3, :] * xext[7:7 + L, :]
            + cw[1:2, :] * xext[6:6 + L, :] + cw[0:1, :] * xext[5:5 + L, :])
    qk = conv * _sigmoid(conv)

    misc = misc_ref[0]
    ls = _log_sigmoid(misc)
    row = lax.broadcasted_iota(jnp.int32, (L, L), 0)
    col = lax.broadcasted_iota(jnp.int32, (L, L), 1)
    causal = row >= col
    tri = jnp.where(causal, 1.0, 0.0).astype(F32)
    cs = _dot(tri, ls, HIGHEST)
    cs_t = cs.T
    misc_t = misc.T
    carry = carry_scr[0:1, :]
    fcol_ref[0] = cs + carry
    for h in range(FOX_HEADS):
        frow_ref[0, h:h + 1, :] = cs_t[M_FXF + h:M_FXF + h + 1, :] + carry[:, M_FXF + h:M_FXF + h + 1]
    frow_ref[0, 4:8, :] = jnp.zeros((4, L), F32)
    carry_scr[0:1, :] = carry + cs[L - 1:L, :]

    for h in range(ML_HEADS):
        qh = qk[:, h * ML_QK:(h + 1) * ML_QK] * (ML_QK ** -0.5)
        kh = qk[:, nqk + h * ML_QK:nqk + (h + 1) * ML_QK]
        vb = v_ref[0, :, h * ML_V:(h + 1) * ML_V].astype(BF16)
        i_col = misc[:, M_MLI + h:M_MLI + h + 1]
        i_row = misc_t[M_MLI + h:M_MLI + h + 1, :]
        b_col = cs[:, M_MLF + h:M_MLF + h + 1]
        b_row = cs_t[M_MLF + h:M_MLF + h + 1, :]
        b_last = b_col[L - 1:L, :]
        m_prev = m_scr[h:h + 1, 0:1]

        d_log = jnp.where(causal, b_col - b_row + i_row, -jnp.inf)
        inter_log = b_col + m_prev
        m_out = jnp.maximum(inter_log, jnp.max(d_log, axis=-1, keepdims=True))
        qb = qh.astype(BF16)
        kb = kh.astype(BF16)
        s = _dot_nt(qb, kb) * jnp.exp(d_log - m_out)
        a_inter = jnp.exp(inter_log - m_out)
        ct = ct_scr[h]
        n_row = n_scr[h]
        num = _dot(s.astype(BF16), vb) + a_inter * _dot(qb, ct.astype(BF16))
        den = jnp.sum(s, axis=-1, keepdims=True) + a_inter * jnp.sum(qh * n_row, axis=-1, keepdims=True)
        hid = num / jnp.maximum(jnp.abs(den), jnp.exp(-m_out))

        w_state = b_last - b_col + i_col
        m_loc = jnp.max(w_state, axis=0, keepdims=True)
        ke = kh * jnp.exp(w_state - m_loc)
        c_loc = _dot_tn(ke.astype(BF16), vb)
        n_loc = jnp.sum(ke, axis=0, keepdims=True)
        m_new = jnp.maximum(b_last + m_prev, m_loc)
        decay = jnp.exp(b_last + m_prev - m_new)
        scale = jnp.exp(m_loc - m_new)
        ct_scr[h] = decay * ct + scale * c_loc
        n_scr[h] = decay * n_row + scale * n_loc
        m_scr[h:h + 1, :] = jnp.broadcast_to(m_new, (1, LANE))

        y = (_rms(hid) * g_ref[:, h * ML_V:(h + 1) * ML_V]
             * _sigmoid(o_ref[0, :, h * ML_V:(h + 1) * ML_V].astype(F32)))
        y_ref[0, :, h * ML_V:(h + 1) * ML_V] = y.astype(BF16)


def _mlstm(proj, misc, conv_w, conv_b, norm_g):
    bsz, s, _ = proj.shape
    L = min(256, s)
    w = 2 * ML_HEADS * ML_QK
    wv = ML_HEADS * ML_V
    return pl.pallas_call(
        functools.partial(_mlstm_kernel, L=L),
        out_shape=(jax.ShapeDtypeStruct((bsz, s, wv), BF16),
                   jax.ShapeDtypeStruct((bsz, s, LANE), F32),
                   jax.ShapeDtypeStruct((bsz, 8, s), F32)),
        grid=(bsz, s // L),
        in_specs=[pl.BlockSpec((1, L, w), lambda b, c: (b, c, C_MLQK // w)),
                  pl.BlockSpec((1, L, wv), lambda b, c: (b, c, C_MLV // wv)),
                  pl.BlockSpec((1, L, wv), lambda b, c: (b, c, C_MLO // wv)),
                  pl.BlockSpec((1, L, LANE), lambda b, c: (b, c, 0)),
                  pl.BlockSpec((ML_CONV, w), lambda b, c: (0, 0)),
                  pl.BlockSpec((1, w), lambda b, c: (0, 0)),
                  pl.BlockSpec((1, wv), lambda b, c: (0, 0))],
        out_specs=(pl.BlockSpec((1, L, wv), lambda b, c: (b, c, 0)),
                   pl.BlockSpec((1, L, LANE), lambda b, c: (b, c, 0)),
                   pl.BlockSpec((1, 8, L), lambda b, c: (b, 0, c))),
        scratch_shapes=[pltpu.VMEM((L + 8, w), F32),
                        pltpu.VMEM((ML_HEADS, ML_QK, ML_V), F32),
                        pltpu.VMEM((ML_HEADS, 1, ML_QK), F32),
                        pltpu.VMEM((8, LANE), F32),
                        pltpu.VMEM((8, LANE), F32)],
        compiler_params=_params(("parallel", "arbitrary")),
        name="mlstm",
    )(proj, proj, proj, misc, conv_w, conv_b.reshape(1, w), norm_g.reshape(1, wv))


def _fox_kernel(q_ref, k_ref, v_ref, fcol_ref, frow_ref, y_ref, k_scr, vt_scr, *, T, S):
    qi = pl.program_id(1)
    d = FOX_DIM

    @pl.when(qi == 0)
    def _():
        for h in range(FOX_HEADS):
            k_scr[h] = k_ref[0, :, h * d:(h + 1) * d].astype(BF16)
            vt_scr[h] = v_ref[0, :, h * d:(h + 1) * d].astype(F32).T.astype(BF16)

    q_t = (q_ref[0].astype(F32) * (d ** -0.5 * LOG2E)).T

    def body(ext):
        kpos = lax.broadcasted_iota(jnp.int32, (ext, 1), 0)
        qpos = (ext - T) + lax.broadcasted_iota(jnp.int32, (1, T), 1)
        causal = kpos <= qpos
        for h in range(FOX_HEADS):
            s = _dot(k_scr[h, 0:ext, :], q_t[h * d:(h + 1) * d, :].astype(BF16))
            s = s + (frow_ref[0, h:h + 1, :] * LOG2E - fcol_ref[0, 0:ext, M_FXF + h:M_FXF + h + 1] * LOG2E)
            s = jnp.where(causal, s, -jnp.inf)
            p = jnp.exp2(s - _col_reduce(jnp.max, s))
            l = _col_reduce(jnp.sum, p)
            out_t = _dot(vt_scr[h, :, 0:ext], p.astype(BF16)) / l
            y_ref[0, :, h * d:(h + 1) * d] = out_t.T.astype(BF16)

    for c in range(S // T):
        pl.when(qi == c)(functools.partial(body, (c + 1) * T))


def _fox(proj, fcol, frow):
    bsz, s, _ = proj.shape
    T = min(256, s)
    w = FOX_HEADS * FOX_DIM
    return pl.pallas_call(
        functools.partial(_fox_kernel, T=T, S=s),
        out_shape=jax.ShapeDtypeStruct((bsz, s, w), BF16),
        grid=(bsz, s // T),
        in_specs=[pl.BlockSpec((1, T, w), lambda b, i: (b, i, C_FOX // w)),
                  pl.BlockSpec((1, s, w), lambda b, i: (b, 0, C_FOX // w + 1)),
                  pl.BlockSpec((1, s, w), lambda b, i: (b, 0, C_FOX // w + 2)),
                  pl.BlockSpec((1, s, LANE), lambda b, i: (b, 0, 0)),
                  pl.BlockSpec((1, 8, T), lambda b, i: (b, 0, i))],
        out_specs=pl.BlockSpec((1, T, w), lambda b, i: (b, i, 0)),
        scratch_shapes=[pltpu.VMEM((FOX_HEADS, s, FOX_DIM), BF16),
                        pltpu.VMEM((FOX_HEADS, FOX_DIM, s), BF16)],
        compiler_params=_params(("parallel", "arbitrary")),
        name="fox_attention",
    )(proj, proj, proj, fcol, frow)


def _dsa_body(ext, qi, q_ref, qidx_ref, misc_q_ref, wuk_ref, wuv_ref, y_ref,
              ckvn_scr, ckvnt_scr, kidx_scr, sel_scr, *, T, n_sel, rank_tile):
    dh = DSA_DIM
    q_t = q_ref[0].astype(F32).T
    qidx_t = qidx_ref[0].astype(F32).T.astype(BF16)
    w_t = misc_q_ref[0].T[M_IW:M_IW + IDX_HEADS, :] * (IDX_HEADS ** -0.5)
    kidx = kidx_scr[0:ext, :]
    score = jnp.zeros((ext, T), F32)
    for h in range(IDX_HEADS):
        lg = _dot(kidx, qidx_t[h * IDX_DIM:(h + 1) * IDX_DIM, :]) * (IDX_DIM ** -0.5)
        score = score + w_t[h:h + 1, :] * jnp.maximum(lg, 0.0)
    kpos = lax.broadcasted_iota(jnp.int32, (ext, 1), 0)
    qchunk = (qi * T + lax.broadcasted_iota(jnp.int32, (1, T), 1)) // CHUNK
    score = jnp.where((kpos // CHUNK) <= qchunk, score, -jnp.inf)

    def as_float(c):
        return pltpu.bitcast(jnp.where(c < 0, c ^ jnp.int32(0x7FFFFFFF), c), F32)

    def count_ge(c):
        return _col_reduce(jnp.sum, jnp.where(score >= as_float(c), 1.0, 0.0))

    t0 = jnp.where(count_ge(jnp.zeros((1, T), jnp.int32)) >= n_sel, jnp.int32(0), jnp.int32(INT_MIN))

    def bis(i, t):
        cand = t + jnp.left_shift(jnp.int32(1), jnp.int32(30) - i)
        return jnp.where(count_ge(cand) >= n_sel, cand, t)

    t = lax.fori_loop(0, 31, bis, t0)
    n_vis = ((qchunk + 1) * CHUNK).astype(F32)
    thr = jnp.where(n_vis <= n_sel, -3.0e38, as_float(t))
    need = n_sel - _col_reduce(jnp.sum, jnp.where(score > thr, 1.0, 0.0))
    r_i = lax.broadcasted_iota(jnp.int32, (rank_tile, rank_tile), 0)
    c_i = lax.broadcasted_iota(jnp.int32, (rank_tile, rank_tile), 1)
    lower = jnp.where(c_i < r_i, 1.0, 0.0).astype(BF16)
    carry = jnp.zeros((1, T), F32)
    for j in range(ext // rank_tile):
        sc = score[j * rank_tile:(j + 1) * rank_tile, :]
        eq = jnp.where(sc == thr, 1.0, 0.0)
        rank = _dot(lower, eq.astype(BF16)) + carry
        carry = carry + _col_reduce(jnp.sum, eq)
        sel_scr[j * rank_tile:(j + 1) * rank_tile, :] = jnp.where(
            sc > thr, 1.0, jnp.where(rank < need, eq, 0.0))

    sel = sel_scr[0:ext, :] > 0.5
    ckvn = ckvn_scr[0:ext, :]
    ckvn_t = ckvnt_scr[:, 0:ext]
    for h in range(DSA_HEADS):
        qa_t = _dot(wuk_ref[h], q_t[h * dh:(h + 1) * dh, :].astype(BF16)) * (dh ** -0.5 * LOG2E)
        lg = _dot(ckvn, qa_t.astype(BF16))
        lg = jnp.where(sel, lg, -jnp.inf)
        p = jnp.exp2(lg - _col_reduce(jnp.max, lg))
        l = _col_reduce(jnp.sum, p)
        lat_t = _dot(ckvn_t, p.astype(BF16)) / l
        out_t = _dot(wuv_ref[h], lat_t.astype(BF16))
        y_ref[0, :, h * dh:(h + 1) * dh] = out_t.T.astype(BF16)


def _dsa_kernel(q_ref, ckv_ref, qidx_ref, misc_all_ref, misc_q_ref, g_ref, wuk_ref, wuv_ref, y_ref,
                ckvn_scr, ckvnt_scr, kidx_scr, sel_scr, *, T, n_sel, extents, rank_tile):
    qi = pl.program_id(1)

    @pl.when(qi == 0)
    def _():
        ckvn = _rms(ckv_ref[0].astype(F32)) * g_ref[...]
        ckvn_scr[...] = ckvn.astype(BF16)
        ckvnt_scr[...] = ckvn.T.astype(BF16)
        kidx_scr[...] = misc_all_ref[0, :, M_IK:M_IK + IDX_DIM].astype(BF16)

    lo = 0
    for tiles in extents:
        @pl.when((qi >= lo) & (qi < tiles))
        def _(ext=tiles * T):
            _dsa_body(ext, qi, q_ref, qidx_ref, misc_q_ref, wuk_ref, wuv_ref, y_ref,
                      ckvn_scr, ckvnt_scr, kidx_scr, sel_scr, T=T, n_sel=n_sel, rank_tile=rank_tile)
        lo = tiles


def _dsa(proj, misc, kv_g, wuk, wuv_t):
    bsz, s, _ = proj.shape
    T = min(256, s)
    n_sel = min(DSA_TOPK, s // 4)
    nq = s // T
    extents = tuple(range(1, nq // 2 + 1)) + tuple(range(nq // 2 + 2, nq + 1, 2)) if nq > 4 else tuple(range(1, nq + 1))
    if extents[-1] != nq:
        extents += (nq,)
    rank_tile = 256
    w = DSA_HEADS * DSA_DIM
    wi = IDX_HEADS * IDX_DIM
    return pl.pallas_call(
        functools.partial(_dsa_kernel, T=T, n_sel=float(n_sel), extents=extents, rank_tile=rank_tile),
        out_shape=jax.ShapeDtypeStruct((bsz, s, w), BF16),
        grid=(bsz, s // T),
        in_specs=[pl.BlockSpec((1, T, w), lambda b, i: (b, i, C_DQ // w)),
                  pl.BlockSpec((1, s, DSA_LATENT), lambda b, i: (b, 0, C_CKV // DSA_LATENT)),
                  pl.BlockSpec((1, T, wi), lambda b, i: (b, i, C_DIQ // wi)),
                  pl.BlockSpec((1, s, LANE), lambda b, i: (b, 0, 0)),
                  pl.BlockSpec((1, T, LANE), lambda b, i: (b, i, 0)),
                  pl.BlockSpec((1, DSA_LATENT), lambda b, i: (0, 0)),
                  pl.BlockSpec((DSA_HEADS, DSA_LATENT, DSA_DIM), lambda b, i: (0, 0, 0)),
                  pl.BlockSpec((DSA_HEADS, DSA_DIM, DSA_LATENT), lambda b, i: (0, 0, 0))],
        out_specs=pl.BlockSpec((1, T, w), lambda b, i: (b, i, 0)),
        scratch_shapes=[pltpu.VMEM((s, DSA_LATENT), BF16),
                        pltpu.VMEM((DSA_LATENT, s), BF16),
                        pltpu.VMEM((s, IDX_DIM), BF16),
                        pltpu.VMEM((s, T), F32)],
        compiler_params=_params(("parallel", "arbitrary")),
        name="dsa_attention",
    )(proj, proj, proj, misc, misc, kv_g.reshape(1, DSA_LATENT), wuk, wuv_t)


def _merge_kernel(yml_ref, ydsa_ref, yfox_ref, g0_ref, g1_ref, g2_ref, x_ref, mod_ref,
                  wml_ref, wdsa_ref, wfox_ref, wout_ref, o_ref):
    merged = (_sigmoid(g0_ref[0].astype(F32)) * _dot(yml_ref[0], wml_ref[...])
              + _sigmoid(g1_ref[0].astype(F32)) * _dot(ydsa_ref[0], wdsa_ref[...])
              + _sigmoid(g2_ref[0].astype(F32)) * _dot(yfox_ref[0], wfox_ref[...]))
    out = _dot(merged.astype(BF16), wout_ref[...])
    o_ref[0] = x_ref[0] + mod_ref[0, 2:3, :] * out


def _merge(x, proj, y_ml, y_dsa, y_fox, mod, w_ml, w_dsa, w_fox, w_out):
    bsz, s, d = x.shape
    tm = min(512, s)
    wb = y_ml.shape[-1]
    gb = C_GATE // d
    yspec = pl.BlockSpec((1, tm, wb), lambda b, i: (b, i, 0))
    wspec = pl.BlockSpec((wb, d), lambda b, i: (0, 0))
    return pl.pallas_call(
        _merge_kernel,
        out_shape=jax.ShapeDtypeStruct((bsz, s, d), F32),
        grid=(bsz, s // tm),
        in_specs=[yspec, yspec, yspec,
                  pl.BlockSpec((1, tm, d), lambda b, i: (b, i, gb)),
                  pl.BlockSpec((1, tm, d), lambda b, i: (b, i, gb + 1)),
                  pl.BlockSpec((1, tm, d), lambda b, i: (b, i, gb + 2)),
                  pl.BlockSpec((1, tm, d), lambda b, i: (b, i, 0)),
                  pl.BlockSpec((1, 6, d), lambda b, i: (b, 0, 0)),
                  wspec, wspec, wspec,
                  pl.BlockSpec((d, d), lambda b, i: (0, 0))],
        out_specs=pl.BlockSpec((1, tm, d), lambda b, i: (b, i, 0)),
        compiler_params=_params(("parallel", "parallel")),
        name="merge_out",
    )(y_ml, y_dsa, y_fox, proj, proj, proj, x, mod, w_ml, w_dsa, w_fox, w_out)


def _router_kernel(x_ref, mod_ref, wr_ref, br_ref, h_ref, route_ref, stat_ref, *, T):
    h = _rms(x_ref[0]) * (1.0 + mod_ref[0, 4:5, :]) + mod_ref[0, 3:4, :]
    h_ref[0] = h
    lg = _dot(h, wr_ref[...], HIGHEST) + br_ref[...]
    lg = lg.T[0:N_EXPERTS, :]
    sub = lax.broadcasted_iota(jnp.int32, (N_EXPERTS, 1), 0).astype(F32)
    vals, idxs, hots = [], [], []
    for _ in range(TOP_K):
        mx = jnp.max(lg, axis=0, keepdims=True)
        idx = jnp.min(jnp.where(lg == mx, sub, float(N_EXPERTS)), axis=0, keepdims=True)
        hot = sub == idx
        vals.append(mx)
        idxs.append(idx)
        hots.append(hot)
        lg = jnp.where(hot, -jnp.inf, lg)
    exps = [jnp.exp(v - vals[0]) for v in vals]
    tot = exps[0] + exps[1] + exps[2] + exps[3]
    multi = jnp.zeros((N_EXPERTS, T), F32)
    for hot in hots:
        multi = multi + jnp.where(hot, 1.0, 0.0)
    r_i = lax.broadcasted_iota(jnp.int32, (T, T), 0)
    c_i = lax.broadcasted_iota(jnp.int32, (T, T), 1)
    upper = jnp.where(r_i < c_i, 1.0, 0.0).astype(BF16)
    before = _dot(multi.astype(BF16), upper)
    cnt = jnp.broadcast_to(jnp.sum(multi, axis=1, keepdims=True), (N_EXPERTS, LANE))
    cnt = jnp.floor((cnt + (ROW_ALIGN - 1.0)) * (1.0 / ROW_ALIGN)) * ROW_ALIGN
    a_i = lax.broadcasted_iota(jnp.int32, (N_EXPERTS, N_EXPERTS), 0)
    b_i = lax.broadcasted_iota(jnp.int32, (N_EXPERTS, N_EXPERTS), 1)
    start = _dot(jnp.where(b_i < a_i, 1.0, 0.0).astype(F32), cnt, HIGHEST)
    stat_ref[0, 0:N_EXPERTS, :] = cnt
    stat_ref[0, N_EXPERTS:2 * N_EXPERTS, :] = start
    where_to = before + start[:, 0:1]
    rows = list(idxs)
    rows += [jnp.sum(jnp.where(hot, where_to, 0.0), axis=0, keepdims=True) for hot in hots]
    rows += [e / tot for e in exps]
    rows.append(jnp.zeros((LANE - 3 * TOP_K, T), F32))
    route_ref[0] = jnp.concatenate(rows, axis=0).T


def _router(x, mod, w_router, b_router):
    bsz, s, d = x.shape
    T = MOE_TILE
    nt = s // T
    ne = w_router.shape[1]
    wr = jnp.zeros((d, LANE), F32).at[:, :ne].set(w_router)
    br = jnp.full((1, LANE), -1e30, F32).at[0, :ne].set(b_router)
    return pl.pallas_call(
        functools.partial(_router_kernel, T=T),
        out_shape=(jax.ShapeDtypeStruct((bsz, s, d), F32),
                   jax.ShapeDtypeStruct((bsz, s, LANE), F32),
                   jax.ShapeDtypeStruct((bsz * nt, 2 * N_EXPERTS, LANE), F32)),
        grid=(bsz, nt),
        in_specs=[pl.BlockSpec((1, T, d), lambda b, i: (b, i, 0)),
                  pl.BlockSpec((1, 6, d), lambda b, i: (b, 0, 0)),
                  pl.BlockSpec((d, LANE), lambda b, i: (0, 0)),
                  pl.BlockSpec((1, LANE), lambda b, i: (0, 0))],
        out_specs=(pl.BlockSpec((1, T, d), lambda b, i: (b, i, 0)),
                   pl.BlockSpec((1, T, LANE), lambda b, i: (b, i, 0)),
                   pl.BlockSpec((1, 2 * N_EXPERTS, LANE), lambda b, i: (b * nt + i, 0, 0))),
        compiler_params=_params(("parallel", "parallel")),
        name="moe_router",
    )(x, mod, wr, br)


def _run_copies(tile, cnt_ref, start_ref, row_ref, make_copy):
    for r in range(N_RUNS):
        c = cnt_ref[tile * N_RUNS + r]
        src = start_ref[tile * N_RUNS + r]
        dst = row_ref[tile * N_RUNS + r]
        size = MOE_TILE
        while size >= ROW_ALIGN:
            done = c & (-2 * size)
            cp = make_copy(pl.multiple_of(src + done, ROW_ALIGN), pl.multiple_of(dst + done, ROW_ALIGN), size)
            pl.when((c & size) != 0)(functools.partial(cp.start, priority=r % 2))
            size //= 2


def _dispatch_kernel(cnt_ref, start_ref, row_ref, pad_end_ref, cnt_end_ref, h_ref, route_ref, xs_hbm,
                     loc0, loc1, zbuf, zsem, sem, *, BM, N_TAIL, N_T):
    t = pl.program_id(0)
    loc = (loc0, loc1)

    @pl.when(t == 0)
    def _():
        zbuf[...] = jnp.zeros(zbuf.shape, F32)
        n_rows = xs_hbm.shape[0]

        def pad_copy(e, g):
            start = pl.multiple_of(pad_end_ref[e] - g * ZERO_ROWS, ZERO_ROWS)
            return (start + ZERO_ROWS > cnt_end_ref[e],
                    pltpu.make_async_copy(zbuf.at[pl.ds(0, ZERO_ROWS)], xs_hbm.at[pl.ds(start, ZERO_ROWS)], zsem))

        def tail_copy(e):
            start = pl.multiple_of(pad_end_ref[N_EXPERTS - 1] + e * BM, BM)
            safe = pl.multiple_of(jnp.minimum(start, n_rows - BM), BM)
            return start < n_rows, pltpu.make_async_copy(zbuf, xs_hbm.at[pl.ds(safe, BM)], zsem)

        copies = [pad_copy(e, g) for e in range(N_EXPERTS) for g in range(1, BM // ZERO_ROWS + 1)]
        copies += [tail_copy(e) for e in range(N_TAIL)]
        for cond, cp in copies:
            pl.when(cond)(cp.start)
        for cond, cp in copies:
            pl.when(cond)(cp.wait)

    pos_t = route_ref[...].T[TOP_K:2 * TOP_K, :]
    slot = lax.broadcasted_iota(jnp.int32, (LOC_ROWS, 1), 0).astype(F32)
    pick = jnp.zeros((LOC_ROWS, MOE_TILE), F32)
    for k in range(TOP_K):
        pick = pick + jnp.where(slot == pos_t[k:k + 1, :], 1.0, 0.0)
    pick = pick.astype(BF16)

    def drain(s):
        pltpu.make_async_copy(loc[s], xs_hbm.at[pl.ds(0, LOC_ROWS)], sem.at[s]).wait()

    for s in range(2):
        @pl.when(t % 2 == s)
        def _(s=s):
            pl.when(t >= 2)(functools.partial(drain, s))
            loc[s][...] = _dot(pick, h_ref[...].astype(BF16))
            _run_copies(t, cnt_ref, start_ref, row_ref,
                        lambda a, b, n: pltpu.make_async_copy(loc[s].at[pl.ds(a, n)], xs_hbm.at[pl.ds(b, n)],
                                                              sem.at[s]))

    @pl.when(t == N_T - 1)
    def _():
        if N_T >= 2:
            drain(N_T % 2)
        drain((N_T - 1) % 2)


def _dispatch(plan, h2, route, n_rows, bm):
    n, d = h2.shape
    pairs = LOC_ROWS
    return pl.pallas_call(
        functools.partial(_dispatch_kernel, BM=bm, N_TAIL=n_rows // bm - (n * TOP_K) // bm, N_T=n // MOE_TILE),
        out_shape=jax.ShapeDtypeStruct((n_rows, d), F32),
        grid_spec=pltpu.PrefetchScalarGridSpec(
            num_scalar_prefetch=5,
            grid=(n // MOE_TILE,),
            in_specs=[pl.BlockSpec((MOE_TILE, d), lambda t, *_: (t, 0)),
                      pl.BlockSpec((MOE_TILE, LANE), lambda t, *_: (t, 0))],
            out_specs=pl.BlockSpec(memory_space=pl.ANY),
            scratch_shapes=[pltpu.VMEM((pairs, d), F32),
                            pltpu.VMEM((pairs, d), F32),
                            pltpu.VMEM((bm, d), F32),
                            pltpu.SemaphoreType.DMA(()),
                            pltpu.SemaphoreType.DMA((2,))]),
        compiler_params=pltpu.CompilerParams(dimension_semantics=("arbitrary",), vmem_limit_bytes=VMEM_LIMIT,
                                             disable_bounds_checks=True),
        name="moe_dispatch",
    )(plan["cnt"], plan["start"], plan["row"], plan["pad_end"], plan["cnt_end"], h2, route)


def _expert_kernel(be_ref, nu_ref, nxt_ref, slot_ref, rows_ref, xs_ref, w1_hbm, b1_ref, w2_hbm, b2_ref, o_ref,
                   w1f, w2f, w1b, w2b, sem, *, layer):
    i = pl.program_id(0)
    de = w2b.shape[0]
    e = be_ref[i]
    prev = be_ref[jnp.maximum(i - 1, 0)]
    fresh = (i == 0) | (e != prev)

    def copies(expert, s):
        return (pltpu.make_async_copy(w1_hbm.at[layer, expert], w1f.at[s], sem.at[s, 0]),
                pltpu.make_async_copy(w2_hbm.at[layer, expert], w2f.at[s], sem.at[s, 1]))

    @pl.when(i == 0)
    def _():
        for cp in copies(e, slot_ref[e]):
            cp.start()

    @pl.when(fresh & (i < nu_ref[0]))
    def _():
        s = slot_ref[e]
        for cp in copies(e, s):
            cp.wait()
        w1b[...] = w1f[s].astype(BF16)
        w2b[...] = w2f[s].astype(BF16)
        nxt = nxt_ref[e]

        @pl.when(nxt >= 0)
        def _():
            for cp in copies(nxt, 1 - s):
                cp.start()

    def ffn(x):
        hdn = _dot(x.astype(BF16), w1b[...]) + b1_ref[...]
        glu = jnp.minimum(hdn[:, :de], SWIGLU_LIMIT)
        lin = jnp.clip(hdn[:, de:], -SWIGLU_LIMIT, SWIGLU_LIMIT)
        act = glu * _sigmoid(SWIGLU_ALPHA * glu) * (lin + 1.0)
        return _dot(act.astype(BF16), w2b[...]) + b2_ref[...]

    half = o_ref.shape[0] // 2
    few = rows_ref[i] <= half

    @pl.when((i < nu_ref[0]) & jnp.logical_not(few))
    def _():
        o_ref[...] = ffn(xs_ref[...])

    @pl.when((i < nu_ref[0]) & few)
    def _():
        o_ref[0:half, :] = ffn(xs_ref[0:half, :])
        o_ref[half:, :] = jnp.zeros((half, o_ref.shape[1]), F32)

    @pl.when(i >= nu_ref[0])
    def _():
        o_ref[...] = jnp.zeros(o_ref.shape, F32)


def _experts(sched, xs, w1, b1, w2, b2, layer, bm):
    n_rows, d = xs.shape
    depth, ne, _, dh2 = w1.shape
    de = w2.shape[2]
    n_blocks = n_rows // bm

    def row_map(i, be, nu, *_):
        return (jnp.minimum(i, nu[0] - 1), 0)

    def b_map(i, be, *_):
        return (layer, be[i], 0, 0)

    return pl.pallas_call(
        functools.partial(_expert_kernel, layer=layer),
        out_shape=jax.ShapeDtypeStruct((n_rows, d), F32),
        grid_spec=pltpu.PrefetchScalarGridSpec(
            num_scalar_prefetch=5,
            grid=(n_blocks,),
            in_specs=[pl.BlockSpec((bm, d), row_map),
                      pl.BlockSpec(memory_space=pl.ANY),
                      pl.BlockSpec((None, None, 1, dh2), b_map),
                      pl.BlockSpec(memory_space=pl.ANY),
                      pl.BlockSpec((None, None, 1, d), b_map)],
            out_specs=pl.BlockSpec((bm, d), lambda i, *_: (i, 0)),
            scratch_shapes=[pltpu.VMEM((2, d, dh2), F32), pltpu.VMEM((2, de, d), F32),
                            pltpu.VMEM((d, dh2), BF16), pltpu.VMEM((de, d), BF16),
                            pltpu.SemaphoreType.DMA((2, 2))]),
        compiler_params=_params(("arbitrary",)),
        name="moe_experts",
    )(sched["blk_expert"], sched["n_used"], sched["next_expert"], sched["slot"], sched["blk_rows"], xs,
      w1, b1.reshape(depth, ne, 1, dh2), w2, b2.reshape(depth, ne, 1, d))


def _combine_kernel(cnt_ref, start_ref, row_ref, yb_hbm, x_ref, route_ref, mod_ref, fg_ref, o_ref,
                    loc0, loc1, sem, *, nt, n_tiles, final):
    t = pl.program_id(0) * nt + pl.program_id(1)
    pairs = LOC_ROWS
    loc = (loc0, loc1)

    def fetch(tile, s):
        _run_copies(tile, cnt_ref, start_ref, row_ref,
                    lambda a, b, n: pltpu.make_async_copy(yb_hbm.at[pl.ds(b, n)], loc[s].at[pl.ds(a, n)],
                                                          sem.at[s]))

    pl.when(t == 0)(functools.partial(fetch, 0, 0))
    route = route_ref[0]
    slot = lax.broadcasted_iota(jnp.int32, (1, pairs), 1).astype(F32)
    wgt = jnp.zeros((MOE_TILE, pairs), F32)
    for k in range(TOP_K):
        wgt = wgt + jnp.where(slot == route[:, TOP_K + k:TOP_K + k + 1],
                              route[:, 2 * TOP_K + k:2 * TOP_K + k + 1], 0.0)
    hi = wgt.astype(BF16)
    lo = (wgt - hi.astype(F32)).astype(BF16)

    for s in range(2):
        @pl.when(t % 2 == s)
        def _(s=s):
            pl.when(t + 1 < n_tiles)(functools.partial(fetch, t + 1, 1 - s))
            pltpu.make_async_copy(yb_hbm.at[pl.ds(0, pairs)], loc[s], sem.at[s]).wait()
            rows = loc[s][...].astype(BF16)
            y = _dot(hi, rows) + _dot(lo, rows)
            out = x_ref[0] + mod_ref[0, 5:6, :] * y
            if final:
                out = _rms(out) * fg_ref[...]
            o_ref[0] = out


def _combine(plan, yb, x, route, mod, final_g, final):
    bsz, s, d = x.shape
    nt = s // MOE_TILE
    pairs = LOC_ROWS
    return pl.pallas_call(
        functools.partial(_combine_kernel, nt=nt, n_tiles=bsz * nt, final=final),
        out_shape=jax.ShapeDtypeStruct((bsz, s, d), F32),
        grid_spec=pltpu.PrefetchScalarGridSpec(
            num_scalar_prefetch=3,
            grid=(bsz, nt),
            in_specs=[pl.BlockSpec(memory_space=pl.ANY),
                      pl.BlockSpec((1, MOE_TILE, d), lambda b, i, *_: (b, i, 0)),
                      pl.BlockSpec((1, MOE_TILE, LANE), lambda b, i, *_: (b, i, 0)),
                      pl.BlockSpec((1, 6, d), lambda b, i, *_: (b, 0, 0)),
                      pl.BlockSpec((1, d), lambda b, i, *_: (0, 0))],
            out_specs=pl.BlockSpec((1, MOE_TILE, d), lambda b, i, *_: (b, i, 0)),
            scratch_shapes=[pltpu.VMEM((pairs, d), F32), pltpu.VMEM((pairs, d), F32),
                            pltpu.SemaphoreType.DMA((2,))]),
        compiler_params=pltpu.CompilerParams(dimension_semantics=("arbitrary", "arbitrary"),
                                             vmem_limit_bytes=VMEM_LIMIT, disable_bounds_checks=True),
        name="moe_combine",
    )(plan["cnt"], plan["start"], plan["row"], yb, x, route, mod, final_g.reshape(1, d))


def _rearranged_in_proj(w_in, b_in):
    sizes = (ML_HEADS * ML_QK, ML_HEADS * ML_QK, ML_HEADS * ML_V, ML_HEADS * ML_V, ML_HEADS, ML_HEADS,
             DSA_HEADS * DSA_DIM, DSA_LATENT, IDX_HEADS * IDX_DIM, IDX_DIM, IDX_HEADS,
             3 * FOX_HEADS * FOX_DIM, FOX_HEADS, N_BRANCH * w_in.shape[0])
    offs = [0]
    for sz in sizes:
        offs.append(offs[-1] + sz)
    (o_mq, o_mk, o_mv, o_mo, o_mi, o_mf, o_dq, o_ckv, o_iq, o_ik, o_iw, o_fx, o_ff, o_g, o_end) = offs
    pad = LANE - (IDX_DIM + 2 * ML_HEADS + IDX_HEADS + FOX_HEADS)

    def cols(a):
        parts = [a[..., o_mq:o_mi],
                 a[..., o_dq:o_ckv],
                 a[..., o_iq:o_ik],
                 a[..., o_ckv:o_iq],
                 a[..., o_ik:o_iw],
                 a[..., o_mi:o_dq],
                 a[..., o_iw:o_fx],
                 a[..., o_ff:o_g],
                 jnp.zeros(a.shape[:-1] + (pad,), a.dtype),
                 a[..., o_fx:o_ff],
                 a[..., o_g:o_end]]
        return jnp.concatenate(parts, axis=-1)

    return cols(w_in).astype(BF16), cols(b_in.reshape(1, -1))


def _moe_plan(stats, bm, n_blocks):
    cnt = stats[:, :N_EXPERTS, 0].astype(jnp.int32)
    start = stats[:, N_EXPERTS:, 0].astype(jnp.int32)
    total = jnp.sum(cnt, axis=0)
    padded = (total + bm - 1) // bm * bm
    pad_end = jnp.cumsum(padded)
    pad_start = pad_end - padded
    row = pad_start[None, :] + jnp.cumsum(cnt, axis=0) - cnt
    used = jnp.sum(cnt, axis=1, keepdims=True)
    cnt = jnp.concatenate([cnt, LOC_ROWS - used], axis=1)
    start = jnp.concatenate([start, used], axis=1)
    parity = jnp.arange(cnt.shape[0], dtype=jnp.int32)[:, None] % 2
    row = jnp.concatenate([row, n_blocks * bm + parity * MOE_TILE], axis=1)
    blk_row = jnp.arange(n_blocks + 1, dtype=jnp.int32) * bm
    blk_expert = jnp.minimum(jnp.sum((pad_end[None, :] <= blk_row[:, None]).astype(jnp.int32), axis=1),
                             N_EXPERTS - 1)
    n_used = (pad_end[-1:] // bm).astype(jnp.int32)
    present = padded > 0
    ids = jnp.arange(N_EXPERTS, dtype=jnp.int32)
    later = jnp.where(present[None, :] & (ids[None, :] > ids[:, None]), ids[None, :], N_EXPERTS)
    nxt = jnp.min(later, axis=1)
    run_end = jnp.sum(jnp.where(ids[None, :] == blk_expert[:, None], (pad_start + total)[None, :], 0), axis=1)
    blk_rows = jnp.clip(run_end - blk_row, 0, bm).astype(jnp.int32)
    sched = dict(blk_expert=blk_expert, n_used=n_used, blk_rows=blk_rows,
                 next_expert=jnp.where(nxt < N_EXPERTS, nxt, -1).astype(jnp.int32),
                 slot=((jnp.cumsum(present) - present) % 2).astype(jnp.int32))
    plan = dict(cnt=cnt.reshape(-1), start=start.reshape(-1), row=row.reshape(-1).astype(jnp.int32),
                pad_end=pad_end.astype(jnp.int32), cnt_end=(pad_start + total).astype(jnp.int32))
    return plan, sched


def kernel(x, c, w_ada, b_ada, w_in, b_in, conv_w, conv_b, ml_norm_g, kv_norm_g, w_uk, w_uv,
           w_br_ml, w_br_dsa, w_br_fox, w_out, w_router, b_router, w1, b1, w2, b2, final_g):
    bsz, s, d = x.shape
    depth = w_in.shape[0]
    n = bsz * s
    bm = 2 * MOE_TILE
    n_blocks = -(-(n * TOP_K + (ROW_ALIGN - 1) * N_EXPERTS * (n // MOE_TILE)) // bm) + N_EXPERTS
    mods = _ada_mod(c, w_ada, b_ada).reshape(depth, bsz, 6, d)
    for l in range(depth):
        mod = mods[l]
        w_r, b_r = _rearranged_in_proj(w_in[l], b_in[l])
        proj, misc = _in_proj(x, mod, w_r, b_r)
        y_ml, fcol, frow = _mlstm(proj, misc, conv_w[l], conv_b[l], ml_norm_g[l])
        y_fox = _fox(proj, fcol, frow)
        y_dsa = _dsa(proj, misc, kv_norm_g[l], w_uk[l].astype(BF16), jnp.swapaxes(w_uv[l], 1, 2).astype(BF16))
        x = _merge(x, proj, y_ml, y_dsa, y_fox, mod, w_br_ml[l].astype(BF16), w_br_dsa[l].astype(BF16),
                   w_br_fox[l].astype(BF16), w_out[l].astype(BF16))
        h2, route, stats = _router(x, mod, w_router[l], b_router[l])
        plan, sched = _moe_plan(stats, bm, n_blocks)
        xs = _dispatch(plan, h2.reshape(n, d), route.reshape(n, LANE), (n_blocks + 1) * bm, bm)
        yb = _experts(sched, xs, w1, b1, w2, b2, l, bm)
        x = _combine(plan, yb, x, route, mod, final_g, final=(l == depth - 1))
    return x
```
